```python
import math, functools
import jax, jax.numpy as jnp
from jax import lax
import numpy as np

D_MODEL = 1024
BATCH = 2
SEQ = 8192
DEPTH = 2
DEC_BATCH = 32
DEC_SEQ = 8
PAST_LEN = 16384
PAGE_SIZE = 128

SSD_EXPAND = 2
D_INNER = SSD_EXPAND * D_MODEL
SSD_HEAD_DIM = 64
SSD_HEADS = D_INNER // SSD_HEAD_DIM
SSD_GROUPS = 4
SSD_HPG = SSD_HEADS // SSD_GROUPS
D_STATE = 128
CONV_WIDTH = 4
CONV_DIM = D_INNER + 2 * SSD_GROUPS * D_STATE
SSD_CHUNK = 128
N_HEADS = 16
KV_HEADS = 4
HEAD_DIM = 64
Q_PER_KV = N_HEADS // KV_HEADS
IDX_HEADS = 8
IDX_DIM = 64
TOPK_MAX = 256
Q_BLOCK = 128
ATT_SCALE = HEAD_DIM ** -0.5
IDX_W_SCALE = (IDX_HEADS ** -0.5) * (IDX_DIM ** -0.5)
D_FF = 2816
EPS = 1e-6
PROJ_SIZES = (D_INNER, CONV_DIM, SSD_HEADS, N_HEADS * HEAD_DIM, KV_HEADS * HEAD_DIM,
              KV_HEADS * HEAD_DIM, IDX_HEADS * IDX_DIM, IDX_DIM, IDX_HEADS, D_MODEL, D_MODEL)
PROJ_SPLITS = tuple(sum(PROJ_SIZES[:i + 1]) for i in range(len(PROJ_SIZES) - 1))
D_PROJ = sum(PROJ_SIZES)

kernel_name = 'hybrid_ssd_dsa_macaron_step'

F32 = jnp.float32


def rms_norm(x, g):
    x32 = x.astype(F32)
    y = x32 * lax.rsqrt(jnp.mean(x32 * x32, axis=-1, keepdims=True) + EPS)
    return (y * g.astype(F32)).astype(x.dtype)


def half_swiglu(x, g, w1, w2):
    a, b = jnp.split(rms_norm(x, g) @ w1, 2, axis=-1)
    return x + 0.5 * ((jax.nn.silu(a) * b) @ w2)


def gather_rows(rows, idx):
    return jax.vmap(lambda r, i: r[i])(rows, idx)


def causal_conv(xbc, buf, w, b):
    l = xbc.shape[1]
    xp = jnp.concatenate([buf.astype(xbc.dtype), xbc], axis=1)
    out = b
    for i in range(CONV_WIDTH):
        out = out + xp[:, i:i + l] * w[i]
    return jax.nn.silu(out), xp[:, -(CONV_WIDTH - 1):]


def ssd_scan(xs, dt, a, bm, cm, h0):
    b_, l = xs.shape[:2]
    q = math.gcd(l, SSD_CHUNK)
    c = l // q
    x = (xs.astype(F32) * dt[..., None]).reshape(b_, c, q, SSD_GROUPS, SSD_HPG, SSD_HEAD_DIM)
    la = (dt * a).reshape(b_, c, q, SSD_GROUPS, SSD_HPG).transpose(0, 1, 3, 4, 2)
    a_cs = jnp.cumsum(la, axis=-1)
    tril = jnp.tril(jnp.ones((q, q), bool))
    lmat = jnp.exp(jnp.where(tril, a_cs[..., :, None] - a_cs[..., None, :], -jnp.inf))
    bc = bm.astype(F32).reshape(b_, c, q, SSD_GROUPS, D_STATE)
    cc = cm.astype(F32).reshape(b_, c, q, SSD_GROUPS, D_STATE)
    cb = jnp.einsum('bclgn,bcsgn->bcgls', cc, bc)
    y_diag = jnp.einsum('bcgjls,bcsgjp->bclgjp', cb[:, :, :, None] * lmat, x)
    decay = jnp.exp(a_cs[..., -1:] - a_cs).transpose(0, 1, 4, 2, 3)
    states = jnp.einsum('bclgn,bclgjp->bcgjpn', bc, x * decay[..., None])
    chunk_decay = jnp.exp(a_cs[..., -1])

    def step(h, inp):
        st, dec = inp
        return dec[..., None, None] * h + st, h

    h_init = h0.astype(F32).reshape(b_, SSD_GROUPS, SSD_HPG, SSD_HEAD_DIM, D_STATE)
    h_fin, h_in = lax.scan(step, h_init, (jnp.moveaxis(states, 1, 0), jnp.moveaxis(chunk_decay, 1, 0)))
    h_in = jnp.moveaxis(h_in, 0, 1)
    y_off = jnp.einsum('bclgn,bcgjpn->bclgjp', cc, h_in) * jnp.exp(a_cs).transpose(0, 1, 4, 2, 3)[..., None]
    y = (y_diag + y_off).reshape(b_, l, SSD_HEADS, SSD_HEAD_DIM)
    return y, h_fin.reshape(b_, SSD_HEADS, SSD_HEAD_DIM, D_STATE)


def ssd_branch(z, xbc, dt_raw, buf, h0, conv_w, conv_b, dt_bias, a_log, d_skip, norm_g):
    xbc, new_buf = causal_conv(xbc, buf, conv_w, conv_b)
    xs, bm, cm = jnp.split(xbc, [D_INNER, D_INNER + SSD_GROUPS * D_STATE], axis=-1)
    b_, l = xs.shape[:2]
    xs = xs.reshape(b_, l, SSD_HEADS, SSD_HEAD_DIM)
    bm = bm.reshape(b_, l, SSD_GROUPS, D_STATE)
    cm = cm.reshape(b_, l, SSD_GROUPS, D_STATE)
    dt = jax.nn.softplus(dt_raw.astype(F32) + dt_bias.astype(F32))
    a = -jnp.exp(a_log.astype(F32))
    y, h_fin = ssd_scan(xs, dt, a, bm, cm, h0)
    y = y + d_skip.astype(F32)[:, None] * xs.astype(F32)
    y = y.reshape(b_, l, D_INNER) * jax.nn.silu(z.astype(F32))
    yg = y.reshape(b_, l, SSD_GROUPS, D_INNER // SSD_GROUPS)
    yg = yg * lax.rsqrt(jnp.mean(yg * yg, axis=-1, keepdims=True) + EPS)
    y = yg.reshape(b_, l, D_INNER) * norm_g.astype(F32)
    return y.astype(z.dtype), new_buf, h_fin


def indexer_scores(qi, wi, ki_all):
    dots = jnp.einsum('bthd,bsd->bths', qi.astype(F32), ki_all.astype(F32))
    return jnp.einsum('bths,bth->bts', jax.nn.relu(dots), wi.astype(F32) * IDX_W_SCALE)


def sparse_attend(q, kg, vg, valid):
    s = jnp.einsum('btkgd,btskd->btkgs', q.astype(F32), kg.astype(F32)) * ATT_SCALE
    s = jnp.where(valid[:, :, None, None, :], s, -jnp.inf)
    p = jax.nn.softmax(s, axis=-1)
    return jnp.einsum('btkgs,btskd->btkgd', p, vg.astype(F32)).astype(q.dtype)


def dsa_prompt(q, k, v, qi, ki, wi):
    b_, l = q.shape[:2]
    topk = min(TOPK_MAX, l // 4)
    nb = l // Q_BLOCK

    def to_blocks(t):
        return jnp.moveaxis(t.reshape((b_, nb, Q_BLOCK) + t.shape[2:]), 1, 0)

    key_pos = jnp.arange(l)
    q_pos = key_pos.reshape(nb, Q_BLOCK)

    def block(args):
        qb, qib, wib, tpos = args
        sc = indexer_scores(qib, wib, ki)
        sc = jnp.where((key_pos[None, :] <= tpos[:, None])[None], sc, -jnp.inf)
        _, idx = lax.top_k(sc, topk)
        valid = idx <= tpos[None, :, None]
        return sparse_attend(qb, gather_rows(k, idx), gather_rows(v, idx), valid)

    out = lax.map(block, (to_blocks(q), to_blocks(qi), to_blocks(wi), q_pos))
    return jnp.moveaxis(out, 0, 1).reshape(b_, l, N_HEADS * HEAD_DIM)


def dsa_sample(q, k, v, qi, ki, wi, ck, cv, ci, page_table):
    db, t = q.shape[:2]
    ps = ck.shape[1]
    past = page_table.shape[1] * ps
    l = past + t
    topk = min(TOPK_MAX, l // 4)
    ki_past = ci[page_table].reshape(db, past, IDX_DIM).astype(ki.dtype)
    ki_all = jnp.concatenate([ki_past, ki], axis=1)
    sc = indexer_scores(qi, wi, ki_all)
    q_pos = past + jnp.arange(t)
    key_pos = jnp.arange(l)
    sc = jnp.where(key_pos[None, None, :] <= q_pos[None, :, None], sc, -jnp.inf)
    _, idx = lax.top_k(sc, topk)
    in_past = (idx < past)[..., None, None]
    pidx = jnp.minimum(idx, past - 1)
    phys = gather_rows(page_table, pidx // ps)
    off = pidx % ps
    nidx = jnp.clip(idx - past, 0, t - 1)
    kg = jnp.where(in_past, ck[phys, off].astype(k.dtype), gather_rows(k, nidx))
    vg = jnp.where(in_past, cv[phys, off].astype(v.dtype), gather_rows(v, nidx))
    valid = idx <= q_pos[None, :, None]
    return sparse_attend(q, kg, vg, valid).reshape(db, t, N_HEADS * HEAD_DIM)


def token_mixer(h, w_in, conv_w, conv_b, dt_bias, a_log, d_skip, ssd_norm,
                w_br_ssd, w_br_attn, w_out, conv_buf, h0, attend):
    b_, l = h.shape[:2]
    z, xbc, dt_raw, q, k, v, qi, ki, wi, g_s, g_a = jnp.split(h @ w_in, PROJ_SPLITS, axis=-1)
    y_ssd, new_buf, h_fin = ssd_branch(z, xbc, dt_raw, conv_buf, h0, conv_w, conv_b,
                                       dt_bias, a_log, d_skip, ssd_norm)
    q = q.reshape(b_, l, KV_HEADS, Q_PER_KV, HEAD_DIM)
    k = k.reshape(b_, l, KV_HEADS, HEAD_DIM)
    v = v.reshape(b_, l, KV_HEADS, HEAD_DIM)
    qi = qi.reshape(b_, l, IDX_HEADS, IDX_DIM)
    y_attn = attend(q, k, v, qi, ki, wi)
    merged = jax.nn.sigmoid(g_s) * (y_ssd @ w_br_ssd) + jax.nn.sigmoid(g_a) * (y_attn @ w_br_attn)
    return merged @ w_out, new_buf, h_fin, k, v, ki


def setup_inputs(seed: int = 0) -> dict:
    key = jax.random.key(seed)
    ks = jax.random.split(key, 32)
    n_pages = PAST_LEN // PAGE_SIZE
    n_used = DEC_BATCH * n_pages
    n_pool = n_used + n_used // 4

    def nrm(k, shape, scale):
        return jax.random.normal(k, shape, F32) * scale

    def gain(k, shape):
        return 1.0 + 0.02 * jax.random.normal(k, shape, F32)

    dt0 = jnp.exp(jax.random.uniform(ks[14], (DEPTH, SSD_HEADS), F32, math.log(1e-3), math.log(1e-1)))
    return {
        'x_prompt': nrm(ks[0], (BATCH, SEQ, D_MODEL), 1.0),
        'x_sample': nrm(ks[1], (DEC_BATCH, DEC_SEQ, D_MODEL), 1.0),
        'cache_k': nrm(ks[2], (DEPTH, n_pool, PAGE_SIZE, KV_HEADS, HEAD_DIM), 1.0),
        'cache_v': nrm(ks[3], (DEPTH, n_pool, PAGE_SIZE, KV_HEADS, HEAD_DIM), 1.0),
        'cache_idx_k': nrm(ks[4], (DEPTH, n_pool, PAGE_SIZE, IDX_DIM), 1.0),
        'state_ssm': nrm(ks[5], (DEPTH, DEC_BATCH, SSD_HEADS, SSD_HEAD_DIM, D_STATE), 0.1),
        'state_conv': nrm(ks[6], (DEPTH, DEC_BATCH, CONV_WIDTH - 1, CONV_DIM), 1.0),
        'page_table': jax.random.permutation(ks[7], n_pool)[:n_used].reshape(DEC_BATCH, n_pages).astype(jnp.int32),
        'ffn1_norm': gain(ks[8], (DEPTH, D_MODEL)),
        'ffn1_w1': nrm(ks[9], (DEPTH, D_MODEL, 2 * D_FF), D_MODEL ** -0.5),
        'ffn1_w2': nrm(ks[10], (DEPTH, D_FF, D_MODEL), D_FF ** -0.5),
        'mix_norm': gain(ks[11], (DEPTH, D_MODEL)),
        'w_in': nrm(ks[12], (DEPTH, D_MODEL, D_PROJ), D_MODEL ** -0.5),
        'conv_w': nrm(ks[13], (DEPTH, CONV_WIDTH, CONV_DIM), CONV_WIDTH ** -0.5),
        'conv_b': nrm(ks[15], (DEPTH, CONV_DIM), 0.01),
        'dt_bias': dt0 + jnp.log(-jnp.expm1(-dt0)),
        'a_log': jnp.log(jax.random.uniform(ks[16], (DEPTH, SSD_HEADS), F32, 1.0, 16.0)),
        'd_skip': 1.0 + 0.1 * jax.random.normal(ks[17], (DEPTH, SSD_HEADS), F32),
        'ssd_norm': gain(ks[18], (DEPTH, D_INNER)),
        'w_branch_ssd': nrm(ks[19], (DEPTH, D_INNER, D_MODEL), D_INNER ** -0.5),
        'w_branch_attn': nrm(ks[20], (DEPTH, N_HEADS * HEAD_DIM, D_MODEL), (N_HEADS * HEAD_DIM) ** -0.5),
        'w_out': nrm(ks[21], (DEPTH, D_MODEL, D_MODEL), D_MODEL ** -0.5),
        'ffn2_norm': gain(ks[22], (DEPTH, D_MODEL)),
        'ffn2_w1': nrm(ks[23], (DEPTH, D_MODEL, 2 * D_FF), D_MODEL ** -0.5),
        'ffn2_w2': nrm(ks[24], (DEPTH, D_FF, D_MODEL), D_FF ** -0.5),
        'final_norm': gain(ks[25], (D_MODEL,)),
    }


def reference(x_prompt, x_sample, cache_k, cache_v, cache_idx_k, state_ssm, state_conv, page_table,
              ffn1_norm, ffn1_w1, ffn1_w2, mix_norm, w_in, conv_w, conv_b, dt_bias, a_log, d_skip,
              ssd_norm, w_branch_ssd, w_branch_attn, w_out, ffn2_norm, ffn2_w1, ffn2_w2, final_norm):
    yp, ys = x_prompt, x_sample
    bp = x_prompt.shape[0]
    kp_l, vp_l, ip_l, sp_l, cp_l = [], [], [], [], []
    ks_l, vs_l, is_l, ss_l, cs_l = [], [], [], [], []
    for l in range(DEPTH):
        yp = half_swiglu(yp, ffn1_norm[l], ffn1_w1[l], ffn1_w2[l])
        ys = half_swiglu(ys, ffn1_norm[l], ffn1_w1[l], ffn1_w2[l])
        mix_w = (w_in[l], conv_w[l], conv_b[l], dt_bias[l], a_log[l], d_skip[l], ssd_norm[l],
                 w_branch_ssd[l], w_branch_attn[l], w_out[l])
        buf0 = jnp.zeros((bp, CONV_WIDTH - 1, CONV_DIM), x_prompt.dtype)
        h00 = jnp.zeros((bp, SSD_HEADS, SSD_HEAD_DIM, D_STATE), F32)
        mp, cbp, hfp, kp, vp, kip = token_mixer(rms_norm(yp, mix_norm[l]), *mix_w, buf0, h00, dsa_prompt)
        att_s = functools.partial(dsa_sample, ck=cache_k[l], cv=cache_v[l], ci=cache_idx_k[l], page_table=page_table)
        ms, cbs, hfs, kss, vss, kis = token_mixer(rms_norm(ys, mix_norm[l]), *mix_w, state_conv[l], state_ssm[l], att_s)
        yp = yp + mp
        ys = ys + ms
        yp = half_swiglu(yp, ffn2_norm[l], ffn2_w1[l], ffn2_w2[l])
        ys = half_swiglu(ys, ffn2_norm[l], ffn2_w1[l], ffn2_w2[l])
        kp_l.append(kp); vp_l.append(vp); ip_l.append(kip); sp_l.append(hfp); cp_l.append(cbp)
        ks_l.append(kss); vs_l.append(vss); is_l.append(kis); ss_l.append(hfs); cs_l.append(cbs)
    y_prompt = rms_norm(yp, final_norm)
    y_sample = rms_norm(ys, final_norm)
    return (y_prompt, y_sample,
            jnp.stack(kp_l), jnp.stack(vp_l), jnp.stack(ip_l), jnp.stack(sp_l), jnp.stack(cp_l),
            jnp.stack(ks_l), jnp.stack(vs_l), jnp.stack(is_l), jnp.stack(ss_l), jnp.stack(cs_l))
```

```python
import functools
import math

import jax
import jax.numpy as jnp
import numpy as np
from jax import lax
from jax.experimental import pallas as pl
from jax.experimental.pallas import tpu as pltpu

F32 = jnp.float32
BF16 = jnp.bfloat16
I32 = jnp.int32

EPS = 1e-6
SSD_CHUNK = 128
TOPK_MAX = 256
LANES = 128
SUBLANES = 8
VMEM_LIMIT_BYTES = 56 * 1024 * 1024
NEG = -1e30
INT_MIN = -(2 ** 31)


def _params(*sem):
    return pltpu.CompilerParams(dimension_semantics=sem, vmem_limit_bytes=VMEM_LIMIT_BYTES)


def _sigmoid(x):
    return 1.0 / (1.0 + jnp.exp(-x))


def _rms(x, g):
    return x * lax.rsqrt(jnp.mean(x * x, axis=-1, keepdims=True) + EPS) * g


def _dot(a, b):
    return jnp.dot(a, b, preferred_element_type=F32)


def _dot_nt(a, b):
    return lax.dot_general(a, b, (((1,), (1,)), ((), ())), preferred_element_type=F32)


def _split2(x):
    hi = x.astype(BF16)
    lo = (x - hi.astype(F32)).astype(BF16)
    return hi, lo


def _split3(x):
    hi = x.astype(BF16)
    r = x - hi.astype(F32)
    mid = r.astype(BF16)
    lo = (r - mid.astype(F32)).astype(BF16)
    return hi, mid, lo


def _ffn_kernel(*refs, post_norm):
    if post_norm:
        x_ref, g_ref, wa_ref, wb_ref, w2_ref, pg_ref, o_ref, h_s, acc_s = refs
    else:
        x_ref, g_ref, wa_ref, wb_ref, w2_ref, o_ref, h_s, acc_s = refs
    f = pl.program_id(1)

    @pl.when(f == 0)
    def _():
        h_s[...] = _rms(x_ref[...], g_ref[...]).astype(BF16)
        acc_s[...] = jnp.zeros_like(acc_s)

    h = h_s[...]
    a = _dot(h, wa_ref[...])
    b = _dot(h, wb_ref[...])
    u = (a * _sigmoid(a) * b).astype(BF16)
    acc_s[...] += _dot(u, w2_ref[...])

    @pl.when(f == pl.num_programs(1) - 1)
    def _():
        y = x_ref[...] + 0.5 * acc_s[...]
        if post_norm:
            y = _rms(y, pg_ref[...])
        o_ref[...] = y


def _ff_tile(d_ff):
    best = None
    for t in range(LANES, d_ff + 1, LANES):
        if d_ff % t == 0 and t <= 1536:
            best = t
    assert best is not None, d_ff
    return best


def _ffn(x, g, w1, w2, post_gain=None):
    t, d = x.shape
    d_ff = w2.shape[0]
    tm = min(512, t)
    tf = _ff_tile(d_ff)
    nf = d_ff // tf
    assert t % tm == 0
    post_norm = post_gain is not None
    in_specs = [
        pl.BlockSpec((tm, d), lambda i, f: (i, 0)),
        pl.BlockSpec((1, d), lambda i, f: (0, 0)),
        pl.BlockSpec((d, tf), lambda i, f: (0, f)),
        pl.BlockSpec((d, tf), lambda i, f: (0, f + nf)),
        pl.BlockSpec((tf, d), lambda i, f: (f, 0)),
    ]
    args = [x, g.reshape(1, d), w1, w1, w2]
    if post_norm:
        in_specs.append(pl.BlockSpec((1, d), lambda i, f: (0, 0)))
        args.append(post_gain.reshape(1, d))
    return pl.pallas_call(
        functools.partial(_ffn_kernel, post_norm=post_norm),
        grid=(t // tm, nf),
        in_specs=in_specs,
        out_specs=pl.BlockSpec((tm, d), lambda i, f: (i, 0)),
        out_shape=jax.ShapeDtypeStruct((t, d), F32),
        scratch_shapes=[pltpu.VMEM((tm, d), BF16), pltpu.VMEM((tm, d), F32)],
        compiler_params=_params("parallel", "arbitrary"),
        name="ffn",
    )(*args)


def _norm_linear_kernel(*refs, out_dtypes):
    n_w = len(out_dtypes)
    x_ref, g_ref = refs[:2]
    w_refs = refs[2:2 + n_w]
    o_refs = list(refs[2 + n_w:])
    h = _rms(x_ref[...], g_ref[...]).astype(BF16)
    for w_ref, dts in zip(w_refs, out_dtypes):
        r = _dot(h, w_ref[...])
        for dt in dts:
            o_refs.pop(0)[...] = r.astype(dt)


def _norm_linear(x, g, weights, out_dtypes, tm):
    t, d = x.shape
    tm = min(tm, t)
    assert t % tm == 0
    in_specs = [pl.BlockSpec((tm, d), lambda i: (i, 0)), pl.BlockSpec((1, d), lambda i: (0, 0))]
    out_specs, out_shape = [], []
    for w, dts in zip(weights, out_dtypes):
        n = w.shape[1]
        in_specs.append(pl.BlockSpec((d, n), lambda i: (0, 0)))
        for dt in dts:
            out_specs.append(pl.BlockSpec((tm, n), lambda i: (i, 0)))
            out_shape.append(jax.ShapeDtypeStruct((t, n), dt))
    return pl.pallas_call(
        functools.partial(_norm_linear_kernel, out_dtypes=tuple(tuple(d_) for d_ in out_dtypes)),
        grid=(t // tm,),
        in_specs=in_specs,
        out_specs=out_specs,
        out_shape=out_shape,
        compiler_params=_params("parallel"),
        name="norm_linear",
    )(x, g.reshape(1, d), *weights)


def _ssd_kernel(z_ref, xbc_ref, dt_ref, buf_ref, h0_ref, cw_ref, cb_ref, dtb_ref, alog_ref, dsk_ref, ng_ref, e_ref,
                y_ref, hfin_ref, xp_s, ht_s, *, qin, d_inner, n_groups, d_state, conv_w):
    q = SSD_CHUNK
    c = pl.program_id(1)
    hp_blocks = d_inner // LANES
    gw = d_inner // n_groups
    assert d_state == LANES and gw % LANES == 0
    pad = SUBLANES
    hist = conv_w - 1

    @pl.when(c == 0)
    def _():
        xp_s[0:pad, :] = buf_ref[0]
        if qin < q:
            xp_s[pad + qin:pad + q, :] = jnp.zeros((q - qin, xp_s.shape[1]), F32)
        for i in range(hp_blocks):
            ht_s[:, i * LANES:(i + 1) * LANES] = h0_ref[0, i * LANES:(i + 1) * LANES, :].T

    xp_s[pad:pad + qin, :] = xbc_ref[0]
    acc = cb_ref[...]
    for i in range(conv_w):
        off = pad - hist + i
        acc = acc + xp_s[off:off + q, :] * cw_ref[i:i + 1, :]
    xc = acc * _sigmoid(acc)
    tail = xp_s[pad + qin - hist:pad + qin, :]
    xp_s[pad - hist:pad, :] = tail

    xs = xc[:, :d_inner]
    bm = xc[:, d_inner:d_inner + n_groups * d_state]
    cm = xc[:, d_inner + n_groups * d_state:]

    dt_raw = dt_ref[0] + dtb_ref[...]
    dt = jnp.maximum(dt_raw, 0.0) + jnp.log1p(jnp.exp(-jnp.abs(dt_raw)))
    if qin < q:
        dt = jnp.concatenate([dt, jnp.zeros((q - qin, LANES), F32)], axis=0)
    la = dt * (-jnp.exp(alog_ref[...]))

    ri = lax.broadcasted_iota(I32, (q, q), 0)
    ci = lax.broadcasted_iota(I32, (q, q), 1)
    causal = ri >= ci
    tril = jnp.where(causal, 1.0, 0.0).astype(BF16)
    eye = jnp.where(ri == ci, 1.0, 0.0).astype(BF16)
    a_cs = sum(_dot(tril, p) for p in _split3(la))
    a_cs_t = sum(_dot_nt(eye, p) for p in _split3(a_cs))
    dec = jnp.exp(a_cs[q - 1:q, :] - a_cs)
    eac = jnp.exp(a_cs)
    stacked = jnp.concatenate([dt, dec, eac], axis=0)
    expd = sum(_dot(p, e_ref[...]) for p in _split2(stacked))
    dt_e, dec_e, eac_e = expd[0:q], expd[q:2 * q], expd[2 * q:3 * q]

    x = xs * dt_e
    xb = x.astype(BF16)
    xd = (x * dec_e).astype(BF16)
    lane = lax.broadcasted_iota(I32, (q, LANES), 1)
    hd = LANES // 2
    heads_per_group = gw // hd

    y_parts = []
    for g in range(n_groups):
        gs = slice(g * gw, (g + 1) * gw)
        cg = cm[:, g * d_state:(g + 1) * d_state].astype(BF16)
        bg = bm[:, g * d_state:(g + 1) * d_state]
        cb = _dot_nt(cg, bg.astype(BF16))
        y_off = _dot(cg, ht_s[:, gs].astype(BF16))
        pair_parts = []
        for p in range(heads_per_group // 2):
            xp = xb[:, g * gw + p * LANES:g * gw + (p + 1) * LANES]
            res = []
            for k in range(2):
                j = g * heads_per_group + 2 * p + k
                diff = a_cs[:, j:j + 1] - a_cs_t[j:j + 1, :]
                lm = jnp.exp(jnp.where(causal, diff, NEG))
                res.append(_dot((cb * lm).astype(BF16), xp))
            pair_parts.append(jnp.where(lane < hd, res[0], res[1]))
        y_diag = jnp.concatenate(pair_parts, axis=1)
        y_parts.append(y_diag + y_off * eac_e[:, gs])
        st = _dot(bg.T.astype(BF16), xd[:, gs])
        ht_s[:, gs] = ht_s[:, gs] * eac_e[q - 1:q, gs] + st

    zz = z_ref[0]
    outs = []
    for g in range(n_groups):
        gs = slice(g * gw, (g + 1) * gw)
        yv = (y_parts[g][:qin] + dsk_ref[:, gs] * xs[:qin, gs]) * (zz[:, gs] * _sigmoid(zz[:, gs]))
        ms = jnp.mean(yv * yv, axis=-1, keepdims=True)
        outs.append(yv * lax.rsqrt(ms + EPS) * ng_ref[:, gs])
    y_ref[0] = jnp.concatenate(outs, axis=1).astype(y_ref.dtype)

    @pl.when(c == pl.num_programs(1) - 1)
    def _():
        for i in range(hp_blocks):
            hfin_ref[0, i * LANES:(i + 1) * LANES, :] = ht_s[:, i * LANES:(i + 1) * LANES].T


def _ssd(z, xbc, dtp, buf, h0, conv_w, conv_b, dt_bias, a_log, d_skip, norm_g, *, n_groups, d_state):
    b, l, d_inner = z.shape
    conv_dim = xbc.shape[-1]
    n_heads, p_dim, n_state = h0.shape[1:]
    width = conv_w.shape[0]
    hist = width - 1
    assert n_heads <= LANES and p_dim == LANES // 2 and n_state == d_state and hist <= SUBLANES
    qin = math.gcd(l, SSD_CHUNK)
    assert qin % SUBLANES == 0 and qin >= hist
    nc = l // qin
    assert nc == 1 or qin == SSD_CHUNK
    hp = n_heads * p_dim
    buf8 = jnp.pad(buf, ((0, 0), (SUBLANES - hist, 0), (0, 0)))
    cw8 = jnp.pad(conv_w, ((0, SUBLANES - width), (0, 0)))
    pad1 = lambda v: jnp.pad(v.reshape(1, -1), ((0, 0), (0, LANES - n_heads)))
    expand = (np.arange(LANES)[:, None] == (np.arange(d_inner)[None, :] // p_dim)).astype(np.float32)
    full = lambda shape: pl.BlockSpec(shape, lambda i, c: (0,) * len(shape))
    y, hfin = pl.pallas_call(
        functools.partial(_ssd_kernel, qin=qin, d_inner=d_inner, n_groups=n_groups, d_state=d_state, conv_w=width),
        grid=(b, nc),
        in_specs=[
            pl.BlockSpec((1, qin, d_inner), lambda i, c: (i, c, 0)),
            pl.BlockSpec((1, qin, conv_dim), lambda i, c: (i, c, 0)),
            pl.BlockSpec((1, qin, LANES), lambda i, c: (i, c, 0)),
            pl.BlockSpec((1, SUBLANES, conv_dim), lambda i, c: (i, 0, 0)),
            pl.BlockSpec((1, hp, n_state), lambda i, c: (i, 0, 0)),
            full((SUBLANES, conv_dim)), full((1, conv_dim)), full((1, LANES)), full((1, LANES)),
            full((1, d_inner)), full((1, d_inner)), full((LANES, d_inner)),
        ],
        out_specs=[
            pl.BlockSpec((1, qin, d_inner), lambda i, c: (i, c, 0)),
            pl.BlockSpec((1, hp, n_state), lambda i, c: (i, 0, 0)),
        ],
        out_shape=[jax.ShapeDtypeStruct((b, l, d_inner), BF16), jax.ShapeDtypeStruct((b, hp, n_state), F32)],
        scratch_shapes=[pltpu.VMEM((SUBLANES + SSD_CHUNK, conv_dim), F32), pltpu.VMEM((n_state, hp), F32)],
        compiler_params=_params("parallel", "arbitrary"),
        name="ssd",
    )(z, xbc, dtp, buf8, h0.reshape(b, hp, n_state), cw8, conv_b.reshape(1, -1), pad1(dt_bias), pad1(a_log),
      jnp.repeat(d_skip, p_dim).reshape(1, -1), norm_g.reshape(1, -1), jnp.asarray(expand, BF16))
    return y, hfin.reshape(b, n_heads, p_dim, n_state)


def _score_key(sc, valid):
    bits = lax.bitcast_convert_type(sc + 0.0, I32)
    key = jnp.where(bits < 0, bits ^ jnp.int32(0x7FFFFFFF), bits)
    return jnp.where(valid, key, jnp.int32(INT_MIN))


def _kth_largest_key(count_ge, rows, k):
    def body(b, u):
        bit = lax.shift_left(jnp.int32(1), jnp.int32(31) - b)
        u_c = u | bit
        cnt = count_ge(u_c ^ jnp.int32(INT_MIN))
        return jnp.where(cnt >= k, u_c, u)
    u = lax.fori_loop(0, 32, body, jnp.zeros((rows, 1), I32))
    return u ^ jnp.int32(INT_MIN)


def _tie_cutoff(count_tie_before, need, rows, n_bits):
    def body(b, p):
        bit = lax.shift_left(jnp.int32(1), jnp.int32(n_bits - 1) - b)
        p_c = p | bit
        return jnp.where(count_tie_before(p_c) <= need - 1, p_c, p)
    return lax.fori_loop(0, n_bits, body, jnp.zeros((rows, 1), I32))


def _dsa_prompt_kernel(q_ref, qi_ref, sa_ref, kk_ref, k_ref, v_ref, y_ref,
                       keys_s, qpad_s, qipad_s, wb_s, m_s, l_s, acc_s,
                       *, tq, tk, n_heads, kv_heads, idx_heads, topk, att_scale, idx_w_scale, wi_lane, seq_len):
    i = pl.program_id(1)
    hd = LANES // 2
    per_kv = n_heads // kv_heads
    n_kt = (i * tq + tq - 1) // tk + 1
    lane = lax.broadcasted_iota(I32, (tq, LANES), 1)
    low = lane < hd

    qf = q_ref[0].astype(F32) * att_scale
    zeros = jnp.zeros((tq, LANES), F32)
    for h in range(n_heads):
        g, r = divmod(h, per_kv)
        slot = (g // 2) * per_kv + r
        sl = qf[:, slot * LANES:(slot + 1) * LANES]
        half = jnp.where(low, sl, 0.0) if g % 2 == 0 else jnp.where(low, 0.0, sl)
        row = [zeros] * (kv_heads // 2)
        row[g // 2] = half
        qpad_s[h * tq:(h + 1) * tq, :] = jnp.concatenate(row, axis=1).astype(BF16)
    qif = qi_ref[0].astype(F32)
    sa = sa_ref[0]
    for h in range(idx_heads):
        sl = qif[:, (h // 2) * LANES:(h // 2 + 1) * LANES]
        half = jnp.where(low, sl, 0.0) if h % 2 == 0 else jnp.where(low, 0.0, sl)
        qipad_s[h * tq:(h + 1) * tq, :] = half.astype(BF16)
        wb_s[h] = jnp.broadcast_to(sa[:, wi_lane + h:wi_lane + h + 1] * idx_w_scale, (tq, LANES))

    qpos = i * tq + lax.broadcasted_iota(I32, (tq, tk), 0)
    kiota = lax.broadcasted_iota(I32, (tq, tk), 1)

    def score_tile(j, carry):
        kt = kk_ref[0, pl.ds(pl.multiple_of(j * tk, tk), tk), :]
        r = _dot_nt(qipad_s[...], kt)
        parts = []
        for c in range(tk // LANES):
            acc = jnp.zeros((tq, LANES), F32)
            for h in range(idx_heads):
                acc = acc + jnp.maximum(r[h * tq:(h + 1) * tq, c * LANES:(c + 1) * LANES], 0.0) * wb_s[h]
            parts.append(acc)
        sc = jnp.concatenate(parts, axis=1)
        keys_s[j] = _score_key(sc, j * tk + kiota <= qpos)
        return carry
    lax.fori_loop(0, n_kt, score_tile, 0)

    def count(pred):
        def body(j, cnt):
            hit = jnp.where(pred(keys_s[j], j), 1, 0)
            for c in range(tk // LANES):
                cnt = cnt + hit[:, c * LANES:(c + 1) * LANES]
            return cnt
        cnt = lax.fori_loop(0, n_kt, body, jnp.zeros((tq, LANES), I32))
        return jnp.sum(cnt, axis=1, keepdims=True)

    thr = _kth_largest_key(lambda t: count(lambda key, j: key >= t), tq, topk)
    c_ge = count(lambda key, j: key >= thr)
    tie = (c_ge > topk) & (thr > INT_MIN)

    @pl.when(jnp.max(jnp.where(tie, 1, 0)) > 0)
    def _():
        need = topk - count(lambda key, j: key > thr)
        cut = _tie_cutoff(lambda p: count(lambda key, j: (key == thr) & (j * tk + kiota < p)),
                          need, tq, int(seq_len).bit_length())
        def drop(j, carry):
            key = keys_s[j]
            keys_s[j] = jnp.where(tie & (key == thr) & (j * tk + kiota > cut), jnp.int32(INT_MIN), key)
            return carry
        lax.fori_loop(0, n_kt, drop, 0)

    thr_sel = jnp.maximum(thr, jnp.int32(INT_MIN + 1))

    m_s[...] = jnp.full(m_s.shape, NEG, F32)
    l_s[...] = jnp.zeros(l_s.shape, F32)
    acc_s[...] = jnp.zeros(acc_s.shape, F32)

    def attend_tile(j, carry):
        ks = pl.ds(pl.multiple_of(j * tk, tk), tk)
        s_all = _dot_nt(qpad_s[...], k_ref[0, ks, :])
        vt = v_ref[0, ks, :]
        bias = jnp.where(keys_s[j] >= thr_sel, 0.0, NEG)
        for g in range(kv_heads):
            ps = []
            for r in range(per_kv):
                rows = slice((g * per_kv + r) * tq, (g * per_kv + r + 1) * tq)
                s = s_all[rows] + bias
                m_old = m_s[rows]
                m_new = jnp.maximum(m_old, jnp.max(s, axis=1, keepdims=True))
                alpha = jnp.exp(m_old - m_new)
                p = jnp.exp(s - m_new)
                l_s[rows] = alpha * l_s[rows] + jnp.sum(p, axis=1, keepdims=True)
                m_s[rows] = m_new
                acc_s[rows] = acc_s[rows] * alpha
                ps.append(p.astype(BF16))
            grows = slice(g * per_kv * tq, (g + 1) * per_kv * tq)
            acc_s[grows] += _dot(jnp.concatenate(ps, axis=0), vt[:, (g // 2) * LANES:(g // 2 + 1) * LANES])
        return carry
    lax.fori_loop(0, n_kt, attend_tile, 0)

    for s in range(n_heads // 2):
        pb, r = divmod(s, per_kv)
        h_lo = (2 * pb) * per_kv + r
        h_hi = (2 * pb + 1) * per_kv + r
        o_lo = acc_s[h_lo * tq:(h_lo + 1) * tq] / l_s[h_lo * tq:(h_lo + 1) * tq]
        o_hi = acc_s[h_hi * tq:(h_hi + 1) * tq] / l_s[h_hi * tq:(h_hi + 1) * tq]
        y_ref[0, :, s * LANES:(s + 1) * LANES] = jnp.where(low, o_lo, o_hi).astype(y_ref.dtype)


def _dsa_prompt(q, qi, sa, kk, k, v, *, n_heads, kv_heads, idx_heads, wi_lane, att_scale, idx_w_scale):
    b, l, dq = q.shape
    tq, tk = 128, 256
    tk = min(tk, l)
    tq = min(tq, l)
    assert l % tq == 0 and l % tk == 0 and kv_heads % 2 == 0 and idx_heads % 2 == 0
    topk = min(TOPK_MAX, l // 4)
    kvw = k.shape[-1]
    return pl.pallas_call(
        functools.partial(_dsa_prompt_kernel, tq=tq, tk=tk, n_heads=n_heads, kv_heads=kv_heads, idx_heads=idx_heads,
                          topk=topk, att_scale=att_scale, idx_w_scale=idx_w_scale, wi_lane=wi_lane, seq_len=l),
        grid=(b, l // tq),
        in_specs=[
            pl.BlockSpec((1, tq, dq), lambda bi, i: (bi, i, 0)),
            pl.BlockSpec((1, tq, qi.shape[-1]), lambda bi, i: (bi, i, 0)),
            pl.BlockSpec((1, tq, LANES), lambda bi, i: (bi, i, 0)),
            pl.BlockSpec((1, l, LANES), lambda bi, i: (bi, 0, 0)),
            pl.BlockSpec((1, l, kvw), lambda bi, i: (bi, 0, 0)),
            pl.BlockSpec((1, l, kvw), lambda bi, i: (bi, 0, 0)),
        ],
        out_specs=pl.BlockSpec((1, tq, dq), lambda bi, i: (bi, i, 0)),
        out_shape=jax.ShapeDtypeStruct((b, l, dq), BF16),
        scratch_shapes=[
            pltpu.VMEM((l // tk, tq, tk), I32),
            pltpu.VMEM((n_heads * tq, kvw), BF16),
            pltpu.VMEM((idx_heads * tq, LANES), BF16),
            pltpu.VMEM((idx_heads, tq, LANES), F32),
            pltpu.VMEM((n_heads * tq, 1), F32),
            pltpu.VMEM((n_heads * tq, 1), F32),
            pltpu.VMEM((n_heads * tq, LANES), F32),
        ],
        compiler_params=_params("parallel", "arbitrary"),
        name="dsa_prompt",
    )(q, qi, sa, kk, k, v)


def _dsa_sample_select_kernel(pt_ref, qi_ref, sa_ref, kn_ref, *rest, pg, t_new, idx_heads, page, topk,
                              idx_w_scale, wi_lane, n_keys):
    page_refs = rest[:pg]
    keys_ref, thr_ref, qi_s, w_s, kn_s = rest[pg:]
    i = pl.program_id(1)
    hd = LANES // 2
    n_steps = pl.num_programs(1)

    @pl.when(i == 0)
    def _():
        qif = qi_ref[0]
        sa = sa_ref[0]
        for h in range(idx_heads):
            qi_s[h * t_new:(h + 1) * t_new, :] = qif[:, h * LANES:h * LANES + hd].astype(F32)
            w_s[h * t_new:(h + 1) * t_new, :] = jnp.broadcast_to(
                sa[:, wi_lane + h:wi_lane + h + 1] * idx_w_scale, (t_new, LANES))
        kn_s[...] = jnp.zeros(kn_s.shape, F32)
        kn_s[0:t_new, :] = kn_ref[0][:, 0:hd].astype(F32)

    def scores(keys_bf):
        ww = jnp.maximum(_dot_nt(qi_s[...].astype(BF16), keys_bf), 0.0) * w_s[...]
        sc = ww[0:t_new]
        for h in range(1, idx_heads):
            sc = sc + ww[h * t_new:(h + 1) * t_new]
        return sc

    always = jnp.full((t_new, page), True)
    for r in range(pg):
        off = pl.multiple_of((i * pg + r) * page, page)
        keys_ref[0, :, pl.ds(off, page)] = _score_key(scores(page_refs[r][0].astype(BF16)), always)

    @pl.when(i == n_steps - 1)
    def _():
        ti = lax.broadcasted_iota(I32, (t_new, page), 0)
        ki = lax.broadcasted_iota(I32, (t_new, page), 1)
        past = n_keys - page
        keys_ref[0, :, past:n_keys] = _score_key(scores(kn_s[...].astype(BF16)), ki <= ti)

        def count(pred):
            hit = jnp.where(pred(keys_ref[0]), 1, 0)
            cnt = hit[:, 0:page]
            for c in range(1, n_keys // page):
                cnt = cnt + hit[:, c * page:(c + 1) * page]
            return jnp.sum(cnt, axis=1, keepdims=True)

        thr = _kth_largest_key(lambda t: count(lambda key: key >= t), t_new, topk)
        c_ge = count(lambda key: key >= thr)
        tie = (c_ge > topk) & (thr > INT_MIN)

        @pl.when(jnp.max(jnp.where(tie, 1, 0)) > 0)
        def _():
            pos = lax.broadcasted_iota(I32, (t_new, n_keys), 1)
            need = topk - count(lambda key: key > thr)
            cut = _tie_cutoff(lambda p: count(lambda key: (key == thr) & (pos < p)), need, t_new,
                              int(n_keys).bit_length())
            key = keys_ref[0]
            keys_ref[0] = jnp.where(tie & (key == thr) & (pos > cut), jnp.int32(INT_MIN), key)

        thr_ref[0] = jnp.broadcast_to(jnp.maximum(thr, jnp.int32(INT_MIN + 1)), (t_new, LANES))


def _dsa_sample_attend_kernel(pt_ref, q_ref, keys_ref, thr_ref, kn_ref, vn_ref, *rest, pg, t_new, n_heads, kv_heads,
                              page, att_scale, n_keys):
    k_refs = rest[:pg]
    v_refs = rest[pg:2 * pg]
    y_ref, qpad_s, m_s, l_s, acc_s, kn_s, vn_s = rest[2 * pg:]
    i = pl.program_id(1)
    hd = LANES // 2
    per_kv = n_heads // kv_heads
    n_steps = pl.num_programs(1)
    lane = lax.broadcasted_iota(I32, (t_new, LANES), 1)
    low = lane < hd

    @pl.when(i == 0)
    def _():
        qf = q_ref[0].astype(F32) * att_scale
        zeros = jnp.zeros((t_new, LANES), F32)
        for h in range(n_heads):
            g, r = divmod(h, per_kv)
            slot = (g // 2) * per_kv + r
            sl = qf[:, slot * LANES:(slot + 1) * LANES]
            half = jnp.where(low, sl, 0.0) if g % 2 == 0 else jnp.where(low, 0.0, sl)
            row = [zeros] * (kv_heads // 2)
            row[g // 2] = half
            qpad_s[h * t_new:(h + 1) * t_new, :] = jnp.concatenate(row, axis=1)
        m_s[...] = jnp.full(m_s.shape, NEG, F32)
        l_s[...] = jnp.zeros(l_s.shape, F32)
        acc_s[...] = jnp.zeros(acc_s.shape, F32)
        kn_s[...] = jnp.zeros(kn_s.shape, F32)
        vn_s[...] = jnp.zeros(vn_s.shape, F32)
        kn_s[0:t_new, :] = kn_ref[0].astype(F32)
        vn_s[0:t_new, :] = vn_ref[0].astype(F32)

    thr = thr_ref[0]

    def attend(kb, vb, key):
        s = _dot_nt(qpad_s[...].astype(BF16), kb)
        bias = jnp.where(key >= thr, 0.0, NEG)
        s = s + jnp.concatenate([bias] * n_heads, axis=0)
        m_old = m_s[...]
        m_new = jnp.maximum(m_old, jnp.max(s, axis=1, keepdims=True))
        alpha = jnp.exp(m_old - m_new)
        p = jnp.exp(s - m_new)
        l_s[...] = alpha * l_s[...] + jnp.sum(p, axis=1, keepdims=True)
        m_s[...] = m_new
        acc_s[...] = acc_s[...] * alpha + _dot(p.astype(BF16), vb)

    for r in range(pg):
        off = pl.multiple_of((i * pg + r) * page, page)
        attend(k_refs[r][0].astype(BF16), v_refs[r][0].astype(BF16), keys_ref[0, :, pl.ds(off, page)])

    @pl.when(i == n_steps - 1)
    def _():
        attend(kn_s[...].astype(BF16), vn_s[...].astype(BF16), keys_ref[0, :, n_keys - page:n_keys])
        o = acc_s[...] / l_s[...]
        for s in range(n_heads // 2):
            pb, r = divmod(s, per_kv)
            h_lo = (2 * pb) * per_kv + r
            h_hi = (2 * pb + 1) * per_kv + r
            o_lo = o[h_lo * t_new:(h_lo + 1) * t_new, pb * LANES:(pb + 1) * LANES]
            o_hi = o[h_hi * t_new:(h_hi + 1) * t_new, pb * LANES:(pb + 1) * LANES]
            y_ref[0, :, s * LANES:(s + 1) * LANES] = jnp.where(low, o_lo, o_hi).astype(y_ref.dtype)


def _pages_per_step(n_pages):
    for pg in (16, 8, 4, 2, 1):
        if n_pages % pg == 0:
            return pg


def _dsa_sample(q, qiw, sa, kk_new, k_new, v_new, ck, cv, ci, page_table, *, n_heads, kv_heads, idx_heads, wi_lane,
                att_scale, idx_w_scale):
    db, t_new, dq = q.shape
    n_pool, page, kvw = ck.shape
    n_pages = page_table.shape[1]
    assert page == LANES and t_new % SUBLANES == 0 and t_new <= page
    pg = _pages_per_step(n_pages)
    n_steps = n_pages // pg
    n_keys = (n_pages + 1) * page
    topk = min(TOPK_MAX, (n_pages * page + t_new) // 4)
    hd = LANES // 2
    page_spec = lambda w, r: pl.BlockSpec((1, page, w), lambda b, i, pt: (pt[b, i * pg + r], 0, 0))
    row_spec = lambda w: pl.BlockSpec((1, t_new, w), lambda b, i, pt: (b, 0, 0))

    keys, thr = pl.pallas_call(
        functools.partial(_dsa_sample_select_kernel, pg=pg, t_new=t_new, idx_heads=idx_heads, page=page, topk=topk,
                          idx_w_scale=idx_w_scale, wi_lane=wi_lane, n_keys=n_keys),
        grid_spec=pltpu.PrefetchScalarGridSpec(
            num_scalar_prefetch=1,
            grid=(db, n_steps),
            in_specs=[row_spec(qiw.shape[-1]), row_spec(LANES), row_spec(LANES)] + [page_spec(hd, r) for r in range(pg)],
            out_specs=[row_spec(n_keys), row_spec(LANES)],
            scratch_shapes=[pltpu.VMEM((idx_heads * t_new, hd), F32), pltpu.VMEM((idx_heads * t_new, LANES), F32),
                            pltpu.VMEM((page, hd), F32)],
        ),
        out_shape=[jax.ShapeDtypeStruct((db, t_new, n_keys), I32), jax.ShapeDtypeStruct((db, t_new, LANES), I32)],
        compiler_params=_params("parallel", "arbitrary"),
        name="dsa_sample_select",
    )(page_table, qiw, sa, kk_new, *([ci] * pg))

    return pl.pallas_call(
        functools.partial(_dsa_sample_attend_kernel, pg=pg, t_new=t_new, n_heads=n_heads, kv_heads=kv_heads, page=page,
                          att_scale=att_scale, n_keys=n_keys),
        grid_spec=pltpu.PrefetchScalarGridSpec(
            num_scalar_prefetch=1,
            grid=(db, n_steps),
            in_specs=[row_spec(dq), row_spec(n_keys), row_spec(LANES), row_spec(kvw), row_spec(kvw)]
                     + [page_spec(kvw, r) for r in range(pg)] * 2,
            out_specs=row_spec(dq),
            scratch_shapes=[pltpu.VMEM((n_heads * t_new, kvw), F32), pltpu.VMEM((n_heads * t_new, 1), F32),
                            pltpu.VMEM((n_heads * t_new, 1), F32), pltpu.VMEM((n_heads * t_new, kvw), F32),
                            pltpu.VMEM((page, kvw), F32), pltpu.VMEM((page, kvw), F32)],
        ),
        out_shape=jax.ShapeDtypeStruct((db, t_new, dq), BF16),
        compiler_params=_params("parallel", "arbitrary"),
        name="dsa_sample_attend",
    )(page_table, q, keys, thr, k_new, v_new, *([ck] * pg), *([cv] * pg))


def _merge_kernel(x_ref, ys_ref, ya_ref, gs_ref, ga_ref, ps_ref, pa_ref, wo_ref, o_ref):
    merged = (_sigmoid(gs_ref[...]) * _dot(ys_ref[...], ps_ref[...])
              + _sigmoid(ga_ref[...]) * _dot(ya_ref[...], pa_ref[...]))
    o_ref[...] = x_ref[...] + _dot(merged.astype(BF16), wo_ref[...])


def _merge(x, y_ssd, y_attn, g_s, g_a, p_ssd, p_attn, w_out):
    t, d = x.shape
    tm = min(512, t)
    assert t % tm == 0
    rows = lambda w: pl.BlockSpec((tm, w), lambda i: (i, 0))
    full = lambda a: pl.BlockSpec(a.shape, lambda i: (0, 0))
    return pl.pallas_call(
        _merge_kernel,
        grid=(t // tm,),
        in_specs=[rows(d), rows(y_ssd.shape[1]), rows(y_attn.shape[1]), rows(d), rows(d),
                  full(p_ssd), full(p_attn), full(w_out)],
        out_specs=rows(d),
        out_shape=jax.ShapeDtypeStruct((t, d), F32),
        compiler_params=_params("parallel"),
        name="merge",
    )(x, y_ssd, y_attn, g_s, g_a, p_ssd, p_attn, w_out)


def _pair_slot_perm(n_heads, kv_heads, hd):
    per_kv = n_heads // kv_heads
    cols = []
    for s in range(n_heads // 2):
        pb, r = divmod(s, per_kv)
        for h in ((2 * pb) * per_kv + r, (2 * pb + 1) * per_kv + r):
            cols.extend(range(h * hd, (h + 1) * hd))
    return np.asarray(cols, np.int32)


def kernel(x_prompt, x_sample, cache_k, cache_v, cache_idx_k, state_ssm, state_conv, page_table, ffn1_norm, ffn1_w1, ffn1_w2, mix_norm, w_in, conv_w, conv_b, dt_bias, a_log, d_skip, ssd_norm, w_branch_ssd, w_branch_attn, w_out, ffn2_norm, ffn2_w1, ffn2_w2, final_norm):
    bp, seq, d_model = x_prompt.shape
    db, dseq, _ = x_sample.shape
    depth, n_pool, page, kv_heads, head_dim = cache_k.shape
    idx_dim = cache_idx_k.shape[-1]
    ssd_heads, ssd_hd, d_state = state_ssm.shape[2:]
    conv_dim = state_conv.shape[-1]
    d_inner = ssd_norm.shape[-1]
    n_groups = (conv_dim - d_inner) // (2 * d_state)
    attn_dim = w_branch_attn.shape[1]
    n_heads = attn_dim // head_dim
    kvw = kv_heads * head_dim
    d_proj = w_in.shape[-1]
    idx_heads = (d_proj - (d_inner + conv_dim + ssd_heads + attn_dim + 2 * kvw + idx_dim + 2 * d_model)) // (idx_dim + 1)
    assert head_dim == LANES // 2 and idx_dim == LANES // 2 and ssd_heads + idx_heads <= LANES
    att_scale = head_dim ** -0.5
    idx_w_scale = (idx_heads ** -0.5) * (idx_dim ** -0.5)
    sizes = (d_inner, conv_dim, ssd_heads, attn_dim, kvw, kvw, idx_heads * idx_dim, idx_dim, idx_heads, d_model, d_model)
    assert sum(sizes) == d_proj
    offs = np.concatenate([[0], np.cumsum(sizes)])
    perm = _pair_slot_perm(n_heads, kv_heads, head_dim)
    wi_lane = ssd_heads

    tp, ts = bp * seq, db * dseq
    yp = x_prompt.reshape(tp, d_model)
    ys = x_sample.reshape(ts, d_model)
    ck = cache_k.reshape(depth, n_pool, page, kvw)
    cv = cache_v.reshape(depth, n_pool, page, kvw)
    dsa_kw = dict(n_heads=n_heads, kv_heads=kv_heads, idx_heads=idx_heads, wi_lane=wi_lane, att_scale=att_scale,
                  idx_w_scale=idx_w_scale)
    outs = {n: [] for n in ("kp", "vp", "ip", "sp", "cp", "ks", "vs", "is", "ss", "cs")}
    hist = conv_w.shape[1] - 1

    for l in range(depth):
        wl = w_in[l]
        col = lambda i: wl[:, offs[i]:offs[i + 1]]
        w_z, w_xbc, w_dt, w_q, w_k, w_v, w_qi, w_ki, w_wi, w_gs, w_ga = [col(i) for i in range(11)]
        w_sa = jnp.concatenate([w_dt, w_wi, jnp.zeros((d_model, LANES - ssd_heads - idx_heads), F32)], axis=1)
        w_qi_wide = jnp.pad(w_qi.reshape(d_model, idx_heads, idx_dim), ((0, 0), (0, 0), (0, LANES - idx_dim)))
        bf = lambda w: w.astype(BF16)
        wa = [bf(w_z), bf(w_xbc)]
        wb_common = [bf(w_q[:, perm]), bf(w_k), bf(w_v), bf(jnp.concatenate([w_ki, w_ki], axis=1)), bf(w_sa),
                     bf(w_gs), bf(w_ga)]
        dt_common = [(BF16,), (F32, BF16), (F32, BF16), (F32, BF16), (F32,), (F32,), (F32,)]
        f1w1, f1w2, f2w1, f2w2 = bf(ffn1_w1[l]), bf(ffn1_w2[l]), bf(ffn2_w1[l]), bf(ffn2_w2[l])
        p_ssd, p_attn, wo = bf(w_branch_ssd[l]), bf(w_branch_attn[l][perm, :]), bf(w_out[l])
        last = l == depth - 1

        def mixer(y, b, s, qi_weight, attend, buf, h0):
            z, xbc = _norm_linear(y, mix_norm[l], wa, [(F32,), (F32,)], 256)
            q, k, kb, v, vb, kk, kkb, sa, g_s, g_a, qi = _norm_linear(
                y, mix_norm[l], wb_common + [qi_weight], dt_common + [(BF16,)], 256)
            r3 = lambda a: a.reshape(b, s, a.shape[-1])
            y_ssd, h_fin = _ssd(r3(z), r3(xbc), r3(sa), buf, h0, conv_w[l], conv_b[l], dt_bias[l], a_log[l],
                                d_skip[l], ssd_norm[l], n_groups=n_groups, d_state=d_state)
            y_attn = attend(r3(q), r3(qi), r3(sa), r3(kkb), r3(kb), r3(vb))
            y = _merge(y, y_ssd.reshape(b * s, d_inner), y_attn.reshape(b * s, attn_dim), g_s, g_a, p_ssd, p_attn, wo)
            new_buf = r3(xbc)[:, s - hist:, :]
            return (y, new_buf, h_fin, k.reshape(b, s, kv_heads, head_dim), v.reshape(b, s, kv_heads, head_dim),
                    r3(kk)[:, :, :idx_dim])

        yp = _ffn(yp, ffn1_norm[l], f1w1, f1w2)
        ys = _ffn(ys, ffn1_norm[l], f1w1, f1w2)

        yp, cbp, hfp, kp, vp, kip = mixer(
            yp, bp, seq, bf(w_qi), functools.partial(_dsa_prompt, **dsa_kw),
            jnp.zeros((bp, hist, conv_dim), F32), jnp.zeros((bp, ssd_heads, ssd_hd, d_state), F32))
        att_s = lambda q, qi, sa, kkb, kb, vb: _dsa_sample(q, qi, sa, kkb, kb, vb, ck[l], cv[l], cache_idx_k[l],
                                                          page_table, **dsa_kw)
        ys, cbs, hfs, kss, vss, kis = mixer(
            ys, db, dseq, bf(w_qi_wide.reshape(d_model, idx_heads * LANES)), att_s, state_conv[l], state_ssm[l])

        pg_ = final_norm if last else None
        yp = _ffn(yp, ffn2_norm[l], f2w1, f2w2, pg_)
        ys = _ffn(ys, ffn2_norm[l], f2w1, f2w2, pg_)
        for n, a in zip(("kp", "vp", "ip", "sp", "cp", "ks", "vs", "is", "ss", "cs"),
                        (kp, vp, kip, hfp, cbp, kss, vss, kis, hfs, cbs)):
            outs[n].append(a)

    st = lambda n: jnp.stack(outs[n])
    return (yp.reshape(bp, seq, d_model), ys.reshape(db, dseq, d_model),
            st("kp"), st("vp"), st("ip"), st("sp"), st("cp"),
            st("ks"), st("vs"), st("is"), st("ss"), st("cs"))
```

```python
import functools
import math

import jax
import jax.numpy as jnp
import numpy as np
from jax import lax
from jax.experimental import pallas as pl
from jax.experimental.pallas import tpu as pltpu

F32 = jnp.float32
BF16 = jnp.bfloat16
I32 = jnp.int32

EPS = 1e-6
SSD_CHUNK = 128
TOPK_MAX = 256
LANES = 128
SUBLANES = 8
VMEM_LIMIT_BYTES = 56 * 1024 * 1024
NEG = -1e30
INT_MIN = -(2 ** 31)


def _params(*sem):
    return pltpu.CompilerParams(dimension_semantics=sem, vmem_limit_bytes=VMEM_LIMIT_BYTES)


def _sigmoid(x):
    return 1.0 / (1.0 + jnp.exp(-x))


def _rms(x, g):
    return x * lax.rsqrt(jnp.mean(x * x, axis=-1, keepdims=True) + EPS) * g


def _dot(a, b):
    return jnp.dot(a, b, preferred_element_type=F32)


def _dot_nt(a, b):
    return lax.dot_general(a, b, (((1,), (1,)), ((), ())), preferred_element_type=F32)


def _split2(x):
    hi = x.astype(BF16)
    lo = (x - hi.astype(F32)).astype(BF16)
    return hi, lo


def _split3(x):
    hi = x.astype(BF16)
    r = x - hi.astype(F32)
    mid = r.astype(BF16)
    lo = (r - mid.astype(F32)).astype(BF16)
    return hi, mid, lo


def _ffn_kernel(*refs, post_norm):
    if post_norm:
        x_ref, g_ref, wa_ref, wb_ref, w2_ref, pg_ref, o_ref, h_s, acc_s = refs
    else:
        x_ref, g_ref, wa_ref, wb_ref, w2_ref, o_ref, h_s, acc_s = refs
    f = pl.program_id(1)

    @pl.when(f == 0)
    def _():
        h_s[...] = _rms(x_ref[...], g_ref[...]).astype(BF16)
        acc_s[...] = jnp.zeros_like(acc_s)

    h = h_s[...]
    a = _dot(h, wa_ref[...])
    b = _dot(h, wb_ref[...])
    u = (a * _sigmoid(a) * b).astype(BF16)
    acc_s[...] += _dot(u, w2_ref[...])

    @pl.when(f == pl.num_programs(1) - 1)
    def _():
        y = x_ref[...] + 0.5 * acc_s[...]
        if post_norm:
            y = _rms(y, pg_ref[...])
        o_ref[...] = y


def _ff_tile(d_ff):
    best = None
    for t in range(LANES, d_ff + 1, LANES):
        if d_ff % t == 0 and t <= 1536:
            best = t
    assert best is not None, d_ff
    return best


def _ffn(x, g, w1, w2, post_gain=None):
    t, d = x.shape
    d_ff = w2.shape[0]
    tm = min(512, t)
    tf = _ff_tile(d_ff)
    nf = d_ff // tf
    assert t % tm == 0
    post_norm = post_gain is not None
    in_specs = [
        pl.BlockSpec((tm, d), lambda i, f: (i, 0)),
        pl.BlockSpec((1, d), lambda i, f: (0, 0)),
        pl.BlockSpec((d, tf), lambda i, f: (0, f)),
        pl.BlockSpec((d, tf), lambda i, f: (0, f + nf)),
        pl.BlockSpec((tf, d), lambda i, f: (f, 0)),
    ]
    args = [x, g.reshape(1, d), w1, w1, w2]
    if post_norm:
        in_specs.append(pl.BlockSpec((1, d), lambda i, f: (0, 0)))
        args.append(post_gain.reshape(1, d))
    return pl.pallas_call(
        functools.partial(_ffn_kernel, post_norm=post_norm),
        grid=(t // tm, nf),
        in_specs=in_specs,
        out_specs=pl.BlockSpec((tm, d), lambda i, f: (i, 0)),
        out_shape=jax.ShapeDtypeStruct((t, d), F32),
        scratch_shapes=[pltpu.VMEM((tm, d), BF16), pltpu.VMEM((tm, d), F32)],
        compiler_params=_params("parallel", "arbitrary"),
        name="ffn",
    )(*args)


def _norm_linear_kernel(*refs, out_dtypes):
    n_w = len(out_dtypes)
    x_ref, g_ref = refs[:2]
    w_refs = refs[2:2 + n_w]
    o_refs = list(refs[2 + n_w:])
    h = _rms(x_ref[...], g_ref[...]).astype(BF16)
    for w_ref, dts in zip(w_refs, out_dtypes):
        r = _dot(h, w_ref[...])
        for dt in dts:
            o_refs.pop(0)[...] = r.astype(dt)


def _norm_linear(x, g, weights, out_dtypes, tm):
    t, d = x.shape
    tm = min(tm, t)
    assert t % tm == 0
    in_specs = [pl.BlockSpec((tm, d), lambda i: (i, 0)), pl.BlockSpec((1, d), lambda i: (0, 0))]
    out_specs, out_shape = [], []
    for w, dts in zip(weights, out_dtypes):
        n = w.shape[1]
        in_specs.append(pl.BlockSpec((d, n), lambda i: (0, 0)))
        for dt in dts:
            out_specs.append(pl.BlockSpec((tm, n), lambda i: (i, 0)))
            out_shape.append(jax.ShapeDtypeStruct((t, n), dt))
    return pl.pallas_call(
        functools.partial(_norm_linear_kernel, out_dtypes=tuple(tuple(d_) for d_ in out_dtypes)),
        grid=(t // tm,),
        in_specs=in_specs,
        out_specs=out_specs,
        out_shape=out_shape,
        compiler_params=_params("parallel"),
        name="norm_linear",
    )(x, g.reshape(1, d), *weights)


def _ssd_kernel(z_ref, xbc_ref, dt_ref, buf_ref, h0_ref, cw_ref, cb_ref, dtb_ref, alog_ref, dsk_ref, ng_ref, e_ref,
                y_ref, hfin_ref, xp_s, ht_s, *, qin, d_inner, n_groups, d_state, conv_w):
    q = SSD_CHUNK
    c = pl.program_id(1)
    hp_blocks = d_inner // LANES
    gw = d_inner // n_groups
    assert d_state == LANES and gw % LANES == 0
    pad = SUBLANES
    hist = conv_w - 1

    @pl.when(c == 0)
    def _():
        xp_s[0:pad, :] = buf_ref[0]
        if qin < q:
            xp_s[pad + qin:pad + q, :] = jnp.zeros((q - qin, xp_s.shape[1]), F32)
        for i in range(hp_blocks):
            ht_s[:, i * LANES:(i + 1) * LANES] = h0_ref[0, i * LANES:(i + 1) * LANES, :].T

    xp_s[pad:pad + qin, :] = xbc_ref[0]
    acc = cb_ref[...]
    for i in range(conv_w):
        off = pad - hist + i
        acc = acc + xp_s[off:off + q, :] * cw_ref[i:i + 1, :]
    xc = acc * _sigmoid(acc)
    tail = xp_s[pad + qin - hist:pad + qin, :]
    xp_s[pad - hist:pad, :] = tail

    xs = xc[:, :d_inner]
    bm = xc[:, d_inner:d_inner + n_groups * d_state]
    cm = xc[:, d_inner + n_groups * d_state:]

    dt_raw = dt_ref[0] + dtb_ref[...]
    dt = jnp.maximum(dt_raw, 0.0) + jnp.log1p(jnp.exp(-jnp.abs(dt_raw)))
    if qin < q:
        dt = jnp.concatenate([dt, jnp.zeros((q - qin, LANES), F32)], axis=0)
    la = dt * (-jnp.exp(alog_ref[...]))

    ri = lax.broadcasted_iota(I32, (q, q), 0)
    ci = lax.broadcasted_iota(I32, (q, q), 1)
    causal = ri >= ci
    tril = jnp.where(causal, 1.0, 0.0).astype(BF16)
    eye = jnp.where(ri == ci, 1.0, 0.0).astype(BF16)
    a_cs = sum(_dot(tril, p) for p in _split3(la))
    a_cs_t = sum(_dot_nt(eye, p) for p in _split3(a_cs))
    dec = jnp.exp(a_cs[q - 1:q, :] - a_cs)
    eac = jnp.exp(a_cs)
    stacked = jnp.concatenate([dt, dec, eac], axis=0)
    expd = sum(_dot(p, e_ref[...]) for p in _split2(stacked))
    dt_e, dec_e, eac_e = expd[0:q], expd[q:2 * q], expd[2 * q:3 * q]

    x = xs * dt_e
    xb = x.astype(BF16)
    xd = (x * dec_e).astype(BF16)
    lane = lax.broadcasted_iota(I32, (q, LANES), 1)
    hd = LANES // 2
    heads_per_group = gw // hd

    y_parts = []
    for g in range(n_groups):
        gs = slice(g * gw, (g + 1) * gw)
        cg = cm[:, g * d_state:(g + 1) * d_state].astype(BF16)
        bg = bm[:, g * d_state:(g + 1) * d_state]
        cb = _dot_nt(cg, bg.astype(BF16))
        y_off = _dot(cg, ht_s[:, gs].astype(BF16))
        pair_parts = []
        for p in range(heads_per_group // 2):
            xp = xb[:, g * gw + p * LANES:g * gw + (p + 1) * LANES]
            res = []
            for k in range(2):
                j = g * heads_per_group + 2 * p + k
                diff = a_cs[:, j:j + 1] - a_cs_t[j:j + 1, :]
                lm = jnp.exp(jnp.where(causal, diff, NEG))
                res.append(_dot((cb * lm).astype(BF16), xp))
            pair_parts.append(jnp.where(lane < hd, res[0], res[1]))
        y_diag = jnp.concatenate(pair_parts, axis=1)
        y_parts.append(y_diag + y_off * eac_e[:, gs])
        st = _dot(bg.T.astype(BF16), xd[:, gs])
        ht_s[:, gs] = ht_s[:, gs] * eac_e[q - 1:q, gs] + st

    zz = z_ref[0]
    outs = []
    for g in range(n_groups):
        gs = slice(g * gw, (g + 1) * gw)
        yv = (y_parts[g][:qin] + dsk_ref[:, gs] * xs[:qin, gs]) * (zz[:, gs] * _sigmoid(zz[:, gs]))
        ms = jnp.mean(yv * yv, axis=-1, keepdims=True)
        outs.append(yv * lax.rsqrt(ms + EPS) * ng_ref[:, gs])
    y_ref[0] = jnp.concatenate(outs, axis=1).astype(y_ref.dtype)

    @pl.when(c == pl.num_programs(1) - 1)
    def _():
        for i in range(hp_blocks):
            hfin_ref[0, i * LANES:(i + 1) * LANES, :] = ht_s[:, i * LANES:(i + 1) * LANES].T


def _ssd(z, xbc, dtp, buf, h0, conv_w, conv_b, dt_bias, a_log, d_skip, norm_g, *, n_groups, d_state):
    b, l, d_inner = z.shape
    conv_dim = xbc.shape[-1]
    n_heads, p_dim, n_state = h0.shape[1:]
    width = conv_w.shape[0]
    hist = width - 1
    assert n_heads <= LANES and p_dim == LANES // 2 and n_state == d_state and hist <= SUBLANES
    qin = math.gcd(l, SSD_CHUNK)
    assert qin % SUBLANES == 0 and qin >= hist
    nc = l // qin
    assert nc == 1 or qin == SSD_CHUNK
    hp = n_heads * p_dim
    buf8 = jnp.pad(buf, ((0, 0), (SUBLANES - hist, 0), (0, 0)))
    cw8 = jnp.pad(conv_w, ((0, SUBLANES - width), (0, 0)))
    pad1 = lambda v: jnp.pad(v.reshape(1, -1), ((0, 0), (0, LANES - n_heads)))
    expand = (np.arange(LANES)[:, None] == (np.arange(d_inner)[None, :] // p_dim)).astype(np.float32)
    full = lambda shape: pl.BlockSpec(shape, lambda i, c: (0,) * len(shape))
    y, hfin = pl.pallas_call(
        functools.partial(_ssd_kernel, qin=qin, d_inner=d_inner, n_groups=n_groups, d_state=d_state, conv_w=width),
        grid=(b, nc),
        in_specs=[
            pl.BlockSpec((1, qin, d_inner), lambda i, c: (i, c, 0)),
            pl.BlockSpec((1, qin, conv_dim), lambda i, c: (i, c, 0)),
            pl.BlockSpec((1, qin, LANES), lambda i, c: (i, c, 0)),
            pl.BlockSpec((1, SUBLANES, conv_dim), lambda i, c: (i, 0, 0)),
            pl.BlockSpec((1, hp, n_state), lambda i, c: (i, 0, 0)),
            full((SUBLANES, conv_dim)), full((1, conv_dim)), full((1, LANES)), full((1, LANES)),
            full((1, d_inner)), full((1, d_inner)), full((LANES, d_inner)),
        ],
        out_specs=[
            pl.BlockSpec((1, qin, d_inner), lambda i, c: (i, c, 0)),
            pl.BlockSpec((1, hp, n_state), lambda i, c: (i, 0, 0)),
        ],
        out_shape=[jax.ShapeDtypeStruct((b, l, d_inner), BF16), jax.ShapeDtypeStruct((b, hp, n_state), F32)],
        scratch_shapes=[pltpu.VMEM((SUBLANES + SSD_CHUNK, conv_dim), F32), pltpu.VMEM((n_state, hp), F32)],
        compiler_params=_params("parallel", "arbitrary"),
        name="ssd",
    )(z, xbc, dtp, buf8, h0.reshape(b, hp, n_state), cw8, conv_b.reshape(1, -1), pad1(dt_bias), pad1(a_log),
      jnp.repeat(d_skip, p_dim).reshape(1, -1), norm_g.reshape(1, -1), jnp.asarray(expand, BF16))
    return y, hfin.reshape(b, n_heads, p_dim, n_state)


def _score_key(sc, valid):
    bits = lax.bitcast_convert_type(sc + 0.0, I32)
    key = jnp.where(bits < 0, bits ^ jnp.int32(0x7FFFFFFF), bits)
    return jnp.where(valid, key, jnp.int32(INT_MIN))


def _kth_largest_key(count_ge, rows, k):
    def body(b, u):
        bit = lax.shift_left(jnp.int32(1), jnp.int32(31) - b)
        u_c = u | bit
        cnt = count_ge(u_c ^ jnp.int32(INT_MIN))
        return jnp.where(cnt >= k, u_c, u)
    u = lax.fori_loop(0, 32, body, jnp.zeros((rows, 1), I32))
    return u ^ jnp.int32(INT_MIN)


def _tie_cutoff(count_tie_before, need, rows, n_bits):
    def body(b, p):
        bit = lax.shift_left(jnp.int32(1), jnp.int32(n_bits - 1) - b)
        p_c = p | bit
        return jnp.where(count_tie_before(p_c) <= need - 1, p_c, p)
    return lax.fori_loop(0, n_bits, body, jnp.zeros((rows, 1), I32))


def _dsa_prompt_kernel(q_ref, qi_ref, sa_ref, kk_ref, k_ref, vx_ref, y_ref,
                       keys_s, qpad_s, qipad_s, wb_s, m_s, l_s, acc_s,
                       *, tq, tk, n_heads, kv_heads, idx_heads, topk, att_scale, idx_w_scale, wi_lane, seq_len):
    i = pl.program_id(1)
    hd = LANES // 2
    per_kv = n_heads // kv_heads
    n_kt = (i * tq + tq - 1) // tk + 1
    lane = lax.broadcasted_iota(I32, (tq, LANES), 1)
    low = lane < hd

    qf = q_ref[0].astype(F32) * att_scale
    zeros = jnp.zeros((tq, LANES), F32)
    for h in range(n_heads):
        g, r = divmod(h, per_kv)
        slot = (g // 2) * per_kv + r
        sl = qf[:, slot * LANES:(slot + 1) * LANES]
        half = jnp.where(low, sl, 0.0) if g % 2 == 0 else jnp.where(low, 0.0, sl)
        row = [zeros] * (kv_heads // 2)
        row[g // 2] = half
        qpad_s[h * tq:(h + 1) * tq, :] = jnp.concatenate(row, axis=1).astype(BF16)
    qif = qi_ref[0].astype(F32)
    sa = sa_ref[0]
    for h in range(idx_heads):
        sl = qif[:, (h // 2) * LANES:(h // 2 + 1) * LANES]
        half = jnp.where(low, sl, 0.0) if h % 2 == 0 else jnp.where(low, 0.0, sl)
        qipad_s[h * tq:(h + 1) * tq, :] = half.astype(BF16)
        wb_s[h] = jnp.broadcast_to(sa[:, wi_lane + h:wi_lane + h + 1] * idx_w_scale, (tq, LANES))

    qpos = i * tq + lax.broadcasted_iota(I32, (tq, tk), 0)
    kiota = lax.broadcasted_iota(I32, (tq, tk), 1)

    def score_tile(j, carry):
        kt = kk_ref[0, pl.ds(pl.multiple_of(j * tk, tk), tk), :]
        r = _dot_nt(qipad_s[...], kt)
        parts = []
        for c in range(tk // LANES):
            acc = jnp.zeros((tq, LANES), F32)
            for h in range(idx_heads):
                acc = acc + jnp.maximum(r[h * tq:(h + 1) * tq, c * LANES:(c + 1) * LANES], 0.0) * wb_s[h]
            parts.append(acc)
        sc = jnp.concatenate(parts, axis=1)
        keys_s[j] = _score_key(sc, j * tk + kiota <= qpos)
        return carry
    lax.fori_loop(0, n_kt, score_tile, 0)

    n_chunks = tk // LANES

    def count(pred):
        def body(j, cnts):
            return tuple(cnts[c] + jnp.where(pred(keys_s[j, :, c * LANES:(c + 1) * LANES], j * tk + c * LANES), 1, 0)
                         for c in range(n_chunks))
        cnts = lax.fori_loop(0, n_kt, body, (jnp.zeros((tq, LANES), I32),) * n_chunks)
        return jnp.sum(sum(cnts), axis=1, keepdims=True)

    def count_ge(t):
        tb = jnp.broadcast_to(t, (tq, LANES))
        return count(lambda key, base: key >= tb)

    thr = _kth_largest_key(count_ge, tq, topk)
    c_ge = count_ge(thr)
    tie = (c_ge > topk) & (thr > INT_MIN)
    liota = lax.broadcasted_iota(I32, (tq, LANES), 1)

    @pl.when(jnp.max(jnp.where(tie, 1, 0)) > 0)
    def _():
        need = topk - count(lambda key, base: key > thr)
        cut = _tie_cutoff(lambda p: count(lambda key, base: (key == thr) & (base + liota < p)),
                          need, tq, int(seq_len).bit_length())
        def drop(j, carry):
            key = keys_s[j]
            keys_s[j] = jnp.where(tie & (key == thr) & (j * tk + kiota > cut), jnp.int32(INT_MIN), key)
            return carry
        lax.fori_loop(0, n_kt, drop, 0)

    thr_b = jnp.broadcast_to(jnp.maximum(thr, jnp.int32(INT_MIN + 1)), (tq, LANES))

    m_s[...] = jnp.full(m_s.shape, NEG, F32)
    l_s[...] = jnp.zeros(l_s.shape, F32)
    acc_s[...] = jnp.zeros(acc_s.shape, F32)

    def attend_tile(j, carry):
        ks = pl.ds(pl.multiple_of(j * tk, tk), tk)
        s_all = _dot_nt(qpad_s[...], k_ref[0, ks, :])
        bias = [jnp.where(keys_s[j, :, c * LANES:(c + 1) * LANES] >= thr_b, 0.0, NEG) for c in range(n_chunks)]
        for g in range(kv_heads):
            ps, alphas = [], []
            for r in range(per_kv):
                rows = slice((g * per_kv + r) * tq, (g * per_kv + r + 1) * tq)
                s = [s_all[rows, c * LANES:(c + 1) * LANES] + bias[c] for c in range(n_chunks)]
                smax = s[0]
                for c in range(1, n_chunks):
                    smax = jnp.maximum(smax, s[c])
                m_old = m_s[rows]
                m_new = jnp.maximum(m_old, jnp.max(smax, axis=1, keepdims=True))
                m_s[rows] = m_new
                alphas.append(jnp.exp(m_old - m_new))
                ps.append(jnp.concatenate([jnp.exp(s[c] - m_new) for c in range(n_chunks)], axis=1).astype(BF16))
            grows = slice(g * per_kv * tq, (g + 1) * per_kv * tq)
            pv = _dot(jnp.concatenate(ps, axis=0), vx_ref[0, ks, (g // 2) * 2 * LANES:(g // 2 + 1) * 2 * LANES])
            alpha = jnp.concatenate(alphas, axis=0)
            acc_s[grows] = acc_s[grows] * alpha + pv[:, :LANES]
            l_s[grows] = l_s[grows] * alpha + pv[:, LANES:]
        return carry
    lax.fori_loop(0, n_kt, attend_tile, 0)

    for s in range(n_heads // 2):
        pb, r = divmod(s, per_kv)
        h_lo = (2 * pb) * per_kv + r
        h_hi = (2 * pb + 1) * per_kv + r
        o_lo = acc_s[h_lo * tq:(h_lo + 1) * tq] / l_s[h_lo * tq:(h_lo + 1) * tq]
        o_hi = acc_s[h_hi * tq:(h_hi + 1) * tq] / l_s[h_hi * tq:(h_hi + 1) * tq]
        y_ref[0, :, s * LANES:(s + 1) * LANES] = jnp.where(low, o_lo, o_hi).astype(y_ref.dtype)


def _dsa_prompt(q, qi, sa, kk, k, v, *, n_heads, kv_heads, idx_heads, wi_lane, att_scale, idx_w_scale):
    b, l, dq = q.shape
    tq, tk = 128, 256
    tk = min(tk, l)
    tq = min(tq, l)
    assert l % tq == 0 and l % tk == 0 and kv_heads % 2 == 0 and idx_heads % 2 == 0
    topk = min(TOPK_MAX, l // 4)
    kvw = k.shape[-1]
    ones = jnp.ones((b, l, LANES), BF16)
    vx = jnp.concatenate([a for p in range(kv_heads // 2) for a in (v[:, :, p * LANES:(p + 1) * LANES], ones)], axis=-1)
    return pl.pallas_call(
        functools.partial(_dsa_prompt_kernel, tq=tq, tk=tk, n_heads=n_heads, kv_heads=kv_heads, idx_heads=idx_heads,
                          topk=topk, att_scale=att_scale, idx_w_scale=idx_w_scale, wi_lane=wi_lane, seq_len=l),
        grid=(b, l // tq),
        in_specs=[
            pl.BlockSpec((1, tq, dq), lambda bi, i: (bi, i, 0)),
            pl.BlockSpec((1, tq, qi.shape[-1]), lambda bi, i: (bi, i, 0)),
            pl.BlockSpec((1, tq, LANES), lambda bi, i: (bi, i, 0)),
            pl.BlockSpec((1, l, LANES), lambda bi, i: (bi, 0, 0)),
            pl.BlockSpec((1, l, kvw), lambda bi, i: (bi, 0, 0)),
            pl.BlockSpec((1, l, vx.shape[-1]), lambda bi, i: (bi, 0, 0)),
        ],
        out_specs=pl.BlockSpec((1, tq, dq), lambda bi, i: (bi, i, 0)),
        out_shape=jax.ShapeDtypeStruct((b, l, dq), BF16),
        scratch_shapes=[
            pltpu.VMEM((l // tk, tq, tk), I32),
            pltpu.VMEM((n_heads * tq, kvw), BF16),
            pltpu.VMEM((idx_heads * tq, LANES), BF16),
            pltpu.VMEM((idx_heads, tq, LANES), F32),
            pltpu.VMEM((n_heads * tq, LANES), F32),
            pltpu.VMEM((n_heads * tq, LANES), F32),
            pltpu.VMEM((n_heads * tq, LANES), F32),
        ],
        compiler_params=_params("parallel", "arbitrary"),
        name="dsa_prompt",
    )(q, qi, sa, kk, k, vx)


def _dsa_sample_select_kernel(pt_ref, qi_ref, sa_ref, kn_ref, *rest, pg, t_new, idx_heads, page, topk,
                              idx_w_scale, wi_lane, n_keys):
    page_refs = rest[:pg]
    keys_ref, thr_ref, qi_s, w_s, kn_s = rest[pg:]
    i = pl.program_id(1)
    hd = LANES // 2
    n_steps = pl.num_programs(1)

    @pl.when(i == 0)
    def _():
        qif = qi_ref[0]
        sa = sa_ref[0]
        for h in range(idx_heads):
            qi_s[h * t_new:(h + 1) * t_new, :] = qif[:, h * LANES:h * LANES + hd].astype(F32)
            w_s[h * t_new:(h + 1) * t_new, :] = jnp.broadcast_to(
                sa[:, wi_lane + h:wi_lane + h + 1] * idx_w_scale, (t_new, LANES))
        kn_s[...] = jnp.zeros(kn_s.shape, F32)
        kn_s[0:t_new, :] = kn_ref[0][:, 0:hd].astype(F32)

    def scores(keys_bf):
        ww = jnp.maximum(_dot_nt(qi_s[...].astype(BF16), keys_bf), 0.0) * w_s[...]
        sc = ww[0:t_new]
        for h in range(1, idx_heads):
            sc = sc + ww[h * t_new:(h + 1) * t_new]
        return sc

    always = jnp.full((t_new, page), True)
    for r in range(pg):
        off = pl.multiple_of((i * pg + r) * page, page)
        keys_ref[0, :, pl.ds(off, page)] = _score_key(scores(page_refs[r][0, 0].astype(BF16)), always)

    @pl.when(i == n_steps - 1)
    def _():
        ti = lax.broadcasted_iota(I32, (t_new, page), 0)
        ki = lax.broadcasted_iota(I32, (t_new, page), 1)
        past = n_keys - page
        keys_ref[0, :, past:n_keys] = _score_key(scores(kn_s[...].astype(BF16)), ki <= ti)

        def count(pred):
            hit = jnp.where(pred(keys_ref[0]), 1, 0)
            cnt = hit[:, 0:page]
            for c in range(1, n_keys // page):
                cnt = cnt + hit[:, c * page:(c + 1) * page]
            return jnp.sum(cnt, axis=1, keepdims=True)

        thr = _kth_largest_key(lambda t: count(lambda key: key >= t), t_new, topk)
        c_ge = count(lambda key: key >= thr)
        tie = (c_ge > topk) & (thr > INT_MIN)

        @pl.when(jnp.max(jnp.where(tie, 1, 0)) > 0)
        def _():
            pos = lax.broadcasted_iota(I32, (t_new, n_keys), 1)
            need = topk - count(lambda key: key > thr)
            cut = _tie_cutoff(lambda p: count(lambda key: (key == thr) & (pos < p)), need, t_new,
                              int(n_keys).bit_length())
            key = keys_ref[0]
            keys_ref[0] = jnp.where(tie & (key == thr) & (pos > cut), jnp.int32(INT_MIN), key)

        thr_ref[0] = jnp.broadcast_to(jnp.maximum(thr, jnp.int32(INT_MIN + 1)), (t_new, LANES))


def _dsa_sample_attend_kernel(pt_ref, q_ref, keys_ref, thr_ref, kn_ref, vn_ref, *rest, pg, t_new, n_heads, kv_heads,
                              page, att_scale, n_keys):
    k_refs = rest[:pg]
    v_refs = rest[pg:2 * pg]
    y_ref, qpad_s, m_s, l_s, acc_s, kn_s, vn_s = rest[2 * pg:]
    i = pl.program_id(1)
    hd = LANES // 2
    per_kv = n_heads // kv_heads
    n_steps = pl.num_programs(1)
    lane = lax.broadcasted_iota(I32, (t_new, LANES), 1)
    low = lane < hd

    @pl.when(i == 0)
    def _():
        qf = q_ref[0].astype(F32) * att_scale
        zeros = jnp.zeros((t_new, LANES), F32)
        for h in range(n_heads):
            g, r = divmod(h, per_kv)
            slot = (g // 2) * per_kv + r
            sl = qf[:, slot * LANES:(slot + 1) * LANES]
            half = jnp.where(low, sl, 0.0) if g % 2 == 0 else jnp.where(low, 0.0, sl)
            row = [zeros] * (kv_heads // 2)
            row[g // 2] = half
            qpad_s[h * t_new:(h + 1) * t_new, :] = jnp.concatenate(row, axis=1)
        m_s[...] = jnp.full(m_s.shape, NEG, F32)
        l_s[...] = jnp.zeros(l_s.shape, F32)
        acc_s[...] = jnp.zeros(acc_s.shape, F32)
        kn_s[...] = jnp.zeros(kn_s.shape, F32)
        vn_s[...] = jnp.zeros(vn_s.shape, F32)
        kn_s[0:t_new, :] = kn_ref[0].astype(F32)
        vn_s[0:t_new, :] = vn_ref[0].astype(F32)

    thr = thr_ref[0]

    def attend(kb, vb, key):
        s = _dot_nt(qpad_s[...].astype(BF16), kb)
        bias = jnp.where(key >= thr, 0.0, NEG)
        s = s + jnp.concatenate([bias] * n_heads, axis=0)
        m_old = m_s[...]
        m_new = jnp.maximum(m_old, jnp.max(s, axis=1, keepdims=True))
        alpha = jnp.exp(m_old - m_new)
        p = jnp.exp(s - m_new)
        l_s[...] = alpha * l_s[...] + jnp.sum(p, axis=1, keepdims=True)
        m_s[...] = m_new
        acc_s[...] = acc_s[...] * alpha + _dot(p.astype(BF16), vb)

    for r in range(pg):
        off = pl.multiple_of((i * pg + r) * page, page)
        attend(k_refs[r][0, 0].astype(BF16), v_refs[r][0, 0].astype(BF16), keys_ref[0, :, pl.ds(off, page)])

    @pl.when(i == n_steps - 1)
    def _():
        attend(kn_s[...].astype(BF16), vn_s[...].astype(BF16), keys_ref[0, :, n_keys - page:n_keys])
        o = acc_s[...] / l_s[...]
        for s in range(n_heads // 2):
            pb, r = divmod(s, per_kv)
            h_lo = (2 * pb) * per_kv + r
            h_hi = (2 * pb + 1) * per_kv + r
            o_lo = o[h_lo * t_new:(h_lo + 1) * t_new, pb * LANES:(pb + 1) * LANES]
            o_hi = o[h_hi * t_new:(h_hi + 1) * t_new, pb * LANES:(pb + 1) * LANES]
            y_ref[0, :, s * LANES:(s + 1) * LANES] = jnp.where(low, o_lo, o_hi).astype(y_ref.dtype)


def _pages_per_step(n_pages):
    for pg in (16, 8, 4, 2, 1):
        if n_pages % pg == 0:
            return pg


def _dsa_sample(q, qiw, sa, kk_new, k_new, v_new, ck, cv, ci, page_table, layer, *, n_heads, kv_heads, idx_heads,
                wi_lane, att_scale, idx_w_scale):
    db, t_new, dq = q.shape
    _, n_pool, page, kvw = ck.shape
    n_pages = page_table.shape[1]
    assert page == LANES and t_new % SUBLANES == 0 and t_new <= page
    pg = _pages_per_step(n_pages)
    n_steps = n_pages // pg
    n_keys = (n_pages + 1) * page
    topk = min(TOPK_MAX, (n_pages * page + t_new) // 4)
    hd = LANES // 2
    page_spec = lambda w, r: pl.BlockSpec((1, 1, page, w), lambda b, i, pt: (layer, pt[b, i * pg + r], 0, 0))
    row_spec = lambda w: pl.BlockSpec((1, t_new, w), lambda b, i, pt: (b, 0, 0))

    keys, thr = pl.pallas_call(
        functools.partial(_dsa_sample_select_kernel, pg=pg, t_new=t_new, idx_heads=idx_heads, page=page, topk=topk,
                          idx_w_scale=idx_w_scale, wi_lane=wi_lane, n_keys=n_keys),
        grid_spec=pltpu.PrefetchScalarGridSpec(
            num_scalar_prefetch=1,
            grid=(db, n_steps),
            in_specs=[row_spec(qiw.shape[-1]), row_spec(LANES), row_spec(LANES)] + [page_spec(hd, r) for r in range(pg)],
            out_specs=[row_spec(n_keys), row_spec(LANES)],
            scratch_shapes=[pltpu.VMEM((idx_heads * t_new, hd), F32), pltpu.VMEM((idx_heads * t_new, LANES), F32),
                            pltpu.VMEM((page, hd), F32)],
        ),
        out_shape=[jax.ShapeDtypeStruct((db, t_new, n_keys), I32), jax.ShapeDtypeStruct((db, t_new, LANES), I32)],
        compiler_params=_params("parallel", "arbitrary"),
        name="dsa_sample_select",
    )(page_table, qiw, sa, kk_new, *([ci] * pg))

    return pl.pallas_call(
        functools.partial(_dsa_sample_attend_kernel, pg=pg, t_new=t_new, n_heads=n_heads, kv_heads=kv_heads, page=page,
                          att_scale=att_scale, n_keys=n_keys),
        grid_spec=pltpu.PrefetchScalarGridSpec(
            num_scalar_prefetch=1,
            grid=(db, n_steps),
            in_specs=[row_spec(dq), row_spec(n_keys), row_spec(LANES), row_spec(kvw), row_spec(kvw)]
                     + [page_spec(kvw, r) for r in range(pg)] * 2,
            out_specs=row_spec(dq),
            scratch_shapes=[pltpu.VMEM((n_heads * t_new, kvw), F32), pltpu.VMEM((n_heads * t_new, 1), F32),
                            pltpu.VMEM((n_heads * t_new, 1), F32), pltpu.VMEM((n_heads * t_new, kvw), F32),
                            pltpu.VMEM((page, kvw), F32), pltpu.VMEM((page, kvw), F32)],
        ),
        out_shape=jax.ShapeDtypeStruct((db, t_new, dq), BF16),
        compiler_params=_params("parallel", "arbitrary"),
        name="dsa_sample_attend",
    )(page_table, q, keys, thr, k_new, v_new, *([ck] * pg), *([cv] * pg))


def _merge_kernel(x_ref, ys_ref, ya_ref, gs_ref, ga_ref, ps_ref, pa_ref, wo_ref, o_ref):
    merged = (_sigmoid(gs_ref[...]) * _dot(ys_ref[...], ps_ref[...])
              + _sigmoid(ga_ref[...]) * _dot(ya_ref[...], pa_ref[...]))
    o_ref[...] = x_ref[...] + _dot(merged.astype(BF16), wo_ref[...])


def _merge(x, y_ssd, y_attn, g_s, g_a, p_ssd, p_attn, w_out):
    t, d = x.shape
    tm = min(512, t)
    assert t % tm == 0
    rows = lambda w: pl.BlockSpec((tm, w), lambda i: (i, 0))
    full = lambda a: pl.BlockSpec(a.shape, lambda i: (0, 0))
    return pl.pallas_call(
        _merge_kernel,
        grid=(t // tm,),
        in_specs=[rows(d), rows(y_ssd.shape[1]), rows(y_attn.shape[1]), rows(d), rows(d),
                  full(p_ssd), full(p_attn), full(w_out)],
        out_specs=rows(d),
        out_shape=jax.ShapeDtypeStruct((t, d), F32),
        compiler_params=_params("parallel"),
        name="merge",
    )(x, y_ssd, y_attn, g_s, g_a, p_ssd, p_attn, w_out)


def _pair_slot_perm(n_heads, kv_heads, hd):
    per_kv = n_heads // kv_heads
    cols = []
    for s in range(n_heads // 2):
        pb, r = divmod(s, per_kv)
        for h in ((2 * pb) * per_kv + r, (2 * pb + 1) * per_kv + r):
            cols.extend(range(h * hd, (h + 1) * hd))
    return np.asarray(cols, np.int32)


def kernel(x_prompt, x_sample, cache_k, cache_v, cache_idx_k, state_ssm, state_conv, page_table, ffn1_norm, ffn1_w1, ffn1_w2, mix_norm, w_in, conv_w, conv_b, dt_bias, a_log, d_skip, ssd_norm, w_branch_ssd, w_branch_attn, w_out, ffn2_norm, ffn2_w1, ffn2_w2, final_norm):
    bp, seq, d_model = x_prompt.shape
    db, dseq, _ = x_sample.shape
    depth, n_pool, page, kv_heads, head_dim = cache_k.shape
    idx_dim = cache_idx_k.shape[-1]
    ssd_heads, ssd_hd, d_state = state_ssm.shape[2:]
    conv_dim = state_conv.shape[-1]
    d_inner = ssd_norm.shape[-1]
    n_groups = (conv_dim - d_inner) // (2 * d_state)
    attn_dim = w_branch_attn.shape[1]
    n_heads = attn_dim // head_dim
    kvw = kv_heads * head_dim
    d_proj = w_in.shape[-1]
    idx_heads = (d_proj - (d_inner + conv_dim + ssd_heads + attn_dim + 2 * kvw + idx_dim + 2 * d_model)) // (idx_dim + 1)
    assert head_dim == LANES // 2 and idx_dim == LANES // 2 and ssd_heads + idx_heads <= LANES
    att_scale = head_dim ** -0.5
    idx_w_scale = (idx_heads ** -0.5) * (idx_dim ** -0.5)
    sizes = (d_inner, conv_dim, ssd_heads, attn_dim, kvw, kvw, idx_heads * idx_dim, idx_dim, idx_heads, d_model, d_model)
    assert sum(sizes) == d_proj
    offs = np.concatenate([[0], np.cumsum(sizes)])
    perm = _pair_slot_perm(n_heads, kv_heads, head_dim)
    wi_lane = ssd_heads

    tp, ts = bp * seq, db * dseq
    yp = x_prompt.reshape(tp, d_model)
    ys = x_sample.reshape(ts, d_model)
    ck = cache_k.reshape(depth, n_pool, page, kvw)
    cv = cache_v.reshape(depth, n_pool, page, kvw)
    dsa_kw = dict(n_heads=n_heads, kv_heads=kv_heads, idx_heads=idx_heads, wi_lane=wi_lane, att_scale=att_scale,
                  idx_w_scale=idx_w_scale)
    outs = {n: [] for n in ("kp", "vp", "ip", "sp", "cp", "ks", "vs", "is", "ss", "cs")}
    hist = conv_w.shape[1] - 1

    for l in range(depth):
        wl = w_in[l]
        col = lambda i: wl[:, offs[i]:offs[i + 1]]
        w_z, w_xbc, w_dt, w_q, w_k, w_v, w_qi, w_ki, w_wi, w_gs, w_ga = [col(i) for i in range(11)]
        w_sa = jnp.concatenate([w_dt, w_wi, jnp.zeros((d_model, LANES - ssd_heads - idx_heads), F32)], axis=1)
        w_qi_wide = jnp.pad(w_qi.reshape(d_model, idx_heads, idx_dim), ((0, 0), (0, 0), (0, LANES - idx_dim)))
        bf = lambda w: w.astype(BF16)
        wa = [bf(w_z), bf(w_xbc)]
        wb_common = [bf(w_q[:, perm]), bf(w_k), bf(w_v), bf(jnp.concatenate([w_ki, w_ki], axis=1)), bf(w_sa),
                     bf(w_gs), bf(w_ga)]
        dt_common = [(BF16,), (F32, BF16), (F32, BF16), (F32, BF16), (F32,), (F32,), (F32,)]
        f1w1, f1w2, f2w1, f2w2 = bf(ffn1_w1[l]), bf(ffn1_w2[l]), bf(ffn2_w1[l]), bf(ffn2_w2[l])
        p_ssd, p_attn, wo = bf(w_branch_ssd[l]), bf(w_branch_attn[l][perm, :]), bf(w_out[l])
        last = l == depth - 1

        def mixer(y, b, s, qi_weight, attend, buf, h0):
            z, xbc = _norm_linear(y, mix_norm[l], wa, [(F32,), (F32,)], 256)
            q, k, kb, v, vb, kk, kkb, sa, g_s, g_a, qi = _norm_linear(
                y, mix_norm[l], wb_common + [qi_weight], dt_common + [(BF16,)], 256)
            r3 = lambda a: a.reshape(b, s, a.shape[-1])
            y_ssd, h_fin = _ssd(r3(z), r3(xbc), r3(sa), buf, h0, conv_w[l], conv_b[l], dt_bias[l], a_log[l],
                                d_skip[l], ssd_norm[l], n_groups=n_groups, d_state=d_state)
            y_attn = attend(r3(q), r3(qi), r3(sa), r3(kkb), r3(kb), r3(vb))
            y = _merge(y, y_ssd.reshape(b * s, d_inner), y_attn.reshape(b * s, attn_dim), g_s, g_a, p_ssd, p_attn, wo)
            new_buf = r3(xbc)[:, s - hist:, :]
            return (y, new_buf, h_fin, k.reshape(b, s, kv_heads, head_dim), v.reshape(b, s, kv_heads, head_dim),
                    r3(kk)[:, :, :idx_dim])

        yp = _ffn(yp, ffn1_norm[l], f1w1, f1w2)
        ys = _ffn(ys, ffn1_norm[l], f1w1, f1w2)

        yp, cbp, hfp, kp, vp, kip = mixer(
            yp, bp, seq, bf(w_qi), functools.partial(_dsa_prompt, **dsa_kw),
            jnp.zeros((bp, hist, conv_dim), F32), jnp.zeros((bp, ssd_heads, ssd_hd, d_state), F32))
        att_s = functools.partial(_dsa_sample, ck=ck, cv=cv, ci=cache_idx_k, page_table=page_table, layer=l, **dsa_kw)
        ys, cbs, hfs, kss, vss, kis = mixer(
            ys, db, dseq, bf(w_qi_wide.reshape(d_model, idx_heads * LANES)), att_s, state_conv[l], state_ssm[l])

        pg_ = final_norm if last else None
        yp = _ffn(yp, ffn2_norm[l], f2w1, f2w2, pg_)
        ys = _ffn(ys, ffn2_norm[l], f2w1, f2w2, pg_)
        for n, a in zip(("kp", "vp", "ip", "sp", "cp", "ks", "vs", "is", "ss", "cs"),
                        (kp, vp, kip, hfp, cbp, kss, vss, kis, hfs, cbs)):
            outs[n].append(a)

    st = lambda n: jnp.stack(outs[n])
    return (yp.reshape(bp, seq, d_model), ys.reshape(db, dseq, d_model),
            st("kp"), st("vp"), st("ip"), st("sp"), st("cp"),
            st("ks"), st("vs"), st("is"), st("ss"), st("cs"))
```

```python
import functools
import math

import jax
import jax.numpy as jnp
import numpy as np
from jax import lax
from jax.experimental import pallas as pl
from jax.experimental.pallas import tpu as pltpu

F32 = jnp.float32
BF16 = jnp.bfloat16
I32 = jnp.int32

EPS = 1e-6
SSD_CHUNK = 128
TOPK_MAX = 256
LANES = 128
SUBLANES = 8
VMEM_LIMIT_BYTES = 56 * 1024 * 1024
NEG = -1e30
INT_MIN = -(2 ** 31)


def _params(*sem):
    return pltpu.CompilerParams(dimension_semantics=sem, vmem_limit_bytes=VMEM_LIMIT_BYTES)


def _sigmoid(x):
    return 1.0 / (1.0 + jnp.exp(-x))


def _rms(x, g):
    return x * lax.rsqrt(jnp.mean(x * x, axis=-1, keepdims=True) + EPS) * g


def _dot(a, b):
    return jnp.dot(a, b, preferred_element_type=F32)


def _dot_nt(a, b):
    return lax.dot_general(a, b, (((1,), (1,)), ((), ())), preferred_element_type=F32)


def _split2(x):
    hi = x.astype(BF16)
    lo = (x - hi.astype(F32)).astype(BF16)
    return hi, lo


def _split3(x):
    hi = x.astype(BF16)
    r = x - hi.astype(F32)
    mid = r.astype(BF16)
    lo = (r - mid.astype(F32)).astype(BF16)
    return hi, mid, lo


def _ffn_kernel(*refs, post_norm):
    if post_norm:
        x_ref, g_ref, wa_ref, wb_ref, w2_ref, pg_ref, o_ref, h_s, acc_s = refs
    else:
        x_ref, g_ref, wa_ref, wb_ref, w2_ref, o_ref, h_s, acc_s = refs
    f = pl.program_id(1)

    @pl.when(f == 0)
    def _():
        h_s[...] = _rms(x_ref[...], g_ref[...]).astype(BF16)
        acc_s[...] = jnp.zeros_like(acc_s)

    h = h_s[...]
    a = _dot(h, wa_ref[...])
    b = _dot(h, wb_ref[...])
    u = (a * _sigmoid(a) * b).astype(BF16)
    acc_s[...] += _dot(u, w2_ref[...])

    @pl.when(f == pl.num_programs(1) - 1)
    def _():
        y = x_ref[...] + 0.5 * acc_s[...]
        if post_norm:
            y = _rms(y, pg_ref[...])
        o_ref[...] = y


def _ff_tile(d_ff):
    best = None
    for t in range(LANES, d_ff + 1, LANES):
        if d_ff % t == 0 and t <= 1536:
            best = t
    assert best is not None, d_ff
    return best


def _ffn(x, g, w1, w2, post_gain=None):
    t, d = x.shape
    d_ff = w2.shape[0]
    tm = min(512, t)
    tf = _ff_tile(d_ff)
    nf = d_ff // tf
    assert t % tm == 0
    post_norm = post_gain is not None
    in_specs = [
        pl.BlockSpec((tm, d), lambda i, f: (i, 0)),
        pl.BlockSpec((1, d), lambda i, f: (0, 0)),
        pl.BlockSpec((d, tf), lambda i, f: (0, f)),
        pl.BlockSpec((d, tf), lambda i, f: (0, f + nf)),
        pl.BlockSpec((tf, d), lambda i, f: (f, 0)),
    ]
    args = [x, g.reshape(1, d), w1, w1, w2]
    if post_norm:
        in_specs.append(pl.BlockSpec((1, d), lambda i, f: (0, 0)))
        args.append(post_gain.reshape(1, d))
    return pl.pallas_call(
        functools.partial(_ffn_kernel, post_norm=post_norm),
        grid=(t // tm, nf),
        in_specs=in_specs,
        out_specs=pl.BlockSpec((tm, d), lambda i, f: (i, 0)),
        out_shape=jax.ShapeDtypeStruct((t, d), F32),
        scratch_shapes=[pltpu.VMEM((tm, d), BF16), pltpu.VMEM((tm, d), F32)],
        compiler_params=_params("parallel", "arbitrary"),
        name="ffn",
    )(*args)


def _norm_linear_kernel(*refs, out_dtypes):
    n_w = len(out_dtypes)
    x_ref, g_ref = refs[:2]
    w_refs = refs[2:2 + n_w]
    o_refs = list(refs[2 + n_w:])
    h = _rms(x_ref[...], g_ref[...]).astype(BF16)
    for w_ref, dts in zip(w_refs, out_dtypes):
        r = _dot(h, w_ref[...])
        for dt in dts:
            o_refs.pop(0)[...] = r.astype(dt)


def _norm_linear(x, g, weights, out_dtypes, tm):
    t, d = x.shape
    tm = min(tm, t)
    assert t % tm == 0
    in_specs = [pl.BlockSpec((tm, d), lambda i: (i, 0)), pl.BlockSpec((1, d), lambda i: (0, 0))]
    out_specs, out_shape = [], []
    for w, dts in zip(weights, out_dtypes):
        n = w.shape[1]
        in_specs.append(pl.BlockSpec((d, n), lambda i: (0, 0)))
        for dt in dts:
            out_specs.append(pl.BlockSpec((tm, n), lambda i: (i, 0)))
            out_shape.append(jax.ShapeDtypeStruct((t, n), dt))
    return pl.pallas_call(
        functools.partial(_norm_linear_kernel, out_dtypes=tuple(tuple(d_) for d_ in out_dtypes)),
        grid=(t // tm,),
        in_specs=in_specs,
        out_specs=out_specs,
        out_shape=out_shape,
        compiler_params=_params("parallel"),
        name="norm_linear",
    )(x, g.reshape(1, d), *weights)


def _ssd_kernel(z_ref, xbc_ref, dt_ref, buf_ref, h0_ref, cw_ref, cb_ref, dtb_ref, alog_ref, dsk_ref, ng_ref, e_ref,
                y_ref, hfin_ref, xp_s, ht_s, *, qin, d_inner, n_groups, d_state, conv_w):
    q = SSD_CHUNK
    c = pl.program_id(1)
    hp_blocks = d_inner // LANES
    gw = d_inner // n_groups
    assert d_state == LANES and gw % LANES == 0
    pad = SUBLANES
    hist = conv_w - 1

    @pl.when(c == 0)
    def _():
        xp_s[0:pad, :] = buf_ref[0]
        if qin < q:
            xp_s[pad + qin:pad + q, :] = jnp.zeros((q - qin, xp_s.shape[1]), F32)
        for i in range(hp_blocks):
            ht_s[:, i * LANES:(i + 1) * LANES] = h0_ref[0, i * LANES:(i + 1) * LANES, :].T

    xp_s[pad:pad + qin, :] = xbc_ref[0]
    acc = cb_ref[...]
    for i in range(conv_w):
        off = pad - hist + i
        acc = acc + xp_s[off:off + q, :] * cw_ref[i:i + 1, :]
    xc = acc * _sigmoid(acc)
    tail = xp_s[pad + qin - hist:pad + qin, :]
    xp_s[pad - hist:pad, :] = tail

    xs = xc[:, :d_inner]
    bm = xc[:, d_inner:d_inner + n_groups * d_state]
    cm = xc[:, d_inner + n_groups * d_state:]

    dt_raw = dt_ref[0] + dtb_ref[...]
    dt = jnp.maximum(dt_raw, 0.0) + jnp.log1p(jnp.exp(-jnp.abs(dt_raw)))
    if qin < q:
        dt = jnp.concatenate([dt, jnp.zeros((q - qin, LANES), F32)], axis=0)
    la = dt * (-jnp.exp(alog_ref[...]))

    ri = lax.broadcasted_iota(I32, (q, q), 0)
    ci = lax.broadcasted_iota(I32, (q, q), 1)
    causal = ri >= ci
    tril = jnp.where(causal, 1.0, 0.0).astype(BF16)
    eye = jnp.where(ri == ci, 1.0, 0.0).astype(BF16)
    a_cs = sum(_dot(tril, p) for p in _split3(la))
    a_cs_t = sum(_dot_nt(eye, p) for p in _split3(a_cs))
    dec = jnp.exp(a_cs[q - 1:q, :] - a_cs)
    eac = jnp.exp(a_cs)
    stacked = jnp.concatenate([dt, dec, eac], axis=0)
    expd = sum(_dot(p, e_ref[...]) for p in _split2(stacked))
    dt_e, dec_e, eac_e = expd[0:q], expd[q:2 * q], expd[2 * q:3 * q]

    x = xs * dt_e
    xb = x.astype(BF16)
    xd = (x * dec_e).astype(BF16)
    lane = lax.broadcasted_iota(I32, (q, LANES), 1)
    hd = LANES // 2
    heads_per_group = gw // hd

    y_parts = []
    for g in range(n_groups):
        gs = slice(g * gw, (g + 1) * gw)
        cg = cm[:, g * d_state:(g + 1) * d_state].astype(BF16)
        bg = bm[:, g * d_state:(g + 1) * d_state]
        cb = _dot_nt(cg, bg.astype(BF16))
        y_off = _dot(cg, ht_s[:, gs].astype(BF16))
        pair_parts = []
        for p in range(heads_per_group // 2):
            xp = xb[:, g * gw + p * LANES:g * gw + (p + 1) * LANES]
            res = []
            for k in range(2):
                j = g * heads_per_group + 2 * p + k
                diff = a_cs[:, j:j + 1] - a_cs_t[j:j + 1, :]
                lm = jnp.exp(jnp.where(causal, diff, NEG))
                res.append(_dot((cb * lm).astype(BF16), xp))
            pair_parts.append(jnp.where(lane < hd, res[0], res[1]))
        y_diag = jnp.concatenate(pair_parts, axis=1)
        y_parts.append(y_diag + y_off * eac_e[:, gs])
        st = _dot(bg.T.astype(BF16), xd[:, gs])
        ht_s[:, gs] = ht_s[:, gs] * eac_e[q - 1:q, gs] + st

    zz = z_ref[0]
    outs = []
    for g in range(n_groups):
        gs = slice(g * gw, (g + 1) * gw)
        yv = (y_parts[g][:qin] + dsk_ref[:, gs] * xs[:qin, gs]) * (zz[:, gs] * _sigmoid(zz[:, gs]))
        ms = jnp.mean(yv * yv, axis=-1, keepdims=True)
        outs.append(yv * lax.rsqrt(ms + EPS) * ng_ref[:, gs])
    y_ref[0] = jnp.concatenate(outs, axis=1).astype(y_ref.dtype)

    @pl.when(c == pl.num_programs(1) - 1)
    def _():
        for i in range(hp_blocks):
            hfin_ref[0, i * LANES:(i + 1) * LANES, :] = ht_s[:, i * LANES:(i + 1) * LANES].T


def _ssd(z, xbc, dtp, buf, h0, conv_w, conv_b, dt_bias, a_log, d_skip, norm_g, *, n_groups, d_state):
    b, l, d_inner = z.shape
    conv_dim = xbc.shape[-1]
    n_heads, p_dim, n_state = h0.shape[1:]
    width = conv_w.shape[0]
    hist = width - 1
    assert n_heads <= LANES and p_dim == LANES // 2 and n_state == d_state and hist <= SUBLANES
    qin = math.gcd(l, SSD_CHUNK)
    assert qin % SUBLANES == 0 and qin >= hist
    nc = l // qin
    assert nc == 1 or qin == SSD_CHUNK
    hp = n_heads * p_dim
    buf8 = jnp.pad(buf, ((0, 0), (SUBLANES - hist, 0), (0, 0)))
    cw8 = jnp.pad(conv_w, ((0, SUBLANES - width), (0, 0)))
    pad1 = lambda v: jnp.pad(v.reshape(1, -1), ((0, 0), (0, LANES - n_heads)))
    expand = (np.arange(LANES)[:, None] == (np.arange(d_inner)[None, :] // p_dim)).astype(np.float32)
    full = lambda shape: pl.BlockSpec(shape, lambda i, c: (0,) * len(shape))
    y, hfin = pl.pallas_call(
        functools.partial(_ssd_kernel, qin=qin, d_inner=d_inner, n_groups=n_groups, d_state=d_state, conv_w=width),
        grid=(b, nc),
        in_specs=[
            pl.BlockSpec((1, qin, d_inner), lambda i, c: (i, c, 0)),
            pl.BlockSpec((1, qin, conv_dim), lambda i, c: (i, c, 0)),
            pl.BlockSpec((1, qin, LANES), lambda i, c: (i, c, 0)),
            pl.BlockSpec((1, SUBLANES, conv_dim), lambda i, c: (i, 0, 0)),
            pl.BlockSpec((1, hp, n_state), lambda i, c: (i, 0, 0)),
            full((SUBLANES, conv_dim)), full((1, conv_dim)), full((1, LANES)), full((1, LANES)),
            full((1, d_inner)), full((1, d_inner)), full((LANES, d_inner)),
        ],
        out_specs=[
            pl.BlockSpec((1, qin, d_inner), lambda i, c: (i, c, 0)),
            pl.BlockSpec((1, hp, n_state), lambda i, c: (i, 0, 0)),
        ],
        out_shape=[jax.ShapeDtypeStruct((b, l, d_inner), BF16), jax.ShapeDtypeStruct((b, hp, n_state), F32)],
        scratch_shapes=[pltpu.VMEM((SUBLANES + SSD_CHUNK, conv_dim), F32), pltpu.VMEM((n_state, hp), F32)],
        compiler_params=_params("parallel", "arbitrary"),
        name="ssd",
    )(z, xbc, dtp, buf8, h0.reshape(b, hp, n_state), cw8, conv_b.reshape(1, -1), pad1(dt_bias), pad1(a_log),
      jnp.repeat(d_skip, p_dim).reshape(1, -1), norm_g.reshape(1, -1), jnp.asarray(expand, BF16))
    return y, hfin.reshape(b, n_heads, p_dim, n_state)


def _score_key(sc, valid):
    bits = lax.bitcast_convert_type(sc + 0.0, I32)
    key = jnp.where(bits < 0, bits ^ jnp.int32(0x7FFFFFFF), bits)
    return jnp.where(valid, key, jnp.int32(INT_MIN))


def _kth_largest_key(count_ge, rows, k):
    def body(b, t):
        cand = t + lax.shift_left(jnp.int32(1), jnp.int32(31) - b)
        return jnp.where(count_ge(cand) >= k, cand, t)
    return lax.fori_loop(0, 32, body, jnp.full((rows, 1), INT_MIN, I32))


def _tie_cutoff(count_tie_before, need, rows, n_bits):
    def body(b, p):
        bit = lax.shift_left(jnp.int32(1), jnp.int32(n_bits - 1) - b)
        p_c = p | bit
        return jnp.where(count_tie_before(p_c) <= need - 1, p_c, p)
    return lax.fori_loop(0, n_bits, body, jnp.zeros((rows, 1), I32))


def _dsa_prompt_kernel(q_ref, qi_ref, sa_ref, kk_ref, k_ref, vx_ref, y_ref,
                       keys_s, qpad_s, qipad_s, wb_s, m_s, acc_s,
                       *, tq, tk, n_heads, kv_heads, idx_heads, topk, att_scale, idx_w_scale, wi_lane, seq_len):
    i = pl.program_id(1)
    hd = LANES // 2
    per_kv = n_heads // kv_heads
    n_kt = (i * tq + tq - 1) // tk + 1
    lane = lax.broadcasted_iota(I32, (tq, LANES), 1)
    low = lane < hd

    qf = q_ref[0].astype(F32) * (att_scale * math.log2(math.e))
    zeros = jnp.zeros((tq, LANES), F32)
    for h in range(n_heads):
        g, r = divmod(h, per_kv)
        slot = (g // 2) * per_kv + r
        sl = qf[:, slot * LANES:(slot + 1) * LANES]
        half = jnp.where(low, sl, 0.0) if g % 2 == 0 else jnp.where(low, 0.0, sl)
        row = [zeros] * (kv_heads // 2)
        row[g // 2] = half
        qpad_s[h * tq:(h + 1) * tq, :] = jnp.concatenate(row, axis=1).astype(BF16)
    qif = qi_ref[0].astype(F32)
    sa = sa_ref[0]
    for h in range(idx_heads):
        sl = qif[:, (h // 2) * LANES:(h // 2 + 1) * LANES]
        half = jnp.where(low, sl, 0.0) if h % 2 == 0 else jnp.where(low, 0.0, sl)
        qipad_s[h * tq:(h + 1) * tq, :] = half.astype(BF16)
        wb_s[h] = jnp.broadcast_to(sa[:, wi_lane + h:wi_lane + h + 1] * idx_w_scale, (tq, LANES))

    qpos = i * tq + lax.broadcasted_iota(I32, (tq, tk), 0)
    kiota = lax.broadcasted_iota(I32, (tq, tk), 1)

    def score_tile(j, carry):
        kt = kk_ref[0, pl.ds(pl.multiple_of(j * tk, tk), tk), :]
        r = _dot_nt(qipad_s[...], kt)
        parts = []
        for c in range(tk // LANES):
            acc = jnp.zeros((tq, LANES), F32)
            for h in range(idx_heads):
                acc = acc + jnp.maximum(r[h * tq:(h + 1) * tq, c * LANES:(c + 1) * LANES], 0.0) * wb_s[h]
            parts.append(acc)
        sc = jnp.concatenate(parts, axis=1)
        keys_s[j] = _score_key(sc, j * tk + kiota <= qpos)
        return carry
    lax.fori_loop(0, n_kt, score_tile, 0)

    n_chunks = tk // LANES

    def count(pred):
        def body(j, cnt):
            for c in range(n_chunks):
                cnt = cnt + jnp.where(pred(keys_s[j, :, c * LANES:(c + 1) * LANES], j * tk + c * LANES), 1, 0)
            return cnt
        cnt = lax.fori_loop(0, n_kt, body, jnp.zeros((tq, LANES), I32))
        return jnp.sum(cnt, axis=1, keepdims=True)

    def count_ge(t):
        tb = jnp.broadcast_to(t, (tq, LANES))
        return count(lambda key, base: key >= tb)

    thr = _kth_largest_key(count_ge, tq, topk)
    c_ge = count_ge(thr)
    tie = (c_ge > topk) & (thr > INT_MIN)
    liota = lax.broadcasted_iota(I32, (tq, LANES), 1)

    @pl.when(jnp.max(jnp.where(tie, 1, 0)) > 0)
    def _():
        need = topk - count(lambda key, base: key > thr)
        cut = _tie_cutoff(lambda p: count(lambda key, base: (key == thr) & (base + liota < p)),
                          need, tq, int(seq_len).bit_length())
        def drop(j, carry):
            key = keys_s[j]
            keys_s[j] = jnp.where(tie & (key == thr) & (j * tk + kiota > cut), jnp.int32(INT_MIN), key)
            return carry
        lax.fori_loop(0, n_kt, drop, 0)

    thr_b = jnp.broadcast_to(jnp.maximum(thr, jnp.int32(INT_MIN + 1)), (tq, LANES))

    m_s[...] = jnp.full(m_s.shape, NEG, F32)
    acc_s[...] = jnp.zeros(acc_s.shape, F32)

    def attend_tile(j, carry):
        ks = pl.ds(pl.multiple_of(j * tk, tk), tk)
        s_all = _dot_nt(qpad_s[...], k_ref[0, ks, :])
        bias = [jnp.where(keys_s[j, :, c * LANES:(c + 1) * LANES] >= thr_b, 0.0, NEG) for c in range(n_chunks)]
        for g in range(kv_heads):
            ps, alphas = [], []
            for r in range(per_kv):
                rows = slice((g * per_kv + r) * tq, (g * per_kv + r + 1) * tq)
                s = [s_all[rows, c * LANES:(c + 1) * LANES] + bias[c] for c in range(n_chunks)]
                smax = s[0]
                for c in range(1, n_chunks):
                    smax = jnp.maximum(smax, s[c])
                m_old = m_s[rows]
                m_new = jnp.maximum(m_old, jnp.max(smax, axis=1, keepdims=True))
                m_s[rows] = m_new
                alphas.append(jnp.exp2(m_old - m_new))
                ps.append(jnp.concatenate([jnp.exp2(s[c] - m_new) for c in range(n_chunks)], axis=1).astype(BF16))
            grows = slice(g * per_kv * tq, (g + 1) * per_kv * tq)
            pv = _dot(jnp.concatenate(ps, axis=0), vx_ref[0, ks, g * LANES:(g + 1) * LANES])
            acc_s[grows] = acc_s[grows] * jnp.concatenate(alphas, axis=0) + pv
        return carry
    lax.fori_loop(0, n_kt, attend_tile, 0)

    def normalised(h):
        a = acc_s[h * tq:(h + 1) * tq]
        return a / pltpu.roll(a, hd, 1)
    for s in range(n_heads // 2):
        pb, r = divmod(s, per_kv)
        o_lo = normalised((2 * pb) * per_kv + r)
        o_hi = normalised((2 * pb + 1) * per_kv + r)
        y_ref[0, :, s * LANES:(s + 1) * LANES] = jnp.where(low, o_lo, o_hi).astype(y_ref.dtype)


def _dsa_prompt(q, qi, sa, kk, k, v, *, n_heads, kv_heads, idx_heads, wi_lane, att_scale, idx_w_scale):
    b, l, dq = q.shape
    tq, tk = 128, 256
    tk = min(tk, l)
    tq = min(tq, l)
    assert l % tq == 0 and l % tk == 0 and kv_heads % 2 == 0 and idx_heads % 2 == 0
    topk = min(TOPK_MAX, l // 4)
    kvw = k.shape[-1]
    hd = LANES // 2
    ones = jnp.ones((b, l, hd), BF16)
    vx = jnp.concatenate([a for g in range(kv_heads)
                          for a in ((v[:, :, g * hd:(g + 1) * hd], ones) if g % 2 == 0 else
                                    (ones, v[:, :, g * hd:(g + 1) * hd]))], axis=-1)
    return pl.pallas_call(
        functools.partial(_dsa_prompt_kernel, tq=tq, tk=tk, n_heads=n_heads, kv_heads=kv_heads, idx_heads=idx_heads,
                          topk=topk, att_scale=att_scale, idx_w_scale=idx_w_scale, wi_lane=wi_lane, seq_len=l),
        grid=(b, l // tq),
        in_specs=[
            pl.BlockSpec((1, tq, dq), lambda bi, i: (bi, i, 0)),
            pl.BlockSpec((1, tq, qi.shape[-1]), lambda bi, i: (bi, i, 0)),
            pl.BlockSpec((1, tq, LANES), lambda bi, i: (bi, i, 0)),
            pl.BlockSpec((1, l, LANES), lambda bi, i: (bi, 0, 0)),
            pl.BlockSpec((1, l, kvw), lambda bi, i: (bi, 0, 0)),
            pl.BlockSpec((1, l, vx.shape[-1]), lambda bi, i: (bi, 0, 0)),
        ],
        out_specs=pl.BlockSpec((1, tq, dq), lambda bi, i: (bi, i, 0)),
        out_shape=jax.ShapeDtypeStruct((b, l, dq), BF16),
        scratch_shapes=[
            pltpu.VMEM((l // tk, tq, tk), I32),
            pltpu.VMEM((n_heads * tq, kvw), BF16),
            pltpu.VMEM((idx_heads * tq, LANES), BF16),
            pltpu.VMEM((idx_heads, tq, LANES), F32),
            pltpu.VMEM((n_heads * tq, LANES), F32),
            pltpu.VMEM((n_heads * tq, LANES), F32),
        ],
        compiler_params=_params("parallel", "arbitrary"),
        name="dsa_prompt",
    )(q, qi, sa, kk, k, vx)


def _dsa_sample_scores_kernel(pt_ref, qi_ref, sa_ref, kn_ref, *rest, pg, t_new, idx_heads, page,
                              idx_w_scale, wi_lane, n_keys):
    page_refs = rest[:pg]
    keys_ref, qi_s, w_s, kn_s = rest[pg:]
    i = pl.program_id(1)
    hd = LANES // 2
    n_steps = pl.num_programs(1)

    @pl.when(i == 0)
    def _():
        qif = qi_ref[0]
        sa = sa_ref[0]
        for h in range(idx_heads):
            qi_s[h * t_new:(h + 1) * t_new, :] = qif[:, h * LANES:h * LANES + hd].astype(F32)
            w_s[h * t_new:(h + 1) * t_new, :] = jnp.broadcast_to(
                sa[:, wi_lane + h:wi_lane + h + 1] * idx_w_scale, (t_new, LANES))
        kn_s[...] = jnp.zeros(kn_s.shape, F32)
        kn_s[0:t_new, :] = kn_ref[0][:, 0:hd].astype(F32)

    def scores(dots):
        ww = jnp.maximum(dots, 0.0) * w_s[...]
        sc = ww[0:t_new]
        for h in range(1, idx_heads):
            sc = sc + ww[h * t_new:(h + 1) * t_new]
        return sc

    always = jnp.full((t_new, page), True)
    kt = jnp.concatenate([page_refs[r][0, 0].astype(BF16) for r in range(pg)], axis=1)
    dots = _dot(qi_s[...].astype(BF16), kt)
    for r in range(pg):
        off = pl.multiple_of((i * pg + r) * page, page)
        keys_ref[0, :, pl.ds(off, page)] = _score_key(scores(dots[:, r * page:(r + 1) * page]), always)

    @pl.when(i == n_steps - 1)
    def _():
        ti = lax.broadcasted_iota(I32, (t_new, page), 0)
        ki = lax.broadcasted_iota(I32, (t_new, page), 1)
        past = n_keys - page
        dots_new = _dot_nt(qi_s[...].astype(BF16), kn_s[...].astype(BF16))
        keys_ref[0, :, past:n_keys] = _score_key(scores(dots_new), ki <= ti)


def _dsa_sample_threshold_kernel(keys_ref, thr_ref, cut_ref, *, rb, topk, n_keys):
    n_chunks = n_keys // LANES
    liota = lax.broadcasted_iota(I32, (rb, LANES), 1)
    for b in range(keys_ref.shape[0] // rb):
        rows = slice(b * rb, (b + 1) * rb)

        def count(pred):
            def body(c, cnt):
                off = pl.multiple_of(c * LANES, LANES)
                return cnt + jnp.where(pred(keys_ref[rows, pl.ds(off, LANES)], off), 1, 0)
            cnt = lax.fori_loop(0, n_chunks, body, jnp.zeros((rb, LANES), I32))
            return jnp.sum(cnt, axis=1, keepdims=True)

        def count_ge(t):
            tb = jnp.broadcast_to(t, (rb, LANES))
            return count(lambda key, off: key >= tb)

        thr = _kth_largest_key(count_ge, rb, topk)
        tie = (count_ge(thr) > topk) & (thr > INT_MIN)
        thr_ref[rows, :] = jnp.broadcast_to(jnp.maximum(thr, jnp.int32(INT_MIN + 1)), (rb, LANES))
        cut_ref[rows, :] = jnp.full((rb, LANES), n_keys, I32)

        @pl.when(jnp.max(jnp.where(tie, 1, 0)) > 0)
        def _():
            need = topk - count(lambda key, off: key > thr)
            cut = _tie_cutoff(lambda p: count(lambda key, off: (key == thr) & (off + liota < p)), need, rb,
                              int(n_keys).bit_length())
            cut_ref[rows, :] = jnp.broadcast_to(jnp.where(tie, cut, n_keys), (rb, LANES))


def _dsa_sample_attend_kernel(pt_ref, q_ref, keys_ref, thr_ref, cut_ref, kn_ref, vn_ref, *rest, pg, pc, t_new,
                              n_heads, kv_heads, page, att_scale, n_keys):
    k_refs = rest[:pg]
    v_refs = rest[pg:2 * pg]
    y_ref, qpad_s, m_s, l_s, acc_s, kn_s, vn_s = rest[2 * pg:]
    i = pl.program_id(1)
    hd = LANES // 2
    per_kv = n_heads // kv_heads
    n_steps = pl.num_programs(1)
    lane = lax.broadcasted_iota(I32, (t_new, LANES), 1)
    low = lane < hd

    @pl.when(i == 0)
    def _():
        qf = q_ref[0].astype(F32) * att_scale
        zeros = jnp.zeros((t_new, LANES), F32)
        for h in range(n_heads):
            g, r = divmod(h, per_kv)
            slot = (g // 2) * per_kv + r
            sl = qf[:, slot * LANES:(slot + 1) * LANES]
            half = jnp.where(low, sl, 0.0) if g % 2 == 0 else jnp.where(low, 0.0, sl)
            row = [zeros] * (kv_heads // 2)
            row[g // 2] = half
            qpad_s[h * t_new:(h + 1) * t_new, :] = jnp.concatenate(row, axis=1)
        m_s[...] = jnp.full(m_s.shape, NEG, F32)
        l_s[...] = jnp.zeros(l_s.shape, F32)
        acc_s[...] = jnp.zeros(acc_s.shape, F32)
        kn_s[...] = jnp.zeros(kn_s.shape, F32)
        vn_s[...] = jnp.zeros(vn_s.shape, F32)
        kn_s[0:t_new, :] = kn_ref[0].astype(F32)
        vn_s[0:t_new, :] = vn_ref[0].astype(F32)

    thr = thr_ref[0]
    cut = cut_ref[0]

    def attend(s, key, pos0, pv):
        n = key.shape[1]
        wide = lambda a: jnp.concatenate([a] * (n // LANES), axis=1)
        pos = pos0 + lax.broadcasted_iota(I32, (t_new, n), 1)
        sel = (key > wide(thr)) | ((key == wide(thr)) & (pos <= wide(cut)))
        bias = jnp.where(sel, 0.0, NEG)
        s = s + jnp.concatenate([bias] * n_heads, axis=0)
        m_old = m_s[...]
        m_new = jnp.maximum(m_old, jnp.max(s, axis=1, keepdims=True))
        alpha = jnp.exp(m_old - m_new)
        p = jnp.exp(s - m_new)
        l_s[...] = alpha * l_s[...] + jnp.sum(p, axis=1, keepdims=True)
        m_s[...] = m_new
        acc_s[...] = acc_s[...] * alpha + pv(p.astype(BF16))

    qb = qpad_s[...].astype(BF16)
    kvw = qb.shape[1]
    for c in range(pg // pc):
        off = pl.multiple_of((i * pg + c * pc) * page, page)
        kt = jnp.concatenate([k_refs[c * pc + r][0, 0].reshape(kvw, page).astype(BF16) for r in range(pc)], axis=1)
        vt = jnp.concatenate([v_refs[c * pc + r][0, 0].reshape(kvw, page).astype(BF16) for r in range(pc)], axis=1)
        attend(_dot(qb, kt), keys_ref[0, :, pl.ds(off, pc * page)], off, lambda p: _dot_nt(p, vt))

    @pl.when(i == n_steps - 1)
    def _():
        attend(_dot_nt(qb, kn_s[...].astype(BF16)), keys_ref[0, :, n_keys - page:n_keys], n_keys - page,
               lambda p: _dot(p, vn_s[...].astype(BF16)))
        o = acc_s[...] / l_s[...]
        for s in range(n_heads // 2):
            pb, r = divmod(s, per_kv)
            h_lo = (2 * pb) * per_kv + r
            h_hi = (2 * pb + 1) * per_kv + r
            o_lo = o[h_lo * t_new:(h_lo + 1) * t_new, pb * LANES:(pb + 1) * LANES]
            o_hi = o[h_hi * t_new:(h_hi + 1) * t_new, pb * LANES:(pb + 1) * LANES]
            y_ref[0, :, s * LANES:(s + 1) * LANES] = jnp.where(low, o_lo, o_hi).astype(y_ref.dtype)


def _pages_per_step(n_pages):
    for pg in (16, 8, 4, 2, 1):
        if n_pages % pg == 0:
            return pg


def _dsa_sample(q, qiw, sa, kk_new, k_new, v_new, ck, cv, ci, page_table, layer, *, n_heads, kv_heads, idx_heads,
                wi_lane, att_scale, idx_w_scale):
    db, t_new, dq = q.shape
    _, n_pool, kv_heads_, hd, page = ck.shape
    kvw = kv_heads_ * hd
    n_pages = page_table.shape[1]
    assert page == LANES and t_new % SUBLANES == 0 and t_new <= page and kv_heads_ == kv_heads and hd == LANES // 2
    pg = _pages_per_step(n_pages)
    pc = pg
    n_steps = n_pages // pg
    n_keys = (n_pages + 1) * page
    topk = min(TOPK_MAX, (n_pages * page + t_new) // 4)
    idx_spec = lambda r: pl.BlockSpec((1, 1, hd, page), lambda b, i, pt: (layer, pt[b, i * pg + r], 0, 0))
    kv_spec = lambda r: pl.BlockSpec((1, 1, kv_heads, hd, page), lambda b, i, pt: (layer, pt[b, i * pg + r], 0, 0, 0))
    row_spec = lambda w: pl.BlockSpec((1, t_new, w), lambda b, i, pt: (b, 0, 0))

    keys = pl.pallas_call(
        functools.partial(_dsa_sample_scores_kernel, pg=pg, t_new=t_new, idx_heads=idx_heads, page=page,
                          idx_w_scale=idx_w_scale, wi_lane=wi_lane, n_keys=n_keys),
        grid_spec=pltpu.PrefetchScalarGridSpec(
            num_scalar_prefetch=1,
            grid=(db, n_steps),
            in_specs=[row_spec(qiw.shape[-1]), row_spec(LANES), row_spec(LANES)] + [idx_spec(r) for r in range(pg)],
            out_specs=row_spec(n_keys),
            scratch_shapes=[pltpu.VMEM((idx_heads * t_new, hd), F32), pltpu.VMEM((idx_heads * t_new, LANES), F32),
                            pltpu.VMEM((page, hd), F32)],
        ),
        out_shape=jax.ShapeDtypeStruct((db, t_new, n_keys), I32),
        compiler_params=_params("parallel", "arbitrary"),
        name="dsa_sample_scores",
    )(page_table, qiw, sa, kk_new, *([ci] * pg))

    n_rows = db * t_new
    rb = math.gcd(n_rows, LANES)
    thr, cut = pl.pallas_call(
        functools.partial(_dsa_sample_threshold_kernel, rb=rb, topk=topk, n_keys=n_keys),
        out_shape=[jax.ShapeDtypeStruct((n_rows, LANES), I32)] * 2,
        compiler_params=pltpu.CompilerParams(vmem_limit_bytes=VMEM_LIMIT_BYTES),
        name="dsa_sample_threshold",
    )(keys.reshape(n_rows, n_keys))
    thr = thr.reshape(db, t_new, LANES)
    cut = cut.reshape(db, t_new, LANES)

    return pl.pallas_call(
        functools.partial(_dsa_sample_attend_kernel, pg=pg, pc=pc, t_new=t_new, n_heads=n_heads, kv_heads=kv_heads,
                          page=page, att_scale=att_scale, n_keys=n_keys),
        grid_spec=pltpu.PrefetchScalarGridSpec(
            num_scalar_prefetch=1,
            grid=(db, n_steps),
            in_specs=[row_spec(dq), row_spec(n_keys), row_spec(LANES), row_spec(LANES), row_spec(kvw), row_spec(kvw)]
                     + [kv_spec(r) for r in range(pg)] * 2,
            out_specs=row_spec(dq),
            scratch_shapes=[pltpu.VMEM((n_heads * t_new, kvw), F32), pltpu.VMEM((n_heads * t_new, 1), F32),
                            pltpu.VMEM((n_heads * t_new, 1), F32), pltpu.VMEM((n_heads * t_new, kvw), F32),
                            pltpu.VMEM((page, kvw), F32), pltpu.VMEM((page, kvw), F32)],
        ),
        out_shape=jax.ShapeDtypeStruct((db, t_new, dq), BF16),
        compiler_params=_params("parallel", "arbitrary"),
        name="dsa_sample_attend",
    )(page_table, q, keys, thr, cut, k_new, v_new, *([ck] * pg), *([cv] * pg))


def _merge_kernel(x_ref, ys_ref, ya_ref, gs_ref, ga_ref, ps_ref, pa_ref, wo_ref, o_ref):
    merged = (_sigmoid(gs_ref[...]) * _dot(ys_ref[...], ps_ref[...])
              + _sigmoid(ga_ref[...]) * _dot(ya_ref[...], pa_ref[...]))
    o_ref[...] = x_ref[...] + _dot(merged.astype(BF16), wo_ref[...])


def _merge(x, y_ssd, y_attn, g_s, g_a, p_ssd, p_attn, w_out):
    t, d = x.shape
    tm = min(512, t)
    assert t % tm == 0
    rows = lambda w: pl.BlockSpec((tm, w), lambda i: (i, 0))
    full = lambda a: pl.BlockSpec(a.shape, lambda i: (0, 0))
    return pl.pallas_call(
        _merge_kernel,
        grid=(t // tm,),
        in_specs=[rows(d), rows(y_ssd.shape[1]), rows(y_attn.shape[1]), rows(d), rows(d),
                  full(p_ssd), full(p_attn), full(w_out)],
        out_specs=rows(d),
        out_shape=jax.ShapeDtypeStruct((t, d), F32),
        compiler_params=_params("parallel"),
        name="merge",
    )(x, y_ssd, y_attn, g_s, g_a, p_ssd, p_attn, w_out)


def _pair_slot_perm(n_heads, kv_heads, hd):
    per_kv = n_heads // kv_heads
    cols = []
    for s in range(n_heads // 2):
        pb, r = divmod(s, per_kv)
        for h in ((2 * pb) * per_kv + r, (2 * pb + 1) * per_kv + r):
            cols.extend(range(h * hd, (h + 1) * hd))
    return np.asarray(cols, np.int32)


def kernel(x_prompt, x_sample, cache_k, cache_v, cache_idx_k, state_ssm, state_conv, page_table, ffn1_norm, ffn1_w1, ffn1_w2, mix_norm, w_in, conv_w, conv_b, dt_bias, a_log, d_skip, ssd_norm, w_branch_ssd, w_branch_attn, w_out, ffn2_norm, ffn2_w1, ffn2_w2, final_norm):
    bp, seq, d_model = x_prompt.shape
    db, dseq, _ = x_sample.shape
    depth, n_pool, page, kv_heads, head_dim = cache_k.shape
    idx_dim = cache_idx_k.shape[-1]
    ssd_heads, ssd_hd, d_state = state_ssm.shape[2:]
    conv_dim = state_conv.shape[-1]
    d_inner = ssd_norm.shape[-1]
    n_groups = (conv_dim - d_inner) // (2 * d_state)
    attn_dim = w_branch_attn.shape[1]
    n_heads = attn_dim // head_dim
    kvw = kv_heads * head_dim
    d_proj = w_in.shape[-1]
    idx_heads = (d_proj - (d_inner + conv_dim + ssd_heads + attn_dim + 2 * kvw + idx_dim + 2 * d_model)) // (idx_dim + 1)
    assert head_dim == LANES // 2 and idx_dim == LANES // 2 and ssd_heads + idx_heads <= LANES
    att_scale = head_dim ** -0.5
    idx_w_scale = (idx_heads ** -0.5) * (idx_dim ** -0.5)
    sizes = (d_inner, conv_dim, ssd_heads, attn_dim, kvw, kvw, idx_heads * idx_dim, idx_dim, idx_heads, d_model, d_model)
    assert sum(sizes) == d_proj
    offs = np.concatenate([[0], np.cumsum(sizes)])
    perm = _pair_slot_perm(n_heads, kv_heads, head_dim)
    wi_lane = ssd_heads

    tp, ts = bp * seq, db * dseq
    yp = x_prompt.reshape(tp, d_model)
    ys = x_sample.reshape(ts, d_model)
    ck = jnp.transpose(cache_k, (0, 1, 3, 4, 2))
    cv = jnp.transpose(cache_v, (0, 1, 3, 4, 2))
    ci = jnp.transpose(cache_idx_k, (0, 1, 3, 2))
    dsa_kw = dict(n_heads=n_heads, kv_heads=kv_heads, idx_heads=idx_heads, wi_lane=wi_lane, att_scale=att_scale,
                  idx_w_scale=idx_w_scale)
    outs = {n: [] for n in ("kp", "vp", "ip", "sp", "cp", "ks", "vs", "is", "ss", "cs")}
    hist = conv_w.shape[1] - 1

    for l in range(depth):
        wl = w_in[l]
        col = lambda i: wl[:, offs[i]:offs[i + 1]]
        w_z, w_xbc, w_dt, w_q, w_k, w_v, w_qi, w_ki, w_wi, w_gs, w_ga = [col(i) for i in range(11)]
        w_sa = jnp.concatenate([w_dt, w_wi, jnp.zeros((d_model, LANES - ssd_heads - idx_heads), F32)], axis=1)
        w_qi_wide = jnp.pad(w_qi.reshape(d_model, idx_heads, idx_dim), ((0, 0), (0, 0), (0, LANES - idx_dim)))
        bf = lambda w: w.astype(BF16)
        wa = [bf(w_z), bf(w_xbc)]
        wb_common = [bf(w_q[:, perm]), bf(w_k), bf(w_v), bf(jnp.concatenate([w_ki, w_ki], axis=1)), bf(w_sa),
                     bf(w_gs), bf(w_ga)]
        dt_common = [(BF16,), (F32, BF16), (F32, BF16), (F32, BF16), (F32,), (F32,), (F32,)]
        f1w1, f1w2, f2w1, f2w2 = bf(ffn1_w1[l]), bf(ffn1_w2[l]), bf(ffn2_w1[l]), bf(ffn2_w2[l])
        p_ssd, p_attn, wo = bf(w_branch_ssd[l]), bf(w_branch_attn[l][perm, :]), bf(w_out[l])
        last = l == depth - 1

        def mixer(y, b, s, qi_weight, attend, buf, h0):
            z, xbc = _norm_linear(y, mix_norm[l], wa, [(F32,), (F32,)], 256)
            q, k, kb, v, vb, kk, kkb, sa, g_s, g_a, qi = _norm_linear(
                y, mix_norm[l], wb_common + [qi_weight], dt_common + [(BF16,)], 256)
            r3 = lambda a: a.reshape(b, s, a.shape[-1])
            y_ssd, h_fin = _ssd(r3(z), r3(xbc), r3(sa), buf, h0, conv_w[l], conv_b[l], dt_bias[l], a_log[l],
                                d_skip[l], ssd_norm[l], n_groups=n_groups, d_state=d_state)
            y_attn = attend(r3(q), r3(qi), r3(sa), r3(kkb), r3(kb), r3(vb))
            y = _merge(y, y_ssd.reshape(b * s, d_inner), y_attn.reshape(b * s, attn_dim), g_s, g_a, p_ssd, p_attn, wo)
            new_buf = r3(xbc)[:, s - hist:, :]
            return (y, new_buf, h_fin, k.reshape(b, s, kv_heads, head_dim), v.reshape(b, s, kv_heads, head_dim),
                    r3(kk)[:, :, :idx_dim])

        yp = _ffn(yp, ffn1_norm[l], f1w1, f1w2)
        ys = _ffn(ys, ffn1_norm[l], f1w1, f1w2)

        yp, cbp, hfp, kp, vp, kip = mixer(
            yp, bp, seq, bf(w_qi), functools.partial(_dsa_prompt, **dsa_kw),
            jnp.zeros((bp, hist, conv_dim), F32), jnp.zeros((bp, ssd_heads, ssd_hd, d_state), F32))
        att_s = functools.partial(_dsa_sample, ck=ck, cv=cv, ci=ci, page_table=page_table, layer=l, **dsa_kw)
        ys, cbs, hfs, kss, vss, kis = mixer(
            ys, db, dseq, bf(w_qi_wide.reshape(d_model, idx_heads * LANES)), att_s, state_conv[l], state_ssm[l])

        pg_ = final_norm if last else None
        yp = _ffn(yp, ffn2_norm[l], f2w1, f2w2, pg_)
        ys = _ffn(ys, ffn2_norm[l], f2w1, f2w2, pg_)
        for n, a in zip(("kp", "vp", "ip", "sp", "cp", "ks", "vs", "is", "ss", "cs"),
                        (kp, vp, kip, hfp, cbp, kss, vss, kis, hfs, cbs)):
            outs[n].append(a)

    st = lambda n: jnp.stack(outs[n])
    return (yp.reshape(bp, seq, d_model), ys.reshape(db, dseq, d_model),
            st("kp"), st("vp"), st("ip"), st("sp"), st("cp"),
            st("ks"), st("vs"), st("is"), st("ss"), st("cs"))
```

```python
import functools
import math

import jax
import jax.numpy as jnp
import numpy as np
from jax import lax
from jax.experimental import pallas as pl
from jax.experimental.pallas import tpu as pltpu

F32 = jnp.float32
BF16 = jnp.bfloat16
I32 = jnp.int32

EPS = 1e-6
SSD_CHUNK = 128
TOPK_MAX = 256
LANES = 128
SUBLANES = 8
VMEM_LIMIT_BYTES = 56 * 1024 * 1024
NEG = -1e30
INT_MIN = -(2 ** 31)


def _params(*sem):
    return pltpu.CompilerParams(dimension_semantics=sem, vmem_limit_bytes=VMEM_LIMIT_BYTES)


def _sigmoid(x):
    return 1.0 / (1.0 + jnp.exp(-x))


def _rms(x, g):
    return x * lax.rsqrt(jnp.mean(x * x, axis=-1, keepdims=True) + EPS) * g


def _dot(a, b):
    return jnp.dot(a, b, preferred_element_type=F32)


def _dot_nt(a, b):
    return lax.dot_general(a, b, (((1,), (1,)), ((), ())), preferred_element_type=F32)


def _split2(x):
    hi = x.astype(BF16)
    lo = (x - hi.astype(F32)).astype(BF16)
    return hi, lo


def _split3(x):
    hi = x.astype(BF16)
    r = x - hi.astype(F32)
    mid = r.astype(BF16)
    lo = (r - mid.astype(F32)).astype(BF16)
    return hi, mid, lo


def _ffn_kernel(*refs, post_norm):
    if post_norm:
        x_ref, g_ref, wa_ref, wb_ref, w2_ref, pg_ref, o_ref, h_s, acc_s = refs
    else:
        x_ref, g_ref, wa_ref, wb_ref, w2_ref, o_ref, h_s, acc_s = refs
    f = pl.program_id(1)

    @pl.when(f == 0)
    def _():
        h_s[...] = _rms(x_ref[...], g_ref[...]).astype(BF16)
        acc_s[...] = jnp.zeros_like(acc_s)

    h = h_s[...]
    a = _dot(h, wa_ref[...])
    b = _dot(h, wb_ref[...])
    u = (a * _sigmoid(a) * b).astype(BF16)
    acc_s[...] += _dot(u, w2_ref[...])

    @pl.when(f == pl.num_programs(1) - 1)
    def _():
        y = x_ref[...] + 0.5 * acc_s[...]
        if post_norm:
            y = _rms(y, pg_ref[...])
        o_ref[...] = y


def _ff_tile(d_ff):
    best = None
    for t in range(LANES, d_ff + 1, LANES):
        if d_ff % t == 0 and t <= 1536:
            best = t
    assert best is not None, d_ff
    return best


def _ffn(x, g, w1, w2, post_gain=None):
    t, d = x.shape
    d_ff = w2.shape[0]
    tm = min(512, t)
    tf = _ff_tile(d_ff)
    nf = d_ff // tf
    assert t % tm == 0
    post_norm = post_gain is not None
    in_specs = [
        pl.BlockSpec((tm, d), lambda i, f: (i, 0)),
        pl.BlockSpec((1, d), lambda i, f: (0, 0)),
        pl.BlockSpec((d, tf), lambda i, f: (0, f)),
        pl.BlockSpec((d, tf), lambda i, f: (0, f + nf)),
        pl.BlockSpec((tf, d), lambda i, f: (f, 0)),
    ]
    args = [x, g.reshape(1, d), w1, w1, w2]
    if post_norm:
        in_specs.append(pl.BlockSpec((1, d), lambda i, f: (0, 0)))
        args.append(post_gain.reshape(1, d))
    return pl.pallas_call(
        functools.partial(_ffn_kernel, post_norm=post_norm),
        grid=(t // tm, nf),
        in_specs=in_specs,
        out_specs=pl.BlockSpec((tm, d), lambda i, f: (i, 0)),
        out_shape=jax.ShapeDtypeStruct((t, d), F32),
        scratch_shapes=[pltpu.VMEM((tm, d), BF16), pltpu.VMEM((tm, d), F32)],
        compiler_params=_params("parallel", "arbitrary"),
        name="ffn",
    )(*args)


def _norm_linear_kernel(*refs, out_dtypes):
    n_w = len(out_dtypes)
    x_ref, g_ref = refs[:2]
    w_refs = refs[2:2 + n_w]
    o_refs = list(refs[2 + n_w:])
    h = _rms(x_ref[...], g_ref[...]).astype(BF16)
    for w_ref, dts in zip(w_refs, out_dtypes):
        r = _dot(h, w_ref[...])
        for dt in dts:
            o_refs.pop(0)[...] = r.astype(dt)


def _norm_linear(x, g, weights, out_dtypes, tm):
    t, d = x.shape
    tm = min(tm, t)
    assert t % tm == 0
    in_specs = [pl.BlockSpec((tm, d), lambda i: (i, 0)), pl.BlockSpec((1, d), lambda i: (0, 0))]
    out_specs, out_shape = [], []
    for w, dts in zip(weights, out_dtypes):
        n = w.shape[1]
        in_specs.append(pl.BlockSpec((d, n), lambda i: (0, 0)))
        for dt in dts:
            out_specs.append(pl.BlockSpec((tm, n), lambda i: (i, 0)))
            out_shape.append(jax.ShapeDtypeStruct((t, n), dt))
    return pl.pallas_call(
        functools.partial(_norm_linear_kernel, out_dtypes=tuple(tuple(d_) for d_ in out_dtypes)),
        grid=(t // tm,),
        in_specs=in_specs,
        out_specs=out_specs,
        out_shape=out_shape,
        compiler_params=_params("parallel"),
        name="norm_linear",
    )(x, g.reshape(1, d), *weights)


def _ssd_kernel(z_ref, xbc_ref, dt_ref, buf_ref, h0_ref, cw_ref, cb_ref, dtb_ref, alog_ref, dsk_ref, ng_ref, e_ref,
                y_ref, hfin_ref, xp_s, ht_s, *, qin, d_inner, n_groups, d_state, conv_w):
    q = SSD_CHUNK
    c = pl.program_id(1)
    hp_blocks = d_inner // LANES
    gw = d_inner // n_groups
    assert d_state == LANES and gw % LANES == 0
    pad = SUBLANES
    hist = conv_w - 1

    @pl.when(c == 0)
    def _():
        xp_s[0:pad, :] = buf_ref[0]
        if qin < q:
            xp_s[pad + qin:pad + q, :] = jnp.zeros((q - qin, xp_s.shape[1]), F32)
        for i in range(hp_blocks):
            ht_s[:, i * LANES:(i + 1) * LANES] = h0_ref[0, i * LANES:(i + 1) * LANES, :].T

    xp_s[pad:pad + qin, :] = xbc_ref[0]
    acc = cb_ref[...]
    for i in range(conv_w):
        off = pad - hist + i
        acc = acc + xp_s[off:off + q, :] * cw_ref[i:i + 1, :]
    xc = acc * _sigmoid(acc)
    tail = xp_s[pad + qin - hist:pad + qin, :]
    xp_s[pad - hist:pad, :] = tail

    xs = xc[:, :d_inner]
    bm = xc[:, d_inner:d_inner + n_groups * d_state]
    cm = xc[:, d_inner + n_groups * d_state:]

    dt_raw = dt_ref[0] + dtb_ref[...]
    dt = jnp.maximum(dt_raw, 0.0) + jnp.log1p(jnp.exp(-jnp.abs(dt_raw)))
    if qin < q:
        dt = jnp.concatenate([dt, jnp.zeros((q - qin, LANES), F32)], axis=0)
    la = dt * (-jnp.exp(alog_ref[...]))

    ri = lax.broadcasted_iota(I32, (q, q), 0)
    ci = lax.broadcasted_iota(I32, (q, q), 1)
    causal = ri >= ci
    tril = jnp.where(causal, 1.0, 0.0).astype(BF16)
    eye = jnp.where(ri == ci, 1.0, 0.0).astype(BF16)
    a_cs = sum(_dot(tril, p) for p in _split3(la))
    a_cs_t = sum(_dot_nt(eye, p) for p in _split3(a_cs))
    dec = jnp.exp(a_cs[q - 1:q, :] - a_cs)
    eac = jnp.exp(a_cs)
    stacked = jnp.concatenate([dt, dec, eac], axis=0)
    expd = sum(_dot(p, e_ref[...]) for p in _split2(stacked))
    dt_e, dec_e, eac_e = expd[0:q], expd[q:2 * q], expd[2 * q:3 * q]

    x = xs * dt_e
    xb = x.astype(BF16)
    xd = (x * dec_e).astype(BF16)
    lane = lax.broadcasted_iota(I32, (q, LANES), 1)
    hd = LANES // 2
    heads_per_group = gw // hd

    y_parts = []
    for g in range(n_groups):
        gs = slice(g * gw, (g + 1) * gw)
        cg = cm[:, g * d_state:(g + 1) * d_state].astype(BF16)
        bg = bm[:, g * d_state:(g + 1) * d_state]
        cb = _dot_nt(cg, bg.astype(BF16))
        y_off = _dot(cg, ht_s[:, gs].astype(BF16))
        pair_parts = []
        for p in range(heads_per_group // 2):
            xp = xb[:, g * gw + p * LANES:g * gw + (p + 1) * LANES]
            res = []
            for k in range(2):
                j = g * heads_per_group + 2 * p + k
                diff = a_cs[:, j:j + 1] - a_cs_t[j:j + 1, :]
                lm = jnp.exp(jnp.where(causal, diff, NEG))
                res.append(_dot((cb * lm).astype(BF16), xp))
            pair_parts.append(jnp.where(lane < hd, res[0], res[1]))
        y_diag = jnp.concatenate(pair_parts, axis=1)
        y_parts.append(y_diag + y_off * eac_e[:, gs])
        st = _dot(bg.T.astype(BF16), xd[:, gs])
        ht_s[:, gs] = ht_s[:, gs] * eac_e[q - 1:q, gs] + st

    zz = z_ref[0]
    outs = []
    for g in range(n_groups):
        gs = slice(g * gw, (g + 1) * gw)
        yv = (y_parts[g][:qin] + dsk_ref[:, gs] * xs[:qin, gs]) * (zz[:, gs] * _sigmoid(zz[:, gs]))
        ms = jnp.mean(yv * yv, axis=-1, keepdims=True)
        outs.append(yv * lax.rsqrt(ms + EPS) * ng_ref[:, gs])
    y_ref[0] = jnp.concatenate(outs, axis=1).astype(y_ref.dtype)

    @pl.when(c == pl.num_programs(1) - 1)
    def _():
        for i in range(hp_blocks):
            hfin_ref[0, i * LANES:(i + 1) * LANES, :] = ht_s[:, i * LANES:(i + 1) * LANES].T


def _ssd(z, xbc, dtp, buf, h0, conv_w, conv_b, dt_bias, a_log, d_skip, norm_g, *, n_groups, d_state):
    b, l, d_inner = z.shape
    conv_dim = xbc.shape[-1]
    n_heads, p_dim, n_state = h0.shape[1:]
    width = conv_w.shape[0]
    hist = width - 1
    assert n_heads <= LANES and p_dim == LANES // 2 and n_state == d_state and hist <= SUBLANES
    qin = math.gcd(l, SSD_CHUNK)
    assert qin % SUBLANES == 0 and qin >= hist
    nc = l // qin
    assert nc == 1 or qin == SSD_CHUNK
    hp = n_heads * p_dim
    buf8 = jnp.pad(buf, ((0, 0), (SUBLANES - hist, 0), (0, 0)))
    cw8 = jnp.pad(conv_w, ((0, SUBLANES - width), (0, 0)))
    pad1 = lambda v: jnp.pad(v.reshape(1, -1), ((0, 0), (0, LANES - n_heads)))
    expand = (np.arange(LANES)[:, None] == (np.arange(d_inner)[None, :] // p_dim)).astype(np.float32)
    full = lambda shape: pl.BlockSpec(shape, lambda i, c: (0,) * len(shape))
    y, hfin = pl.pallas_call(
        functools.partial(_ssd_kernel, qin=qin, d_inner=d_inner, n_groups=n_groups, d_state=d_state, conv_w=width),
        grid=(b, nc),
        in_specs=[
            pl.BlockSpec((1, qin, d_inner), lambda i, c: (i, c, 0)),
            pl.BlockSpec((1, qin, conv_dim), lambda i, c: (i, c, 0)),
            pl.BlockSpec((1, qin, LANES), lambda i, c: (i, c, 0)),
            pl.BlockSpec((1, SUBLANES, conv_dim), lambda i, c: (i, 0, 0)),
            pl.BlockSpec((1, hp, n_state), lambda i, c: (i, 0, 0)),
            full((SUBLANES, conv_dim)), full((1, conv_dim)), full((1, LANES)), full((1, LANES)),
            full((1, d_inner)), full((1, d_inner)), full((LANES, d_inner)),
        ],
        out_specs=[
            pl.BlockSpec((1, qin, d_inner), lambda i, c: (i, c, 0)),
            pl.BlockSpec((1, hp, n_state), lambda i, c: (i, 0, 0)),
        ],
        out_shape=[jax.ShapeDtypeStruct((b, l, d_inner), BF16), jax.ShapeDtypeStruct((b, hp, n_state), F32)],
        scratch_shapes=[pltpu.VMEM((SUBLANES + SSD_CHUNK, conv_dim), F32), pltpu.VMEM((n_state, hp), F32)],
        compiler_params=_params("parallel", "arbitrary"),
        name="ssd",
    )(z, xbc, dtp, buf8, h0.reshape(b, hp, n_state), cw8, conv_b.reshape(1, -1), pad1(dt_bias), pad1(a_log),
      jnp.repeat(d_skip, p_dim).reshape(1, -1), norm_g.reshape(1, -1), jnp.asarray(expand, BF16))
    return y, hfin.reshape(b, n_heads, p_dim, n_state)


def _score_key(sc, valid):
    bits = lax.bitcast_convert_type(sc + 0.0, I32)
    key = jnp.where(bits < 0, bits ^ jnp.int32(0x7FFFFFFF), bits)
    return jnp.where(valid, key, jnp.int32(INT_MIN))


def _kth_largest_key(count_ge, rows, k):
    def body(b, t):
        cand = t + lax.shift_left(jnp.int32(1), jnp.int32(31) - b)
        return jnp.where(count_ge(cand) >= k, cand, t)
    return lax.fori_loop(0, 32, body, jnp.full((rows, 1), INT_MIN, I32))


def _tie_cutoff(count_tie_before, need, rows, n_bits):
    def body(b, p):
        bit = lax.shift_left(jnp.int32(1), jnp.int32(n_bits - 1) - b)
        p_c = p | bit
        return jnp.where(count_tie_before(p_c) <= need - 1, p_c, p)
    return lax.fori_loop(0, n_bits, body, jnp.zeros((rows, 1), I32))


def _dsa_prompt_kernel(q_ref, qi_ref, sa_ref, kk_ref, k_ref, vx_ref, y_ref,
                       keys_s, qpad_s, qipad_s, wb_s, m_s, acc_s,
                       *, tq, tk, n_heads, kv_heads, idx_heads, topk, att_scale, idx_w_scale, wi_lane, seq_len):
    i = pl.program_id(1)
    hd = LANES // 2
    per_kv = n_heads // kv_heads
    n_kt = (i * tq + tq - 1) // tk + 1
    lane = lax.broadcasted_iota(I32, (tq, LANES), 1)
    low = lane < hd

    qf = q_ref[0].astype(F32) * (att_scale * math.log2(math.e))
    zeros = jnp.zeros((tq, LANES), F32)
    for h in range(n_heads):
        g, r = divmod(h, per_kv)
        slot = (g // 2) * per_kv + r
        sl = qf[:, slot * LANES:(slot + 1) * LANES]
        half = jnp.where(low, sl, 0.0) if g % 2 == 0 else jnp.where(low, 0.0, sl)
        row = [zeros] * (kv_heads // 2)
        row[g // 2] = half
        qpad_s[h * tq:(h + 1) * tq, :] = jnp.concatenate(row, axis=1).astype(BF16)
    qif = qi_ref[0].astype(F32)
    sa = sa_ref[0]
    for h in range(idx_heads):
        sl = qif[:, (h // 2) * LANES:(h // 2 + 1) * LANES]
        half = jnp.where(low, sl, 0.0) if h % 2 == 0 else jnp.where(low, 0.0, sl)
        qipad_s[h * tq:(h + 1) * tq, :] = half.astype(BF16)
        wb_s[h] = jnp.broadcast_to(sa[:, wi_lane + h:wi_lane + h + 1] * idx_w_scale, (tq, LANES))

    qpos = i * tq + lax.broadcasted_iota(I32, (tq, tk), 0)
    kiota = lax.broadcasted_iota(I32, (tq, tk), 1)

    def score_tile(j, carry):
        kt = kk_ref[0, pl.ds(pl.multiple_of(j * tk, tk), tk), :]
        r = _dot_nt(qipad_s[...], kt)
        parts = []
        for c in range(tk // LANES):
            acc = jnp.zeros((tq, LANES), F32)
            for h in range(idx_heads):
                acc = acc + jnp.maximum(r[h * tq:(h + 1) * tq, c * LANES:(c + 1) * LANES], 0.0) * wb_s[h]
            parts.append(acc)
        sc = jnp.concatenate(parts, axis=1)
        keys_s[j] = _score_key(sc, j * tk + kiota <= qpos)
        return carry
    lax.fori_loop(0, n_kt, score_tile, 0)

    n_chunks = tk // LANES

    def count(pred):
        def body(j, cnt):
            for c in range(n_chunks):
                cnt = cnt + jnp.where(pred(keys_s[j, :, c * LANES:(c + 1) * LANES], j * tk + c * LANES), 1, 0)
            return cnt
        cnt = lax.fori_loop(0, n_kt, body, jnp.zeros((tq, LANES), I32))
        return jnp.sum(cnt, axis=1, keepdims=True)

    def count_ge(t):
        tb = jnp.broadcast_to(t, (tq, LANES))
        return count(lambda key, base: key >= tb)

    thr = _kth_largest_key(count_ge, tq, topk)
    c_ge = count_ge(thr)
    tie = (c_ge > topk) & (thr > INT_MIN)
    liota = lax.broadcasted_iota(I32, (tq, LANES), 1)

    @pl.when(jnp.max(jnp.where(tie, 1, 0)) > 0)
    def _():
        need = topk - count(lambda key, base: key > thr)
        cut = _tie_cutoff(lambda p: count(lambda key, base: (key == thr) & (base + liota < p)),
                          need, tq, int(seq_len).bit_length())
        def drop(j, carry):
            key = keys_s[j]
            keys_s[j] = jnp.where(tie & (key == thr) & (j * tk + kiota > cut), jnp.int32(INT_MIN), key)
            return carry
        lax.fori_loop(0, n_kt, drop, 0)

    thr_b = jnp.broadcast_to(jnp.maximum(thr, jnp.int32(INT_MIN + 1)), (tq, LANES))

    m_s[...] = jnp.full(m_s.shape, NEG, F32)
    acc_s[...] = jnp.zeros(acc_s.shape, F32)

    def attend_tile(j, carry):
        ks = pl.ds(pl.multiple_of(j * tk, tk), tk)
        s_all = _dot_nt(qpad_s[...], k_ref[0, ks, :])
        bias = [jnp.where(keys_s[j, :, c * LANES:(c + 1) * LANES] >= thr_b, 0.0, NEG) for c in range(n_chunks)]
        for g in range(kv_heads):
            ps, alphas = [], []
            for r in range(per_kv):
                rows = slice((g * per_kv + r) * tq, (g * per_kv + r + 1) * tq)
                s = [s_all[rows, c * LANES:(c + 1) * LANES] + bias[c] for c in range(n_chunks)]
                smax = s[0]
                for c in range(1, n_chunks):
                    smax = jnp.maximum(smax, s[c])
                m_old = m_s[rows]
                m_new = jnp.maximum(m_old, jnp.max(smax, axis=1, keepdims=True))
                m_s[rows] = m_new
                alphas.append(jnp.exp2(m_old - m_new))
                ps.append(jnp.concatenate([jnp.exp2(s[c] - m_new) for c in range(n_chunks)], axis=1).astype(BF16))
            grows = slice(g * per_kv * tq, (g + 1) * per_kv * tq)
            pv = _dot(jnp.concatenate(ps, axis=0), vx_ref[0, ks, g * LANES:(g + 1) * LANES])
            acc_s[grows] = acc_s[grows] * jnp.concatenate(alphas, axis=0) + pv
        return carry
    lax.fori_loop(0, n_kt, attend_tile, 0)

    def normalised(h):
        a = acc_s[h * tq:(h + 1) * tq]
        return a / pltpu.roll(a, hd, 1)
    for s in range(n_heads // 2):
        pb, r = divmod(s, per_kv)
        o_lo = normalised((2 * pb) * per_kv + r)
        o_hi = normalised((2 * pb + 1) * per_kv + r)
        y_ref[0, :, s * LANES:(s + 1) * LANES] = jnp.where(low, o_lo, o_hi).astype(y_ref.dtype)


def _dsa_prompt(q, qi, sa, kk, k, v, *, n_heads, kv_heads, idx_heads, wi_lane, att_scale, idx_w_scale):
    b, l, dq = q.shape
    tq, tk = 128, 512
    tk = min(tk, l)
    tq = min(tq, l)
    assert l % tq == 0 and l % tk == 0 and kv_heads % 2 == 0 and idx_heads % 2 == 0
    topk = min(TOPK_MAX, l // 4)
    kvw = k.shape[-1]
    hd = LANES // 2
    ones = jnp.ones((b, l, hd), BF16)
    vx = jnp.concatenate([a for g in range(kv_heads)
                          for a in ((v[:, :, g * hd:(g + 1) * hd], ones) if g % 2 == 0 else
                                    (ones, v[:, :, g * hd:(g + 1) * hd]))], axis=-1)
    return pl.pallas_call(
        functools.partial(_dsa_prompt_kernel, tq=tq, tk=tk, n_heads=n_heads, kv_heads=kv_heads, idx_heads=idx_heads,
                          topk=topk, att_scale=att_scale, idx_w_scale=idx_w_scale, wi_lane=wi_lane, seq_len=l),
        grid=(b, l // tq),
        in_specs=[
            pl.BlockSpec((1, tq, dq), lambda bi, i: (bi, i, 0)),
            pl.BlockSpec((1, tq, qi.shape[-1]), lambda bi, i: (bi, i, 0)),
            pl.BlockSpec((1, tq, LANES), lambda bi, i: (bi, i, 0)),
            pl.BlockSpec((1, l, LANES), lambda bi, i: (bi, 0, 0)),
            pl.BlockSpec((1, l, kvw), lambda bi, i: (bi, 0, 0)),
            pl.BlockSpec((1, l, vx.shape[-1]), lambda bi, i: (bi, 0, 0)),
        ],
        out_specs=pl.BlockSpec((1, tq, dq), lambda bi, i: (bi, i, 0)),
        out_shape=jax.ShapeDtypeStruct((b, l, dq), BF16),
        scratch_shapes=[
            pltpu.VMEM((l // tk, tq, tk), I32),
            pltpu.VMEM((n_heads * tq, kvw), BF16),
            pltpu.VMEM((idx_heads * tq, LANES), BF16),
            pltpu.VMEM((idx_heads, tq, LANES), F32),
            pltpu.VMEM((n_heads * tq, LANES), F32),
            pltpu.VMEM((n_heads * tq, LANES), F32),
        ],
        compiler_params=_params("parallel", "arbitrary"),
        name="dsa_prompt",
    )(q, qi, sa, kk, k, vx)


def _dsa_sample_scores_kernel(pt_ref, qi_ref, sa_ref, kn_ref, *rest, pg, t_new, idx_heads, page,
                              idx_w_scale, wi_lane, n_keys):
    page_refs = rest[:pg]
    keys_ref, qi_s, w_s, kn_s = rest[pg:]
    i = pl.program_id(1)
    hd = LANES // 2
    n_steps = pl.num_programs(1)

    @pl.when(i == 0)
    def _():
        qif = qi_ref[0]
        sa = sa_ref[0]
        for h in range(idx_heads):
            qi_s[h * t_new:(h + 1) * t_new, :] = qif[:, h * LANES:h * LANES + hd].astype(F32)
            w_s[h * t_new:(h + 1) * t_new, :] = jnp.broadcast_to(
                sa[:, wi_lane + h:wi_lane + h + 1] * idx_w_scale, (t_new, LANES))
        kn_s[...] = jnp.zeros(kn_s.shape, F32)
        kn_s[0:t_new, :] = kn_ref[0][:, 0:hd].astype(F32)

    def scores(dots):
        ww = jnp.maximum(dots, 0.0) * w_s[...]
        sc = ww[0:t_new]
        for h in range(1, idx_heads):
            sc = sc + ww[h * t_new:(h + 1) * t_new]
        return sc

    always = jnp.full((t_new, page), True)
    kt = jnp.concatenate([page_refs[r][0, 0].astype(BF16) for r in range(pg)], axis=1)
    dots = _dot(qi_s[...].astype(BF16), kt)
    for r in range(pg):
        off = pl.multiple_of((i * pg + r) * page, page)
        keys_ref[0, :, pl.ds(off, page)] = _score_key(scores(dots[:, r * page:(r + 1) * page]), always)

    @pl.when(i == n_steps - 1)
    def _():
        ti = lax.broadcasted_iota(I32, (t_new, page), 0)
        ki = lax.broadcasted_iota(I32, (t_new, page), 1)
        past = n_keys - page
        dots_new = _dot_nt(qi_s[...].astype(BF16), kn_s[...].astype(BF16))
        keys_ref[0, :, past:n_keys] = _score_key(scores(dots_new), ki <= ti)


def _dsa_sample_threshold_kernel(keys_ref, thr_ref, cut_ref, *, rb, topk, n_keys):
    n_chunks = n_keys // LANES
    liota = lax.broadcasted_iota(I32, (rb, LANES), 1)
    for b in range(keys_ref.shape[0] // rb):
        rows = slice(b * rb, (b + 1) * rb)

        def count(pred):
            def body(c, cnt):
                off = pl.multiple_of(c * LANES, LANES)
                return cnt + jnp.where(pred(keys_ref[rows, pl.ds(off, LANES)], off), 1, 0)
            cnt = lax.fori_loop(0, n_chunks, body, jnp.zeros((rb, LANES), I32))
            return jnp.sum(cnt, axis=1, keepdims=True)

        def count_ge(t):
            tb = jnp.broadcast_to(t, (rb, LANES))
            return count(lambda key, off: key >= tb)

        thr = _kth_largest_key(count_ge, rb, topk)
        tie = (count_ge(thr) > topk) & (thr > INT_MIN)
        thr_ref[rows, :] = jnp.broadcast_to(jnp.maximum(thr, jnp.int32(INT_MIN + 1)), (rb, LANES))
        cut_ref[rows, :] = jnp.full((rb, LANES), n_keys, I32)

        @pl.when(jnp.max(jnp.where(tie, 1, 0)) > 0)
        def _():
            need = topk - count(lambda key, off: key > thr)
            cut = _tie_cutoff(lambda p: count(lambda key, off: (key == thr) & (off + liota < p)), need, rb,
                              int(n_keys).bit_length())
            cut_ref[rows, :] = jnp.broadcast_to(jnp.where(tie, cut, n_keys), (rb, LANES))


def _dsa_sample_attend_kernel(pt_ref, q_ref, keys_ref, thr_ref, cut_ref, kn_ref, vn_ref, *rest, pg, pc, t_new,
                              n_heads, kv_heads, page, att_scale, n_keys):
    k_refs = rest[:pg]
    v_refs = rest[pg:2 * pg]
    y_ref, qpad_s, m_s, l_s, acc_s, kn_s, vn_s = rest[2 * pg:]
    i = pl.program_id(1)
    hd = LANES // 2
    per_kv = n_heads // kv_heads
    n_steps = pl.num_programs(1)
    lane = lax.broadcasted_iota(I32, (t_new, LANES), 1)
    low = lane < hd

    @pl.when(i == 0)
    def _():
        qf = q_ref[0].astype(F32) * att_scale
        zeros = jnp.zeros((t_new, LANES), F32)
        for h in range(n_heads):
            g, r = divmod(h, per_kv)
            slot = (g // 2) * per_kv + r
            sl = qf[:, slot * LANES:(slot + 1) * LANES]
            half = jnp.where(low, sl, 0.0) if g % 2 == 0 else jnp.where(low, 0.0, sl)
            row = [zeros] * (kv_heads // 2)
            row[g // 2] = half
            qpad_s[h * t_new:(h + 1) * t_new, :] = jnp.concatenate(row, axis=1)
        m_s[...] = jnp.full(m_s.shape, NEG, F32)
        l_s[...] = jnp.zeros(l_s.shape, F32)
        acc_s[...] = jnp.zeros(acc_s.shape, F32)
        kn_s[...] = jnp.zeros(kn_s.shape, F32)
        vn_s[...] = jnp.zeros(vn_s.shape, F32)
        kn_s[0:t_new, :] = kn_ref[0].astype(F32)
        vn_s[0:t_new, :] = vn_ref[0].astype(F32)

    thr = thr_ref[0]
    cut = cut_ref[0]

    def attend(s, key, pos0, pv):
        n = key.shape[1]
        wide = lambda a: jnp.concatenate([a] * (n // LANES), axis=1)
        pos = pos0 + lax.broadcasted_iota(I32, (t_new, n), 1)
        sel = (key > wide(thr)) | ((key == wide(thr)) & (pos <= wide(cut)))
        bias = jnp.where(sel, 0.0, NEG)
        s = s + jnp.concatenate([bias] * n_heads, axis=0)
        m_old = m_s[...]
        m_new = jnp.maximum(m_old, jnp.max(s, axis=1, keepdims=True))
        alpha = jnp.exp(m_old - m_new)
        p = jnp.exp(s - m_new)
        l_s[...] = alpha * l_s[...] + jnp.sum(p, axis=1, keepdims=True)
        m_s[...] = m_new
        acc_s[...] = acc_s[...] * alpha + pv(p.astype(BF16))

    qb = qpad_s[...].astype(BF16)
    kvw = qb.shape[1]
    for c in range(pg // pc):
        off = pl.multiple_of((i * pg + c * pc) * page, page)
        kt = jnp.concatenate([k_refs[c * pc + r][0, 0].reshape(kvw, page).astype(BF16) for r in range(pc)], axis=1)
        vt = jnp.concatenate([v_refs[c * pc + r][0, 0].reshape(kvw, page).astype(BF16) for r in range(pc)], axis=1)
        attend(_dot(qb, kt), keys_ref[0, :, pl.ds(off, pc * page)], off, lambda p: _dot_nt(p, vt))

    @pl.when(i == n_steps - 1)
    def _():
        attend(_dot_nt(qb, kn_s[...].astype(BF16)), keys_ref[0, :, n_keys - page:n_keys], n_keys - page,
               lambda p: _dot(p, vn_s[...].astype(BF16)))
        o = acc_s[...] / l_s[...]
        for s in range(n_heads // 2):
            pb, r = divmod(s, per_kv)
            h_lo = (2 * pb) * per_kv + r
            h_hi = (2 * pb + 1) * per_kv + r
            o_lo = o[h_lo * t_new:(h_lo + 1) * t_new, pb * LANES:(pb + 1) * LANES]
            o_hi = o[h_hi * t_new:(h_hi + 1) * t_new, pb * LANES:(pb + 1) * LANES]
            y_ref[0, :, s * LANES:(s + 1) * LANES] = jnp.where(low, o_lo, o_hi).astype(y_ref.dtype)


def _pages_per_step(n_pages):
    for pg in (16, 8, 4, 2, 1):
        if n_pages % pg == 0:
            return pg


def _dsa_sample(q, qiw, sa, kk_new, k_new, v_new, ck, cv, ci, page_table, layer, *, n_heads, kv_heads, idx_heads,
                wi_lane, att_scale, idx_w_scale):
    db, t_new, dq = q.shape
    _, n_pool, kv_heads_, hd, page = ck.shape
    kvw = kv_heads_ * hd
    n_pages = page_table.shape[1]
    assert page == LANES and t_new % SUBLANES == 0 and t_new <= page and kv_heads_ == kv_heads and hd == LANES // 2
    pg = _pages_per_step(n_pages)
    pc = pg
    n_steps = n_pages // pg
    n_keys = (n_pages + 1) * page
    topk = min(TOPK_MAX, (n_pages * page + t_new) // 4)
    idx_spec = lambda r: pl.BlockSpec((1, 1, hd, page), lambda b, i, pt: (layer, pt[b, i * pg + r], 0, 0))
    kv_spec = lambda r: pl.BlockSpec((1, 1, kv_heads, hd, page), lambda b, i, pt: (layer, pt[b, i * pg + r], 0, 0, 0))
    row_spec = lambda w: pl.BlockSpec((1, t_new, w), lambda b, i, pt: (b, 0, 0))

    keys = pl.pallas_call(
        functools.partial(_dsa_sample_scores_kernel, pg=pg, t_new=t_new, idx_heads=idx_heads, page=page,
                          idx_w_scale=idx_w_scale, wi_lane=wi_lane, n_keys=n_keys),
        grid_spec=pltpu.PrefetchScalarGridSpec(
            num_scalar_prefetch=1,
            grid=(db, n_steps),
            in_specs=[row_spec(qiw.shape[-1]), row_spec(LANES), row_spec(LANES)] + [idx_spec(r) for r in range(pg)],
            out_specs=row_spec(n_keys),
            scratch_shapes=[pltpu.VMEM((idx_heads * t_new, hd), F32), pltpu.VMEM((idx_heads * t_new, LANES), F32),
                            pltpu.VMEM((page, hd), F32)],
        ),
        out_shape=jax.ShapeDtypeStruct((db, t_new, n_keys), I32),
        compiler_params=_params("parallel", "arbitrary"),
        name="dsa_sample_scores",
    )(page_table, qiw, sa, kk_new, *([ci] * pg))

    n_rows = db * t_new
    rb = math.gcd(n_rows, LANES)
    thr, cut = pl.pallas_call(
        functools.partial(_dsa_sample_threshold_kernel, rb=rb, topk=topk, n_keys=n_keys),
        out_shape=[jax.ShapeDtypeStruct((n_rows, LANES), I32)] * 2,
        compiler_params=pltpu.CompilerParams(vmem_limit_bytes=VMEM_LIMIT_BYTES),
        name="dsa_sample_threshold",
    )(keys.reshape(n_rows, n_keys))
    thr = thr.reshape(db, t_new, LANES)
    cut = cut.reshape(db, t_new, LANES)

    return pl.pallas_call(
        functools.partial(_dsa_sample_attend_kernel, pg=pg, pc=pc, t_new=t_new, n_heads=n_heads, kv_heads=kv_heads,
                          page=page, att_scale=att_scale, n_keys=n_keys),
        grid_spec=pltpu.PrefetchScalarGridSpec(
            num_scalar_prefetch=1,
            grid=(db, n_steps),
            in_specs=[row_spec(dq), row_spec(n_keys), row_spec(LANES), row_spec(LANES), row_spec(kvw), row_spec(kvw)]
                     + [kv_spec(r) for r in range(pg)] * 2,
            out_specs=row_spec(dq),
            scratch_shapes=[pltpu.VMEM((n_heads * t_new, kvw), F32), pltpu.VMEM((n_heads * t_new, 1), F32),
                            pltpu.VMEM((n_heads * t_new, 1), F32), pltpu.VMEM((n_heads * t_new, kvw), F32),
                            pltpu.VMEM((page, kvw), F32), pltpu.VMEM((page, kvw), F32)],
        ),
        out_shape=jax.ShapeDtypeStruct((db, t_new, dq), BF16),
        compiler_params=_params("parallel", "arbitrary"),
        name="dsa_sample_attend",
    )(page_table, q, keys, thr, cut, k_new, v_new, *([ck] * pg), *([cv] * pg))


def _merge_kernel(x_ref, ys_ref, ya_ref, gs_ref, ga_ref, ps_ref, pa_ref, wo_ref, o_ref):
    merged = (_sigmoid(gs_ref[...]) * _dot(ys_ref[...], ps_ref[...])
              + _sigmoid(ga_ref[...]) * _dot(ya_ref[...], pa_ref[...]))
    o_ref[...] = x_ref[...] + _dot(merged.astype(BF16), wo_ref[...])


def _merge(x, y_ssd, y_attn, g_s, g_a, p_ssd, p_attn, w_out):
    t, d = x.shape
    tm = min(512, t)
    assert t % tm == 0
    rows = lambda w: pl.BlockSpec((tm, w), lambda i: (i, 0))
    full = lambda a: pl.BlockSpec(a.shape, lambda i: (0, 0))
    return pl.pallas_call(
        _merge_kernel,
        grid=(t // tm,),
        in_specs=[rows(d), rows(y_ssd.shape[1]), rows(y_attn.shape[1]), rows(d), rows(d),
                  full(p_ssd), full(p_attn), full(w_out)],
        out_specs=rows(d),
        out_shape=jax.ShapeDtypeStruct((t, d), F32),
        compiler_params=_params("parallel"),
        name="merge",
    )(x, y_ssd, y_attn, g_s, g_a, p_ssd, p_attn, w_out)


def _pair_slot_perm(n_heads, kv_heads, hd):
    per_kv = n_heads // kv_heads
    cols = []
    for s in range(n_heads // 2):
        pb, r = divmod(s, per_kv)
        for h in ((2 * pb) * per_kv + r, (2 * pb + 1) * per_kv + r):
            cols.extend(range(h * hd, (h + 1) * hd))
    return np.asarray(cols, np.int32)


def kernel(x_prompt, x_sample, cache_k, cache_v, cache_idx_k, state_ssm, state_conv, page_table, ffn1_norm, ffn1_w1, ffn1_w2, mix_norm, w_in, conv_w, conv_b, dt_bias, a_log, d_skip, ssd_norm, w_branch_ssd, w_branch_attn, w_out, ffn2_norm, ffn2_w1, ffn2_w2, final_norm):
    bp, seq, d_model = x_prompt.shape
    db, dseq, _ = x_sample.shape
    depth, n_pool, page, kv_heads, head_dim = cache_k.shape
    idx_dim = cache_idx_k.shape[-1]
    ssd_heads, ssd_hd, d_state = state_ssm.shape[2:]
    conv_dim = state_conv.shape[-1]
    d_inner = ssd_norm.shape[-1]
    n_groups = (conv_dim - d_inner) // (2 * d_state)
    attn_dim = w_branch_attn.shape[1]
    n_heads = attn_dim // head_dim
    kvw = kv_heads * head_dim
    d_proj = w_in.shape[-1]
    idx_heads = (d_proj - (d_inner + conv_dim + ssd_heads + attn_dim + 2 * kvw + idx_dim + 2 * d_model)) // (idx_dim + 1)
    assert head_dim == LANES // 2 and idx_dim == LANES // 2 and ssd_heads + idx_heads <= LANES
    att_scale = head_dim ** -0.5
    idx_w_scale = (idx_heads ** -0.5) * (idx_dim ** -0.5)
    sizes = (d_inner, conv_dim, ssd_heads, attn_dim, kvw, kvw, idx_heads * idx_dim, idx_dim, idx_heads, d_model, d_model)
    assert sum(sizes) == d_proj
    offs = np.concatenate([[0], np.cumsum(sizes)])
    perm = _pair_slot_perm(n_heads, kv_heads, head_dim)
    wi_lane = ssd_heads

    tp, ts = bp * seq, db * dseq
    yp = x_prompt.reshape(tp, d_model)
    ys = x_sample.reshape(ts, d_model)
    ck = jnp.transpose(cache_k, (0, 1, 3, 4, 2))
    cv = jnp.transpose(cache_v, (0, 1, 3, 4, 2))
    ci = jnp.transpose(cache_idx_k, (0, 1, 3, 2))
    dsa_kw = dict(n_heads=n_heads, kv_heads=kv_heads, idx_heads=idx_heads, wi_lane=wi_lane, att_scale=att_scale,
                  idx_w_scale=idx_w_scale)
    outs = {n: [] for n in ("kp", "vp", "ip", "sp", "cp", "ks", "vs", "is", "ss", "cs")}
    hist = conv_w.shape[1] - 1

    for l in range(depth):
        wl = w_in[l]
        col = lambda i: wl[:, offs[i]:offs[i + 1]]
        w_z, w_xbc, w_dt, w_q, w_k, w_v, w_qi, w_ki, w_wi, w_gs, w_ga = [col(i) for i in range(11)]
        w_sa = jnp.concatenate([w_dt, w_wi, jnp.zeros((d_model, LANES - ssd_heads - idx_heads), F32)], axis=1)
        w_qi_wide = jnp.pad(w_qi.reshape(d_model, idx_heads, idx_dim), ((0, 0), (0, 0), (0, LANES - idx_dim)))
        bf = lambda w: w.astype(BF16)
        wa = [bf(w_z), bf(w_xbc)]
        wb_common = [bf(w_q[:, perm]), bf(w_k), bf(w_v), bf(jnp.concatenate([w_ki, w_ki], axis=1)), bf(w_sa),
                     bf(w_gs), bf(w_ga)]
        dt_common = [(BF16,), (F32, BF16), (F32, BF16), (F32, BF16), (F32,), (F32,), (F32,)]
        f1w1, f1w2, f2w1, f2w2 = bf(ffn1_w1[l]), bf(ffn1_w2[l]), bf(ffn2_w1[l]), bf(ffn2_w2[l])
        p_ssd, p_attn, wo = bf(w_branch_ssd[l]), bf(w_branch_attn[l][perm, :]), bf(w_out[l])
        last = l == depth - 1

        def mixer(y, b, s, qi_weight, attend, buf, h0):
            z, xbc = _norm_linear(y, mix_norm[l], wa, [(F32,), (F32,)], 256)
            q, k, kb, v, vb, kk, kkb, sa, g_s, g_a, qi = _norm_linear(
                y, mix_norm[l], wb_common + [qi_weight], dt_common + [(BF16,)], 256)
            r3 = lambda a: a.reshape(b, s, a.shape[-1])
            y_ssd, h_fin = _ssd(r3(z), r3(xbc), r3(sa), buf, h0, conv_w[l], conv_b[l], dt_bias[l], a_log[l],
                                d_skip[l], ssd_norm[l], n_groups=n_groups, d_state=d_state)
            y_attn = attend(r3(q), r3(qi), r3(sa), r3(kkb), r3(kb), r3(vb))
            y = _merge(y, y_ssd.reshape(b * s, d_inner), y_attn.reshape(b * s, attn_dim), g_s, g_a, p_ssd, p_attn, wo)
            new_buf = r3(xbc)[:, s - hist:, :]
            return (y, new_buf, h_fin, k.reshape(b, s, kv_heads, head_dim), v.reshape(b, s, kv_heads, head_dim),
                    r3(kk)[:, :, :idx_dim])

        yp = _ffn(yp, ffn1_norm[l], f1w1, f1w2)
        ys = _ffn(ys, ffn1_norm[l], f1w1, f1w2)

        yp, cbp, hfp, kp, vp, kip = mixer(
            yp, bp, seq, bf(w_qi), functools.partial(_dsa_prompt, **dsa_kw),
            jnp.zeros((bp, hist, conv_dim), F32), jnp.zeros((bp, ssd_heads, ssd_hd, d_state), F32))
        att_s = functools.partial(_dsa_sample, ck=ck, cv=cv, ci=ci, page_table=page_table, layer=l, **dsa_kw)
        ys, cbs, hfs, kss, vss, kis = mixer(
            ys, db, dseq, bf(w_qi_wide.reshape(d_model, idx_heads * LANES)), att_s, state_conv[l], state_ssm[l])

        pg_ = final_norm if last else None
        yp = _ffn(yp, ffn2_norm[l], f2w1, f2w2, pg_)
        ys = _ffn(ys, ffn2_norm[l], f2w1, f2w2, pg_)
        for n, a in zip(("kp", "vp", "ip", "sp", "cp", "ks", "vs", "is", "ss", "cs"),
                        (kp, vp, kip, hfp, cbp, kss, vss, kis, hfs, cbs)):
            outs[n].append(a)

    st = lambda n: jnp.stack(outs[n])
    return (yp.reshape(bp, seq, d_model), ys.reshape(db, dseq, d_model),
            st("kp"), st("vp"), st("ip"), st("sp"), st("cp"),
            st("ks"), st("vs"), st("is"), st("ss"), st("cs"))
```

```python
import functools
import math

import jax
import jax.numpy as jnp
import numpy as np
from jax import lax
from jax.experimental import pallas as pl
from jax.experimental.pallas import tpu as pltpu

F32 = jnp.float32
BF16 = jnp.bfloat16
I32 = jnp.int32

EPS = 1e-6
SSD_CHUNK = 128
TOPK_MAX = 256
LANES = 128
SUBLANES = 8
VMEM_LIMIT_BYTES = 56 * 1024 * 1024
NEG = -1e30
INT_MIN = -(2 ** 31)


def _params(*sem):
    return pltpu.CompilerParams(dimension_semantics=sem, vmem_limit_bytes=VMEM_LIMIT_BYTES)


def _sigmoid(x):
    return 1.0 / (1.0 + jnp.exp(-x))


def _rms(x, g):
    return x * lax.rsqrt(jnp.mean(x * x, axis=-1, keepdims=True) + EPS) * g


def _dot(a, b):
    return jnp.dot(a, b, preferred_element_type=F32)


def _dot_nt(a, b):
    return lax.dot_general(a, b, (((1,), (1,)), ((), ())), preferred_element_type=F32)


def _split2(x):
    hi = x.astype(BF16)
    lo = (x - hi.astype(F32)).astype(BF16)
    return hi, lo


def _split3(x):
    hi = x.astype(BF16)
    r = x - hi.astype(F32)
    mid = r.astype(BF16)
    lo = (r - mid.astype(F32)).astype(BF16)
    return hi, mid, lo


def _ffn_kernel(*refs, post_norm):
    if post_norm:
        x_ref, g_ref, wa_ref, wb_ref, w2_ref, pg_ref, o_ref, h_s, acc_s = refs
    else:
        x_ref, g_ref, wa_ref, wb_ref, w2_ref, o_ref, h_s, acc_s = refs
    f = pl.program_id(1)

    @pl.when(f == 0)
    def _():
        h_s[...] = _rms(x_ref[...], g_ref[...]).astype(BF16)
        acc_s[...] = jnp.zeros_like(acc_s)

    h = h_s[...]
    a = _dot(h, wa_ref[...])
    b = _dot(h, wb_ref[...])
    u = (a * _sigmoid(a) * b).astype(BF16)
    acc_s[...] += _dot(u, w2_ref[...])

    @pl.when(f == pl.num_programs(1) - 1)
    def _():
        y = x_ref[...] + 0.5 * acc_s[...]
        if post_norm:
            y = _rms(y, pg_ref[...])
        o_ref[...] = y


def _ff_tile(d_ff):
    best = None
    for t in range(LANES, d_ff + 1, LANES):
        if d_ff % t == 0 and t <= 1536:
            best = t
    assert best is not None, d_ff
    return best


def _ffn(x, g, w1, w2, post_gain=None):
    t, d = x.shape
    d_ff = w2.shape[0]
    tm = min(512, t)
    tf = _ff_tile(d_ff)
    nf = d_ff // tf
    assert t % tm == 0
    post_norm = post_gain is not None
    in_specs = [
        pl.BlockSpec((tm, d), lambda i, f: (i, 0)),
        pl.BlockSpec((1, d), lambda i, f: (0, 0)),
        pl.BlockSpec((d, tf), lambda i, f: (0, f)),
        pl.BlockSpec((d, tf), lambda i, f: (0, f + nf)),
        pl.BlockSpec((tf, d), lambda i, f: (f, 0)),
    ]
    args = [x, g.reshape(1, d), w1, w1, w2]
    if post_norm:
        in_specs.append(pl.BlockSpec((1, d), lambda i, f: (0, 0)))
        args.append(post_gain.reshape(1, d))
    return pl.pallas_call(
        functools.partial(_ffn_kernel, post_norm=post_norm),
        grid=(t // tm, nf),
        in_specs=in_specs,
        out_specs=pl.BlockSpec((tm, d), lambda i, f: (i, 0)),
        out_shape=jax.ShapeDtypeStruct((t, d), F32),
        scratch_shapes=[pltpu.VMEM((tm, d), BF16), pltpu.VMEM((tm, d), F32)],
        compiler_params=_params("parallel", "arbitrary"),
        name="ffn",
    )(*args)


def _norm_linear_kernel(*refs, out_dtypes):
    n_w = len(out_dtypes)
    x_ref, g_ref = refs[:2]
    w_refs = refs[2:2 + n_w]
    o_refs = list(refs[2 + n_w:])
    h = _rms(x_ref[...], g_ref[...]).astype(BF16)
    for w_ref, dts in zip(w_refs, out_dtypes):
        r = _dot(h, w_ref[...])
        for dt in dts:
            o_refs.pop(0)[...] = r.astype(dt)


def _norm_linear(x, g, weights, out_dtypes, tm):
    t, d = x.shape
    tm = min(tm, t)
    assert t % tm == 0
    in_specs = [pl.BlockSpec((tm, d), lambda i: (i, 0)), pl.BlockSpec((1, d), lambda i: (0, 0))]
    out_specs, out_shape = [], []
    for w, dts in zip(weights, out_dtypes):
        n = w.shape[1]
        in_specs.append(pl.BlockSpec((d, n), lambda i: (0, 0)))
        for dt in dts:
            out_specs.append(pl.BlockSpec((tm, n), lambda i: (i, 0)))
            out_shape.append(jax.ShapeDtypeStruct((t, n), dt))
    return pl.pallas_call(
        functools.partial(_norm_linear_kernel, out_dtypes=tuple(tuple(d_) for d_ in out_dtypes)),
        grid=(t // tm,),
        in_specs=in_specs,
        out_specs=out_specs,
        out_shape=out_shape,
        compiler_params=_params("parallel"),
        name="norm_linear",
    )(x, g.reshape(1, d), *weights)


def _ssd_kernel(z_ref, xbc_ref, dt_ref, buf_ref, h0_ref, cw_ref, cb_ref, dtb_ref, alog_ref, dsk_ref, ng_ref, e_ref,
                y_ref, hfin_ref, xp_s, ht_s, *, qin, d_inner, n_groups, d_state, conv_w):
    q = SSD_CHUNK
    c = pl.program_id(1)
    hp_blocks = d_inner // LANES
    gw = d_inner // n_groups
    assert d_state == LANES and gw % LANES == 0
    pad = SUBLANES
    hist = conv_w - 1

    @pl.when(c == 0)
    def _():
        xp_s[0:pad, :] = buf_ref[0]
        if qin < q:
            xp_s[pad + qin:pad + q, :] = jnp.zeros((q - qin, xp_s.shape[1]), F32)
        for i in range(hp_blocks):
            ht_s[:, i * LANES:(i + 1) * LANES] = h0_ref[0, i * LANES:(i + 1) * LANES, :].T

    xp_s[pad:pad + qin, :] = xbc_ref[0]
    x_cur = xp_s[pad:pad + q, :]
    x_prev = xp_s[0:pad, :]
    row8 = lax.broadcasted_iota(I32, (pad, 1), 0)
    acc = cb_ref[...] + x_cur * cw_ref[hist:hist + 1, :]
    for i in range(hist):
        s = hist - i
        rolled = pltpu.roll(x_cur, s, 0)
        head = jnp.where(row8 < s, pltpu.roll(x_prev, s, 0), rolled[0:pad])
        acc = acc + jnp.concatenate([head, rolled[pad:]], axis=0) * cw_ref[i:i + 1, :]
    xc = acc * _sigmoid(acc)
    tail = xp_s[pad + qin - hist:pad + qin, :]
    xp_s[pad - hist:pad, :] = tail

    xs = xc[:, :d_inner]
    bm = xc[:, d_inner:d_inner + n_groups * d_state]
    cm = xc[:, d_inner + n_groups * d_state:]

    dt_raw = dt_ref[0] + dtb_ref[...]
    dt = jnp.maximum(dt_raw, 0.0) + jnp.log1p(jnp.exp(-jnp.abs(dt_raw)))
    if qin < q:
        dt = jnp.concatenate([dt, jnp.zeros((q - qin, LANES), F32)], axis=0)
    la = dt * (-jnp.exp(alog_ref[...]))

    ri = lax.broadcasted_iota(I32, (q, q), 0)
    ci = lax.broadcasted_iota(I32, (q, q), 1)
    causal = ri >= ci
    tril = jnp.where(causal, 1.0, 0.0).astype(BF16)
    eye = jnp.where(ri == ci, 1.0, 0.0).astype(BF16)
    a_cs = sum(_dot(tril, p) for p in _split3(la))
    a_cs_t = sum(_dot_nt(eye, p) for p in _split3(a_cs))
    dec = jnp.exp(a_cs[q - 1:q, :] - a_cs)
    eac = jnp.exp(a_cs)
    stacked = jnp.concatenate([dt, dec, eac], axis=0)
    expd = sum(_dot(p, e_ref[...]) for p in _split2(stacked))
    dt_e, dec_e, eac_e = expd[0:q], expd[q:2 * q], expd[2 * q:3 * q]

    x = xs * dt_e
    xb = x.astype(BF16)
    xd = (x * dec_e).astype(BF16)
    lane = lax.broadcasted_iota(I32, (q, LANES), 1)
    hd = LANES // 2
    heads_per_group = gw // hd

    y_parts = []
    for g in range(n_groups):
        gs = slice(g * gw, (g + 1) * gw)
        cg = cm[:, g * d_state:(g + 1) * d_state].astype(BF16)
        bg = bm[:, g * d_state:(g + 1) * d_state]
        cb = _dot_nt(cg, bg.astype(BF16))
        y_off = _dot(cg, ht_s[:, gs].astype(BF16))
        pair_parts = []
        for p in range(heads_per_group // 2):
            xp = xb[:, g * gw + p * LANES:g * gw + (p + 1) * LANES]
            res = []
            for k in range(2):
                j = g * heads_per_group + 2 * p + k
                diff = a_cs[:, j:j + 1] - a_cs_t[j:j + 1, :]
                lm = jnp.exp(jnp.where(causal, diff, NEG))
                res.append(_dot((cb * lm).astype(BF16), xp))
            pair_parts.append(jnp.where(lane < hd, res[0], res[1]))
        y_diag = jnp.concatenate(pair_parts, axis=1)
        y_parts.append(y_diag + y_off * eac_e[:, gs])
        st = _dot(bg.T.astype(BF16), xd[:, gs])
        ht_s[:, gs] = ht_s[:, gs] * eac_e[q - 1:q, gs] + st

    zz = z_ref[0]
    outs = []
    for g in range(n_groups):
        gs = slice(g * gw, (g + 1) * gw)
        yv = (y_parts[g][:qin] + dsk_ref[:, gs] * xs[:qin, gs]) * (zz[:, gs] * _sigmoid(zz[:, gs]))
        ms = jnp.mean(yv * yv, axis=-1, keepdims=True)
        outs.append(yv * lax.rsqrt(ms + EPS) * ng_ref[:, gs])
    y_ref[0] = jnp.concatenate(outs, axis=1).astype(y_ref.dtype)

    @pl.when(c == pl.num_programs(1) - 1)
    def _():
        for i in range(hp_blocks):
            hfin_ref[0, i * LANES:(i + 1) * LANES, :] = ht_s[:, i * LANES:(i + 1) * LANES].T


def _ssd(z, xbc, dtp, buf, h0, conv_w, conv_b, dt_bias, a_log, d_skip, norm_g, *, n_groups, d_state):
    b, l, d_inner = z.shape
    conv_dim = xbc.shape[-1]
    n_heads, p_dim, n_state = h0.shape[1:]
    width = conv_w.shape[0]
    hist = width - 1
    assert n_heads <= LANES and p_dim == LANES // 2 and n_state == d_state and hist <= SUBLANES
    qin = math.gcd(l, SSD_CHUNK)
    assert qin % SUBLANES == 0 and qin >= hist
    nc = l // qin
    assert nc == 1 or qin == SSD_CHUNK
    hp = n_heads * p_dim
    buf8 = jnp.pad(buf, ((0, 0), (SUBLANES - hist, 0), (0, 0)))
    cw8 = jnp.pad(conv_w, ((0, SUBLANES - width), (0, 0)))
    pad1 = lambda v: jnp.pad(v.reshape(1, -1), ((0, 0), (0, LANES - n_heads)))
    expand = (np.arange(LANES)[:, None] == (np.arange(d_inner)[None, :] // p_dim)).astype(np.float32)
    full = lambda shape: pl.BlockSpec(shape, lambda i, c: (0,) * len(shape))
    y, hfin = pl.pallas_call(
        functools.partial(_ssd_kernel, qin=qin, d_inner=d_inner, n_groups=n_groups, d_state=d_state, conv_w=width),
        grid=(b, nc),
        in_specs=[
            pl.BlockSpec((1, qin, d_inner), lambda i, c: (i, c, 0)),
            pl.BlockSpec((1, qin, conv_dim), lambda i, c: (i, c, 0)),
            pl.BlockSpec((1, qin, LANES), lambda i, c: (i, c, 0)),
            pl.BlockSpec((1, SUBLANES, conv_dim), lambda i, c: (i, 0, 0)),
            pl.BlockSpec((1, hp, n_state), lambda i, c: (i, 0, 0)),
            full((SUBLANES, conv_dim)), full((1, conv_dim)), full((1, LANES)), full((1, LANES)),
            full((1, d_inner)), full((1, d_inner)), full((LANES, d_inner)),
        ],
        out_specs=[
            pl.BlockSpec((1, qin, d_inner), lambda i, c: (i, c, 0)),
            pl.BlockSpec((1, hp, n_state), lambda i, c: (i, 0, 0)),
        ],
        out_shape=[jax.ShapeDtypeStruct((b, l, d_inner), BF16), jax.ShapeDtypeStruct((b, hp, n_state), F32)],
        scratch_shapes=[pltpu.VMEM((SUBLANES + SSD_CHUNK, conv_dim), F32), pltpu.VMEM((n_state, hp), F32)],
        compiler_params=_params("parallel", "arbitrary"),
        name="ssd",
    )(z, xbc, dtp, buf8, h0.reshape(b, hp, n_state), cw8, conv_b.reshape(1, -1), pad1(dt_bias), pad1(a_log),
      jnp.repeat(d_skip, p_dim).reshape(1, -1), norm_g.reshape(1, -1), jnp.asarray(expand, BF16))
    return y, hfin.reshape(b, n_heads, p_dim, n_state)


def _score_key(sc, valid):
    bits = lax.bitcast_convert_type(sc + 0.0, I32)
    key = jnp.where(bits < 0, bits ^ jnp.int32(0x7FFFFFFF), bits)
    return jnp.where(valid, key, jnp.int32(INT_MIN))


def _kth_largest_key(count_ge, rows, k):
    def body(b, t):
        cand = t + lax.shift_left(jnp.int32(1), jnp.int32(31) - b)
        return jnp.where(count_ge(cand) >= k, cand, t)
    return lax.fori_loop(0, 32, body, jnp.full((rows, 1), INT_MIN, I32))


def _tie_cutoff(count_tie_before, need, rows, n_bits):
    def body(b, p):
        bit = lax.shift_left(jnp.int32(1), jnp.int32(n_bits - 1) - b)
        p_c = p | bit
        return jnp.where(count_tie_before(p_c) <= need - 1, p_c, p)
    return lax.fori_loop(0, n_bits, body, jnp.zeros((rows, 1), I32))


def _dsa_prompt_kernel(q_ref, qi_ref, sa_ref, kk_ref, k_ref, vx_ref, y_ref,
                       keys_s, qpad_s, qipad_s, wb_s, m_s, acc_s,
                       *, tq, tk, n_heads, kv_heads, idx_heads, topk, att_scale, idx_w_scale, wi_lane, seq_len):
    i = pl.program_id(1)
    hd = LANES // 2
    per_kv = n_heads // kv_heads
    n_kt = (i * tq + tq - 1) // tk + 1
    lane = lax.broadcasted_iota(I32, (tq, LANES), 1)
    low = lane < hd

    qf = q_ref[0].astype(F32) * (att_scale * math.log2(math.e))
    zeros = jnp.zeros((tq, LANES), F32)
    for h in range(n_heads):
        g, r = divmod(h, per_kv)
        slot = (g // 2) * per_kv + r
        sl = qf[:, slot * LANES:(slot + 1) * LANES]
        half = jnp.where(low, sl, 0.0) if g % 2 == 0 else jnp.where(low, 0.0, sl)
        row = [zeros] * (kv_heads // 2)
        row[g // 2] = half
        qpad_s[h * tq:(h + 1) * tq, :] = jnp.concatenate(row, axis=1).astype(BF16)
    qif = qi_ref[0].astype(F32)
    sa = sa_ref[0]
    for h in range(idx_heads):
        sl = qif[:, (h // 2) * LANES:(h // 2 + 1) * LANES]
        half = jnp.where(low, sl, 0.0) if h % 2 == 0 else jnp.where(low, 0.0, sl)
        qipad_s[h * tq:(h + 1) * tq, :] = half.astype(BF16)
        wb_s[h] = jnp.broadcast_to(sa[:, wi_lane + h:wi_lane + h + 1] * idx_w_scale, (tq, LANES))

    qpos = i * tq + lax.broadcasted_iota(I32, (tq, tk), 0)
    kiota = lax.broadcasted_iota(I32, (tq, tk), 1)

    tiles_per_step = 2 if keys_s.shape[0] % 2 == 0 else 1

    def score_tiles(jj, carry):
        for u in range(tiles_per_step):
            j = jj * tiles_per_step + u
            kt = kk_ref[0, pl.ds(pl.multiple_of(j * tk, tk), tk), :]
            r = _dot_nt(qipad_s[...], kt)
            parts = []
            for c in range(tk // LANES):
                acc = jnp.zeros((tq, LANES), F32)
                for h in range(idx_heads):
                    acc = acc + jnp.maximum(r[h * tq:(h + 1) * tq, c * LANES:(c + 1) * LANES], 0.0) * wb_s[h]
                parts.append(acc)
            sc = jnp.concatenate(parts, axis=1)
            keys_s[j] = _score_key(sc, j * tk + kiota <= qpos)
        return carry
    lax.fori_loop(0, (n_kt + tiles_per_step - 1) // tiles_per_step, score_tiles, 0)

    n_chunks = tk // LANES

    def count(pred):
        def body(j, cnt):
            for c in range(n_chunks):
                cnt = cnt + jnp.where(pred(keys_s[j, :, c * LANES:(c + 1) * LANES], j * tk + c * LANES), 1.0, 0.0)
            return cnt
        cnt = lax.fori_loop(0, n_kt, body, jnp.zeros((tq, LANES), F32))
        return jnp.sum(cnt, axis=1, keepdims=True)

    def count_ge(t):
        tb = jnp.broadcast_to(t, (tq, LANES))
        return count(lambda key, base: key >= tb)

    thr = _kth_largest_key(count_ge, tq, topk)
    c_ge = count_ge(thr)
    tie = (c_ge > topk) & (thr > INT_MIN)
    liota = lax.broadcasted_iota(I32, (tq, LANES), 1)

    @pl.when(jnp.max(jnp.where(tie, 1, 0)) > 0)
    def _():
        need = topk - count(lambda key, base: key > thr)
        cut = _tie_cutoff(lambda p: count(lambda key, base: (key == thr) & (base + liota < p)),
                          need, tq, int(seq_len).bit_length())
        def drop(j, carry):
            key = keys_s[j]
            keys_s[j] = jnp.where(tie & (key == thr) & (j * tk + kiota > cut), jnp.int32(INT_MIN), key)
            return carry
        lax.fori_loop(0, n_kt, drop, 0)

    thr_b = jnp.broadcast_to(jnp.maximum(thr, jnp.int32(INT_MIN + 1)), (tq, LANES))

    m_s[...] = jnp.full(m_s.shape, NEG, F32)
    acc_s[...] = jnp.zeros(acc_s.shape, F32)

    def attend_tile(j, carry):
        ks = pl.ds(pl.multiple_of(j * tk, tk), tk)
        s_all = _dot_nt(qpad_s[...], k_ref[0, ks, :])
        bias = [jnp.where(keys_s[j, :, c * LANES:(c + 1) * LANES] >= thr_b, 0.0, NEG) for c in range(n_chunks)]
        for g in range(kv_heads):
            ps, alphas = [], []
            for r in range(per_kv):
                rows = slice((g * per_kv + r) * tq, (g * per_kv + r + 1) * tq)
                s = [s_all[rows, c * LANES:(c + 1) * LANES] + bias[c] for c in range(n_chunks)]
                smax = s[0]
                for c in range(1, n_chunks):
                    smax = jnp.maximum(smax, s[c])
                m_old = m_s[rows]
                m_new = jnp.maximum(m_old, jnp.max(smax, axis=1, keepdims=True))
                m_s[rows] = m_new
                alphas.append(jnp.exp2(m_old - m_new))
                ps.append(jnp.concatenate([jnp.exp2(s[c] - m_new) for c in range(n_chunks)], axis=1).astype(BF16))
            grows = slice(g * per_kv * tq, (g + 1) * per_kv * tq)
            pv = _dot(jnp.concatenate(ps, axis=0), vx_ref[0, ks, g * LANES:(g + 1) * LANES])
            acc_s[grows] = acc_s[grows] * jnp.concatenate(alphas, axis=0) + pv
        return carry
    lax.fori_loop(0, n_kt, attend_tile, 0)

    def normalised(h):
        a = acc_s[h * tq:(h + 1) * tq]
        return a / pltpu.roll(a, hd, 1)
    for s in range(n_heads // 2):
        pb, r = divmod(s, per_kv)
        o_lo = normalised((2 * pb) * per_kv + r)
        o_hi = normalised((2 * pb + 1) * per_kv + r)
        y_ref[0, :, s * LANES:(s + 1) * LANES] = jnp.where(low, o_lo, o_hi).astype(y_ref.dtype)


def _dsa_prompt(q, qi, sa, kk, k, v, *, n_heads, kv_heads, idx_heads, wi_lane, att_scale, idx_w_scale):
    b, l, dq = q.shape
    tq, tk = 128, 512
    tk = min(tk, l)
    tq = min(tq, l)
    assert l % tq == 0 and l % tk == 0 and kv_heads % 2 == 0 and idx_heads % 2 == 0
    topk = min(TOPK_MAX, l // 4)
    kvw = k.shape[-1]
    hd = LANES // 2
    ones = jnp.ones((b, l, hd), BF16)
    vx = jnp.concatenate([a for g in range(kv_heads)
                          for a in ((v[:, :, g * hd:(g + 1) * hd], ones) if g % 2 == 0 else
                                    (ones, v[:, :, g * hd:(g + 1) * hd]))], axis=-1)
    return pl.pallas_call(
        functools.partial(_dsa_prompt_kernel, tq=tq, tk=tk, n_heads=n_heads, kv_heads=kv_heads, idx_heads=idx_heads,
                          topk=topk, att_scale=att_scale, idx_w_scale=idx_w_scale, wi_lane=wi_lane, seq_len=l),
        grid=(b, l // tq),
        in_specs=[
            pl.BlockSpec((1, tq, dq), lambda bi, i: (bi, i, 0)),
            pl.BlockSpec((1, tq, qi.shape[-1]), lambda bi, i: (bi, i, 0)),
            pl.BlockSpec((1, tq, LANES), lambda bi, i: (bi, i, 0)),
            pl.BlockSpec((1, l, LANES), lambda bi, i: (bi, 0, 0), pipeline_mode=pl.Buffered(1)),
            pl.BlockSpec((1, l, kvw), lambda bi, i: (bi, 0, 0), pipeline_mode=pl.Buffered(1)),
            pl.BlockSpec((1, l, vx.shape[-1]), lambda bi, i: (bi, 0, 0), pipeline_mode=pl.Buffered(1)),
        ],
        out_specs=pl.BlockSpec((1, tq, dq), lambda bi, i: (bi, i, 0)),
        out_shape=jax.ShapeDtypeStruct((b, l, dq), BF16),
        scratch_shapes=[
            pltpu.VMEM((l // tk, tq, tk), I32),
            pltpu.VMEM((n_heads * tq, kvw), BF16),
            pltpu.VMEM((idx_heads * tq, LANES), BF16),
            pltpu.VMEM((idx_heads, tq, LANES), F32),
            pltpu.VMEM((n_heads * tq, LANES), F32),
            pltpu.VMEM((n_heads * tq, LANES), F32),
        ],
        compiler_params=_params("parallel", "arbitrary"),
        name="dsa_prompt",
    )(q, qi, sa, kk, k, vx)


def _dsa_sample_scores_kernel(pt_ref, qi_ref, sa_ref, kn_ref, *rest, pg, t_new, idx_heads, page,
                              idx_w_scale, wi_lane, n_keys):
    page_refs = rest[:pg]
    keys_ref, qi_s, w_s, kn_s = rest[pg:]
    i = pl.program_id(1)
    hd = LANES // 2
    n_steps = pl.num_programs(1)

    @pl.when(i == 0)
    def _():
        qif = qi_ref[0]
        sa = sa_ref[0]
        for h in range(idx_heads):
            qi_s[h * t_new:(h + 1) * t_new, :] = qif[:, h * LANES:h * LANES + hd].astype(F32)
            w_s[h * t_new:(h + 1) * t_new, :] = jnp.broadcast_to(
                sa[:, wi_lane + h:wi_lane + h + 1] * idx_w_scale, (t_new, LANES))
        kn_s[...] = jnp.zeros(kn_s.shape, F32)
        kn_s[0:t_new, :] = kn_ref[0][:, 0:hd].astype(F32)

    def scores(dots):
        ww = jnp.maximum(dots, 0.0) * w_s[...]
        sc = ww[0:t_new]
        for h in range(1, idx_heads):
            sc = sc + ww[h * t_new:(h + 1) * t_new]
        return sc

    always = jnp.full((t_new, page), True)
    kt = jnp.concatenate([page_refs[r][0, 0].astype(BF16) for r in range(pg)], axis=1)
    dots = _dot(qi_s[...].astype(BF16), kt)
    for r in range(pg):
        off = pl.multiple_of((i * pg + r) * page, page)
        keys_ref[0, :, pl.ds(off, page)] = _score_key(scores(dots[:, r * page:(r + 1) * page]), always)

    @pl.when(i == n_steps - 1)
    def _():
        ti = lax.broadcasted_iota(I32, (t_new, page), 0)
        ki = lax.broadcasted_iota(I32, (t_new, page), 1)
        past = n_keys - page
        dots_new = _dot_nt(qi_s[...].astype(BF16), kn_s[...].astype(BF16))
        keys_ref[0, :, past:n_keys] = _score_key(scores(dots_new), ki <= ti)


def _dsa_sample_threshold_kernel(keys_ref, thr_ref, cut_ref, *, rb, topk, n_keys):
    n_chunks = n_keys // LANES
    unroll = next(u for u in (4, 3, 2, 1) if n_chunks % u == 0)
    liota = lax.broadcasted_iota(I32, (rb, LANES), 1)
    for b in range(keys_ref.shape[0] // rb):
        rows = slice(b * rb, (b + 1) * rb)

        def count(pred):
            def body(cc, cnt):
                for u in range(unroll):
                    off = pl.multiple_of((cc * unroll + u) * LANES, LANES)
                    cnt = cnt + jnp.where(pred(keys_ref[rows, pl.ds(off, LANES)], off), 1.0, 0.0)
                return cnt
            cnt = lax.fori_loop(0, n_chunks // unroll, body, jnp.zeros((rb, LANES), F32))
            return jnp.sum(cnt, axis=1, keepdims=True)

        def count_ge(t):
            tb = jnp.broadcast_to(t, (rb, LANES))
            return count(lambda key, off: key >= tb)

        thr = _kth_largest_key(count_ge, rb, topk)
        tie = (count_ge(thr) > topk) & (thr > INT_MIN)
        thr_ref[rows, :] = jnp.broadcast_to(jnp.maximum(thr, jnp.int32(INT_MIN + 1)), (rb, LANES))
        cut_ref[rows, :] = jnp.full((rb, LANES), n_keys, I32)

        @pl.when(jnp.max(jnp.where(tie, 1, 0)) > 0)
        def _():
            need = topk - count(lambda key, off: key > thr)
            cut = _tie_cutoff(lambda p: count(lambda key, off: (key == thr) & (off + liota < p)), need, rb,
                              int(n_keys).bit_length())
            cut_ref[rows, :] = jnp.broadcast_to(jnp.where(tie, cut, n_keys), (rb, LANES))


def _dsa_sample_attend_kernel(pt_ref, q_ref, keys_ref, thr_ref, cut_ref, kn_ref, vn_ref, *rest, pg, pc, t_new,
                              n_heads, kv_heads, page, att_scale, n_keys):
    k_refs = rest[:pg]
    v_refs = rest[pg:2 * pg]
    y_ref, qpad_s, m_s, l_s, acc_s, kn_s, vn_s = rest[2 * pg:]
    i = pl.program_id(1)
    hd = LANES // 2
    per_kv = n_heads // kv_heads
    n_steps = pl.num_programs(1)
    lane = lax.broadcasted_iota(I32, (t_new, LANES), 1)
    low = lane < hd

    @pl.when(i == 0)
    def _():
        qf = q_ref[0].astype(F32) * att_scale
        zeros = jnp.zeros((t_new, LANES), F32)
        for h in range(n_heads):
            g, r = divmod(h, per_kv)
            slot = (g // 2) * per_kv + r
            sl = qf[:, slot * LANES:(slot + 1) * LANES]
            half = jnp.where(low, sl, 0.0) if g % 2 == 0 else jnp.where(low, 0.0, sl)
            row = [zeros] * (kv_heads // 2)
            row[g // 2] = half
            qpad_s[h * t_new:(h + 1) * t_new, :] = jnp.concatenate(row, axis=1)
        m_s[...] = jnp.full(m_s.shape, NEG, F32)
        l_s[...] = jnp.zeros(l_s.shape, F32)
        acc_s[...] = jnp.zeros(acc_s.shape, F32)
        kn_s[...] = jnp.zeros(kn_s.shape, F32)
        vn_s[...] = jnp.zeros(vn_s.shape, F32)
        kn_s[0:t_new, :] = kn_ref[0].astype(F32)
        vn_s[0:t_new, :] = vn_ref[0].astype(F32)

    thr = thr_ref[0]
    cut = cut_ref[0]

    def attend(s, key, pos0, pv, chain):
        n = key.shape[1]
        wide = lambda a: jnp.concatenate([a] * (n // LANES), axis=1)
        pos = pos0 + lax.broadcasted_iota(I32, (t_new, n), 1)
        sel = (key > wide(thr)) | ((key == wide(thr)) & (pos <= wide(cut)))
        bias = jnp.where(sel, 0.0, NEG)
        s = s + jnp.concatenate([bias] * n_heads, axis=0)
        m_old = m_s[chain]
        m_new = jnp.maximum(m_old, jnp.max(s, axis=1, keepdims=True))
        alpha = jnp.exp(m_old - m_new)
        p = jnp.exp(s - m_new)
        l_s[chain] = alpha * l_s[chain] + jnp.sum(p, axis=1, keepdims=True)
        m_s[chain] = m_new
        acc_s[chain] = acc_s[chain] * alpha + pv(p.astype(BF16))

    n_chains = m_s.shape[0]
    qb = qpad_s[...].astype(BF16)
    kvw = qb.shape[1]
    for c in range(pg // pc):
        off = pl.multiple_of((i * pg + c * pc) * page, page)
        kt = jnp.concatenate([k_refs[c * pc + r][0, 0].reshape(kvw, page).astype(BF16) for r in range(pc)], axis=1)
        vt = jnp.concatenate([v_refs[c * pc + r][0, 0].reshape(kvw, page).astype(BF16) for r in range(pc)], axis=1)
        attend(_dot(qb, kt), keys_ref[0, :, pl.ds(off, pc * page)], off, lambda p: _dot_nt(p, vt), c % n_chains)

    @pl.when(i == n_steps - 1)
    def _():
        attend(_dot_nt(qb, kn_s[...].astype(BF16)), keys_ref[0, :, n_keys - page:n_keys], n_keys - page,
               lambda p: _dot(p, vn_s[...].astype(BF16)), 0)
        m = m_s[0]
        for c in range(1, n_chains):
            m = jnp.maximum(m, m_s[c])
        l = sum(l_s[c] * jnp.exp(m_s[c] - m) for c in range(n_chains))
        acc = sum(acc_s[c] * jnp.exp(m_s[c] - m) for c in range(n_chains))
        o = acc / l
        for s in range(n_heads // 2):
            pb, r = divmod(s, per_kv)
            h_lo = (2 * pb) * per_kv + r
            h_hi = (2 * pb + 1) * per_kv + r
            o_lo = o[h_lo * t_new:(h_lo + 1) * t_new, pb * LANES:(pb + 1) * LANES]
            o_hi = o[h_hi * t_new:(h_hi + 1) * t_new, pb * LANES:(pb + 1) * LANES]
            y_ref[0, :, s * LANES:(s + 1) * LANES] = jnp.where(low, o_lo, o_hi).astype(y_ref.dtype)


def _pages_per_step(n_pages):
    for pg in (16, 8, 4, 2, 1):
        if n_pages % pg == 0:
            return pg


def _dsa_sample(q, qiw, sa, kk_new, k_new, v_new, ck, cv, ci, page_table, layer, *, n_heads, kv_heads, idx_heads,
                wi_lane, att_scale, idx_w_scale):
    db, t_new, dq = q.shape
    _, n_pool, kv_heads_, hd, page = ck.shape
    kvw = kv_heads_ * hd
    n_pages = page_table.shape[1]
    assert page == LANES and t_new % SUBLANES == 0 and t_new <= page and kv_heads_ == kv_heads and hd == LANES // 2
    pg = _pages_per_step(n_pages)
    n_chains = 2 if pg % 2 == 0 else 1
    pc = pg // n_chains
    n_steps = n_pages // pg
    n_keys = (n_pages + 1) * page
    topk = min(TOPK_MAX, (n_pages * page + t_new) // 4)
    idx_spec = lambda r: pl.BlockSpec((1, 1, hd, page), lambda b, i, pt: (layer, pt[b, i * pg + r], 0, 0))
    kv_spec = lambda r: pl.BlockSpec((1, 1, kv_heads, hd, page), lambda b, i, pt: (layer, pt[b, i * pg + r], 0, 0, 0))
    row_spec = lambda w: pl.BlockSpec((1, t_new, w), lambda b, i, pt: (b, 0, 0))

    keys = pl.pallas_call(
        functools.partial(_dsa_sample_scores_kernel, pg=pg, t_new=t_new, idx_heads=idx_heads, page=page,
                          idx_w_scale=idx_w_scale, wi_lane=wi_lane, n_keys=n_keys),
        grid_spec=pltpu.PrefetchScalarGridSpec(
            num_scalar_prefetch=1,
            grid=(db, n_steps),
            in_specs=[row_spec(qiw.shape[-1]), row_spec(LANES), row_spec(LANES)] + [idx_spec(r) for r in range(pg)],
            out_specs=row_spec(n_keys),
            scratch_shapes=[pltpu.VMEM((idx_heads * t_new, hd), F32), pltpu.VMEM((idx_heads * t_new, LANES), F32),
                            pltpu.VMEM((page, hd), F32)],
        ),
        out_shape=jax.ShapeDtypeStruct((db, t_new, n_keys), I32),
        compiler_params=_params("parallel", "arbitrary"),
        name="dsa_sample_scores",
    )(page_table, qiw, sa, kk_new, *([ci] * pg))

    n_rows = db * t_new
    rb = math.gcd(n_rows, LANES)
    thr, cut = pl.pallas_call(
        functools.partial(_dsa_sample_threshold_kernel, rb=rb, topk=topk, n_keys=n_keys),
        out_shape=[jax.ShapeDtypeStruct((n_rows, LANES), I32)] * 2,
        compiler_params=pltpu.CompilerParams(vmem_limit_bytes=VMEM_LIMIT_BYTES),
        name="dsa_sample_threshold",
    )(keys.reshape(n_rows, n_keys))
    thr = thr.reshape(db, t_new, LANES)
    cut = cut.reshape(db, t_new, LANES)

    return pl.pallas_call(
        functools.partial(_dsa_sample_attend_kernel, pg=pg, pc=pc, t_new=t_new, n_heads=n_heads, kv_heads=kv_heads,
                          page=page, att_scale=att_scale, n_keys=n_keys),
        grid_spec=pltpu.PrefetchScalarGridSpec(
            num_scalar_prefetch=1,
            grid=(db, n_steps),
            in_specs=[row_spec(dq), row_spec(n_keys), row_spec(LANES), row_spec(LANES), row_spec(kvw), row_spec(kvw)]
                     + [kv_spec(r) for r in range(pg)] * 2,
            out_specs=row_spec(dq),
            scratch_shapes=[pltpu.VMEM((n_heads * t_new, kvw), F32), pltpu.VMEM((n_chains, n_heads * t_new, 1), F32),
                            pltpu.VMEM((n_chains, n_heads * t_new, 1), F32),
                            pltpu.VMEM((n_chains, n_heads * t_new, kvw), F32),
                            pltpu.VMEM((page, kvw), F32), pltpu.VMEM((page, kvw), F32)],
        ),
        out_shape=jax.ShapeDtypeStruct((db, t_new, dq), BF16),
        compiler_params=_params("parallel", "arbitrary"),
        name="dsa_sample_attend",
    )(page_table, q, keys, thr, cut, k_new, v_new, *([ck] * pg), *([cv] * pg))


def _merge_kernel(x_ref, ys_ref, ya_ref, gs_ref, ga_ref, ps_ref, pa_ref, wo_ref, o_ref):
    merged = (_sigmoid(gs_ref[...]) * _dot(ys_ref[...], ps_ref[...])
              + _sigmoid(ga_ref[...]) * _dot(ya_ref[...], pa_ref[...]))
    o_ref[...] = x_ref[...] + _dot(merged.astype(BF16), wo_ref[...])


def _merge(x, y_ssd, y_attn, g_s, g_a, p_ssd, p_attn, w_out):
    t, d = x.shape
    tm = min(512, t)
    assert t % tm == 0
    rows = lambda w: pl.BlockSpec((tm, w), lambda i: (i, 0))
    full = lambda a: pl.BlockSpec(a.shape, lambda i: (0, 0))
    return pl.pallas_call(
        _merge_kernel,
        grid=(t // tm,),
        in_specs=[rows(d), rows(y_ssd.shape[1]), rows(y_attn.shape[1]), rows(d), rows(d),
                  full(p_ssd), full(p_attn), full(w_out)],
        out_specs=rows(d),
        out_shape=jax.ShapeDtypeStruct((t, d), F32),
        compiler_params=_params("parallel"),
        name="merge",
    )(x, y_ssd, y_attn, g_s, g_a, p_ssd, p_attn, w_out)


def _pair_slot_perm(n_heads, kv_heads, hd):
    per_kv = n_heads // kv_heads
    cols = []
    for s in range(n_heads // 2):
        pb, r = divmod(s, per_kv)
        for h in ((2 * pb) * per_kv + r, (2 * pb + 1) * per_kv + r):
            cols.extend(range(h * hd, (h + 1) * hd))
    return np.asarray(cols, np.int32)


def kernel(x_prompt, x_sample, cache_k, cache_v, cache_idx_k, state_ssm, state_conv, page_table, ffn1_norm, ffn1_w1, ffn1_w2, mix_norm, w_in, conv_w, conv_b, dt_bias, a_log, d_skip, ssd_norm, w_branch_ssd, w_branch_attn, w_out, ffn2_norm, ffn2_w1, ffn2_w2, final_norm):
    bp, seq, d_model = x_prompt.shape
    db, dseq, _ = x_sample.shape
    depth, n_pool, page, kv_heads, head_dim = cache_k.shape
    idx_dim = cache_idx_k.shape[-1]
    ssd_heads, ssd_hd, d_state = state_ssm.shape[2:]
    conv_dim = state_conv.shape[-1]
    d_inner = ssd_norm.shape[-1]
    n_groups = (conv_dim - d_inner) // (2 * d_state)
    attn_dim = w_branch_attn.shape[1]
    n_heads = attn_dim // head_dim
    kvw = kv_heads * head_dim
    d_proj = w_in.shape[-1]
    idx_heads = (d_proj - (d_inner + conv_dim + ssd_heads + attn_dim + 2 * kvw + idx_dim + 2 * d_model)) // (idx_dim + 1)
    assert head_dim == LANES // 2 and idx_dim == LANES // 2 and ssd_heads + idx_heads <= LANES
    att_scale = head_dim ** -0.5
    idx_w_scale = (idx_heads ** -0.5) * (idx_dim ** -0.5)
    sizes = (d_inner, conv_dim, ssd_heads, attn_dim, kvw, kvw, idx_heads * idx_dim, idx_dim, idx_heads, d_model, d_model)
    assert sum(sizes) == d_proj
    offs = np.concatenate([[0], np.cumsum(sizes)])
    perm = _pair_slot_perm(n_heads, kv_heads, head_dim)
    wi_lane = ssd_heads

    tp, ts = bp * seq, db * dseq
    yp = x_prompt.reshape(tp, d_model)
    ys = x_sample.reshape(ts, d_model)
    ck = jnp.transpose(cache_k, (0, 1, 3, 4, 2))
    cv = jnp.transpose(cache_v, (0, 1, 3, 4, 2))
    ci = jnp.transpose(cache_idx_k, (0, 1, 3, 2))
    dsa_kw = dict(n_heads=n_heads, kv_heads=kv_heads, idx_heads=idx_heads, wi_lane=wi_lane, att_scale=att_scale,
                  idx_w_scale=idx_w_scale)
    outs = {n: [] for n in ("kp", "vp", "ip", "sp", "cp", "ks", "vs", "is", "ss", "cs")}
    hist = conv_w.shape[1] - 1

    for l in range(depth):
        wl = w_in[l]
        col = lambda i: wl[:, offs[i]:offs[i + 1]]
        w_z, w_xbc, w_dt, w_q, w_k, w_v, w_qi, w_ki, w_wi, w_gs, w_ga = [col(i) for i in range(11)]
        w_sa = jnp.concatenate([w_dt, w_wi, jnp.zeros((d_model, LANES - ssd_heads - idx_heads), F32)], axis=1)
        w_qi_wide = jnp.pad(w_qi.reshape(d_model, idx_heads, idx_dim), ((0, 0), (0, 0), (0, LANES - idx_dim)))
        bf = lambda w: w.astype(BF16)
        wa = [bf(w_z), bf(w_xbc)]
        wb_common = [bf(w_q[:, perm]), bf(w_k), bf(w_v), bf(jnp.concatenate([w_ki, w_ki], axis=1)), bf(w_sa),
                     bf(w_gs), bf(w_ga)]
        dt_common = [(BF16,), (F32, BF16), (F32, BF16), (F32, BF16), (F32,), (F32,), (F32,)]
        f1w1, f1w2, f2w1, f2w2 = bf(ffn1_w1[l]), bf(ffn1_w2[l]), bf(ffn2_w1[l]), bf(ffn2_w2[l])
        p_ssd, p_attn, wo = bf(w_branch_ssd[l]), bf(w_branch_attn[l][perm, :]), bf(w_out[l])
        last = l == depth - 1

        def mixer(y, b, s, qi_weight, attend, buf, h0):
            z, xbc = _norm_linear(y, mix_norm[l], wa, [(F32,), (F32,)], 256)
            q, k, kb, v, vb, kk, kkb, sa, g_s, g_a, qi = _norm_linear(
                y, mix_norm[l], wb_common + [qi_weight], dt_common + [(BF16,)], 256)
            r3 = lambda a: a.reshape(b, s, a.shape[-1])
            y_ssd, h_fin = _ssd(r3(z), r3(xbc), r3(sa), buf, h0, conv_w[l], conv_b[l], dt_bias[l], a_log[l],
                                d_skip[l], ssd_norm[l], n_groups=n_groups, d_state=d_state)
            y_attn = attend(r3(q), r3(qi), r3(sa), r3(kkb), r3(kb), r3(vb))
            y = _merge(y, y_ssd.reshape(b * s, d_inner), y_attn.reshape(b * s, attn_dim), g_s, g_a, p_ssd, p_attn, wo)
            new_buf = r3(xbc)[:, s - hist:, :]
            return (y, new_buf, h_fin, k.reshape(b, s, kv_heads, head_dim), v.reshape(b, s, kv_heads, head_dim),
                    r3(kk)[:, :, :idx_dim])

        yp = _ffn(yp, ffn1_norm[l], f1w1, f1w2)
        ys = _ffn(ys, ffn1_norm[l], f1w1, f1w2)

        yp, cbp, hfp, kp, vp, kip = mixer(
            yp, bp, seq, bf(w_qi), functools.partial(_dsa_prompt, **dsa_kw),
            jnp.zeros((bp, hist, conv_dim), F32), jnp.zeros((bp, ssd_heads, ssd_hd, d_state), F32))
        att_s = functools.partial(_dsa_sample, ck=ck, cv=cv, ci=ci, page_table=page_table, layer=l, **dsa_kw)
        ys, cbs, hfs, kss, vss, kis = mixer(
            ys, db, dseq, bf(w_qi_wide.reshape(d_model, idx_heads * LANES)), att_s, state_conv[l], state_ssm[l])

        pg_ = final_norm if last else None
        yp = _ffn(yp, ffn2_norm[l], f2w1, f2w2, pg_)
        ys = _ffn(ys, ffn2_norm[l], f2w1, f2w2, pg_)
        for n, a in zip(("kp", "vp", "ip", "sp", "cp", "ks", "vs", "is", "ss", "cs"),
                        (kp, vp, kip, hfp, cbp, kss, vss, kis, hfs, cbs)):
            outs[n].append(a)

    st = lambda n: jnp.stack(outs[n])
    return (yp.reshape(bp, seq, d_model), ys.reshape(db, dseq, d_model),
            st("kp"), st("vp"), st("ip"), st("sp"), st("cp"),
            st("ks"), st("vs"), st("is"), st("ss"), st("cs"))
```

```python
import functools
import math

import jax
import jax.numpy as jnp
import numpy as np
from jax import lax
from jax.experimental import pallas as pl
from jax.experimental.pallas import tpu as pltpu

F32 = jnp.float32
BF16 = jnp.bfloat16
I32 = jnp.int32

EPS = 1e-6
SSD_CHUNK = 128
TOPK_MAX = 256
LANES = 128
SUBLANES = 8
VMEM_LIMIT_BYTES = 56 * 1024 * 1024
NEG = -1e30
INT_MIN = -(2 ** 31)


def _params(*sem):
    return pltpu.CompilerParams(dimension_semantics=sem, vmem_limit_bytes=VMEM_LIMIT_BYTES)


def _sigmoid(x):
    return 1.0 / (1.0 + jnp.exp(-x))


def _rms(x, g):
    return x * lax.rsqrt(jnp.mean(x * x, axis=-1, keepdims=True) + EPS) * g


def _dot(a, b):
    return jnp.dot(a, b, preferred_element_type=F32)


def _dot_nt(a, b):
    return lax.dot_general(a, b, (((1,), (1,)), ((), ())), preferred_element_type=F32)


def _split2(x):
    hi = x.astype(BF16)
    lo = (x - hi.astype(F32)).astype(BF16)
    return hi, lo


def _split3(x):
    hi = x.astype(BF16)
    r = x - hi.astype(F32)
    mid = r.astype(BF16)
    lo = (r - mid.astype(F32)).astype(BF16)
    return hi, mid, lo


def _ffn_kernel(*refs, post_norm):
    if post_norm:
        x_ref, g_ref, wa_ref, wb_ref, w2_ref, pg_ref, o_ref, h_s, acc_s = refs
    else:
        x_ref, g_ref, wa_ref, wb_ref, w2_ref, o_ref, h_s, acc_s = refs
    f = pl.program_id(1)

    @pl.when(f == 0)
    def _():
        h_s[...] = _rms(x_ref[...], g_ref[...]).astype(BF16)
        acc_s[...] = jnp.zeros_like(acc_s)

    h = h_s[...]
    a = _dot(h, wa_ref[...])
    b = _dot(h, wb_ref[...])
    u = (a * _sigmoid(a) * b).astype(BF16)
    acc_s[...] += _dot(u, w2_ref[...])

    @pl.when(f == pl.num_programs(1) - 1)
    def _():
        y = x_ref[...] + 0.5 * acc_s[...]
        if post_norm:
            y = _rms(y, pg_ref[...])
        o_ref[...] = y


def _ff_tile(d_ff):
    best = None
    for t in range(LANES, d_ff + 1, LANES):
        if d_ff % t == 0 and t <= 1536:
            best = t
    assert best is not None, d_ff
    return best


def _ffn(x, g, w1, w2, post_gain=None):
    t, d = x.shape
    d_ff = w2.shape[0]
    tm = min(512, t)
    tf = _ff_tile(d_ff)
    nf = d_ff // tf
    assert t % tm == 0
    post_norm = post_gain is not None
    in_specs = [
        pl.BlockSpec((tm, d), lambda i, f: (i, 0)),
        pl.BlockSpec((1, d), lambda i, f: (0, 0)),
        pl.BlockSpec((d, tf), lambda i, f: (0, f)),
        pl.BlockSpec((d, tf), lambda i, f: (0, f + nf)),
        pl.BlockSpec((tf, d), lambda i, f: (f, 0)),
    ]
    args = [x, g.reshape(1, d), w1, w1, w2]
    if post_norm:
        in_specs.append(pl.BlockSpec((1, d), lambda i, f: (0, 0)))
        args.append(post_gain.reshape(1, d))
    return pl.pallas_call(
        functools.partial(_ffn_kernel, post_norm=post_norm),
        grid=(t // tm, nf),
        in_specs=in_specs,
        out_specs=pl.BlockSpec((tm, d), lambda i, f: (i, 0)),
        out_shape=jax.ShapeDtypeStruct((t, d), F32),
        scratch_shapes=[pltpu.VMEM((tm, d), BF16), pltpu.VMEM((tm, d), F32)],
        compiler_params=_params("parallel", "arbitrary"),
        name="ffn",
    )(*args)


def _norm_linear_kernel(*refs, out_dtypes):
    n_w = len(out_dtypes)
    x_ref, g_ref = refs[:2]
    w_refs = refs[2:2 + n_w]
    o_refs = list(refs[2 + n_w:])
    h = _rms(x_ref[...], g_ref[...]).astype(BF16)
    for w_ref, dts in zip(w_refs, out_dtypes):
        r = _dot(h, w_ref[...])
        for dt in dts:
            o_refs.pop(0)[...] = r.astype(dt)


def _norm_linear(x, g, weights, out_dtypes, tm):
    t, d = x.shape
    tm = min(tm, t)
    assert t % tm == 0
    in_specs = [pl.BlockSpec((tm, d), lambda i: (i, 0)), pl.BlockSpec((1, d), lambda i: (0, 0))]
    out_specs, out_shape = [], []
    for w, dts in zip(weights, out_dtypes):
        n = w.shape[1]
        in_specs.append(pl.BlockSpec((d, n), lambda i: (0, 0)))
        for dt in dts:
            out_specs.append(pl.BlockSpec((tm, n), lambda i: (i, 0)))
            out_shape.append(jax.ShapeDtypeStruct((t, n), dt))
    return pl.pallas_call(
        functools.partial(_norm_linear_kernel, out_dtypes=tuple(tuple(d_) for d_ in out_dtypes)),
        grid=(t // tm,),
        in_specs=in_specs,
        out_specs=out_specs,
        out_shape=out_shape,
        compiler_params=_params("parallel"),
        name="norm_linear",
    )(x, g.reshape(1, d), *weights)


def _ssd_kernel(z_ref, xbc_ref, dt_ref, buf_ref, h0_ref, cw_ref, cb_ref, dtb_ref, alog_ref, dsk_ref, ng_ref, e_ref,
                y_ref, hfin_ref, xp_s, ht_s, *, qin, d_inner, n_groups, d_state, conv_w):
    q = SSD_CHUNK
    c = pl.program_id(1)
    hp_blocks = d_inner // LANES
    gw = d_inner // n_groups
    assert d_state == LANES and gw % LANES == 0
    pad = SUBLANES
    hist = conv_w - 1

    @pl.when(c == 0)
    def _():
        xp_s[0:pad, :] = buf_ref[0]
        if qin < q:
            xp_s[pad + qin:pad + q, :] = jnp.zeros((q - qin, xp_s.shape[1]), F32)
        for i in range(hp_blocks):
            ht_s[:, i * LANES:(i + 1) * LANES] = h0_ref[0, i * LANES:(i + 1) * LANES, :].T

    xp_s[pad:pad + qin, :] = xbc_ref[0]
    x_cur = xp_s[pad:pad + q, :]
    x_prev = xp_s[0:pad, :]
    row8 = lax.broadcasted_iota(I32, (pad, 1), 0)
    acc = cb_ref[...] + x_cur * cw_ref[hist:hist + 1, :]
    for i in range(hist):
        s = hist - i
        rolled = pltpu.roll(x_cur, s, 0)
        head = jnp.where(row8 < s, pltpu.roll(x_prev, s, 0), rolled[0:pad])
        acc = acc + jnp.concatenate([head, rolled[pad:]], axis=0) * cw_ref[i:i + 1, :]
    xc = acc * _sigmoid(acc)
    tail = xp_s[pad + qin - hist:pad + qin, :]
    xp_s[pad - hist:pad, :] = tail

    xs = xc[:, :d_inner]
    bm = xc[:, d_inner:d_inner + n_groups * d_state]
    cm = xc[:, d_inner + n_groups * d_state:]

    dt_raw = dt_ref[0] + dtb_ref[...]
    dt = jnp.maximum(dt_raw, 0.0) + jnp.log1p(jnp.exp(-jnp.abs(dt_raw)))
    if qin < q:
        dt = jnp.concatenate([dt, jnp.zeros((q - qin, LANES), F32)], axis=0)
    la = dt * (-jnp.exp(alog_ref[...]))

    ri = lax.broadcasted_iota(I32, (q, q), 0)
    ci = lax.broadcasted_iota(I32, (q, q), 1)
    causal = ri >= ci
    tril = jnp.where(causal, 1.0, 0.0).astype(BF16)
    eye = jnp.where(ri == ci, 1.0, 0.0).astype(BF16)
    a_cs = sum(_dot(tril, p) for p in _split3(la))
    a_cs_t = sum(_dot_nt(eye, p) for p in _split3(a_cs))
    dec = jnp.exp(a_cs[q - 1:q, :] - a_cs)
    eac = jnp.exp(a_cs)
    stacked = jnp.concatenate([dt, dec, eac], axis=0)
    expd = sum(_dot(p, e_ref[...]) for p in _split2(stacked))
    dt_e, dec_e, eac_e = expd[0:q], expd[q:2 * q], expd[2 * q:3 * q]

    x = xs * dt_e
    xb = x.astype(BF16)
    xd = (x * dec_e).astype(BF16)
    lane = lax.broadcasted_iota(I32, (q, LANES), 1)
    hd = LANES // 2
    heads_per_group = gw // hd

    y_parts = []
    for g in range(n_groups):
        gs = slice(g * gw, (g + 1) * gw)
        cg = cm[:, g * d_state:(g + 1) * d_state].astype(BF16)
        bg = bm[:, g * d_state:(g + 1) * d_state]
        cb = _dot_nt(cg, bg.astype(BF16))
        y_off = _dot(cg, ht_s[:, gs].astype(BF16))
        pair_parts = []
        for p in range(heads_per_group // 2):
            xp = xb[:, g * gw + p * LANES:g * gw + (p + 1) * LANES]
            res = []
            for k in range(2):
                j = g * heads_per_group + 2 * p + k
                diff = a_cs[:, j:j + 1] - a_cs_t[j:j + 1, :]
                lm = jnp.exp(jnp.where(causal, diff, NEG))
                res.append(_dot((cb * lm).astype(BF16), xp))
            pair_parts.append(jnp.where(lane < hd, res[0], res[1]))
        y_diag = jnp.concatenate(pair_parts, axis=1)
        y_parts.append(y_diag + y_off * eac_e[:, gs])
        st = _dot(bg.T.astype(BF16), xd[:, gs])
        ht_s[:, gs] = ht_s[:, gs] * eac_e[q - 1:q, gs] + st

    zz = z_ref[0]
    outs = []
    for g in range(n_groups):
        gs = slice(g * gw, (g + 1) * gw)
        yv = (y_parts[g][:qin] + dsk_ref[:, gs] * xs[:qin, gs]) * (zz[:, gs] * _sigmoid(zz[:, gs]))
        ms = jnp.mean(yv * yv, axis=-1, keepdims=True)
        outs.append(yv * lax.rsqrt(ms + EPS) * ng_ref[:, gs])
    y_ref[0] = jnp.concatenate(outs, axis=1).astype(y_ref.dtype)

    @pl.when(c == pl.num_programs(1) - 1)
    def _():
        for i in range(hp_blocks):
            hfin_ref[0, i * LANES:(i + 1) * LANES, :] = ht_s[:, i * LANES:(i + 1) * LANES].T


def _ssd(z, xbc, dtp, buf, h0, conv_w, conv_b, dt_bias, a_log, d_skip, norm_g, *, n_groups, d_state):
    b, l, d_inner = z.shape
    conv_dim = xbc.shape[-1]
    n_heads, p_dim, n_state = h0.shape[1:]
    width = conv_w.shape[0]
    hist = width - 1
    assert n_heads <= LANES and p_dim == LANES // 2 and n_state == d_state and hist <= SUBLANES
    qin = math.gcd(l, SSD_CHUNK)
    assert qin % SUBLANES == 0 and qin >= hist
    nc = l // qin
    assert nc == 1 or qin == SSD_CHUNK
    hp = n_heads * p_dim
    buf8 = jnp.pad(buf, ((0, 0), (SUBLANES - hist, 0), (0, 0)))
    cw8 = jnp.pad(conv_w, ((0, SUBLANES - width), (0, 0)))
    pad1 = lambda v: jnp.pad(v.reshape(1, -1), ((0, 0), (0, LANES - n_heads)))
    expand = (np.arange(LANES)[:, None] == (np.arange(d_inner)[None, :] // p_dim)).astype(np.float32)
    full = lambda shape: pl.BlockSpec(shape, lambda i, c: (0,) * len(shape))
    y, hfin = pl.pallas_call(
        functools.partial(_ssd_kernel, qin=qin, d_inner=d_inner, n_groups=n_groups, d_state=d_state, conv_w=width),
        grid=(b, nc),
        in_specs=[
            pl.BlockSpec((1, qin, d_inner), lambda i, c: (i, c, 0)),
            pl.BlockSpec((1, qin, conv_dim), lambda i, c: (i, c, 0)),
            pl.BlockSpec((1, qin, LANES), lambda i, c: (i, c, 0)),
            pl.BlockSpec((1, SUBLANES, conv_dim), lambda i, c: (i, 0, 0)),
            pl.BlockSpec((1, hp, n_state), lambda i, c: (i, 0, 0)),
            full((SUBLANES, conv_dim)), full((1, conv_dim)), full((1, LANES)), full((1, LANES)),
            full((1, d_inner)), full((1, d_inner)), full((LANES, d_inner)),
        ],
        out_specs=[
            pl.BlockSpec((1, qin, d_inner), lambda i, c: (i, c, 0)),
            pl.BlockSpec((1, hp, n_state), lambda i, c: (i, 0, 0)),
        ],
        out_shape=[jax.ShapeDtypeStruct((b, l, d_inner), BF16), jax.ShapeDtypeStruct((b, hp, n_state), F32)],
        scratch_shapes=[pltpu.VMEM((SUBLANES + SSD_CHUNK, conv_dim), F32), pltpu.VMEM((n_state, hp), F32)],
        compiler_params=_params("parallel", "arbitrary"),
        name="ssd",
    )(z, xbc, dtp, buf8, h0.reshape(b, hp, n_state), cw8, conv_b.reshape(1, -1), pad1(dt_bias), pad1(a_log),
      jnp.repeat(d_skip, p_dim).reshape(1, -1), norm_g.reshape(1, -1), jnp.asarray(expand, BF16))
    return y, hfin.reshape(b, n_heads, p_dim, n_state)


def _score_key(sc, valid):
    bits = lax.bitcast_convert_type(sc + 0.0, I32)
    key = jnp.where(bits < 0, bits ^ jnp.int32(0x7FFFFFFF), bits)
    return jnp.where(valid, key, jnp.int32(INT_MIN))


def _kth_largest_key(count_ge, shape, k):
    def body(b, t):
        cand = t + lax.shift_left(jnp.int32(1), jnp.int32(31) - b)
        return jnp.where(count_ge(cand) >= k, cand, t)
    return lax.fori_loop(0, 32, body, jnp.full(shape, INT_MIN, I32))


def _tie_cutoff(count_tie_before, need, shape, n_bits):
    def body(b, p):
        bit = lax.shift_left(jnp.int32(1), jnp.int32(n_bits - 1) - b)
        p_c = p | bit
        return jnp.where(count_tie_before(p_c) <= need - 1, p_c, p)
    return lax.fori_loop(0, n_bits, body, jnp.zeros(shape, I32))


def _dsa_prompt_kernel(q_ref, qi_ref, sa_ref, kk_ref, k_ref, vx_ref, y_ref,
                       keys_s, qpad_s, qipad_s, w_s, m_s, acc_s,
                       *, tq, tk, n_heads, kv_heads, idx_heads, topk, att_scale, idx_w_scale, wi_lane, seq_len):
    i = pl.program_id(1)
    hd = LANES // 2
    per_kv = n_heads // kv_heads
    n_kt = (i * tq + tq - 1) // tk + 1
    lane = lax.broadcasted_iota(I32, (tq, LANES), 1)
    low = lane < hd

    qf = q_ref[0].astype(F32) * (att_scale * math.log2(math.e))
    zeros = jnp.zeros((tq, LANES), F32)
    for h in range(n_heads):
        g, r = divmod(h, per_kv)
        slot = (g // 2) * per_kv + r
        sl = qf[:, slot * LANES:(slot + 1) * LANES]
        half = jnp.where(low, sl, 0.0) if g % 2 == 0 else jnp.where(low, 0.0, sl)
        row = [zeros] * (kv_heads // 2)
        row[g // 2] = half
        qpad_s[h * tq:(h + 1) * tq, :] = jnp.concatenate(row, axis=1).astype(BF16)
    qif = qi_ref[0].astype(F32)
    for h in range(idx_heads):
        sl = qif[:, (h // 2) * LANES:(h // 2 + 1) * LANES]
        half = jnp.where(low, sl, 0.0) if h % 2 == 0 else jnp.where(low, 0.0, sl)
        qipad_s[h * tq:(h + 1) * tq, :] = half.astype(BF16)
    ri = lax.broadcasted_iota(I32, (LANES, LANES), 0)
    ci = lax.broadcasted_iota(I32, (LANES, LANES), 1)
    eye = jnp.where(ri == ci, 1.0, 0.0).astype(BF16)
    sa_t = sum(_dot_nt(eye, p) for p in _split3(sa_ref[0]))
    w_s[...] = sa_t[wi_lane:wi_lane + idx_heads, :] * idx_w_scale

    qpos = i * tq + lax.broadcasted_iota(I32, (tk, tq), 1)
    kiota = lax.broadcasted_iota(I32, (tk, tq), 0)
    siota = lax.broadcasted_iota(I32, (SUBLANES, tq), 0)

    tiles_per_step = 2 if keys_s.shape[0] % 2 == 0 else 1

    def score_tiles(jj, carry):
        for u in range(tiles_per_step):
            j = jj * tiles_per_step + u
            kt = kk_ref[0, pl.ds(pl.multiple_of(j * tk, tk), tk), :]
            r = _dot_nt(kt, qipad_s[...])
            sc = jnp.zeros((tk, tq), F32)
            for h in range(idx_heads):
                sc = sc + jnp.maximum(r[:, h * tq:(h + 1) * tq], 0.0) * w_s[h:h + 1, :]
            keys_s[j] = _score_key(sc, j * tk + kiota <= qpos)
        return carry
    lax.fori_loop(0, (n_kt + tiles_per_step - 1) // tiles_per_step, score_tiles, 0)

    n_acc = 4

    def count(pred):
        def body(j, cnts):
            cnts = list(cnts)
            for g in range(tk // SUBLANES):
                hit = pred(keys_s[j, g * SUBLANES:(g + 1) * SUBLANES, :], j * tk + g * SUBLANES)
                cnts[g % n_acc] = cnts[g % n_acc] + jnp.where(hit, 1.0, 0.0)
            return tuple(cnts)
        cnts = lax.fori_loop(0, n_kt, body, (jnp.zeros((SUBLANES, tq), F32),) * n_acc)
        return jnp.sum(sum(cnts), axis=0, keepdims=True)

    def count_ge(t):
        tb = jnp.broadcast_to(t, (SUBLANES, tq))
        return count(lambda key, base: key >= tb)

    thr = _kth_largest_key(count_ge, (1, tq), topk)
    c_ge = count_ge(thr)
    tie = (c_ge > topk) & (thr > INT_MIN)

    @pl.when(jnp.max(jnp.where(tie, 1, 0)) > 0)
    def _():
        need = topk - count(lambda key, base: key > thr)
        cut = _tie_cutoff(lambda p: count(lambda key, base: (key == thr) & (base + siota < p)),
                          need, (1, tq), int(seq_len).bit_length())
        def drop(j, carry):
            key = keys_s[j]
            keys_s[j] = jnp.where(tie & (key == thr) & (j * tk + kiota > cut), jnp.int32(INT_MIN), key)
            return carry
        lax.fori_loop(0, n_kt, drop, 0)

    thr_sel = jnp.maximum(thr, jnp.int32(INT_MIN + 1))

    n_chunks = tk // LANES
    m_s[...] = jnp.full(m_s.shape, NEG, F32)
    acc_s[...] = jnp.zeros(acc_s.shape, F32)

    def attend_tile(j, carry):
        ks = pl.ds(pl.multiple_of(j * tk, tk), tk)
        s_all = _dot_nt(qpad_s[...], k_ref[0, ks, :])
        bias = [jnp.where(keys_s[j, c * LANES:(c + 1) * LANES, :] >= thr_sel, 0.0, NEG).T for c in range(n_chunks)]
        for g in range(kv_heads):
            ps, alphas = [], []
            for r in range(per_kv):
                rows = slice((g * per_kv + r) * tq, (g * per_kv + r + 1) * tq)
                s = [s_all[rows, c * LANES:(c + 1) * LANES] + bias[c] for c in range(n_chunks)]
                smax = s[0]
                for c in range(1, n_chunks):
                    smax = jnp.maximum(smax, s[c])
                m_old = m_s[rows]
                m_new = jnp.maximum(m_old, jnp.max(smax, axis=1, keepdims=True))
                m_s[rows] = m_new
                alphas.append(jnp.exp2(m_old - m_new))
                ps.append(jnp.concatenate([jnp.exp2(s[c] - m_new) for c in range(n_chunks)], axis=1).astype(BF16))
            grows = slice(g * per_kv * tq, (g + 1) * per_kv * tq)
            pv = _dot(jnp.concatenate(ps, axis=0), vx_ref[0, ks, g * LANES:(g + 1) * LANES])
            acc_s[grows] = acc_s[grows] * jnp.concatenate(alphas, axis=0) + pv
        return carry
    lax.fori_loop(0, n_kt, attend_tile, 0)

    def normalised(h):
        a = acc_s[h * tq:(h + 1) * tq]
        return a / pltpu.roll(a, hd, 1)
    for s in range(n_heads // 2):
        pb, r = divmod(s, per_kv)
        o_lo = normalised((2 * pb) * per_kv + r)
        o_hi = normalised((2 * pb + 1) * per_kv + r)
        y_ref[0, :, s * LANES:(s + 1) * LANES] = jnp.where(low, o_lo, o_hi).astype(y_ref.dtype)


def _dsa_prompt(q, qi, sa, kk, k, v, *, n_heads, kv_heads, idx_heads, wi_lane, att_scale, idx_w_scale):
    b, l, dq = q.shape
    tq, tk = 128, 512
    tk = min(tk, l)
    tq = min(tq, l)
    assert l % tq == 0 and l % tk == 0 and kv_heads % 2 == 0 and idx_heads % 2 == 0
    topk = min(TOPK_MAX, l // 4)
    kvw = k.shape[-1]
    hd = LANES // 2
    ones = jnp.ones((b, l, hd), BF16)
    vx = jnp.concatenate([a for g in range(kv_heads)
                          for a in ((v[:, :, g * hd:(g + 1) * hd], ones) if g % 2 == 0 else
                                    (ones, v[:, :, g * hd:(g + 1) * hd]))], axis=-1)
    return pl.pallas_call(
        functools.partial(_dsa_prompt_kernel, tq=tq, tk=tk, n_heads=n_heads, kv_heads=kv_heads, idx_heads=idx_heads,
                          topk=topk, att_scale=att_scale, idx_w_scale=idx_w_scale, wi_lane=wi_lane, seq_len=l),
        grid=(b, l // tq),
        in_specs=[
            pl.BlockSpec((1, tq, dq), lambda bi, i: (bi, i, 0)),
            pl.BlockSpec((1, tq, qi.shape[-1]), lambda bi, i: (bi, i, 0)),
            pl.BlockSpec((1, tq, LANES), lambda bi, i: (bi, i, 0)),
            pl.BlockSpec((1, l, LANES), lambda bi, i: (bi, 0, 0), pipeline_mode=pl.Buffered(1)),
            pl.BlockSpec((1, l, kvw), lambda bi, i: (bi, 0, 0), pipeline_mode=pl.Buffered(1)),
            pl.BlockSpec((1, l, vx.shape[-1]), lambda bi, i: (bi, 0, 0), pipeline_mode=pl.Buffered(1)),
        ],
        out_specs=pl.BlockSpec((1, tq, dq), lambda bi, i: (bi, i, 0)),
        out_shape=jax.ShapeDtypeStruct((b, l, dq), BF16),
        scratch_shapes=[
            pltpu.VMEM((l // tk, tk, tq), I32),
            pltpu.VMEM((n_heads * tq, kvw), BF16),
            pltpu.VMEM((idx_heads * tq, LANES), BF16),
            pltpu.VMEM((idx_heads, tq), F32),
            pltpu.VMEM((n_heads * tq, LANES), F32),
            pltpu.VMEM((n_heads * tq, LANES), F32),
        ],
        compiler_params=_params("parallel", "arbitrary"),
        name="dsa_prompt",
    )(q, qi, sa, kk, k, vx)


def _dsa_sample_scores_kernel(pt_ref, qi_ref, sa_ref, kn_ref, *rest, pg, t_new, idx_heads, page,
                              idx_w_scale, wi_lane, n_keys):
    page_refs = rest[:pg]
    keys_ref, qi_s, w_s, kn_s = rest[pg:]
    i = pl.program_id(1)
    hd = LANES // 2
    n_steps = pl.num_programs(1)

    @pl.when(i == 0)
    def _():
        qif = qi_ref[0]
        sa = sa_ref[0]
        for h in range(idx_heads):
            qi_s[h * t_new:(h + 1) * t_new, :] = qif[:, h * LANES:h * LANES + hd].astype(F32)
            w_s[h * t_new:(h + 1) * t_new, :] = jnp.broadcast_to(
                sa[:, wi_lane + h:wi_lane + h + 1] * idx_w_scale, (t_new, LANES))
        kn_s[...] = jnp.zeros(kn_s.shape, F32)
        kn_s[0:t_new, :] = kn_ref[0][:, 0:hd].astype(F32)

    def scores(dots):
        ww = jnp.maximum(dots, 0.0) * w_s[...]
        sc = ww[0:t_new]
        for h in range(1, idx_heads):
            sc = sc + ww[h * t_new:(h + 1) * t_new]
        return sc

    always = jnp.full((t_new, page), True)
    kt = jnp.concatenate([page_refs[r][0, 0].astype(BF16) for r in range(pg)], axis=1)
    dots = _dot(qi_s[...].astype(BF16), kt)
    for r in range(pg):
        off = pl.multiple_of((i * pg + r) * page, page)
        keys_ref[0, :, pl.ds(off, page)] = _score_key(scores(dots[:, r * page:(r + 1) * page]), always)

    @pl.when(i == n_steps - 1)
    def _():
        ti = lax.broadcasted_iota(I32, (t_new, page), 0)
        ki = lax.broadcasted_iota(I32, (t_new, page), 1)
        past = n_keys - page
        dots_new = _dot_nt(qi_s[...].astype(BF16), kn_s[...].astype(BF16))
        keys_ref[0, :, past:n_keys] = _score_key(scores(dots_new), ki <= ti)


def _dsa_sample_threshold_kernel(keys_ref, thr_ref, cut_ref, *, rb, topk, n_keys):
    n_chunks = n_keys // LANES
    unroll = next(u for u in (4, 3, 2, 1) if n_chunks % u == 0)
    liota = lax.broadcasted_iota(I32, (rb, LANES), 1)
    for b in range(keys_ref.shape[0] // rb):
        rows = slice(b * rb, (b + 1) * rb)

        def count(pred):
            def body(cc, cnt):
                for u in range(unroll):
                    off = pl.multiple_of((cc * unroll + u) * LANES, LANES)
                    cnt = cnt + jnp.where(pred(keys_ref[rows, pl.ds(off, LANES)], off), 1.0, 0.0)
                return cnt
            cnt = lax.fori_loop(0, n_chunks // unroll, body, jnp.zeros((rb, LANES), F32))
            return jnp.sum(cnt, axis=1, keepdims=True)

        def count_ge(t):
            tb = jnp.broadcast_to(t, (rb, LANES))
            return count(lambda key, off: key >= tb)

        thr = _kth_largest_key(count_ge, (rb, 1), topk)
        tie = (count_ge(thr) > topk) & (thr > INT_MIN)
        thr_ref[rows, :] = jnp.broadcast_to(jnp.maximum(thr, jnp.int32(INT_MIN + 1)), (rb, LANES))
        cut_ref[rows, :] = jnp.full((rb, LANES), n_keys, I32)

        @pl.when(jnp.max(jnp.where(tie, 1, 0)) > 0)
        def _():
            need = topk - count(lambda key, off: key > thr)
            cut = _tie_cutoff(lambda p: count(lambda key, off: (key == thr) & (off + liota < p)), need, (rb, 1),
                              int(n_keys).bit_length())
            cut_ref[rows, :] = jnp.broadcast_to(jnp.where(tie, cut, n_keys), (rb, LANES))


def _dsa_sample_attend_kernel(pt_ref, q_ref, keys_ref, thr_ref, cut_ref, kn_ref, vn_ref, *rest, pg, pc, t_new,
                              n_heads, kv_heads, page, att_scale, n_keys):
    k_refs = rest[:pg]
    v_refs = rest[pg:2 * pg]
    y_ref, qpad_s, m_s, l_s, acc_s, kn_s, vn_s = rest[2 * pg:]
    i = pl.program_id(1)
    hd = LANES // 2
    per_kv = n_heads // kv_heads
    n_steps = pl.num_programs(1)
    lane = lax.broadcasted_iota(I32, (t_new, LANES), 1)
    low = lane < hd

    @pl.when(i == 0)
    def _():
        qf = q_ref[0].astype(F32) * att_scale
        zeros = jnp.zeros((t_new, LANES), F32)
        for h in range(n_heads):
            g, r = divmod(h, per_kv)
            slot = (g // 2) * per_kv + r
            sl = qf[:, slot * LANES:(slot + 1) * LANES]
            half = jnp.where(low, sl, 0.0) if g % 2 == 0 else jnp.where(low, 0.0, sl)
            row = [zeros] * (kv_heads // 2)
            row[g // 2] = half
            qpad_s[h * t_new:(h + 1) * t_new, :] = jnp.concatenate(row, axis=1)
        m_s[...] = jnp.full(m_s.shape, NEG, F32)
        l_s[...] = jnp.zeros(l_s.shape, F32)
        acc_s[...] = jnp.zeros(acc_s.shape, F32)
        kn_s[...] = jnp.zeros(kn_s.shape, F32)
        vn_s[...] = jnp.zeros(vn_s.shape, F32)
        kn_s[0:t_new, :] = kn_ref[0].astype(F32)
        vn_s[0:t_new, :] = vn_ref[0].astype(F32)

    thr = thr_ref[0]
    cut = cut_ref[0]

    def attend(s, key, pos0, pv, chain):
        n = key.shape[1]
        wide = lambda a: jnp.concatenate([a] * (n // LANES), axis=1)
        pos = pos0 + lax.broadcasted_iota(I32, (t_new, n), 1)
        sel = (key > wide(thr)) | ((key == wide(thr)) & (pos <= wide(cut)))
        bias = jnp.where(sel, 0.0, NEG)
        s = s + jnp.concatenate([bias] * n_heads, axis=0)
        m_old = m_s[chain]
        m_new = jnp.maximum(m_old, jnp.max(s, axis=1, keepdims=True))
        alpha = jnp.exp(m_old - m_new)
        p = jnp.exp(s - m_new)
        l_s[chain] = alpha * l_s[chain] + jnp.sum(p, axis=1, keepdims=True)
        m_s[chain] = m_new
        acc_s[chain] = acc_s[chain] * alpha + pv(p.astype(BF16))

    n_chains = m_s.shape[0]
    qb = qpad_s[...].astype(BF16)
    kvw = qb.shape[1]
    for c in range(pg // pc):
        off = pl.multiple_of((i * pg + c * pc) * page, page)
        kt = jnp.concatenate([k_refs[c * pc + r][0, 0].reshape(kvw, page).astype(BF16) for r in range(pc)], axis=1)
        vt = jnp.concatenate([v_refs[c * pc + r][0, 0].reshape(kvw, page).astype(BF16) for r in range(pc)], axis=1)
        attend(_dot(qb, kt), keys_ref[0, :, pl.ds(off, pc * page)], off, lambda p: _dot_nt(p, vt), c % n_chains)

    @pl.when(i == n_steps - 1)
    def _():
        attend(_dot_nt(qb, kn_s[...].astype(BF16)), keys_ref[0, :, n_keys - page:n_keys], n_keys - page,
               lambda p: _dot(p, vn_s[...].astype(BF16)), 0)
        m = m_s[0]
        for c in range(1, n_chains):
            m = jnp.maximum(m, m_s[c])
        l = sum(l_s[c] * jnp.exp(m_s[c] - m) for c in range(n_chains))
        acc = sum(acc_s[c] * jnp.exp(m_s[c] - m) for c in range(n_chains))
        o = acc / l
        for s in range(n_heads // 2):
            pb, r = divmod(s, per_kv)
            h_lo = (2 * pb) * per_kv + r
            h_hi = (2 * pb + 1) * per_kv + r
            o_lo = o[h_lo * t_new:(h_lo + 1) * t_new, pb * LANES:(pb + 1) * LANES]
            o_hi = o[h_hi * t_new:(h_hi + 1) * t_new, pb * LANES:(pb + 1) * LANES]
            y_ref[0, :, s * LANES:(s + 1) * LANES] = jnp.where(low, o_lo, o_hi).astype(y_ref.dtype)


def _pages_per_step(n_pages):
    for pg in (16, 8, 4, 2, 1):
        if n_pages % pg == 0:
            return pg


def _dsa_sample(q, qiw, sa, kk_new, k_new, v_new, ck, cv, ci, page_table, layer, *, n_heads, kv_heads, idx_heads,
                wi_lane, att_scale, idx_w_scale):
    db, t_new, dq = q.shape
    _, n_pool, kv_heads_, hd, page = ck.shape
    kvw = kv_heads_ * hd
    n_pages = page_table.shape[1]
    assert page == LANES and t_new % SUBLANES == 0 and t_new <= page and kv_heads_ == kv_heads and hd == LANES // 2
    pg = _pages_per_step(n_pages)
    n_chains = 2 if pg % 2 == 0 else 1
    pc = pg // n_chains
    n_steps = n_pages // pg
    n_keys = (n_pages + 1) * page
    topk = min(TOPK_MAX, (n_pages * page + t_new) // 4)
    idx_spec = lambda r: pl.BlockSpec((1, 1, hd, page), lambda b, i, pt: (layer, pt[b, i * pg + r], 0, 0))
    kv_spec = lambda r: pl.BlockSpec((1, 1, kv_heads, hd, page), lambda b, i, pt: (layer, pt[b, i * pg + r], 0, 0, 0))
    row_spec = lambda w: pl.BlockSpec((1, t_new, w), lambda b, i, pt: (b, 0, 0))

    keys = pl.pallas_call(
        functools.partial(_dsa_sample_scores_kernel, pg=pg, t_new=t_new, idx_heads=idx_heads, page=page,
                          idx_w_scale=idx_w_scale, wi_lane=wi_lane, n_keys=n_keys),
        grid_spec=pltpu.PrefetchScalarGridSpec(
            num_scalar_prefetch=1,
            grid=(db, n_steps),
            in_specs=[row_spec(qiw.shape[-1]), row_spec(LANES), row_spec(LANES)] + [idx_spec(r) for r in range(pg)],
            out_specs=row_spec(n_keys),
            scratch_shapes=[pltpu.VMEM((idx_heads * t_new, hd), F32), pltpu.VMEM((idx_heads * t_new, LANES), F32),
                            pltpu.VMEM((page, hd), F32)],
        ),
        out_shape=jax.ShapeDtypeStruct((db, t_new, n_keys), I32),
        compiler_params=_params("parallel", "arbitrary"),
        name="dsa_sample_scores",
    )(page_table, qiw, sa, kk_new, *([ci] * pg))

    n_rows = db * t_new
    rb = math.gcd(n_rows, LANES)
    thr, cut = pl.pallas_call(
        functools.partial(_dsa_sample_threshold_kernel, rb=rb, topk=topk, n_keys=n_keys),
        out_shape=[jax.ShapeDtypeStruct((n_rows, LANES), I32)] * 2,
        compiler_params=pltpu.CompilerParams(vmem_limit_bytes=VMEM_LIMIT_BYTES),
        name="dsa_sample_threshold",
    )(keys.reshape(n_rows, n_keys))
    thr = thr.reshape(db, t_new, LANES)
    cut = cut.reshape(db, t_new, LANES)

    return pl.pallas_call(
        functools.partial(_dsa_sample_attend_kernel, pg=pg, pc=pc, t_new=t_new, n_heads=n_heads, kv_heads=kv_heads,
                          page=page, att_scale=att_scale, n_keys=n_keys),
        grid_spec=pltpu.PrefetchScalarGridSpec(
            num_scalar_prefetch=1,
            grid=(db, n_steps),
            in_specs=[row_spec(dq), row_spec(n_keys), row_spec(LANES), row_spec(LANES), row_spec(kvw), row_spec(kvw)]
                     + [kv_spec(r) for r in range(pg)] * 2,
            out_specs=row_spec(dq),
            scratch_shapes=[pltpu.VMEM((n_heads * t_new, kvw), F32), pltpu.VMEM((n_chains, n_heads * t_new, 1), F32),
                            pltpu.VMEM((n_chains, n_heads * t_new, 1), F32),
                            pltpu.VMEM((n_chains, n_heads * t_new, kvw), F32),
                            pltpu.VMEM((page, kvw), F32), pltpu.VMEM((page, kvw), F32)],
        ),
        out_shape=jax.ShapeDtypeStruct((db, t_new, dq), BF16),
        compiler_params=_params("parallel", "arbitrary"),
        name="dsa_sample_attend",
    )(page_table, q, keys, thr, cut, k_new, v_new, *([ck] * pg), *([cv] * pg))


def _merge_kernel(x_ref, ys_ref, ya_ref, gs_ref, ga_ref, ps_ref, pa_ref, wo_ref, o_ref):
    merged = (_sigmoid(gs_ref[...]) * _dot(ys_ref[...], ps_ref[...])
              + _sigmoid(ga_ref[...]) * _dot(ya_ref[...], pa_ref[...]))
    o_ref[...] = x_ref[...] + _dot(merged.astype(BF16), wo_ref[...])


def _merge(x, y_ssd, y_attn, g_s, g_a, p_ssd, p_attn, w_out):
    t, d = x.shape
    tm = min(512, t)
    assert t % tm == 0
    rows = lambda w: pl.BlockSpec((tm, w), lambda i: (i, 0))
    full = lambda a: pl.BlockSpec(a.shape, lambda i: (0, 0))
    return pl.pallas_call(
        _merge_kernel,
        grid=(t // tm,),
        in_specs=[rows(d), rows(y_ssd.shape[1]), rows(y_attn.shape[1]), rows(d), rows(d),
                  full(p_ssd), full(p_attn), full(w_out)],
        out_specs=rows(d),
        out_shape=jax.ShapeDtypeStruct((t, d), F32),
        compiler_params=_params("parallel"),
        name="merge",
    )(x, y_ssd, y_attn, g_s, g_a, p_ssd, p_attn, w_out)


def _pair_slot_perm(n_heads, kv_heads, hd):
    per_kv = n_heads // kv_heads
    cols = []
    for s in range(n_heads // 2):
        pb, r = divmod(s, per_kv)
        for h in ((2 * pb) * per_kv + r, (2 * pb + 1) * per_kv + r):
            cols.extend(range(h * hd, (h + 1) * hd))
    return np.asarray(cols, np.int32)


def kernel(x_prompt, x_sample, cache_k, cache_v, cache_idx_k, state_ssm, state_conv, page_table, ffn1_norm, ffn1_w1, ffn1_w2, mix_norm, w_in, conv_w, conv_b, dt_bias, a_log, d_skip, ssd_norm, w_branch_ssd, w_branch_attn, w_out, ffn2_norm, ffn2_w1, ffn2_w2, final_norm):
    bp, seq, d_model = x_prompt.shape
    db, dseq, _ = x_sample.shape
    depth, n_pool, page, kv_heads, head_dim = cache_k.shape
    idx_dim = cache_idx_k.shape[-1]
    ssd_heads, ssd_hd, d_state = state_ssm.shape[2:]
    conv_dim = state_conv.shape[-1]
    d_inner = ssd_norm.shape[-1]
    n_groups = (conv_dim - d_inner) // (2 * d_state)
    attn_dim = w_branch_attn.shape[1]
    n_heads = attn_dim // head_dim
    kvw = kv_heads * head_dim
    d_proj = w_in.shape[-1]
    idx_heads = (d_proj - (d_inner + conv_dim + ssd_heads + attn_dim + 2 * kvw + idx_dim + 2 * d_model)) // (idx_dim + 1)
    assert head_dim == LANES // 2 and idx_dim == LANES // 2 and ssd_heads + idx_heads <= LANES
    att_scale = head_dim ** -0.5
    idx_w_scale = (idx_heads ** -0.5) * (idx_dim ** -0.5)
    sizes = (d_inner, conv_dim, ssd_heads, attn_dim, kvw, kvw, idx_heads * idx_dim, idx_dim, idx_heads, d_model, d_model)
    assert sum(sizes) == d_proj
    offs = np.concatenate([[0], np.cumsum(sizes)])
    perm = _pair_slot_perm(n_heads, kv_heads, head_dim)
    wi_lane = ssd_heads

    tp, ts = bp * seq, db * dseq
    yp = x_prompt.reshape(tp, d_model)
    ys = x_sample.reshape(ts, d_model)
    ck = jnp.transpose(cache_k, (0, 1, 3, 4, 2))
    cv = jnp.transpose(cache_v, (0, 1, 3, 4, 2))
    ci = jnp.transpose(cache_idx_k, (0, 1, 3, 2))
    dsa_kw = dict(n_heads=n_heads, kv_heads=kv_heads, idx_heads=idx_heads, wi_lane=wi_lane, att_scale=att_scale,
                  idx_w_scale=idx_w_scale)
    outs = {n: [] for n in ("kp", "vp", "ip", "sp", "cp", "ks", "vs", "is", "ss", "cs")}
    hist = conv_w.shape[1] - 1

    for l in range(depth):
        wl = w_in[l]
        col = lambda i: wl[:, offs[i]:offs[i + 1]]
        w_z, w_xbc, w_dt, w_q, w_k, w_v, w_qi, w_ki, w_wi, w_gs, w_ga = [col(i) for i in range(11)]
        w_sa = jnp.concatenate([w_dt, w_wi, jnp.zeros((d_model, LANES - ssd_heads - idx_heads), F32)], axis=1)
        w_qi_wide = jnp.pad(w_qi.reshape(d_model, idx_heads, idx_dim), ((0, 0), (0, 0), (0, LANES - idx_dim)))
        bf = lambda w: w.astype(BF16)
        wa = [bf(w_z), bf(w_xbc)]
        wb_common = [bf(w_q[:, perm]), bf(w_k), bf(w_v), bf(jnp.concatenate([w_ki, w_ki], axis=1)), bf(w_sa),
                     bf(w_gs), bf(w_ga)]
        dt_common = [(BF16,), (F32, BF16), (F32, BF16), (F32, BF16), (F32,), (F32,), (F32,)]
        f1w1, f1w2, f2w1, f2w2 = bf(ffn1_w1[l]), bf(ffn1_w2[l]), bf(ffn2_w1[l]), bf(ffn2_w2[l])
        p_ssd, p_attn, wo = bf(w_branch_ssd[l]), bf(w_branch_attn[l][perm, :]), bf(w_out[l])
        last = l == depth - 1

        def mixer(y, b, s, qi_weight, attend, buf, h0):
            z, xbc = _norm_linear(y, mix_norm[l], wa, [(F32,), (F32,)], 256)
            q, k, kb, v, vb, kk, kkb, sa, g_s, g_a, qi = _norm_linear(
                y, mix_norm[l], wb_common + [qi_weight], dt_common + [(BF16,)], 256)
            r3 = lambda a: a.reshape(b, s, a.shape[-1])
            y_ssd, h_fin = _ssd(r3(z), r3(xbc), r3(sa), buf, h0, conv_w[l], conv_b[l], dt_bias[l], a_log[l],
                                d_skip[l], ssd_norm[l], n_groups=n_groups, d_state=d_state)
            y_attn = attend(r3(q), r3(qi), r3(sa), r3(kkb), r3(kb), r3(vb))
            y = _merge(y, y_ssd.reshape(b * s, d_inner), y_attn.reshape(b * s, attn_dim), g_s, g_a, p_ssd, p_attn, wo)
            new_buf = r3(xbc)[:, s - hist:, :]
            return (y, new_buf, h_fin, k.reshape(b, s, kv_heads, head_dim), v.reshape(b, s, kv_heads, head_dim),
                    r3(kk)[:, :, :idx_dim])

        yp = _ffn(yp, ffn1_norm[l], f1w1, f1w2)
        ys = _ffn(ys, ffn1_norm[l], f1w1, f1w2)

        yp, cbp, hfp, kp, vp, kip = mixer(
            yp, bp, seq, bf(w_qi), functools.partial(_dsa_prompt, **dsa_kw),
            jnp.zeros((bp, hist, conv_dim), F32), jnp.zeros((bp, ssd_heads, ssd_hd, d_state), F32))
        att_s = functools.partial(_dsa_sample, ck=ck, cv=cv, ci=ci, page_table=page_table, layer=l, **dsa_kw)
        ys, cbs, hfs, kss, vss, kis = mixer(
            ys, db, dseq, bf(w_qi_wide.reshape(d_model, idx_heads * LANES)), att_s, state_conv[l], state_ssm[l])

        pg_ = final_norm if last else None
        yp = _ffn(yp, ffn2_norm[l], f2w1, f2w2, pg_)
        ys = _ffn(ys, ffn2_norm[l], f2w1, f2w2, pg_)
        for n, a in zip(("kp", "vp", "ip", "sp", "cp", "ks", "vs", "is", "ss", "cs"),
                        (kp, vp, kip, hfp, cbp, kss, vss, kis, hfs, cbs)):
            outs[n].append(a)

    st = lambda n: jnp.stack(outs[n])
    return (yp.reshape(bp, seq, d_model), ys.reshape(db, dseq, d_model),
            st("kp"), st("vp"), st("ip"), st("sp"), st("cp"),
            st("ks"), st("vs"), st("is"), st("ss"), st("cs"))
```

```python
import functools
import math

import jax
import jax.numpy as jnp
import numpy as np
from jax import lax
from jax.experimental import pallas as pl
from jax.experimental.pallas import tpu as pltpu

F32 = jnp.float32
BF16 = jnp.bfloat16
I32 = jnp.int32

EPS = 1e-6
SSD_CHUNK = 128
TOPK_MAX = 256
LANES = 128
SUBLANES = 8
VMEM_LIMIT_BYTES = 56 * 1024 * 1024
NEG = -1e30
INT_MIN = -(2 ** 31)

ROW_TILE = 512
PROJ_ROW_TILE = 256
FF_TILE_MAX = 1536
Q_TILE = 128
KEY_TILE = 512
PAGES_PER_STEP_MAX = 16


def _params(*sem):
    return pltpu.CompilerParams(dimension_semantics=sem, vmem_limit_bytes=VMEM_LIMIT_BYTES)


def _sigmoid(x):
    return 1.0 / (1.0 + jnp.exp(-x))


def _rms(x, g):
    return x * lax.rsqrt(jnp.mean(x * x, axis=-1, keepdims=True) + EPS) * g


def _dot(a, b):
    return jnp.dot(a, b, preferred_element_type=F32)


def _dot_nt(a, b):
    return lax.dot_general(a, b, (((1,), (1,)), ((), ())), preferred_element_type=F32)


def _split2(x):
    hi = x.astype(BF16)
    lo = (x - hi.astype(F32)).astype(BF16)
    return hi, lo


def _split3(x):
    hi = x.astype(BF16)
    r = x - hi.astype(F32)
    mid = r.astype(BF16)
    lo = (r - mid.astype(F32)).astype(BF16)
    return hi, mid, lo


def _ffn_kernel(*refs, post_norm):
    if post_norm:
        x_ref, g_ref, wa_ref, wb_ref, w2_ref, pg_ref, o_ref, h_s, acc_s = refs
    else:
        x_ref, g_ref, wa_ref, wb_ref, w2_ref, o_ref, h_s, acc_s = refs
    f = pl.program_id(1)

    @pl.when(f == 0)
    def _():
        h_s[...] = _rms(x_ref[...], g_ref[...]).astype(BF16)
        acc_s[...] = jnp.zeros_like(acc_s)

    h = h_s[...]
    a = _dot(h, wa_ref[...])
    b = _dot(h, wb_ref[...])
    u = (a * _sigmoid(a) * b).astype(BF16)
    acc_s[...] += _dot(u, w2_ref[...])

    @pl.when(f == pl.num_programs(1) - 1)
    def _():
        y = x_ref[...] + 0.5 * acc_s[...]
        if post_norm:
            y = _rms(y, pg_ref[...])
        o_ref[...] = y


def _ff_tile(d_ff):
    best = None
    for t in range(LANES, d_ff + 1, LANES):
        if d_ff % t == 0 and t <= FF_TILE_MAX:
            best = t
    assert best is not None, d_ff
    return best


def _ffn(x, g, w1, w2, post_gain=None):
    t, d = x.shape
    d_ff = w2.shape[0]
    tm = min(ROW_TILE, t)
    tf = _ff_tile(d_ff)
    nf = d_ff // tf
    assert t % tm == 0
    post_norm = post_gain is not None
    in_specs = [
        pl.BlockSpec((tm, d), lambda i, f: (i, 0)),
        pl.BlockSpec((1, d), lambda i, f: (0, 0)),
        pl.BlockSpec((d, tf), lambda i, f: (0, f)),
        pl.BlockSpec((d, tf), lambda i, f: (0, f + nf)),
        pl.BlockSpec((tf, d), lambda i, f: (f, 0)),
    ]
    args = [x, g.reshape(1, d), w1, w1, w2]
    if post_norm:
        in_specs.append(pl.BlockSpec((1, d), lambda i, f: (0, 0)))
        args.append(post_gain.reshape(1, d))
    return pl.pallas_call(
        functools.partial(_ffn_kernel, post_norm=post_norm),
        grid=(t // tm, nf),
        in_specs=in_specs,
        out_specs=pl.BlockSpec((tm, d), lambda i, f: (i, 0)),
        out_shape=jax.ShapeDtypeStruct((t, d), F32),
        scratch_shapes=[pltpu.VMEM((tm, d), BF16), pltpu.VMEM((tm, d), F32)],
        compiler_params=_params("parallel", "arbitrary"),
        name="ffn",
    )(*args)


def _norm_linear_kernel(*refs, out_dtypes):
    n_w = len(out_dtypes)
    x_ref, g_ref = refs[:2]
    w_refs = refs[2:2 + n_w]
    o_refs = list(refs[2 + n_w:])
    h = _rms(x_ref[...], g_ref[...]).astype(BF16)
    for w_ref, dts in zip(w_refs, out_dtypes):
        r = _dot(h, w_ref[...])
        for dt in dts:
            o_refs.pop(0)[...] = r.astype(dt)


def _norm_linear(x, g, weights, out_dtypes, tm):
    t, d = x.shape
    tm = min(tm, t)
    assert t % tm == 0
    in_specs = [pl.BlockSpec((tm, d), lambda i: (i, 0)), pl.BlockSpec((1, d), lambda i: (0, 0))]
    out_specs, out_shape = [], []
    for w, dts in zip(weights, out_dtypes):
        n = w.shape[1]
        in_specs.append(pl.BlockSpec((d, n), lambda i: (0, 0)))
        for dt in dts:
            out_specs.append(pl.BlockSpec((tm, n), lambda i: (i, 0)))
            out_shape.append(jax.ShapeDtypeStruct((t, n), dt))
    return pl.pallas_call(
        functools.partial(_norm_linear_kernel, out_dtypes=tuple(tuple(d_) for d_ in out_dtypes)),
        grid=(t // tm,),
        in_specs=in_specs,
        out_specs=out_specs,
        out_shape=out_shape,
        compiler_params=_params("parallel"),
        name="norm_linear",
    )(x, g.reshape(1, d), *weights)


def _ssd_kernel(z_ref, xbc_ref, dt_ref, buf_ref, h0_ref, cw_ref, cb_ref, dtb_ref, alog_ref, dsk_ref, ng_ref, e_ref,
                y_ref, hfin_ref, xp_s, ht_s, *, qin, d_inner, n_groups, d_state, conv_w):
    q = SSD_CHUNK
    c = pl.program_id(1)
    hp_blocks = d_inner // LANES
    gw = d_inner // n_groups
    assert d_state == LANES and gw % LANES == 0
    pad = SUBLANES
    hist = conv_w - 1

    @pl.when(c == 0)
    def _():
        xp_s[0:pad, :] = buf_ref[0]
        if qin < q:
            xp_s[pad + qin:pad + q, :] = jnp.zeros((q - qin, xp_s.shape[1]), F32)
        for i in range(hp_blocks):
            ht_s[:, i * LANES:(i + 1) * LANES] = h0_ref[0, i * LANES:(i + 1) * LANES, :].T

    xp_s[pad:pad + qin, :] = xbc_ref[0]
    x_cur = xp_s[pad:pad + q, :]
    x_prev = xp_s[0:pad, :]
    row8 = lax.broadcasted_iota(I32, (pad, 1), 0)
    acc = cb_ref[...] + x_cur * cw_ref[hist:hist + 1, :]
    for i in range(hist):
        s = hist - i
        rolled = pltpu.roll(x_cur, s, 0)
        head = jnp.where(row8 < s, pltpu.roll(x_prev, s, 0), rolled[0:pad])
        acc = acc + jnp.concatenate([head, rolled[pad:]], axis=0) * cw_ref[i:i + 1, :]
    xc = acc * _sigmoid(acc)
    tail = xp_s[pad + qin - hist:pad + qin, :]
    xp_s[pad - hist:pad, :] = tail

    xs = xc[:, :d_inner]
    bm = xc[:, d_inner:d_inner + n_groups * d_state]
    cm = xc[:, d_inner + n_groups * d_state:]

    dt_raw = dt_ref[0] + dtb_ref[...]
    dt = jnp.maximum(dt_raw, 0.0) + jnp.log1p(jnp.exp(-jnp.abs(dt_raw)))
    if qin < q:
        dt = jnp.concatenate([dt, jnp.zeros((q - qin, LANES), F32)], axis=0)
    la = dt * (-jnp.exp(alog_ref[...]))

    ri = lax.broadcasted_iota(I32, (q, q), 0)
    ci = lax.broadcasted_iota(I32, (q, q), 1)
    causal = ri >= ci
    tril = jnp.where(causal, 1.0, 0.0).astype(BF16)
    eye = jnp.where(ri == ci, 1.0, 0.0).astype(BF16)
    a_cs = sum(_dot(tril, p) for p in _split3(la))
    a_cs_t = sum(_dot_nt(eye, p) for p in _split3(a_cs))
    dec = jnp.exp(a_cs[q - 1:q, :] - a_cs)
    eac = jnp.exp(a_cs)
    stacked = jnp.concatenate([dt, dec, eac], axis=0)
    expd = sum(_dot(p, e_ref[...]) for p in _split2(stacked))
    dt_e, dec_e, eac_e = expd[0:q], expd[q:2 * q], expd[2 * q:3 * q]

    x = xs * dt_e
    xb = x.astype(BF16)
    xd = (x * dec_e).astype(BF16)
    lane = lax.broadcasted_iota(I32, (q, LANES), 1)
    hd = LANES // 2
    heads_per_group = gw // hd

    y_parts = []
    for g in range(n_groups):
        gs = slice(g * gw, (g + 1) * gw)
        cg = cm[:, g * d_state:(g + 1) * d_state].astype(BF16)
        bg = bm[:, g * d_state:(g + 1) * d_state]
        cb = _dot_nt(cg, bg.astype(BF16))
        y_off = _dot(cg, ht_s[:, gs].astype(BF16))
        pair_parts = []
        for p in range(heads_per_group // 2):
            xp = xb[:, g * gw + p * LANES:g * gw + (p + 1) * LANES]
            res = []
            for k in range(2):
                j = g * heads_per_group + 2 * p + k
                diff = a_cs[:, j:j + 1] - a_cs_t[j:j + 1, :]
                lm = jnp.exp(jnp.where(causal, diff, NEG))
                res.append(_dot((cb * lm).astype(BF16), xp))
            pair_parts.append(jnp.where(lane < hd, res[0], res[1]))
        y_diag = jnp.concatenate(pair_parts, axis=1)
        y_parts.append(y_diag + y_off * eac_e[:, gs])
        st = _dot(bg.T.astype(BF16), xd[:, gs])
        ht_s[:, gs] = ht_s[:, gs] * eac_e[q - 1:q, gs] + st

    zz = z_ref[0]
    outs = []
    for g in range(n_groups):
        gs = slice(g * gw, (g + 1) * gw)
        yv = (y_parts[g][:qin] + dsk_ref[:, gs] * xs[:qin, gs]) * (zz[:, gs] * _sigmoid(zz[:, gs]))
        ms = jnp.mean(yv * yv, axis=-1, keepdims=True)
        outs.append(yv * lax.rsqrt(ms + EPS) * ng_ref[:, gs])
    y_ref[0] = jnp.concatenate(outs, axis=1).astype(y_ref.dtype)

    @pl.when(c == pl.num_programs(1) - 1)
    def _():
        for i in range(hp_blocks):
            hfin_ref[0, i * LANES:(i + 1) * LANES, :] = ht_s[:, i * LANES:(i + 1) * LANES].T


def _ssd(z, xbc, dtp, buf, h0, conv_w, conv_b, dt_bias, a_log, d_skip, norm_g, *, n_groups, d_state):
    b, l, d_inner = z.shape
    conv_dim = xbc.shape[-1]
    n_heads, p_dim, n_state = h0.shape[1:]
    width = conv_w.shape[0]
    hist = width - 1
    assert n_heads <= LANES and p_dim == LANES // 2 and n_state == d_state and hist <= SUBLANES
    qin = math.gcd(l, SSD_CHUNK)
    assert qin % SUBLANES == 0 and qin >= hist
    nc = l // qin
    assert nc == 1 or qin == SSD_CHUNK
    hp = n_heads * p_dim
    buf8 = jnp.pad(buf, ((0, 0), (SUBLANES - hist, 0), (0, 0)))
    cw8 = jnp.pad(conv_w, ((0, SUBLANES - width), (0, 0)))
    pad1 = lambda v: jnp.pad(v.reshape(1, -1), ((0, 0), (0, LANES - n_heads)))
    expand = (np.arange(LANES)[:, None] == (np.arange(d_inner)[None, :] // p_dim)).astype(np.float32)
    full = lambda shape: pl.BlockSpec(shape, lambda i, c: (0,) * len(shape))
    y, hfin = pl.pallas_call(
        functools.partial(_ssd_kernel, qin=qin, d_inner=d_inner, n_groups=n_groups, d_state=d_state, conv_w=width),
        grid=(b, nc),
        in_specs=[
            pl.BlockSpec((1, qin, d_inner), lambda i, c: (i, c, 0)),
            pl.BlockSpec((1, qin, conv_dim), lambda i, c: (i, c, 0)),
            pl.BlockSpec((1, qin, LANES), lambda i, c: (i, c, 0)),
            pl.BlockSpec((1, SUBLANES, conv_dim), lambda i, c: (i, 0, 0)),
            pl.BlockSpec((1, hp, n_state), lambda i, c: (i, 0, 0)),
            full((SUBLANES, conv_dim)), full((1, conv_dim)), full((1, LANES)), full((1, LANES)),
            full((1, d_inner)), full((1, d_inner)), full((LANES, d_inner)),
        ],
        out_specs=[
            pl.BlockSpec((1, qin, d_inner), lambda i, c: (i, c, 0)),
            pl.BlockSpec((1, hp, n_state), lambda i, c: (i, 0, 0)),
        ],
        out_shape=[jax.ShapeDtypeStruct((b, l, d_inner), BF16), jax.ShapeDtypeStruct((b, hp, n_state), F32)],
        scratch_shapes=[pltpu.VMEM((SUBLANES + SSD_CHUNK, conv_dim), F32), pltpu.VMEM((n_state, hp), F32)],
        compiler_params=_params("parallel", "arbitrary"),
        name="ssd",
    )(z, xbc, dtp, buf8, h0.reshape(b, hp, n_state), cw8, conv_b.reshape(1, -1), pad1(dt_bias), pad1(a_log),
      jnp.repeat(d_skip, p_dim).reshape(1, -1), norm_g.reshape(1, -1), jnp.asarray(expand, BF16))
    return y, hfin.reshape(b, n_heads, p_dim, n_state)


def _score_key(sc, valid):
    bits = lax.bitcast_convert_type(sc + 0.0, I32)
    key = jnp.where(bits < 0, bits ^ jnp.int32(0x7FFFFFFF), bits)
    return jnp.where(valid, key, jnp.int32(INT_MIN))


def _kth_largest_key(count_ge, shape, k):
    def body(b, st):
        t, c_t = st
        cand = t + lax.shift_left(jnp.int32(1), jnp.int32(31) - b)
        c = count_ge(cand)
        return jnp.where(c >= k, cand, t), jnp.where(c >= k, c, c_t)
    return lax.fori_loop(0, 32, body, (jnp.full(shape, INT_MIN, I32), jnp.zeros(shape, F32)))


def _tie_cutoff(count_tie_before, need, shape, n_bits):
    def body(b, p):
        bit = lax.shift_left(jnp.int32(1), jnp.int32(n_bits - 1) - b)
        p_c = p | bit
        return jnp.where(count_tie_before(p_c) <= need - 1, p_c, p)
    return lax.fori_loop(0, n_bits, body, jnp.zeros(shape, I32))


def _dsa_prompt_kernel(q_ref, qi_ref, sa_ref, kk_ref, k_ref, vx_ref, y_ref,
                       keys_s, qpad_s, qipad_s, w_s, m_s, acc_s,
                       *, tq, tk, n_heads, kv_heads, idx_heads, topk, att_scale, idx_w_scale, wi_lane, seq_len):
    i = pl.program_id(1)
    hd = LANES // 2
    per_kv = n_heads // kv_heads
    n_kt = (i * tq + tq - 1) // tk + 1
    lane = lax.broadcasted_iota(I32, (tq, LANES), 1)
    low = lane < hd

    qf = q_ref[0].astype(F32) * (att_scale * math.log2(math.e))
    zeros = jnp.zeros((tq, LANES), F32)
    for h in range(n_heads):
        g, r = divmod(h, per_kv)
        slot = (g // 2) * per_kv + r
        sl = qf[:, slot * LANES:(slot + 1) * LANES]
        half = jnp.where(low, sl, 0.0) if g % 2 == 0 else jnp.where(low, 0.0, sl)
        row = [zeros] * (kv_heads // 2)
        row[g // 2] = half
        qpad_s[h * tq:(h + 1) * tq, :] = jnp.concatenate(row, axis=1).astype(BF16)
    qif = qi_ref[0].astype(F32)
    for h in range(idx_heads):
        sl = qif[:, (h // 2) * LANES:(h // 2 + 1) * LANES]
        half = jnp.where(low, sl, 0.0) if h % 2 == 0 else jnp.where(low, 0.0, sl)
        qipad_s[h * tq:(h + 1) * tq, :] = half.astype(BF16)
    ri = lax.broadcasted_iota(I32, (LANES, LANES), 0)
    ci = lax.broadcasted_iota(I32, (LANES, LANES), 1)
    eye = jnp.where(ri == ci, 1.0, 0.0).astype(BF16)
    sa_t = sum(_dot_nt(eye, p) for p in _split3(sa_ref[0]))
    w_s[...] = sa_t[wi_lane:wi_lane + idx_heads, :] * idx_w_scale

    qpos = i * tq + lax.broadcasted_iota(I32, (tk, tq), 1)
    kiota = lax.broadcasted_iota(I32, (tk, tq), 0)
    siota = lax.broadcasted_iota(I32, (SUBLANES, tq), 0)

    tiles_per_step = 2

    def score_tiles(jj, carry):
        for u in range(tiles_per_step):
            j = jj * tiles_per_step + u
            kt = kk_ref[0, pl.ds(pl.multiple_of(j * tk, tk), tk), :]
            r = _dot_nt(kt, qipad_s[...])
            sc = jnp.zeros((tk, tq), F32)
            for h in range(idx_heads):
                sc = sc + jnp.maximum(r[:, h * tq:(h + 1) * tq], 0.0) * w_s[h:h + 1, :]
            keys_s[j] = _score_key(sc, j * tk + kiota <= qpos)
        return carry
    lax.fori_loop(0, (n_kt + tiles_per_step - 1) // tiles_per_step, score_tiles, 0)

    n_acc = 4

    def count(pred):
        def body(j, cnts):
            cnts = list(cnts)
            for g in range(tk // SUBLANES):
                hit = pred(keys_s[j, g * SUBLANES:(g + 1) * SUBLANES, :], j * tk + g * SUBLANES)
                cnts[g % n_acc] = cnts[g % n_acc] + jnp.where(hit, 1.0, 0.0)
            return tuple(cnts)
        cnts = lax.fori_loop(0, n_kt, body, (jnp.zeros((SUBLANES, tq), F32),) * n_acc)
        return jnp.sum(sum(cnts), axis=0, keepdims=True)

    def count_ge(t):
        tb = jnp.broadcast_to(t, (SUBLANES, tq))
        return count(lambda key, base: key >= tb)

    thr, c_ge = _kth_largest_key(count_ge, (1, tq), topk)
    tie = c_ge > topk

    @pl.when(jnp.max(jnp.where(tie, 1, 0)) > 0)
    def _():
        need = topk - count(lambda key, base: key > thr)
        cut = _tie_cutoff(lambda p: count(lambda key, base: (key == thr) & (base + siota < p)),
                          need, (1, tq), int(seq_len).bit_length())
        def drop(j, carry):
            key = keys_s[j]
            keys_s[j] = jnp.where(tie & (key == thr) & (j * tk + kiota > cut), jnp.int32(INT_MIN), key)
            return carry
        lax.fori_loop(0, n_kt, drop, 0)

    thr_sel = jnp.maximum(thr, jnp.int32(INT_MIN + 1))

    n_chunks = tk // LANES
    m_s[...] = jnp.full(m_s.shape, NEG, F32)
    acc_s[...] = jnp.zeros(acc_s.shape, F32)

    def attend_tile(j, carry):
        ks = pl.ds(pl.multiple_of(j * tk, tk), tk)
        s_all = _dot_nt(qpad_s[...], k_ref[0, ks, :])
        bias = [jnp.where(keys_s[j, c * LANES:(c + 1) * LANES, :] >= thr_sel, 0.0, NEG).T for c in range(n_chunks)]
        for g in range(kv_heads):
            ps, alphas = [], []
            for r in range(per_kv):
                rows = slice((g * per_kv + r) * tq, (g * per_kv + r + 1) * tq)
                s = [s_all[rows, c * LANES:(c + 1) * LANES] + bias[c] for c in range(n_chunks)]
                smax = s[0]
                for c in range(1, n_chunks):
                    smax = jnp.maximum(smax, s[c])
                m_old = m_s[rows]
                m_new = jnp.maximum(m_old, jnp.max(smax, axis=1, keepdims=True))
                m_s[rows] = m_new
                alphas.append(jnp.exp2(m_old - m_new))
                ps.append(jnp.concatenate([jnp.exp2(s[c] - m_new) for c in range(n_chunks)], axis=1).astype(BF16))
            grows = slice(g * per_kv * tq, (g + 1) * per_kv * tq)
            pv = _dot(jnp.concatenate(ps, axis=0), vx_ref[0, ks, g * LANES:(g + 1) * LANES])
            acc_s[grows] = acc_s[grows] * jnp.concatenate(alphas, axis=0) + pv
        return carry
    lax.fori_loop(0, n_kt, attend_tile, 0)

    def normalised(h):
        a = acc_s[h * tq:(h + 1) * tq]
        return a / pltpu.roll(a, hd, 1)
    for s in range(n_heads // 2):
        pb, r = divmod(s, per_kv)
        o_lo = normalised((2 * pb) * per_kv + r)
        o_hi = normalised((2 * pb + 1) * per_kv + r)
        y_ref[0, :, s * LANES:(s + 1) * LANES] = jnp.where(low, o_lo, o_hi).astype(y_ref.dtype)


def _dsa_prompt(q, qi, sa, kk, k, v, *, n_heads, kv_heads, idx_heads, wi_lane, att_scale, idx_w_scale):
    b, l, dq = q.shape
    tq, tk = min(Q_TILE, l), min(KEY_TILE, l // 2)
    assert l % tq == 0 and l % (2 * tk) == 0 and kv_heads % 2 == 0 and idx_heads % 2 == 0
    assert wi_lane % SUBLANES == 0 and idx_heads == SUBLANES
    topk = min(TOPK_MAX, l // 4)
    kvw = k.shape[-1]
    hd = LANES // 2
    ones = jnp.ones((b, l, hd), BF16)
    vx = jnp.concatenate([a for g in range(kv_heads)
                          for a in ((v[:, :, g * hd:(g + 1) * hd], ones) if g % 2 == 0 else
                                    (ones, v[:, :, g * hd:(g + 1) * hd]))], axis=-1)
    return pl.pallas_call(
        functools.partial(_dsa_prompt_kernel, tq=tq, tk=tk, n_heads=n_heads, kv_heads=kv_heads, idx_heads=idx_heads,
                          topk=topk, att_scale=att_scale, idx_w_scale=idx_w_scale, wi_lane=wi_lane, seq_len=l),
        grid=(b, l // tq),
        in_specs=[
            pl.BlockSpec((1, tq, dq), lambda bi, i: (bi, i, 0)),
            pl.BlockSpec((1, tq, qi.shape[-1]), lambda bi, i: (bi, i, 0)),
            pl.BlockSpec((1, tq, LANES), lambda bi, i: (bi, i, 0)),
            pl.BlockSpec((1, l, LANES), lambda bi, i: (bi, 0, 0), pipeline_mode=pl.Buffered(1)),
            pl.BlockSpec((1, l, kvw), lambda bi, i: (bi, 0, 0), pipeline_mode=pl.Buffered(1)),
            pl.BlockSpec((1, l, vx.shape[-1]), lambda bi, i: (bi, 0, 0), pipeline_mode=pl.Buffered(1)),
        ],
        out_specs=pl.BlockSpec((1, tq, dq), lambda bi, i: (bi, i, 0)),
        out_shape=jax.ShapeDtypeStruct((b, l, dq), BF16),
        scratch_shapes=[
            pltpu.VMEM((l // tk, tk, tq), I32),
            pltpu.VMEM((n_heads * tq, kvw), BF16),
            pltpu.VMEM((idx_heads * tq, LANES), BF16),
            pltpu.VMEM((idx_heads, tq), F32),
            pltpu.VMEM((n_heads * tq, LANES), F32),
            pltpu.VMEM((n_heads * tq, LANES), F32),
        ],
        compiler_params=_params("parallel", "arbitrary"),
        name="dsa_prompt",
    )(q, qi, sa, kk, k, vx)


def _dsa_sample_scores_kernel(pt_ref, qi_ref, sa_ref, kn_ref, *rest, pg, t_new, idx_heads, page,
                              idx_w_scale, wi_lane, n_keys):
    page_refs = rest[:pg]
    keys_ref, qi_s, w_s, kn_s = rest[pg:]
    i = pl.program_id(1)
    hd = LANES // 2
    n_steps = pl.num_programs(1)

    @pl.when(i == 0)
    def _():
        qif = qi_ref[0]
        sa = sa_ref[0]
        for h in range(idx_heads):
            qi_s[h * t_new:(h + 1) * t_new, :] = qif[:, h * LANES:h * LANES + hd].astype(F32)
            w_s[h * t_new:(h + 1) * t_new, :] = jnp.broadcast_to(
                sa[:, wi_lane + h:wi_lane + h + 1] * idx_w_scale, (t_new, LANES))
        kn_s[...] = jnp.zeros(kn_s.shape, F32)
        kn_s[0:t_new, :] = kn_ref[0][:, 0:hd].astype(F32)

    def scores(dots):
        ww = jnp.maximum(dots, 0.0) * w_s[...]
        sc = ww[0:t_new]
        for h in range(1, idx_heads):
            sc = sc + ww[h * t_new:(h + 1) * t_new]
        return sc

    always = jnp.full((t_new, page), True)
    kt = jnp.concatenate([page_refs[r][0, 0].astype(BF16) for r in range(pg)], axis=1)
    dots = _dot(qi_s[...].astype(BF16), kt)
    for r in range(pg):
        off = pl.multiple_of((i * pg + r) * page, page)
        keys_ref[0, :, pl.ds(off, page)] = _score_key(scores(dots[:, r * page:(r + 1) * page]), always)

    @pl.when(i == n_steps - 1)
    def _():
        ti = lax.broadcasted_iota(I32, (t_new, page), 0)
        ki = lax.broadcasted_iota(I32, (t_new, page), 1)
        past = n_keys - page
        dots_new = _dot_nt(qi_s[...].astype(BF16), kn_s[...].astype(BF16))
        keys_ref[0, :, past:n_keys] = _score_key(scores(dots_new), ki <= ti)


def _dsa_sample_threshold_kernel(keys_ref, thr_ref, cut_ref, *, rb, topk, n_keys):
    n_chunks = n_keys // LANES
    unroll = next(u for u in (4, 3, 2, 1) if n_chunks % u == 0)
    liota = lax.broadcasted_iota(I32, (rb, LANES), 1)
    for b in range(keys_ref.shape[0] // rb):
        rows = slice(b * rb, (b + 1) * rb)

        def count(pred):
            def body(cc, cnt):
                for u in range(unroll):
                    off = pl.multiple_of((cc * unroll + u) * LANES, LANES)
                    cnt = cnt + jnp.where(pred(keys_ref[rows, pl.ds(off, LANES)], off), 1.0, 0.0)
                return cnt
            cnt = lax.fori_loop(0, n_chunks // unroll, body, jnp.zeros((rb, LANES), F32))
            return jnp.sum(cnt, axis=1, keepdims=True)

        def count_ge(t):
            tb = jnp.broadcast_to(t, (rb, LANES))
            return count(lambda key, off: key >= tb)

        thr, c_ge = _kth_largest_key(count_ge, (rb, 1), topk)
        tie = c_ge > topk
        thr_ref[rows, :] = jnp.broadcast_to(jnp.maximum(thr, jnp.int32(INT_MIN + 1)), (rb, LANES))
        cut_ref[rows, :] = jnp.full((rb, LANES), n_keys, I32)

        @pl.when(jnp.max(jnp.where(tie, 1, 0)) > 0)
        def _():
            need = topk - count(lambda key, off: key > thr)
            cut = _tie_cutoff(lambda p: count(lambda key, off: (key == thr) & (off + liota < p)), need, (rb, 1),
                              int(n_keys).bit_length())
            cut_ref[rows, :] = jnp.broadcast_to(jnp.where(tie, cut, n_keys), (rb, LANES))


def _dsa_sample_attend_kernel(pt_ref, q_ref, keys_ref, thr_ref, cut_ref, kn_ref, vn_ref, *rest, pg, pc, t_new,
                              n_heads, kv_heads, page, att_scale, n_keys):
    k_refs = rest[:pg]
    v_refs = rest[pg:2 * pg]
    y_ref, qpad_s, m_s, l_s, acc_s, kn_s, vn_s = rest[2 * pg:]
    i = pl.program_id(1)
    hd = LANES // 2
    per_kv = n_heads // kv_heads
    n_steps = pl.num_programs(1)
    lane = lax.broadcasted_iota(I32, (t_new, LANES), 1)
    low = lane < hd

    @pl.when(i == 0)
    def _():
        qf = q_ref[0].astype(F32) * att_scale
        zeros = jnp.zeros((t_new, LANES), F32)
        for h in range(n_heads):
            g, r = divmod(h, per_kv)
            slot = (g // 2) * per_kv + r
            sl = qf[:, slot * LANES:(slot + 1) * LANES]
            half = jnp.where(low, sl, 0.0) if g % 2 == 0 else jnp.where(low, 0.0, sl)
            row = [zeros] * (kv_heads // 2)
            row[g // 2] = half
            qpad_s[h * t_new:(h + 1) * t_new, :] = jnp.concatenate(row, axis=1)
        m_s[...] = jnp.full(m_s.shape, NEG, F32)
        l_s[...] = jnp.zeros(l_s.shape, F32)
        acc_s[...] = jnp.zeros(acc_s.shape, F32)
        kn_s[...] = jnp.zeros(kn_s.shape, F32)
        vn_s[...] = jnp.zeros(vn_s.shape, F32)
        kn_s[0:t_new, :] = kn_ref[0].astype(F32)
        vn_s[0:t_new, :] = vn_ref[0].astype(F32)

    thr = thr_ref[0]
    cut = cut_ref[0]

    def attend(s, key, pos0, pv, chain):
        n = key.shape[1]
        wide = lambda a: jnp.concatenate([a] * (n // LANES), axis=1)
        pos = pos0 + lax.broadcasted_iota(I32, (t_new, n), 1)
        sel = (key > wide(thr)) | ((key == wide(thr)) & (pos <= wide(cut)))
        bias = jnp.where(sel, 0.0, NEG)
        s = s + jnp.concatenate([bias] * n_heads, axis=0)
        m_old = m_s[chain]
        m_new = jnp.maximum(m_old, jnp.max(s, axis=1, keepdims=True))
        alpha = jnp.exp(m_old - m_new)
        p = jnp.exp(s - m_new)
        l_s[chain] = alpha * l_s[chain] + jnp.sum(p, axis=1, keepdims=True)
        m_s[chain] = m_new
        acc_s[chain] = acc_s[chain] * alpha + pv(p.astype(BF16))

    n_chains = m_s.shape[0]
    qb = qpad_s[...].astype(BF16)
    kvw = qb.shape[1]
    for c in range(pg // pc):
        off = pl.multiple_of((i * pg + c * pc) * page, page)
        kt = jnp.concatenate([k_refs[c * pc + r][0, 0].reshape(kvw, page).astype(BF16) for r in range(pc)], axis=1)
        vt = jnp.concatenate([v_refs[c * pc + r][0, 0].reshape(kvw, page).astype(BF16) for r in range(pc)], axis=1)
        attend(_dot(qb, kt), keys_ref[0, :, pl.ds(off, pc * page)], off, lambda p: _dot_nt(p, vt), c % n_chains)

    @pl.when(i == n_steps - 1)
    def _():
        attend(_dot_nt(qb, kn_s[...].astype(BF16)), keys_ref[0, :, n_keys - page:n_keys], n_keys - page,
               lambda p: _dot(p, vn_s[...].astype(BF16)), 0)
        m = m_s[0]
        for c in range(1, n_chains):
            m = jnp.maximum(m, m_s[c])
        l = sum(l_s[c] * jnp.exp(m_s[c] - m) for c in range(n_chains))
        acc = sum(acc_s[c] * jnp.exp(m_s[c] - m) for c in range(n_chains))
        o = acc / l
        for s in range(n_heads // 2):
            pb, r = divmod(s, per_kv)
            h_lo = (2 * pb) * per_kv + r
            h_hi = (2 * pb + 1) * per_kv + r
            o_lo = o[h_lo * t_new:(h_lo + 1) * t_new, pb * LANES:(pb + 1) * LANES]
            o_hi = o[h_hi * t_new:(h_hi + 1) * t_new, pb * LANES:(pb + 1) * LANES]
            y_ref[0, :, s * LANES:(s + 1) * LANES] = jnp.where(low, o_lo, o_hi).astype(y_ref.dtype)


def _pages_per_step(n_pages):
    return math.gcd(n_pages, PAGES_PER_STEP_MAX)


def _dsa_sample(q, qiw, sa, kk_new, k_new, v_new, ck, cv, ci, page_table, layer, *, n_heads, kv_heads, idx_heads,
                wi_lane, att_scale, idx_w_scale):
    db, t_new, dq = q.shape
    _, n_pool, kv_heads_, hd, page = ck.shape
    kvw = kv_heads_ * hd
    n_pages = page_table.shape[1]
    assert page == LANES and t_new % SUBLANES == 0 and t_new <= page and kv_heads_ == kv_heads and hd == LANES // 2
    pg = _pages_per_step(n_pages)
    n_chains = 2 if pg % 2 == 0 else 1
    pc = pg // n_chains
    n_steps = n_pages // pg
    n_keys = (n_pages + 1) * page
    topk = min(TOPK_MAX, (n_pages * page + t_new) // 4)
    idx_spec = lambda r: pl.BlockSpec((1, 1, hd, page), lambda b, i, pt: (layer, pt[b, i * pg + r], 0, 0))
    kv_spec = lambda r: pl.BlockSpec((1, 1, kv_heads, hd, page), lambda b, i, pt: (layer, pt[b, i * pg + r], 0, 0, 0))
    row_spec = lambda w: pl.BlockSpec((1, t_new, w), lambda b, i, pt: (b, 0, 0))

    keys = pl.pallas_call(
        functools.partial(_dsa_sample_scores_kernel, pg=pg, t_new=t_new, idx_heads=idx_heads, page=page,
                          idx_w_scale=idx_w_scale, wi_lane=wi_lane, n_keys=n_keys),
        grid_spec=pltpu.PrefetchScalarGridSpec(
            num_scalar_prefetch=1,
            grid=(db, n_steps),
            in_specs=[row_spec(qiw.shape[-1]), row_spec(LANES), row_spec(LANES)] + [idx_spec(r) for r in range(pg)],
            out_specs=row_spec(n_keys),
            scratch_shapes=[pltpu.VMEM((idx_heads * t_new, hd), F32), pltpu.VMEM((idx_heads * t_new, LANES), F32),
                            pltpu.VMEM((page, hd), F32)],
        ),
        out_shape=jax.ShapeDtypeStruct((db, t_new, n_keys), I32),
        compiler_params=_params("parallel", "arbitrary"),
        name="dsa_sample_scores",
    )(page_table, qiw, sa, kk_new, *([ci] * pg))

    n_rows = db * t_new
    rb = math.gcd(n_rows, LANES)
    thr, cut = pl.pallas_call(
        functools.partial(_dsa_sample_threshold_kernel, rb=rb, topk=topk, n_keys=n_keys),
        out_shape=[jax.ShapeDtypeStruct((n_rows, LANES), I32)] * 2,
        compiler_params=pltpu.CompilerParams(vmem_limit_bytes=VMEM_LIMIT_BYTES),
        name="dsa_sample_threshold",
    )(keys.reshape(n_rows, n_keys))
    thr = thr.reshape(db, t_new, LANES)
    cut = cut.reshape(db, t_new, LANES)

    return pl.pallas_call(
        functools.partial(_dsa_sample_attend_kernel, pg=pg, pc=pc, t_new=t_new, n_heads=n_heads, kv_heads=kv_heads,
                          page=page, att_scale=att_scale, n_keys=n_keys),
        grid_spec=pltpu.PrefetchScalarGridSpec(
            num_scalar_prefetch=1,
            grid=(db, n_steps),
            in_specs=[row_spec(dq), row_spec(n_keys), row_spec(LANES), row_spec(LANES), row_spec(kvw), row_spec(kvw)]
                     + [kv_spec(r) for r in range(pg)] * 2,
            out_specs=row_spec(dq),
            scratch_shapes=[pltpu.VMEM((n_heads * t_new, kvw), F32), pltpu.VMEM((n_chains, n_heads * t_new, 1), F32),
                            pltpu.VMEM((n_chains, n_heads * t_new, 1), F32),
                            pltpu.VMEM((n_chains, n_heads * t_new, kvw), F32),
                            pltpu.VMEM((page, kvw), F32), pltpu.VMEM((page, kvw), F32)],
        ),
        out_shape=jax.ShapeDtypeStruct((db, t_new, dq), BF16),
        compiler_params=_params("parallel", "arbitrary"),
        name="dsa_sample_attend",
    )(page_table, q, keys, thr, cut, k_new, v_new, *([ck] * pg), *([cv] * pg))


def _merge_kernel(x_ref, ys_ref, ya_ref, gs_ref, ga_ref, ps_ref, pa_ref, wo_ref, o_ref):
    merged = (_sigmoid(gs_ref[...]) * _dot(ys_ref[...], ps_ref[...])
              + _sigmoid(ga_ref[...]) * _dot(ya_ref[...], pa_ref[...]))
    o_ref[...] = x_ref[...] + _dot(merged.astype(BF16), wo_ref[...])


def _merge(x, y_ssd, y_attn, g_s, g_a, p_ssd, p_attn, w_out):
    t, d = x.shape
    tm = min(ROW_TILE, t)
    assert t % tm == 0
    rows = lambda w: pl.BlockSpec((tm, w), lambda i: (i, 0))
    full = lambda a: pl.BlockSpec(a.shape, lambda i: (0, 0))
    return pl.pallas_call(
        _merge_kernel,
        grid=(t // tm,),
        in_specs=[rows(d), rows(y_ssd.shape[1]), rows(y_attn.shape[1]), rows(d), rows(d),
                  full(p_ssd), full(p_attn), full(w_out)],
        out_specs=rows(d),
        out_shape=jax.ShapeDtypeStruct((t, d), F32),
        compiler_params=_params("parallel"),
        name="merge",
    )(x, y_ssd, y_attn, g_s, g_a, p_ssd, p_attn, w_out)


def _pair_slot_perm(n_heads, kv_heads, hd):
    per_kv = n_heads // kv_heads
    cols = []
    for s in range(n_heads // 2):
        pb, r = divmod(s, per_kv)
        for h in ((2 * pb) * per_kv + r, (2 * pb + 1) * per_kv + r):
            cols.extend(range(h * hd, (h + 1) * hd))
    return np.asarray(cols, np.int32)


def kernel(x_prompt, x_sample, cache_k, cache_v, cache_idx_k, state_ssm, state_conv, page_table, ffn1_norm, ffn1_w1, ffn1_w2, mix_norm, w_in, conv_w, conv_b, dt_bias, a_log, d_skip, ssd_norm, w_branch_ssd, w_branch_attn, w_out, ffn2_norm, ffn2_w1, ffn2_w2, final_norm):
    bp, seq, d_model = x_prompt.shape
    db, dseq, _ = x_sample.shape
    depth, n_pool, page, kv_heads, head_dim = cache_k.shape
    idx_dim = cache_idx_k.shape[-1]
    ssd_heads, ssd_hd, d_state = state_ssm.shape[2:]
    conv_dim = state_conv.shape[-1]
    d_inner = ssd_norm.shape[-1]
    n_groups = (conv_dim - d_inner) // (2 * d_state)
    attn_dim = w_branch_attn.shape[1]
    n_heads = attn_dim // head_dim
    kvw = kv_heads * head_dim
    d_proj = w_in.shape[-1]
    idx_heads = (d_proj - (d_inner + conv_dim + ssd_heads + attn_dim + 2 * kvw + idx_dim + 2 * d_model)) // (idx_dim + 1)
    assert head_dim == LANES // 2 and idx_dim == LANES // 2 and ssd_heads + idx_heads <= LANES
    att_scale = head_dim ** -0.5
    idx_w_scale = (idx_heads ** -0.5) * (idx_dim ** -0.5)
    sizes = (d_inner, conv_dim, ssd_heads, attn_dim, kvw, kvw, idx_heads * idx_dim, idx_dim, idx_heads, d_model, d_model)
    assert sum(sizes) == d_proj
    offs = np.concatenate([[0], np.cumsum(sizes)])
    perm = _pair_slot_perm(n_heads, kv_heads, head_dim)
    wi_lane = ssd_heads

    tp, ts = bp * seq, db * dseq
    yp = x_prompt.reshape(tp, d_model)
    ys = x_sample.reshape(ts, d_model)
    ck = jnp.transpose(cache_k, (0, 1, 3, 4, 2))
    cv = jnp.transpose(cache_v, (0, 1, 3, 4, 2))
    ci = jnp.transpose(cache_idx_k, (0, 1, 3, 2))
    dsa_kw = dict(n_heads=n_heads, kv_heads=kv_heads, idx_heads=idx_heads, wi_lane=wi_lane, att_scale=att_scale,
                  idx_w_scale=idx_w_scale)
    outs = {n: [] for n in ("kp", "vp", "ip", "sp", "cp", "ks", "vs", "is", "ss", "cs")}
    hist = conv_w.shape[1] - 1

    for l in range(depth):
        wl = w_in[l]
        col = lambda i: wl[:, offs[i]:offs[i + 1]]
        w_z, w_xbc, w_dt, w_q, w_k, w_v, w_qi, w_ki, w_wi, w_gs, w_ga = [col(i) for i in range(11)]
        w_sa = jnp.concatenate([w_dt, w_wi, jnp.zeros((d_model, LANES - ssd_heads - idx_heads), F32)], axis=1)
        w_qi_wide = jnp.pad(w_qi.reshape(d_model, idx_heads, idx_dim), ((0, 0), (0, 0), (0, LANES - idx_dim)))
        bf = lambda w: w.astype(BF16)
        wa = [bf(w_z), bf(w_xbc)]
        wb_common = [bf(w_q[:, perm]), bf(w_k), bf(w_v), bf(jnp.concatenate([w_ki, w_ki], axis=1)), bf(w_sa),
                     bf(w_gs), bf(w_ga)]
        dt_common = [(BF16,), (F32, BF16), (F32, BF16), (F32, BF16), (F32,), (F32,), (F32,)]
        f1w1, f1w2, f2w1, f2w2 = bf(ffn1_w1[l]), bf(ffn1_w2[l]), bf(ffn2_w1[l]), bf(ffn2_w2[l])
        p_ssd, p_attn, wo = bf(w_branch_ssd[l]), bf(w_branch_attn[l][perm, :]), bf(w_out[l])
        last = l == depth - 1

        def mixer(y, b, s, qi_weight, attend, buf, h0):
            z, xbc = _norm_linear(y, mix_norm[l], wa, [(F32,), (F32,)], PROJ_ROW_TILE)
            q, k, kb, v, vb, kk, kkb, sa, g_s, g_a, qi = _norm_linear(
                y, mix_norm[l], wb_common + [qi_weight], dt_common + [(BF16,)], PROJ_ROW_TILE)
            r3 = lambda a: a.reshape(b, s, a.shape[-1])
            y_ssd, h_fin = _ssd(r3(z), r3(xbc), r3(sa), buf, h0, conv_w[l], conv_b[l], dt_bias[l], a_log[l],
                                d_skip[l], ssd_norm[l], n_groups=n_groups, d_state=d_state)
            y_attn = attend(r3(q), r3(qi), r3(sa), r3(kkb), r3(kb), r3(vb))
            y = _merge(y, y_ssd.reshape(b * s, d_inner), y_attn.reshape(b * s, attn_dim), g_s, g_a, p_ssd, p_attn, wo)
            new_buf = r3(xbc)[:, s - hist:, :]
            return (y, new_buf, h_fin, k.reshape(b, s, kv_heads, head_dim), v.reshape(b, s, kv_heads, head_dim),
                    r3(kk)[:, :, :idx_dim])

        yp = _ffn(yp, ffn1_norm[l], f1w1, f1w2)
        ys = _ffn(ys, ffn1_norm[l], f1w1, f1w2)

        yp, cbp, hfp, kp, vp, kip = mixer(
            yp, bp, seq, bf(w_qi), functools.partial(_dsa_prompt, **dsa_kw),
            jnp.zeros((bp, hist, conv_dim), F32), jnp.zeros((bp, ssd_heads, ssd_hd, d_state), F32))
        att_s = functools.partial(_dsa_sample, ck=ck, cv=cv, ci=ci, page_table=page_table, layer=l, **dsa_kw)
        ys, cbs, hfs, kss, vss, kis = mixer(
            ys, db, dseq, bf(w_qi_wide.reshape(d_model, idx_heads * LANES)), att_s, state_conv[l], state_ssm[l])

        pg_ = final_norm if last else None
        yp = _ffn(yp, ffn2_norm[l], f2w1, f2w2, pg_)
        ys = _ffn(ys, ffn2_norm[l], f2w1, f2w2, pg_)
        for n, a in zip(("kp", "vp", "ip", "sp", "cp", "ks", "vs", "is", "ss", "cs"),
                        (kp, vp, kip, hfp, cbp, kss, vss, kis, hfs, cbs)):
            outs[n].append(a)

    st = lambda n: jnp.stack(outs[n])
    return (yp.reshape(bp, seq, d_model), ys.reshape(db, dseq, d_model),
            st("kp"), st("vp"), st("ip"), st("sp"), st("cp"),
            st("ks"), st("vs"), st("is"), st("ss"), st("cs"))
```

```python
import functools
import math

import jax
import jax.numpy as jnp
import numpy as np
from jax import lax
from jax.experimental import pallas as pl
from jax.experimental.pallas import tpu as pltpu

F32 = jnp.float32
BF16 = jnp.bfloat16
I32 = jnp.int32

EPS = 1e-6
SSD_CHUNK = 128
TOPK_MAX = 256
LANES = 128
SUBLANES = 8
VMEM_LIMIT_BYTES = 56 * 1024 * 1024
NEG = -1e30
INT_MIN = -(2 ** 31)

ROW_TILE = 512
PROJ_ROW_TILE = 256
FF_TILE_MAX = 1536
Q_TILE = 128
KEY_TILE = 512
PAGES_PER_STEP_MAX = 32


def _params(*sem):
    return pltpu.CompilerParams(dimension_semantics=sem, vmem_limit_bytes=VMEM_LIMIT_BYTES)


def _sigmoid(x):
    return 1.0 / (1.0 + jnp.exp(-x))


def _rms(x, g):
    return x * lax.rsqrt(jnp.mean(x * x, axis=-1, keepdims=True) + EPS) * g


def _dot(a, b):
    return jnp.dot(a, b, preferred_element_type=F32)


def _dot_nt(a, b):
    return lax.dot_general(a, b, (((1,), (1,)), ((), ())), preferred_element_type=F32)


def _split2(x):
    hi = x.astype(BF16)
    lo = (x - hi.astype(F32)).astype(BF16)
    return hi, lo


def _split3(x):
    hi = x.astype(BF16)
    r = x - hi.astype(F32)
    mid = r.astype(BF16)
    lo = (r - mid.astype(F32)).astype(BF16)
    return hi, mid, lo


def _ffn_kernel(*refs, post_norm):
    if post_norm:
        x_ref, g_ref, wa_ref, wb_ref, w2_ref, pg_ref, o_ref, h_s, acc_s = refs
    else:
        x_ref, g_ref, wa_ref, wb_ref, w2_ref, o_ref, h_s, acc_s = refs
    f = pl.program_id(1)

    @pl.when(f == 0)
    def _():
        h_s[...] = _rms(x_ref[...], g_ref[...]).astype(BF16)
        acc_s[...] = jnp.zeros_like(acc_s)

    h = h_s[...]
    a = _dot(h, wa_ref[...])
    b = _dot(h, wb_ref[...])
    u = (a * _sigmoid(a) * b).astype(BF16)
    acc_s[...] += _dot(u, w2_ref[...])

    @pl.when(f == pl.num_programs(1) - 1)
    def _():
        y = x_ref[...] + 0.5 * acc_s[...]
        if post_norm:
            y = _rms(y, pg_ref[...])
        o_ref[...] = y


def _ff_tile(d_ff):
    best = None
    for t in range(LANES, d_ff + 1, LANES):
        if d_ff % t == 0 and t <= FF_TILE_MAX:
            best = t
    assert best is not None, d_ff
    return best


def _ffn(x, g, w1, w2, post_gain=None):
    t, d = x.shape
    d_ff = w2.shape[0]
    tm = min(ROW_TILE, t)
    tf = _ff_tile(d_ff)
    nf = d_ff // tf
    assert t % tm == 0
    post_norm = post_gain is not None
    in_specs = [
        pl.BlockSpec((tm, d), lambda i, f: (i, 0)),
        pl.BlockSpec((1, d), lambda i, f: (0, 0)),
        pl.BlockSpec((d, tf), lambda i, f: (0, f)),
        pl.BlockSpec((d, tf), lambda i, f: (0, f + nf)),
        pl.BlockSpec((tf, d), lambda i, f: (f, 0)),
    ]
    args = [x, g.reshape(1, d), w1, w1, w2]
    if post_norm:
        in_specs.append(pl.BlockSpec((1, d), lambda i, f: (0, 0)))
        args.append(post_gain.reshape(1, d))
    return pl.pallas_call(
        functools.partial(_ffn_kernel, post_norm=post_norm),
        grid=(t // tm, nf),
        in_specs=in_specs,
        out_specs=pl.BlockSpec((tm, d), lambda i, f: (i, 0)),
        out_shape=jax.ShapeDtypeStruct((t, d), F32),
        scratch_shapes=[pltpu.VMEM((tm, d), BF16), pltpu.VMEM((tm, d), F32)],
        compiler_params=_params("parallel", "arbitrary"),
        name="ffn",
    )(*args)


def _norm_linear_kernel(*refs, out_dtypes):
    n_w = len(out_dtypes)
    x_ref, g_ref = refs[:2]
    w_refs = refs[2:2 + n_w]
    o_refs = list(refs[2 + n_w:])
    h = _rms(x_ref[...], g_ref[...]).astype(BF16)
    for w_ref, dts in zip(w_refs, out_dtypes):
        r = _dot(h, w_ref[...])
        for dt in dts:
            o_refs.pop(0)[...] = r.astype(dt)


def _norm_linear(x, g, weights, out_dtypes, tm):
    t, d = x.shape
    tm = min(tm, t)
    assert t % tm == 0
    in_specs = [pl.BlockSpec((tm, d), lambda i: (i, 0)), pl.BlockSpec((1, d), lambda i: (0, 0))]
    out_specs, out_shape = [], []
    for w, dts in zip(weights, out_dtypes):
        n = w.shape[1]
        in_specs.append(pl.BlockSpec((d, n), lambda i: (0, 0)))
        for dt in dts:
            out_specs.append(pl.BlockSpec((tm, n), lambda i: (i, 0)))
            out_shape.append(jax.ShapeDtypeStruct((t, n), dt))
    return pl.pallas_call(
        functools.partial(_norm_linear_kernel, out_dtypes=tuple(tuple(d_) for d_ in out_dtypes)),
        grid=(t // tm,),
        in_specs=in_specs,
        out_specs=out_specs,
        out_shape=out_shape,
        compiler_params=_params("parallel"),
        name="norm_linear",
    )(x, g.reshape(1, d), *weights)


def _ssd_kernel(z_ref, xbc_ref, dt_ref, buf_ref, h0_ref, cw_ref, cb_ref, dtb_ref, alog_ref, dsk_ref, ng_ref, e_ref,
                y_ref, hfin_ref, xp_s, ht_s, *, qin, d_inner, n_groups, d_state, conv_w):
    q = SSD_CHUNK
    c = pl.program_id(1)
    hp_blocks = d_inner // LANES
    gw = d_inner // n_groups
    assert d_state == LANES and gw % LANES == 0
    pad = SUBLANES
    hist = conv_w - 1

    @pl.when(c == 0)
    def _():
        xp_s[0:pad, :] = buf_ref[0]
        if qin < q:
            xp_s[pad + qin:pad + q, :] = jnp.zeros((q - qin, xp_s.shape[1]), F32)
        for i in range(hp_blocks):
            ht_s[:, i * LANES:(i + 1) * LANES] = h0_ref[0, i * LANES:(i + 1) * LANES, :].T

    xp_s[pad:pad + qin, :] = xbc_ref[0]
    x_cur = xp_s[pad:pad + q, :]
    x_prev = xp_s[0:pad, :]
    row8 = lax.broadcasted_iota(I32, (pad, 1), 0)
    acc = cb_ref[...] + x_cur * cw_ref[hist:hist + 1, :]
    for i in range(hist):
        s = hist - i
        rolled = pltpu.roll(x_cur, s, 0)
        head = jnp.where(row8 < s, pltpu.roll(x_prev, s, 0), rolled[0:pad])
        acc = acc + jnp.concatenate([head, rolled[pad:]], axis=0) * cw_ref[i:i + 1, :]
    xc = acc * _sigmoid(acc)
    tail = xp_s[pad + qin - hist:pad + qin, :]
    xp_s[pad - hist:pad, :] = tail

    xs = xc[:, :d_inner]
    bm = xc[:, d_inner:d_inner + n_groups * d_state]
    cm = xc[:, d_inner + n_groups * d_state:]

    dt_raw = dt_ref[0] + dtb_ref[...]
    dt = jnp.maximum(dt_raw, 0.0) + jnp.log1p(jnp.exp(-jnp.abs(dt_raw)))
    if qin < q:
        dt = jnp.concatenate([dt, jnp.zeros((q - qin, LANES), F32)], axis=0)
    la = dt * (-jnp.exp(alog_ref[...]))

    ri = lax.broadcasted_iota(I32, (q, q), 0)
    ci = lax.broadcasted_iota(I32, (q, q), 1)
    causal = ri >= ci
    tril = jnp.where(causal, 1.0, 0.0).astype(BF16)
    eye = jnp.where(ri == ci, 1.0, 0.0).astype(BF16)
    a_cs = sum(_dot(tril, p) for p in _split3(la))
    a_cs_t = sum(_dot_nt(eye, p) for p in _split3(a_cs))
    dec = jnp.exp(a_cs[q - 1:q, :] - a_cs)
    eac = jnp.exp(a_cs)
    stacked = jnp.concatenate([dt, dec, eac], axis=0)
    expd = sum(_dot(p, e_ref[...]) for p in _split2(stacked))
    dt_e, dec_e, eac_e = expd[0:q], expd[q:2 * q], expd[2 * q:3 * q]

    x = xs * dt_e
    xb = x.astype(BF16)
    xd = (x * dec_e).astype(BF16)
    lane = lax.broadcasted_iota(I32, (q, LANES), 1)
    hd = LANES // 2
    heads_per_group = gw // hd

    y_parts = []
    for g in range(n_groups):
        gs = slice(g * gw, (g + 1) * gw)
        cg = cm[:, g * d_state:(g + 1) * d_state].astype(BF16)
        bg = bm[:, g * d_state:(g + 1) * d_state]
        cb = _dot_nt(cg, bg.astype(BF16))
        y_off = _dot(cg, ht_s[:, gs].astype(BF16))
        pair_parts = []
        for p in range(heads_per_group // 2):
            xp = xb[:, g * gw + p * LANES:g * gw + (p + 1) * LANES]
            res = []
            for k in range(2):
                j = g * heads_per_group + 2 * p + k
                diff = a_cs[:, j:j + 1] - a_cs_t[j:j + 1, :]
                lm = jnp.exp(jnp.where(causal, diff, NEG))
                res.append(_dot((cb * lm).astype(BF16), xp))
            pair_parts.append(jnp.where(lane < hd, res[0], res[1]))
        y_diag = jnp.concatenate(pair_parts, axis=1)
        y_parts.append(y_diag + y_off * eac_e[:, gs])
        st = _dot(bg.T.astype(BF16), xd[:, gs])
        ht_s[:, gs] = ht_s[:, gs] * eac_e[q - 1:q, gs] + st

    zz = z_ref[0]
    outs = []
    for g in range(n_groups):
        gs = slice(g * gw, (g + 1) * gw)
        yv = (y_parts[g][:qin] + dsk_ref[:, gs] * xs[:qin, gs]) * (zz[:, gs] * _sigmoid(zz[:, gs]))
        ms = jnp.mean(yv * yv, axis=-1, keepdims=True)
        outs.append(yv * lax.rsqrt(ms + EPS) * ng_ref[:, gs])
    y_ref[0] = jnp.concatenate(outs, axis=1).astype(y_ref.dtype)

    @pl.when(c == pl.num_programs(1) - 1)
    def _():
        for i in range(hp_blocks):
            hfin_ref[0, i * LANES:(i + 1) * LANES, :] = ht_s[:, i * LANES:(i + 1) * LANES].T


def _ssd(z, xbc, dtp, buf, h0, conv_w, conv_b, dt_bias, a_log, d_skip, norm_g, *, n_groups, d_state):
    b, l, d_inner = z.shape
    conv_dim = xbc.shape[-1]
    n_heads, p_dim, n_state = h0.shape[1:]
    width = conv_w.shape[0]
    hist = width - 1
    assert n_heads <= LANES and p_dim == LANES // 2 and n_state == d_state and hist <= SUBLANES
    qin = math.gcd(l, SSD_CHUNK)
    assert qin % SUBLANES == 0 and qin >= hist
    nc = l // qin
    assert nc == 1 or qin == SSD_CHUNK
    hp = n_heads * p_dim
    buf8 = jnp.pad(buf, ((0, 0), (SUBLANES - hist, 0), (0, 0)))
    cw8 = jnp.pad(conv_w, ((0, SUBLANES - width), (0, 0)))
    pad1 = lambda v: jnp.pad(v.reshape(1, -1), ((0, 0), (0, LANES - n_heads)))
    expand = (np.arange(LANES)[:, None] == (np.arange(d_inner)[None, :] // p_dim)).astype(np.float32)
    full = lambda shape: pl.BlockSpec(shape, lambda i, c: (0,) * len(shape))
    y, hfin = pl.pallas_call(
        functools.partial(_ssd_kernel, qin=qin, d_inner=d_inner, n_groups=n_groups, d_state=d_state, conv_w=width),
        grid=(b, nc),
        in_specs=[
            pl.BlockSpec((1, qin, d_inner), lambda i, c: (i, c, 0)),
            pl.BlockSpec((1, qin, conv_dim), lambda i, c: (i, c, 0)),
            pl.BlockSpec((1, qin, LANES), lambda i, c: (i, c, 0)),
            pl.BlockSpec((1, SUBLANES, conv_dim), lambda i, c: (i, 0, 0)),
            pl.BlockSpec((1, hp, n_state), lambda i, c: (i, 0, 0)),
            full((SUBLANES, conv_dim)), full((1, conv_dim)), full((1, LANES)), full((1, LANES)),
            full((1, d_inner)), full((1, d_inner)), full((LANES, d_inner)),
        ],
        out_specs=[
            pl.BlockSpec((1, qin, d_inner), lambda i, c: (i, c, 0)),
            pl.BlockSpec((1, hp, n_state), lambda i, c: (i, 0, 0)),
        ],
        out_shape=[jax.ShapeDtypeStruct((b, l, d_inner), BF16), jax.ShapeDtypeStruct((b, hp, n_state), F32)],
        scratch_shapes=[pltpu.VMEM((SUBLANES + SSD_CHUNK, conv_dim), F32), pltpu.VMEM((n_state, hp), F32)],
        compiler_params=_params("parallel", "arbitrary"),
        name="ssd",
    )(z, xbc, dtp, buf8, h0.reshape(b, hp, n_state), cw8, conv_b.reshape(1, -1), pad1(dt_bias), pad1(a_log),
      jnp.repeat(d_skip, p_dim).reshape(1, -1), norm_g.reshape(1, -1), jnp.asarray(expand, BF16))
    return y, hfin.reshape(b, n_heads, p_dim, n_state)


def _score_key(sc, valid):
    bits = lax.bitcast_convert_type(sc + 0.0, I32)
    key = jnp.where(bits < 0, bits ^ jnp.int32(0x7FFFFFFF), bits)
    return jnp.where(valid, key, jnp.int32(INT_MIN))


def _kth_largest_key(count_ge, shape, k):
    def body(b, st):
        t, c_t = st
        cand = t + lax.shift_left(jnp.int32(1), jnp.int32(31) - b)
        c = count_ge(cand)
        return jnp.where(c >= k, cand, t), jnp.where(c >= k, c, c_t)
    return lax.fori_loop(0, 32, body, (jnp.full(shape, INT_MIN, I32), jnp.zeros(shape, F32)))


def _tie_cutoff(count_tie_before, need, shape, n_bits):
    def body(b, p):
        bit = lax.shift_left(jnp.int32(1), jnp.int32(n_bits - 1) - b)
        p_c = p | bit
        return jnp.where(count_tie_before(p_c) <= need - 1, p_c, p)
    return lax.fori_loop(0, n_bits, body, jnp.zeros(shape, I32))


def _dsa_prompt_kernel(q_ref, qi_ref, sa_ref, kk_ref, k_ref, vx_ref, y_ref,
                       keys_s, qpad_s, qipad_s, w_s, m_s, acc_s,
                       *, tq, tk, n_heads, kv_heads, idx_heads, topk, att_scale, idx_w_scale, wi_lane, seq_len):
    i = pl.program_id(1)
    hd = LANES // 2
    per_kv = n_heads // kv_heads
    n_kt = (i * tq + tq - 1) // tk + 1
    lane = lax.broadcasted_iota(I32, (tq, LANES), 1)
    low = lane < hd

    qf = q_ref[0].astype(F32) * (att_scale * math.log2(math.e))
    zeros = jnp.zeros((tq, LANES), F32)
    for h in range(n_heads):
        g, r = divmod(h, per_kv)
        slot = (g // 2) * per_kv + r
        sl = qf[:, slot * LANES:(slot + 1) * LANES]
        half = jnp.where(low, sl, 0.0) if g % 2 == 0 else jnp.where(low, 0.0, sl)
        row = [zeros] * (kv_heads // 2)
        row[g // 2] = half
        qpad_s[h * tq:(h + 1) * tq, :] = jnp.concatenate(row, axis=1).astype(BF16)
    qif = qi_ref[0].astype(F32)
    for h in range(idx_heads):
        sl = qif[:, (h // 2) * LANES:(h // 2 + 1) * LANES]
        half = jnp.where(low, sl, 0.0) if h % 2 == 0 else jnp.where(low, 0.0, sl)
        qipad_s[h * tq:(h + 1) * tq, :] = half.astype(BF16)
    ri = lax.broadcasted_iota(I32, (LANES, LANES), 0)
    ci = lax.broadcasted_iota(I32, (LANES, LANES), 1)
    eye = jnp.where(ri == ci, 1.0, 0.0).astype(BF16)
    sa_t = sum(_dot_nt(eye, p) for p in _split3(sa_ref[0]))
    w_s[...] = sa_t[wi_lane:wi_lane + idx_heads, :] * idx_w_scale

    qpos = i * tq + lax.broadcasted_iota(I32, (tk, tq), 1)
    kiota = lax.broadcasted_iota(I32, (tk, tq), 0)
    siota = lax.broadcasted_iota(I32, (SUBLANES, tq), 0)

    tiles_per_step = 2

    def score_tiles(jj, carry):
        for u in range(tiles_per_step):
            j = jj * tiles_per_step + u
            kt = kk_ref[0, pl.ds(pl.multiple_of(j * tk, tk), tk), :]
            r = _dot_nt(kt, qipad_s[...])
            sc = jnp.zeros((tk, tq), F32)
            for h in range(idx_heads):
                sc = sc + jnp.maximum(r[:, h * tq:(h + 1) * tq], 0.0) * w_s[h:h + 1, :]
            keys_s[j] = _score_key(sc, j * tk + kiota <= qpos)
        return carry
    lax.fori_loop(0, (n_kt + tiles_per_step - 1) // tiles_per_step, score_tiles, 0)

    n_acc = 4

    def count(pred):
        def body(j, cnts):
            cnts = list(cnts)
            for g in range(tk // SUBLANES):
                hit = pred(keys_s[j, g * SUBLANES:(g + 1) * SUBLANES, :], j * tk + g * SUBLANES)
                cnts[g % n_acc] = cnts[g % n_acc] + jnp.where(hit, 1.0, 0.0)
            return tuple(cnts)
        cnts = lax.fori_loop(0, n_kt, body, (jnp.zeros((SUBLANES, tq), F32),) * n_acc)
        return jnp.sum(sum(cnts), axis=0, keepdims=True)

    def count_ge(t):
        tb = jnp.broadcast_to(t, (SUBLANES, tq))
        return count(lambda key, base: key >= tb)

    thr, c_ge = _kth_largest_key(count_ge, (1, tq), topk)
    tie = c_ge > topk

    @pl.when(jnp.max(jnp.where(tie, 1, 0)) > 0)
    def _():
        need = topk - count(lambda key, base: key > thr)
        cut = _tie_cutoff(lambda p: count(lambda key, base: (key == thr) & (base + siota < p)),
                          need, (1, tq), int(seq_len).bit_length())
        def drop(j, carry):
            key = keys_s[j]
            keys_s[j] = jnp.where(tie & (key == thr) & (j * tk + kiota > cut), jnp.int32(INT_MIN), key)
            return carry
        lax.fori_loop(0, n_kt, drop, 0)

    thr_sel = jnp.maximum(thr, jnp.int32(INT_MIN + 1))

    n_chunks = tk // LANES
    m_s[...] = jnp.full(m_s.shape, NEG, F32)
    acc_s[...] = jnp.zeros(acc_s.shape, F32)

    def attend_tile(j, carry):
        ks = pl.ds(pl.multiple_of(j * tk, tk), tk)
        s_all = _dot_nt(qpad_s[...], k_ref[0, ks, :])
        bias = [jnp.where(keys_s[j, c * LANES:(c + 1) * LANES, :] >= thr_sel, 0.0, NEG).T for c in range(n_chunks)]
        for g in range(kv_heads):
            ps, alphas = [], []
            for r in range(per_kv):
                rows = slice((g * per_kv + r) * tq, (g * per_kv + r + 1) * tq)
                s = [s_all[rows, c * LANES:(c + 1) * LANES] + bias[c] for c in range(n_chunks)]
                smax = s[0]
                for c in range(1, n_chunks):
                    smax = jnp.maximum(smax, s[c])
                m_old = m_s[rows]
                m_new = jnp.maximum(m_old, jnp.max(smax, axis=1, keepdims=True))
                m_s[rows] = m_new
                alphas.append(jnp.exp2(m_old - m_new))
                ps.append(jnp.concatenate([jnp.exp2(s[c] - m_new) for c in range(n_chunks)], axis=1).astype(BF16))
            grows = slice(g * per_kv * tq, (g + 1) * per_kv * tq)
            pv = _dot(jnp.concatenate(ps, axis=0), vx_ref[0, ks, g * LANES:(g + 1) * LANES])
            acc_s[grows] = acc_s[grows] * jnp.concatenate(alphas, axis=0) + pv
        return carry
    lax.fori_loop(0, n_kt, attend_tile, 0)

    def normalised(h):
        a = acc_s[h * tq:(h + 1) * tq]
        return a / pltpu.roll(a, hd, 1)
    for s in range(n_heads // 2):
        pb, r = divmod(s, per_kv)
        o_lo = normalised((2 * pb) * per_kv + r)
        o_hi = normalised((2 * pb + 1) * per_kv + r)
        y_ref[0, :, s * LANES:(s + 1) * LANES] = jnp.where(low, o_lo, o_hi).astype(y_ref.dtype)


def _dsa_prompt(q, qi, sa, kk, k, v, *, n_heads, kv_heads, idx_heads, wi_lane, att_scale, idx_w_scale):
    b, l, dq = q.shape
    tq, tk = min(Q_TILE, l), min(KEY_TILE, l // 2)
    assert l % tq == 0 and l % (2 * tk) == 0 and kv_heads % 2 == 0 and idx_heads % 2 == 0
    assert wi_lane % SUBLANES == 0 and idx_heads == SUBLANES
    topk = min(TOPK_MAX, l // 4)
    kvw = k.shape[-1]
    hd = LANES // 2
    ones = jnp.ones((b, l, hd), BF16)
    vx = jnp.concatenate([a for g in range(kv_heads)
                          for a in ((v[:, :, g * hd:(g + 1) * hd], ones) if g % 2 == 0 else
                                    (ones, v[:, :, g * hd:(g + 1) * hd]))], axis=-1)
    return pl.pallas_call(
        functools.partial(_dsa_prompt_kernel, tq=tq, tk=tk, n_heads=n_heads, kv_heads=kv_heads, idx_heads=idx_heads,
                          topk=topk, att_scale=att_scale, idx_w_scale=idx_w_scale, wi_lane=wi_lane, seq_len=l),
        grid=(b, l // tq),
        in_specs=[
            pl.BlockSpec((1, tq, dq), lambda bi, i: (bi, i, 0)),
            pl.BlockSpec((1, tq, qi.shape[-1]), lambda bi, i: (bi, i, 0)),
            pl.BlockSpec((1, tq, LANES), lambda bi, i: (bi, i, 0)),
            pl.BlockSpec((1, l, LANES), lambda bi, i: (bi, 0, 0), pipeline_mode=pl.Buffered(1)),
            pl.BlockSpec((1, l, kvw), lambda bi, i: (bi, 0, 0), pipeline_mode=pl.Buffered(1)),
            pl.BlockSpec((1, l, vx.shape[-1]), lambda bi, i: (bi, 0, 0), pipeline_mode=pl.Buffered(1)),
        ],
        out_specs=pl.BlockSpec((1, tq, dq), lambda bi, i: (bi, i, 0)),
        out_shape=jax.ShapeDtypeStruct((b, l, dq), BF16),
        scratch_shapes=[
            pltpu.VMEM((l // tk, tk, tq), I32),
            pltpu.VMEM((n_heads * tq, kvw), BF16),
            pltpu.VMEM((idx_heads * tq, LANES), BF16),
            pltpu.VMEM((idx_heads, tq), F32),
            pltpu.VMEM((n_heads * tq, LANES), F32),
            pltpu.VMEM((n_heads * tq, LANES), F32),
        ],
        compiler_params=_params("parallel", "arbitrary"),
        name="dsa_prompt",
    )(q, qi, sa, kk, k, vx)


def _dsa_sample_scores_kernel(pt_ref, qi_ref, sa_ref, kn_ref, *rest, pg, t_new, idx_heads, page,
                              idx_w_scale, wi_lane, n_keys):
    page_refs = rest[:pg]
    keys_ref, qi_s, w_s, kn_s = rest[pg:]
    i = pl.program_id(1)
    hd = LANES // 2
    n_steps = pl.num_programs(1)

    @pl.when(i == 0)
    def _():
        qif = qi_ref[0]
        sa = sa_ref[0]
        for h in range(idx_heads):
            qi_s[h * t_new:(h + 1) * t_new, :] = qif[:, h * LANES:h * LANES + hd].astype(F32)
            w_s[h * t_new:(h + 1) * t_new, :] = jnp.broadcast_to(
                sa[:, wi_lane + h:wi_lane + h + 1] * idx_w_scale, (t_new, LANES))
        kn_s[...] = jnp.zeros(kn_s.shape, F32)
        kn_s[0:t_new, :] = kn_ref[0][:, 0:hd].astype(F32)

    def scores(dots):
        ww = jnp.maximum(dots, 0.0) * w_s[...]
        sc = ww[0:t_new]
        for h in range(1, idx_heads):
            sc = sc + ww[h * t_new:(h + 1) * t_new]
        return sc

    always = jnp.full((t_new, page), True)
    kt = jnp.concatenate([page_refs[r][0, 0].astype(BF16) for r in range(pg)], axis=1)
    dots = _dot(qi_s[...].astype(BF16), kt)
    for r in range(pg):
        off = pl.multiple_of((i * pg + r) * page, page)
        keys_ref[0, :, pl.ds(off, page)] = _score_key(scores(dots[:, r * page:(r + 1) * page]), always)

    @pl.when(i == n_steps - 1)
    def _():
        ti = lax.broadcasted_iota(I32, (t_new, page), 0)
        ki = lax.broadcasted_iota(I32, (t_new, page), 1)
        past = n_keys - page
        dots_new = _dot_nt(qi_s[...].astype(BF16), kn_s[...].astype(BF16))
        keys_ref[0, :, past:n_keys] = _score_key(scores(dots_new), ki <= ti)


def _dsa_sample_threshold_kernel(keys_ref, thr_ref, cut_ref, *, rb, topk, n_keys):
    n_chunks = n_keys // LANES
    unroll = next(u for u in (4, 3, 2, 1) if n_chunks % u == 0)
    liota = lax.broadcasted_iota(I32, (rb, LANES), 1)
    for b in range(keys_ref.shape[0] // rb):
        rows = slice(b * rb, (b + 1) * rb)

        def count(pred):
            def body(cc, cnt):
                for u in range(unroll):
                    off = pl.multiple_of((cc * unroll + u) * LANES, LANES)
                    cnt = cnt + jnp.where(pred(keys_ref[rows, pl.ds(off, LANES)], off), 1.0, 0.0)
                return cnt
            cnt = lax.fori_loop(0, n_chunks // unroll, body, jnp.zeros((rb, LANES), F32))
            return jnp.sum(cnt, axis=1, keepdims=True)

        def count_ge(t):
            tb = jnp.broadcast_to(t, (rb, LANES))
            return count(lambda key, off: key >= tb)

        thr, c_ge = _kth_largest_key(count_ge, (rb, 1), topk)
        tie = c_ge > topk
        thr_ref[rows, :] = jnp.broadcast_to(jnp.maximum(thr, jnp.int32(INT_MIN + 1)), (rb, LANES))
        cut_ref[rows, :] = jnp.full((rb, LANES), n_keys, I32)

        @pl.when(jnp.max(jnp.where(tie, 1, 0)) > 0)
        def _():
            need = topk - count(lambda key, off: key > thr)
            cut = _tie_cutoff(lambda p: count(lambda key, off: (key == thr) & (off + liota < p)), need, (rb, 1),
                              int(n_keys).bit_length())
            cut_ref[rows, :] = jnp.broadcast_to(jnp.where(tie, cut, n_keys), (rb, LANES))


def _dsa_sample_attend_kernel(pt_ref, q_ref, keys_ref, thr_ref, cut_ref, kn_ref, vn_ref, *rest, pg, pc, t_new,
                              n_heads, kv_heads, page, att_scale, n_keys):
    k_refs = rest[:pg]
    v_refs = rest[pg:2 * pg]
    y_ref, qpad_s, m_s, l_s, acc_s, kn_s, vn_s = rest[2 * pg:]
    i = pl.program_id(1)
    hd = LANES // 2
    per_kv = n_heads // kv_heads
    n_steps = pl.num_programs(1)
    lane = lax.broadcasted_iota(I32, (t_new, LANES), 1)
    low = lane < hd

    @pl.when(i == 0)
    def _():
        qf = q_ref[0].astype(F32) * att_scale
        zeros = jnp.zeros((t_new, LANES), F32)
        for h in range(n_heads):
            g, r = divmod(h, per_kv)
            slot = (g // 2) * per_kv + r
            sl = qf[:, slot * LANES:(slot + 1) * LANES]
            half = jnp.where(low, sl, 0.0) if g % 2 == 0 else jnp.where(low, 0.0, sl)
            row = [zeros] * (kv_heads // 2)
            row[g // 2] = half
            qpad_s[h * t_new:(h + 1) * t_new, :] = jnp.concatenate(row, axis=1)
        m_s[...] = jnp.full(m_s.shape, NEG, F32)
        l_s[...] = jnp.zeros(l_s.shape, F32)
        acc_s[...] = jnp.zeros(acc_s.shape, F32)
        kn_s[...] = jnp.zeros(kn_s.shape, F32)
        vn_s[...] = jnp.zeros(vn_s.shape, F32)
        kn_s[0:t_new, :] = kn_ref[0].astype(F32)
        vn_s[0:t_new, :] = vn_ref[0].astype(F32)

    thr = thr_ref[0]
    cut = cut_ref[0]

    def attend(s, key, pos0, pv, chain):
        n = key.shape[1]
        wide = lambda a: jnp.concatenate([a] * (n // LANES), axis=1)
        pos = pos0 + lax.broadcasted_iota(I32, (t_new, n), 1)
        sel = (key > wide(thr)) | ((key == wide(thr)) & (pos <= wide(cut)))
        bias = jnp.where(sel, 0.0, NEG)
        s = s + jnp.concatenate([bias] * n_heads, axis=0)
        m_old = m_s[chain]
        m_new = jnp.maximum(m_old, jnp.max(s, axis=1, keepdims=True))
        alpha = jnp.exp(m_old - m_new)
        p = jnp.exp(s - m_new)
        l_s[chain] = alpha * l_s[chain] + jnp.sum(p, axis=1, keepdims=True)
        m_s[chain] = m_new
        acc_s[chain] = acc_s[chain] * alpha + pv(p.astype(BF16))

    n_chains = m_s.shape[0]
    qb = qpad_s[...].astype(BF16)
    kvw = qb.shape[1]
    for c in range(pg // pc):
        off = pl.multiple_of((i * pg + c * pc) * page, page)
        kt = jnp.concatenate([k_refs[c * pc + r][0, 0].reshape(kvw, page).astype(BF16) for r in range(pc)], axis=1)
        vt = jnp.concatenate([v_refs[c * pc + r][0, 0].reshape(kvw, page).astype(BF16) for r in range(pc)], axis=1)
        attend(_dot(qb, kt), keys_ref[0, :, pl.ds(off, pc * page)], off, lambda p: _dot_nt(p, vt), c % n_chains)

    @pl.when(i == n_steps - 1)
    def _():
        attend(_dot_nt(qb, kn_s[...].astype(BF16)), keys_ref[0, :, n_keys - page:n_keys], n_keys - page,
               lambda p: _dot(p, vn_s[...].astype(BF16)), 0)
        m = m_s[0]
        for c in range(1, n_chains):
            m = jnp.maximum(m, m_s[c])
        l = sum(l_s[c] * jnp.exp(m_s[c] - m) for c in range(n_chains))
        acc = sum(acc_s[c] * jnp.exp(m_s[c] - m) for c in range(n_chains))
        o = acc / l
        for s in range(n_heads // 2):
            pb, r = divmod(s, per_kv)
            h_lo = (2 * pb) * per_kv + r
            h_hi = (2 * pb + 1) * per_kv + r
            o_lo = o[h_lo * t_new:(h_lo + 1) * t_new, pb * LANES:(pb + 1) * LANES]
            o_hi = o[h_hi * t_new:(h_hi + 1) * t_new, pb * LANES:(pb + 1) * LANES]
            y_ref[0, :, s * LANES:(s + 1) * LANES] = jnp.where(low, o_lo, o_hi).astype(y_ref.dtype)


def _pages_per_step(n_pages):
    return math.gcd(n_pages, PAGES_PER_STEP_MAX)


def _dsa_sample(q, qiw, sa, kk_new, k_new, v_new, ck, cv, ci, page_table, layer, *, n_heads, kv_heads, idx_heads,
                wi_lane, att_scale, idx_w_scale):
    db, t_new, dq = q.shape
    _, n_pool, kv_heads_, hd, page = ck.shape
    kvw = kv_heads_ * hd
    n_pages = page_table.shape[1]
    assert page == LANES and t_new % SUBLANES == 0 and t_new <= page and kv_heads_ == kv_heads and hd == LANES // 2
    pg = _pages_per_step(n_pages)
    n_chains = 2 if pg % 2 == 0 else 1
    pc = pg // n_chains
    n_steps = n_pages // pg
    n_keys = (n_pages + 1) * page
    topk = min(TOPK_MAX, (n_pages * page + t_new) // 4)
    idx_spec = lambda r: pl.BlockSpec((1, 1, hd, page), lambda b, i, pt: (layer, pt[b, i * pg + r], 0, 0))
    kv_spec = lambda r: pl.BlockSpec((1, 1, kv_heads, hd, page), lambda b, i, pt: (layer, pt[b, i * pg + r], 0, 0, 0))
    row_spec = lambda w: pl.BlockSpec((1, t_new, w), lambda b, i, pt: (b, 0, 0))

    keys = pl.pallas_call(
        functools.partial(_dsa_sample_scores_kernel, pg=pg, t_new=t_new, idx_heads=idx_heads, page=page,
                          idx_w_scale=idx_w_scale, wi_lane=wi_lane, n_keys=n_keys),
        grid_spec=pltpu.PrefetchScalarGridSpec(
            num_scalar_prefetch=1,
            grid=(db, n_steps),
            in_specs=[row_spec(qiw.shape[-1]), row_spec(LANES), row_spec(LANES)] + [idx_spec(r) for r in range(pg)],
            out_specs=row_spec(n_keys),
            scratch_shapes=[pltpu.VMEM((idx_heads * t_new, hd), F32), pltpu.VMEM((idx_heads * t_new, LANES), F32),
                            pltpu.VMEM((page, hd), F32)],
        ),
        out_shape=jax.ShapeDtypeStruct((db, t_new, n_keys), I32),
        compiler_params=_params("parallel", "arbitrary"),
        name="dsa_sample_scores",
    )(page_table, qiw, sa, kk_new, *([ci] * pg))

    n_rows = db * t_new
    rb = math.gcd(n_rows, LANES)
    thr, cut = pl.pallas_call(
        functools.partial(_dsa_sample_threshold_kernel, rb=rb, topk=topk, n_keys=n_keys),
        out_shape=[jax.ShapeDtypeStruct((n_rows, LANES), I32)] * 2,
        compiler_params=pltpu.CompilerParams(vmem_limit_bytes=VMEM_LIMIT_BYTES),
        name="dsa_sample_threshold",
    )(keys.reshape(n_rows, n_keys))
    thr = thr.reshape(db, t_new, LANES)
    cut = cut.reshape(db, t_new, LANES)

    return pl.pallas_call(
        functools.partial(_dsa_sample_attend_kernel, pg=pg, pc=pc, t_new=t_new, n_heads=n_heads, kv_heads=kv_heads,
                          page=page, att_scale=att_scale, n_keys=n_keys),
        grid_spec=pltpu.PrefetchScalarGridSpec(
            num_scalar_prefetch=1,
            grid=(db, n_steps),
            in_specs=[row_spec(dq), row_spec(n_keys), row_spec(LANES), row_spec(LANES), row_spec(kvw), row_spec(kvw)]
                     + [kv_spec(r) for r in range(pg)] * 2,
            out_specs=row_spec(dq),
            scratch_shapes=[pltpu.VMEM((n_heads * t_new, kvw), F32), pltpu.VMEM((n_chains, n_heads * t_new, 1), F32),
                            pltpu.VMEM((n_chains, n_heads * t_new, 1), F32),
                            pltpu.VMEM((n_chains, n_heads * t_new, kvw), F32),
                            pltpu.VMEM((page, kvw), F32), pltpu.VMEM((page, kvw), F32)],
        ),
        out_shape=jax.ShapeDtypeStruct((db, t_new, dq), BF16),
        compiler_params=_params("parallel", "arbitrary"),
        name="dsa_sample_attend",
    )(page_table, q, keys, thr, cut, k_new, v_new, *([ck] * pg), *([cv] * pg))


def _merge_kernel(x_ref, ys_ref, ya_ref, gs_ref, ga_ref, ps_ref, pa_ref, wo_ref, o_ref):
    merged = (_sigmoid(gs_ref[...]) * _dot(ys_ref[...], ps_ref[...])
              + _sigmoid(ga_ref[...]) * _dot(ya_ref[...], pa_ref[...]))
    o_ref[...] = x_ref[...] + _dot(merged.astype(BF16), wo_ref[...])


def _merge(x, y_ssd, y_attn, g_s, g_a, p_ssd, p_attn, w_out):
    t, d = x.shape
    tm = min(ROW_TILE, t)
    assert t % tm == 0
    rows = lambda w: pl.BlockSpec((tm, w), lambda i: (i, 0))
    full = lambda a: pl.BlockSpec(a.shape, lambda i: (0, 0))
    return pl.pallas_call(
        _merge_kernel,
        grid=(t // tm,),
        in_specs=[rows(d), rows(y_ssd.shape[1]), rows(y_attn.shape[1]), rows(d), rows(d),
                  full(p_ssd), full(p_attn), full(w_out)],
        out_specs=rows(d),
        out_shape=jax.ShapeDtypeStruct((t, d), F32),
        compiler_params=_params("parallel"),
        name="merge",
    )(x, y_ssd, y_attn, g_s, g_a, p_ssd, p_attn, w_out)


def _pair_slot_perm(n_heads, kv_heads, hd):
    per_kv = n_heads // kv_heads
    cols = []
    for s in range(n_heads // 2):
        pb, r = divmod(s, per_kv)
        for h in ((2 * pb) * per_kv + r, (2 * pb + 1) * per_kv + r):
            cols.extend(range(h * hd, (h + 1) * hd))
    return np.asarray(cols, np.int32)


def kernel(x_prompt, x_sample, cache_k, cache_v, cache_idx_k, state_ssm, state_conv, page_table, ffn1_norm, ffn1_w1, ffn1_w2, mix_norm, w_in, conv_w, conv_b, dt_bias, a_log, d_skip, ssd_norm, w_branch_ssd, w_branch_attn, w_out, ffn2_norm, ffn2_w1, ffn2_w2, final_norm):
    bp, seq, d_model = x_prompt.shape
    db, dseq, _ = x_sample.shape
    depth, n_pool, page, kv_heads, head_dim = cache_k.shape
    idx_dim = cache_idx_k.shape[-1]
    ssd_heads, ssd_hd, d_state = state_ssm.shape[2:]
    conv_dim = state_conv.shape[-1]
    d_inner = ssd_norm.shape[-1]
    n_groups = (conv_dim - d_inner) // (2 * d_state)
    attn_dim = w_branch_attn.shape[1]
    n_heads = attn_dim // head_dim
    kvw = kv_heads * head_dim
    d_proj = w_in.shape[-1]
    idx_heads = (d_proj - (d_inner + conv_dim + ssd_heads + attn_dim + 2 * kvw + idx_dim + 2 * d_model)) // (idx_dim + 1)
    assert head_dim == LANES // 2 and idx_dim == LANES // 2 and ssd_heads + idx_heads <= LANES
    att_scale = head_dim ** -0.5
    idx_w_scale = (idx_heads ** -0.5) * (idx_dim ** -0.5)
    sizes = (d_inner, conv_dim, ssd_heads, attn_dim, kvw, kvw, idx_heads * idx_dim, idx_dim, idx_heads, d_model, d_model)
    assert sum(sizes) == d_proj
    offs = np.concatenate([[0], np.cumsum(sizes)])
    perm = _pair_slot_perm(n_heads, kv_heads, head_dim)
    wi_lane = ssd_heads

    tp, ts = bp * seq, db * dseq
    yp = x_prompt.reshape(tp, d_model)
    ys = x_sample.reshape(ts, d_model)
    ck = jnp.transpose(cache_k, (0, 1, 3, 4, 2))
    cv = jnp.transpose(cache_v, (0, 1, 3, 4, 2))
    ci = jnp.transpose(cache_idx_k, (0, 1, 3, 2))
    dsa_kw = dict(n_heads=n_heads, kv_heads=kv_heads, idx_heads=idx_heads, wi_lane=wi_lane, att_scale=att_scale,
                  idx_w_scale=idx_w_scale)
    outs = {n: [] for n in ("kp", "vp", "ip", "sp", "cp", "ks", "vs", "is", "ss", "cs")}
    hist = conv_w.shape[1] - 1

    for l in range(depth):
        wl = w_in[l]
        col = lambda i: wl[:, offs[i]:offs[i + 1]]
        w_z, w_xbc, w_dt, w_q, w_k, w_v, w_qi, w_ki, w_wi, w_gs, w_ga = [col(i) for i in range(11)]
        w_sa = jnp.concatenate([w_dt, w_wi, jnp.zeros((d_model, LANES - ssd_heads - idx_heads), F32)], axis=1)
        w_qi_wide = jnp.pad(w_qi.reshape(d_model, idx_heads, idx_dim), ((0, 0), (0, 0), (0, LANES - idx_dim)))
        bf = lambda w: w.astype(BF16)
        wa = [bf(w_z), bf(w_xbc)]
        wb_common = [bf(w_q[:, perm]), bf(w_k), bf(w_v), bf(jnp.concatenate([w_ki, w_ki], axis=1)), bf(w_sa),
                     bf(w_gs), bf(w_ga)]
        dt_common = [(BF16,), (F32, BF16), (F32, BF16), (F32, BF16), (F32,), (F32,), (F32,)]
        f1w1, f1w2, f2w1, f2w2 = bf(ffn1_w1[l]), bf(ffn1_w2[l]), bf(ffn2_w1[l]), bf(ffn2_w2[l])
        p_ssd, p_attn, wo = bf(w_branch_ssd[l]), bf(w_branch_attn[l][perm, :]), bf(w_out[l])
        last = l == depth - 1

        def mixer(y, b, s, qi_weight, attend, buf, h0):
            z, xbc = _norm_linear(y, mix_norm[l], wa, [(F32,), (F32,)], PROJ_ROW_TILE)
            q, k, kb, v, vb, kk, kkb, sa, g_s, g_a, qi = _norm_linear(
                y, mix_norm[l], wb_common + [qi_weight], dt_common + [(BF16,)], PROJ_ROW_TILE)
            r3 = lambda a: a.reshape(b, s, a.shape[-1])
            y_ssd, h_fin = _ssd(r3(z), r3(xbc), r3(sa), buf, h0, conv_w[l], conv_b[l], dt_bias[l], a_log[l],
                                d_skip[l], ssd_norm[l], n_groups=n_groups, d_state=d_state)
            y_attn = attend(r3(q), r3(qi), r3(sa), r3(kkb), r3(kb), r3(vb))
            y = _merge(y, y_ssd.reshape(b * s, d_inner), y_attn.reshape(b * s, attn_dim), g_s, g_a, p_ssd, p_attn, wo)
            new_buf = r3(xbc)[:, s - hist:, :]
            return (y, new_buf, h_fin, k.reshape(b, s, kv_heads, head_dim), v.reshape(b, s, kv_heads, head_dim),
                    r3(kk)[:, :, :idx_dim])

        yp = _ffn(yp, ffn1_norm[l], f1w1, f1w2)
        ys = _ffn(ys, ffn1_norm[l], f1w1, f1w2)

        yp, cbp, hfp, kp, vp, kip = mixer(
            yp, bp, seq, bf(w_qi), functools.partial(_dsa_prompt, **dsa_kw),
            jnp.zeros((bp, hist, conv_dim), F32), jnp.zeros((bp, ssd_heads, ssd_hd, d_state), F32))
        att_s = functools.partial(_dsa_sample, ck=ck, cv=cv, ci=ci, page_table=page_table, layer=l, **dsa_kw)
        ys, cbs, hfs, kss, vss, kis = mixer(
            ys, db, dseq, bf(w_qi_wide.reshape(d_model, idx_heads * LANES)), att_s, state_conv[l], state_ssm[l])

        pg_ = final_norm if last else None
        yp = _ffn(yp, ffn2_norm[l], f2w1, f2w2, pg_)
        ys = _ffn(ys, ffn2_norm[l], f2w1, f2w2, pg_)
        for n, a in zip(("kp", "vp", "ip", "sp", "cp", "ks", "vs", "is", "ss", "cs"),
                        (kp, vp, kip, hfp, cbp, kss, vss, kis, hfs, cbs)):
            outs[n].append(a)

    st = lambda n: jnp.stack(outs[n])
    return (yp.reshape(bp, seq, d_model), ys.reshape(db, dseq, d_model),
            st("kp"), st("vp"), st("ip"), st("sp"), st("cp"),
            st("ks"), st("vs"), st("is"), st("ss"), st("cs"))
```

```python
import functools
import math

import jax
import jax.numpy as jnp
import numpy as np
from jax import lax
from jax.experimental import pallas as pl
from jax.experimental.pallas import tpu as pltpu

F32 = jnp.float32
BF16 = jnp.bfloat16
I32 = jnp.int32

EPS = 1e-6
SSD_CHUNK = 128
TOPK_MAX = 256
LANES = 128
SUBLANES = 8
VMEM_LIMIT_BYTES = 56 * 1024 * 1024
NEG = -1e30
INT_MIN = -(2 ** 31)

ROW_TILE = 512
PROJ_ROW_TILE = 256
FF_TILE_MAX = 1536
Q_TILE = 128
KEY_TILE = 512
PAGES_PER_STEP_MAX = 32


def _params(*sem):
    return pltpu.CompilerParams(dimension_semantics=sem, vmem_limit_bytes=VMEM_LIMIT_BYTES)


def _sigmoid(x):
    return 1.0 / (1.0 + jnp.exp(-x))


def _rms(x, g):
    return x * lax.rsqrt(jnp.mean(x * x, axis=-1, keepdims=True) + EPS) * g


def _dot(a, b):
    return jnp.dot(a, b, preferred_element_type=F32)


def _dot_nt(a, b):
    return lax.dot_general(a, b, (((1,), (1,)), ((), ())), preferred_element_type=F32)


def _split2(x):
    hi = x.astype(BF16)
    lo = (x - hi.astype(F32)).astype(BF16)
    return hi, lo


def _split3(x):
    hi = x.astype(BF16)
    r = x - hi.astype(F32)
    mid = r.astype(BF16)
    lo = (r - mid.astype(F32)).astype(BF16)
    return hi, mid, lo


def _ffn_kernel(*refs, post_norm):
    if post_norm:
        x_ref, g_ref, wa_ref, wb_ref, w2_ref, pg_ref, o_ref, h_s, acc_s = refs
    else:
        x_ref, g_ref, wa_ref, wb_ref, w2_ref, o_ref, h_s, acc_s = refs
    f = pl.program_id(1)

    @pl.when(f == 0)
    def _():
        h_s[...] = _rms(x_ref[...], g_ref[...]).astype(BF16)
        acc_s[...] = jnp.zeros_like(acc_s)

    h = h_s[...]
    a = _dot(h, wa_ref[...])
    b = _dot(h, wb_ref[...])
    u = (a * _sigmoid(a) * b).astype(BF16)
    acc_s[...] += _dot(u, w2_ref[...])

    @pl.when(f == pl.num_programs(1) - 1)
    def _():
        y = x_ref[...] + 0.5 * acc_s[...]
        if post_norm:
            y = _rms(y, pg_ref[...])
        o_ref[...] = y


def _ff_tile(d_ff):
    best = None
    for t in range(LANES, d_ff + 1, LANES):
        if d_ff % t == 0 and t <= FF_TILE_MAX:
            best = t
    assert best is not None, d_ff
    return best


def _ffn(x, g, w1, w2, post_gain=None):
    t, d = x.shape
    d_ff = w2.shape[0]
    tm = min(ROW_TILE, t)
    tf = _ff_tile(d_ff)
    nf = d_ff // tf
    assert t % tm == 0
    post_norm = post_gain is not None
    in_specs = [
        pl.BlockSpec((tm, d), lambda i, f: (i, 0)),
        pl.BlockSpec((1, d), lambda i, f: (0, 0)),
        pl.BlockSpec((d, tf), lambda i, f: (0, f)),
        pl.BlockSpec((d, tf), lambda i, f: (0, f + nf)),
        pl.BlockSpec((tf, d), lambda i, f: (f, 0)),
    ]
    args = [x, g.reshape(1, d), w1, w1, w2]
    if post_norm:
        in_specs.append(pl.BlockSpec((1, d), lambda i, f: (0, 0)))
        args.append(post_gain.reshape(1, d))
    return pl.pallas_call(
        functools.partial(_ffn_kernel, post_norm=post_norm),
        grid=(t // tm, nf),
        in_specs=in_specs,
        out_specs=pl.BlockSpec((tm, d), lambda i, f: (i, 0)),
        out_shape=jax.ShapeDtypeStruct((t, d), F32),
        scratch_shapes=[pltpu.VMEM((tm, d), BF16), pltpu.VMEM((tm, d), F32)],
        compiler_params=_params("parallel", "arbitrary"),
        name="ffn",
    )(*args)


def _norm_linear_kernel(*refs, out_dtypes):
    n_w = len(out_dtypes)
    x_ref, g_ref = refs[:2]
    w_refs = refs[2:2 + n_w]
    o_refs = list(refs[2 + n_w:])
    h = _rms(x_ref[...], g_ref[...]).astype(BF16)
    for w_ref, dts in zip(w_refs, out_dtypes):
        r = _dot(h, w_ref[...])
        for dt in dts:
            o_refs.pop(0)[...] = r.astype(dt)


def _norm_linear(x, g, weights, out_dtypes, tm):
    t, d = x.shape
    tm = min(tm, t)
    assert t % tm == 0
    in_specs = [pl.BlockSpec((tm, d), lambda i: (i, 0)), pl.BlockSpec((1, d), lambda i: (0, 0))]
    out_specs, out_shape = [], []
    for w, dts in zip(weights, out_dtypes):
        n = w.shape[1]
        in_specs.append(pl.BlockSpec((d, n), lambda i: (0, 0)))
        for dt in dts:
            out_specs.append(pl.BlockSpec((tm, n), lambda i: (i, 0)))
            out_shape.append(jax.ShapeDtypeStruct((t, n), dt))
    return pl.pallas_call(
        functools.partial(_norm_linear_kernel, out_dtypes=tuple(tuple(d_) for d_ in out_dtypes)),
        grid=(t // tm,),
        in_specs=in_specs,
        out_specs=out_specs,
        out_shape=out_shape,
        compiler_params=_params("parallel"),
        name="norm_linear",
    )(x, g.reshape(1, d), *weights)


def _ssd_kernel(z_ref, xbc_ref, dt_ref, buf_ref, h0_ref, cw_ref, cb_ref, dtb_ref, alog_ref, dsk_ref, ng_ref, e_ref,
                y_ref, hfin_ref, xp_s, ht_s, *, qin, d_inner, n_groups, d_state, conv_w):
    q = SSD_CHUNK
    c = pl.program_id(1)
    hp_blocks = d_inner // LANES
    gw = d_inner // n_groups
    assert d_state == LANES and gw % LANES == 0
    pad = SUBLANES
    hist = conv_w - 1

    @pl.when(c == 0)
    def _():
        xp_s[0:pad, :] = buf_ref[0, 0]
        if qin < q:
            xp_s[pad + qin:pad + q, :] = jnp.zeros((q - qin, xp_s.shape[1]), F32)
        for i in range(hp_blocks):
            ht_s[:, i * LANES:(i + 1) * LANES] = h0_ref[0, 0, i * LANES:(i + 1) * LANES, :].T

    xp_s[pad:pad + qin, :] = xbc_ref[0]
    x_cur = xp_s[pad:pad + q, :]
    x_prev = xp_s[0:pad, :]
    row8 = lax.broadcasted_iota(I32, (pad, 1), 0)
    acc = cb_ref[...] + x_cur * cw_ref[hist:hist + 1, :]
    for i in range(hist):
        s = hist - i
        rolled = pltpu.roll(x_cur, s, 0)
        head = jnp.where(row8 < s, pltpu.roll(x_prev, s, 0), rolled[0:pad])
        acc = acc + jnp.concatenate([head, rolled[pad:]], axis=0) * cw_ref[i:i + 1, :]
    xc = acc * _sigmoid(acc)
    tail = xp_s[pad + qin - hist:pad + qin, :]
    xp_s[pad - hist:pad, :] = tail

    xs = xc[:, :d_inner]
    bm = xc[:, d_inner:d_inner + n_groups * d_state]
    cm = xc[:, d_inner + n_groups * d_state:]

    dt_raw = dt_ref[0] + dtb_ref[...]
    dt = jnp.maximum(dt_raw, 0.0) + jnp.log1p(jnp.exp(-jnp.abs(dt_raw)))
    if qin < q:
        dt = jnp.concatenate([dt, jnp.zeros((q - qin, LANES), F32)], axis=0)
    la = dt * (-jnp.exp(alog_ref[...]))

    ri = lax.broadcasted_iota(I32, (q, q), 0)
    ci = lax.broadcasted_iota(I32, (q, q), 1)
    causal = ri >= ci
    tril = jnp.where(causal, 1.0, 0.0).astype(BF16)
    eye = jnp.where(ri == ci, 1.0, 0.0).astype(BF16)
    a_cs = sum(_dot(tril, p) for p in _split3(la))
    a_cs_t = sum(_dot_nt(eye, p) for p in _split3(a_cs))
    dec = jnp.exp(a_cs[q - 1:q, :] - a_cs)
    eac = jnp.exp(a_cs)
    stacked = jnp.concatenate([dt, dec, eac], axis=0)
    expd = sum(_dot(p, e_ref[...]) for p in _split2(stacked))
    dt_e, dec_e, eac_e = expd[0:q], expd[q:2 * q], expd[2 * q:3 * q]

    x = xs * dt_e
    xb = x.astype(BF16)
    xd = (x * dec_e).astype(BF16)
    lane = lax.broadcasted_iota(I32, (q, LANES), 1)
    hd = LANES // 2
    heads_per_group = gw // hd

    y_parts = []
    for g in range(n_groups):
        gs = slice(g * gw, (g + 1) * gw)
        cg = cm[:, g * d_state:(g + 1) * d_state].astype(BF16)
        bg = bm[:, g * d_state:(g + 1) * d_state]
        cb = _dot_nt(cg, bg.astype(BF16))
        y_off = _dot(cg, ht_s[:, gs].astype(BF16))
        pair_parts = []
        for p in range(heads_per_group // 2):
            xp = xb[:, g * gw + p * LANES:g * gw + (p + 1) * LANES]
            res = []
            for k in range(2):
                j = g * heads_per_group + 2 * p + k
                diff = a_cs[:, j:j + 1] - a_cs_t[j:j + 1, :]
                lm = jnp.exp(jnp.where(causal, diff, NEG))
                res.append(_dot((cb * lm).astype(BF16), xp))
            pair_parts.append(jnp.where(lane < hd, res[0], res[1]))
        y_diag = jnp.concatenate(pair_parts, axis=1)
        y_parts.append(y_diag + y_off * eac_e[:, gs])
        st = _dot(bg.T.astype(BF16), xd[:, gs])
        ht_s[:, gs] = ht_s[:, gs] * eac_e[q - 1:q, gs] + st

    zz = z_ref[0]
    outs = []
    for g in range(n_groups):
        gs = slice(g * gw, (g + 1) * gw)
        yv = (y_parts[g][:qin] + dsk_ref[:, gs] * xs[:qin, gs]) * (zz[:, gs] * _sigmoid(zz[:, gs]))
        ms = jnp.mean(yv * yv, axis=-1, keepdims=True)
        outs.append(yv * lax.rsqrt(ms + EPS) * ng_ref[:, gs])
    y_ref[0] = jnp.concatenate(outs, axis=1).astype(y_ref.dtype)

    @pl.when(c == pl.num_programs(1) - 1)
    def _():
        for i in range(hp_blocks):
            hfin_ref[0, i * LANES:(i + 1) * LANES, :] = ht_s[:, i * LANES:(i + 1) * LANES].T


def _ssd(z, xbc, dtp, buf, h0, layer, conv_w, conv_b, dt_bias, a_log, d_skip, norm_g, *, n_groups, d_state):
    b, l, d_inner = z.shape
    conv_dim = xbc.shape[-1]
    n_heads, p_dim, n_state = h0.shape[2:]
    width = conv_w.shape[0]
    hist = width - 1
    assert n_heads <= LANES and p_dim == LANES // 2 and n_state == d_state and hist <= SUBLANES
    qin = math.gcd(l, SSD_CHUNK)
    assert qin % SUBLANES == 0 and qin >= hist
    nc = l // qin
    assert nc == 1 or qin == SSD_CHUNK
    hp = n_heads * p_dim
    buf8 = jnp.pad(buf, ((0, 0), (0, 0), (SUBLANES - hist, 0), (0, 0)))
    cw8 = jnp.pad(conv_w, ((0, SUBLANES - width), (0, 0)))
    pad1 = lambda v: jnp.pad(v.reshape(1, -1), ((0, 0), (0, LANES - n_heads)))
    expand = (np.arange(LANES)[:, None] == (np.arange(d_inner)[None, :] // p_dim)).astype(np.float32)
    full = lambda shape: pl.BlockSpec(shape, lambda i, c: (0,) * len(shape))
    y, hfin = pl.pallas_call(
        functools.partial(_ssd_kernel, qin=qin, d_inner=d_inner, n_groups=n_groups, d_state=d_state, conv_w=width),
        grid=(b, nc),
        in_specs=[
            pl.BlockSpec((1, qin, d_inner), lambda i, c: (i, c, 0)),
            pl.BlockSpec((1, qin, conv_dim), lambda i, c: (i, c, 0)),
            pl.BlockSpec((1, qin, LANES), lambda i, c: (i, c, 0)),
            pl.BlockSpec((1, 1, SUBLANES, conv_dim), lambda i, c: (layer, i, 0, 0)),
            pl.BlockSpec((1, 1, hp, n_state), lambda i, c: (layer, i, 0, 0)),
            full((SUBLANES, conv_dim)), full((1, conv_dim)), full((1, LANES)), full((1, LANES)),
            full((1, d_inner)), full((1, d_inner)), full((LANES, d_inner)),
        ],
        out_specs=[
            pl.BlockSpec((1, qin, d_inner), lambda i, c: (i, c, 0)),
            pl.BlockSpec((1, hp, n_state), lambda i, c: (i, 0, 0)),
        ],
        out_shape=[jax.ShapeDtypeStruct((b, l, d_inner), BF16), jax.ShapeDtypeStruct((b, hp, n_state), F32)],
        scratch_shapes=[pltpu.VMEM((SUBLANES + SSD_CHUNK, conv_dim), F32), pltpu.VMEM((n_state, hp), F32)],
        compiler_params=_params("parallel", "arbitrary"),
        name="ssd",
    )(z, xbc, dtp, buf8, h0.reshape(-1, b, hp, n_state), cw8, conv_b.reshape(1, -1), pad1(dt_bias), pad1(a_log),
      jnp.repeat(d_skip, p_dim).reshape(1, -1), norm_g.reshape(1, -1), jnp.asarray(expand, BF16))
    return y, hfin.reshape(b, n_heads, p_dim, n_state)


def _score_key(sc, valid):
    bits = lax.bitcast_convert_type(sc + 0.0, I32)
    key = jnp.where(bits < 0, bits ^ jnp.int32(0x7FFFFFFF), bits)
    return jnp.where(valid, key, jnp.int32(INT_MIN))


def _kth_largest_key(count_ge, shape, k):
    def body(b, st):
        t, c_t = st
        cand = t + lax.shift_left(jnp.int32(1), jnp.int32(31) - b)
        c = count_ge(cand)
        return jnp.where(c >= k, cand, t), jnp.where(c >= k, c, c_t)
    return lax.fori_loop(0, 32, body, (jnp.full(shape, INT_MIN, I32), jnp.zeros(shape, F32)))


def _tie_cutoff(count_tie_before, need, shape, n_bits):
    def body(b, p):
        bit = lax.shift_left(jnp.int32(1), jnp.int32(n_bits - 1) - b)
        p_c = p | bit
        return jnp.where(count_tie_before(p_c) <= need - 1, p_c, p)
    return lax.fori_loop(0, n_bits, body, jnp.zeros(shape, I32))


def _dsa_prompt_kernel(q_ref, qi_ref, sa_ref, kk_ref, k_ref, vx_ref, y_ref,
                       keys_s, qpad_s, qipad_s, w_s, m_s, acc_s,
                       *, tq, tk, n_heads, kv_heads, idx_heads, topk, att_scale, idx_w_scale, wi_lane, seq_len):
    i = pl.program_id(1)
    hd = LANES // 2
    per_kv = n_heads // kv_heads
    n_kt = (i * tq + tq - 1) // tk + 1
    lane = lax.broadcasted_iota(I32, (tq, LANES), 1)
    low = lane < hd

    qf = q_ref[0].astype(F32) * (att_scale * math.log2(math.e))
    zeros = jnp.zeros((tq, LANES), F32)
    for h in range(n_heads):
        g, r = divmod(h, per_kv)
        slot = (g // 2) * per_kv + r
        sl = qf[:, slot * LANES:(slot + 1) * LANES]
        half = jnp.where(low, sl, 0.0) if g % 2 == 0 else jnp.where(low, 0.0, sl)
        row = [zeros] * (kv_heads // 2)
        row[g // 2] = half
        qpad_s[h * tq:(h + 1) * tq, :] = jnp.concatenate(row, axis=1).astype(BF16)
    qif = qi_ref[0].astype(F32)
    for h in range(idx_heads):
        sl = qif[:, (h // 2) * LANES:(h // 2 + 1) * LANES]
        half = jnp.where(low, sl, 0.0) if h % 2 == 0 else jnp.where(low, 0.0, sl)
        qipad_s[h * tq:(h + 1) * tq, :] = half.astype(BF16)
    w_s[...] = sa_ref[0].T[wi_lane:wi_lane + idx_heads, :] * idx_w_scale

    qpos = i * tq + lax.broadcasted_iota(I32, (tk, tq), 1)
    kiota = lax.broadcasted_iota(I32, (tk, tq), 0)
    siota = lax.broadcasted_iota(I32, (SUBLANES, tq), 0)

    tiles_per_step = 2

    def score_tiles(jj, carry):
        for u in range(tiles_per_step):
            j = jj * tiles_per_step + u
            kt = kk_ref[0, pl.ds(pl.multiple_of(j * tk, tk), tk), :]
            r = _dot_nt(kt, qipad_s[...])
            sc = jnp.zeros((tk, tq), F32)
            for h in range(idx_heads):
                sc = sc + jnp.maximum(r[:, h * tq:(h + 1) * tq], 0.0) * w_s[h:h + 1, :]
            keys_s[j] = _score_key(sc, j * tk + kiota <= qpos)
        return carry
    lax.fori_loop(0, (n_kt + tiles_per_step - 1) // tiles_per_step, score_tiles, 0)

    n_acc = 4

    def count(pred):
        def body(j, cnts):
            cnts = list(cnts)
            for g in range(tk // SUBLANES):
                hit = pred(keys_s[j, g * SUBLANES:(g + 1) * SUBLANES, :], j * tk + g * SUBLANES)
                cnts[g % n_acc] = cnts[g % n_acc] + jnp.where(hit, 1.0, 0.0)
            return tuple(cnts)
        cnts = lax.fori_loop(0, n_kt, body, (jnp.zeros((SUBLANES, tq), F32),) * n_acc)
        return jnp.sum(sum(cnts), axis=0, keepdims=True)

    def count_ge(t):
        tb = jnp.broadcast_to(t, (SUBLANES, tq))
        return count(lambda key, base: key >= tb)

    thr, c_ge = _kth_largest_key(count_ge, (1, tq), topk)
    tie = c_ge > topk

    @pl.when(jnp.max(jnp.where(tie, 1, 0)) > 0)
    def _():
        need = topk - count(lambda key, base: key > thr)
        cut = _tie_cutoff(lambda p: count(lambda key, base: (key == thr) & (base + siota < p)),
                          need, (1, tq), int(seq_len).bit_length())
        def drop(j, carry):
            key = keys_s[j]
            keys_s[j] = jnp.where(tie & (key == thr) & (j * tk + kiota > cut), jnp.int32(INT_MIN), key)
            return carry
        lax.fori_loop(0, n_kt, drop, 0)

    thr_sel = jnp.maximum(thr, jnp.int32(INT_MIN + 1))

    n_chunks = tk // LANES
    m_s[...] = jnp.full(m_s.shape, NEG, F32)
    acc_s[...] = jnp.zeros(acc_s.shape, F32)

    def attend_tile(j, carry):
        ks = pl.ds(pl.multiple_of(j * tk, tk), tk)
        s_all = _dot_nt(qpad_s[...], k_ref[0, ks, :])
        bias = [jnp.where(keys_s[j, c * LANES:(c + 1) * LANES, :] >= thr_sel, 0.0, NEG).T for c in range(n_chunks)]
        for g in range(kv_heads):
            ps, alphas = [], []
            for r in range(per_kv):
                rows = slice((g * per_kv + r) * tq, (g * per_kv + r + 1) * tq)
                s = [s_all[rows, c * LANES:(c + 1) * LANES] + bias[c] for c in range(n_chunks)]
                smax = s[0]
                for c in range(1, n_chunks):
                    smax = jnp.maximum(smax, s[c])
                m_old = m_s[rows]
                m_new = jnp.maximum(m_old, jnp.max(smax, axis=1, keepdims=True))
                m_s[rows] = m_new
                alphas.append(jnp.exp2(m_old - m_new))
                ps.append(jnp.concatenate([jnp.exp2(s[c] - m_new) for c in range(n_chunks)], axis=1).astype(BF16))
            grows = slice(g * per_kv * tq, (g + 1) * per_kv * tq)
            pv = _dot(jnp.concatenate(ps, axis=0), vx_ref[0, ks, g * LANES:(g + 1) * LANES])
            acc_s[grows] = acc_s[grows] * jnp.concatenate(alphas, axis=0) + pv
        return carry
    lax.fori_loop(0, n_kt, attend_tile, 0)

    def normalised(h):
        a = acc_s[h * tq:(h + 1) * tq]
        return a / pltpu.roll(a, hd, 1)
    for s in range(n_heads // 2):
        pb, r = divmod(s, per_kv)
        o_lo = normalised((2 * pb) * per_kv + r)
        o_hi = normalised((2 * pb + 1) * per_kv + r)
        y_ref[0, :, s * LANES:(s + 1) * LANES] = jnp.where(low, o_lo, o_hi).astype(y_ref.dtype)


def _dsa_prompt(q, qi, sa, kk, k, v, *, n_heads, kv_heads, idx_heads, wi_lane, att_scale, idx_w_scale):
    b, l, dq = q.shape
    tq, tk = min(Q_TILE, l), min(KEY_TILE, l // 2)
    assert l % tq == 0 and l % (2 * tk) == 0 and kv_heads % 2 == 0 and idx_heads % 2 == 0
    assert wi_lane % SUBLANES == 0 and idx_heads == SUBLANES
    topk = min(TOPK_MAX, l // 4)
    kvw = k.shape[-1]
    hd = LANES // 2
    ones = jnp.ones((b, l, hd), BF16)
    vx = jnp.concatenate([a for g in range(kv_heads)
                          for a in ((v[:, :, g * hd:(g + 1) * hd], ones) if g % 2 == 0 else
                                    (ones, v[:, :, g * hd:(g + 1) * hd]))], axis=-1)
    return pl.pallas_call(
        functools.partial(_dsa_prompt_kernel, tq=tq, tk=tk, n_heads=n_heads, kv_heads=kv_heads, idx_heads=idx_heads,
                          topk=topk, att_scale=att_scale, idx_w_scale=idx_w_scale, wi_lane=wi_lane, seq_len=l),
        grid=(b, l // tq),
        in_specs=[
            pl.BlockSpec((1, tq, dq), lambda bi, i: (bi, i, 0)),
            pl.BlockSpec((1, tq, qi.shape[-1]), lambda bi, i: (bi, i, 0)),
            pl.BlockSpec((1, tq, LANES), lambda bi, i: (bi, i, 0)),
            pl.BlockSpec((1, l, LANES), lambda bi, i: (bi, 0, 0), pipeline_mode=pl.Buffered(1)),
            pl.BlockSpec((1, l, kvw), lambda bi, i: (bi, 0, 0), pipeline_mode=pl.Buffered(1)),
            pl.BlockSpec((1, l, vx.shape[-1]), lambda bi, i: (bi, 0, 0), pipeline_mode=pl.Buffered(1)),
        ],
        out_specs=pl.BlockSpec((1, tq, dq), lambda bi, i: (bi, i, 0)),
        out_shape=jax.ShapeDtypeStruct((b, l, dq), BF16),
        scratch_shapes=[
            pltpu.VMEM((l // tk, tk, tq), I32),
            pltpu.VMEM((n_heads * tq, kvw), BF16),
            pltpu.VMEM((idx_heads * tq, LANES), BF16),
            pltpu.VMEM((idx_heads, tq), F32),
            pltpu.VMEM((n_heads * tq, LANES), F32),
            pltpu.VMEM((n_heads * tq, LANES), F32),
        ],
        compiler_params=_params("parallel", "arbitrary"),
        name="dsa_prompt",
    )(q, qi, sa, kk, k, vx)


def _dsa_sample_scores_kernel(pt_ref, qi_ref, sa_ref, kn_ref, *rest, pg, t_new, idx_heads, page,
                              idx_w_scale, wi_lane, n_keys):
    page_refs = rest[:pg]
    keys_ref, qi_s, w_s, kn_s = rest[pg:]
    i = pl.program_id(1)
    hd = LANES // 2
    n_steps = pl.num_programs(1)

    @pl.when(i == 0)
    def _():
        qif = qi_ref[0]
        sa = sa_ref[0]
        for h in range(idx_heads):
            qi_s[h * t_new:(h + 1) * t_new, :] = qif[:, h * LANES:h * LANES + hd].astype(F32)
            w_s[h * t_new:(h + 1) * t_new, :] = jnp.broadcast_to(
                sa[:, wi_lane + h:wi_lane + h + 1] * idx_w_scale, (t_new, LANES))
        kn_s[...] = jnp.zeros(kn_s.shape, F32)
        kn_s[0:t_new, :] = kn_ref[0][:, 0:hd].astype(F32)

    def scores(dots):
        ww = jnp.maximum(dots, 0.0) * w_s[...]
        sc = ww[0:t_new]
        for h in range(1, idx_heads):
            sc = sc + ww[h * t_new:(h + 1) * t_new]
        return sc

    always = jnp.full((t_new, page), True)
    kt = jnp.concatenate([page_refs[r][0, 0].astype(BF16) for r in range(pg)], axis=1)
    dots = _dot(qi_s[...].astype(BF16), kt)
    for r in range(pg):
        off = pl.multiple_of((i * pg + r) * page, page)
        keys_ref[0, :, pl.ds(off, page)] = _score_key(scores(dots[:, r * page:(r + 1) * page]), always)

    @pl.when(i == n_steps - 1)
    def _():
        ti = lax.broadcasted_iota(I32, (t_new, page), 0)
        ki = lax.broadcasted_iota(I32, (t_new, page), 1)
        past = n_keys - page
        dots_new = _dot_nt(qi_s[...].astype(BF16), kn_s[...].astype(BF16))
        keys_ref[0, :, past:n_keys] = _score_key(scores(dots_new), ki <= ti)


def _dsa_sample_threshold_kernel(keys_ref, thr_ref, cut_ref, *, rb, topk, n_keys):
    n_chunks = n_keys // LANES
    unroll = next(u for u in (4, 3, 2, 1) if n_chunks % u == 0)
    liota = lax.broadcasted_iota(I32, (rb, LANES), 1)
    for b in range(keys_ref.shape[0] // rb):
        rows = slice(b * rb, (b + 1) * rb)

        def count(pred):
            def body(cc, cnt):
                for u in range(unroll):
                    off = pl.multiple_of((cc * unroll + u) * LANES, LANES)
                    cnt = cnt + jnp.where(pred(keys_ref[rows, pl.ds(off, LANES)], off), 1.0, 0.0)
                return cnt
            cnt = lax.fori_loop(0, n_chunks // unroll, body, jnp.zeros((rb, LANES), F32))
            return jnp.sum(cnt, axis=1, keepdims=True)

        def count_ge(t):
            tb = jnp.broadcast_to(t, (rb, LANES))
            return count(lambda key, off: key >= tb)

        thr, c_ge = _kth_largest_key(count_ge, (rb, 1), topk)
        tie = c_ge > topk
        thr_ref[rows, :] = jnp.broadcast_to(jnp.maximum(thr, jnp.int32(INT_MIN + 1)), (rb, LANES))
        cut_ref[rows, :] = jnp.full((rb, LANES), n_keys, I32)

        @pl.when(jnp.max(jnp.where(tie, 1, 0)) > 0)
        def _():
            need = topk - count(lambda key, off: key > thr)
            cut = _tie_cutoff(lambda p: count(lambda key, off: (key == thr) & (off + liota < p)), need, (rb, 1),
                              int(n_keys).bit_length())
            cut_ref[rows, :] = jnp.broadcast_to(jnp.where(tie, cut, n_keys), (rb, LANES))


def _dsa_sample_attend_kernel(pt_ref, q_ref, keys_ref, thr_ref, cut_ref, kn_ref, vn_ref, *rest, pg, pc, t_new,
                              n_heads, kv_heads, page, att_scale, n_keys):
    k_refs = rest[:pg]
    v_refs = rest[pg:2 * pg]
    y_ref, qpad_s, m_s, l_s, acc_s, kn_s, vn_s = rest[2 * pg:]
    i = pl.program_id(1)
    hd = LANES // 2
    per_kv = n_heads // kv_heads
    n_steps = pl.num_programs(1)
    lane = lax.broadcasted_iota(I32, (t_new, LANES), 1)
    low = lane < hd

    @pl.when(i == 0)
    def _():
        qf = q_ref[0].astype(F32) * att_scale
        zeros = jnp.zeros((t_new, LANES), F32)
        for h in range(n_heads):
            g, r = divmod(h, per_kv)
            slot = (g // 2) * per_kv + r
            sl = qf[:, slot * LANES:(slot + 1) * LANES]
            half = jnp.where(low, sl, 0.0) if g % 2 == 0 else jnp.where(low, 0.0, sl)
            row = [zeros] * (kv_heads // 2)
            row[g // 2] = half
            qpad_s[h * t_new:(h + 1) * t_new, :] = jnp.concatenate(row, axis=1)
        m_s[...] = jnp.full(m_s.shape, NEG, F32)
        l_s[...] = jnp.zeros(l_s.shape, F32)
        acc_s[...] = jnp.zeros(acc_s.shape, F32)
        kn_s[...] = jnp.zeros(kn_s.shape, F32)
        vn_s[...] = jnp.zeros(vn_s.shape, F32)
        kn_s[0:t_new, :] = kn_ref[0].astype(F32)
        vn_s[0:t_new, :] = vn_ref[0].astype(F32)

    thr = thr_ref[0]
    cut = cut_ref[0]

    def attend(s, key, pos0, pv, chain):
        n = key.shape[1]
        wide = lambda a: jnp.concatenate([a] * (n // LANES), axis=1)
        pos = pos0 + lax.broadcasted_iota(I32, (t_new, n), 1)
        sel = (key > wide(thr)) | ((key == wide(thr)) & (pos <= wide(cut)))
        bias = jnp.where(sel, 0.0, NEG)
        s = s + jnp.concatenate([bias] * n_heads, axis=0)
        m_old = m_s[chain]
        m_new = jnp.maximum(m_old, jnp.max(s, axis=1, keepdims=True))
        alpha = jnp.exp(m_old - m_new)
        p = jnp.exp(s - m_new)
        l_s[chain] = alpha * l_s[chain] + jnp.sum(p, axis=1, keepdims=True)
        m_s[chain] = m_new
        acc_s[chain] = acc_s[chain] * alpha + pv(p.astype(BF16))

    n_chains = m_s.shape[0]
    qb = qpad_s[...].astype(BF16)
    kvw = qb.shape[1]
    for c in range(pg // pc):
        off = pl.multiple_of((i * pg + c * pc) * page, page)
        kt = jnp.concatenate([k_refs[c * pc + r][0, 0].reshape(kvw, page).astype(BF16) for r in range(pc)], axis=1)
        vt = jnp.concatenate([v_refs[c * pc + r][0, 0].reshape(kvw, page).astype(BF16) for r in range(pc)], axis=1)
        attend(_dot(qb, kt), keys_ref[0, :, pl.ds(off, pc * page)], off, lambda p: _dot_nt(p, vt), c % n_chains)

    @pl.when(i == n_steps - 1)
    def _():
        attend(_dot_nt(qb, kn_s[...].astype(BF16)), keys_ref[0, :, n_keys - page:n_keys], n_keys - page,
               lambda p: _dot(p, vn_s[...].astype(BF16)), 0)
        m = m_s[0]
        for c in range(1, n_chains):
            m = jnp.maximum(m, m_s[c])
        l = sum(l_s[c] * jnp.exp(m_s[c] - m) for c in range(n_chains))
        acc = sum(acc_s[c] * jnp.exp(m_s[c] - m) for c in range(n_chains))
        o = acc / l
        for s in range(n_heads // 2):
            pb, r = divmod(s, per_kv)
            h_lo = (2 * pb) * per_kv + r
            h_hi = (2 * pb + 1) * per_kv + r
            o_lo = o[h_lo * t_new:(h_lo + 1) * t_new, pb * LANES:(pb + 1) * LANES]
            o_hi = o[h_hi * t_new:(h_hi + 1) * t_new, pb * LANES:(pb + 1) * LANES]
            y_ref[0, :, s * LANES:(s + 1) * LANES] = jnp.where(low, o_lo, o_hi).astype(y_ref.dtype)


def _pages_per_step(n_pages):
    return math.gcd(n_pages, PAGES_PER_STEP_MAX)


def _dsa_sample(q, qiw, sa, kk_new, k_new, v_new, ck, cv, ci, page_table, layer, *, n_heads, kv_heads, idx_heads,
                wi_lane, att_scale, idx_w_scale):
    db, t_new, dq = q.shape
    _, n_pool, kv_heads_, hd, page = ck.shape
    kvw = kv_heads_ * hd
    n_pages = page_table.shape[1]
    assert page == LANES and t_new % SUBLANES == 0 and t_new <= page and kv_heads_ == kv_heads and hd == LANES // 2
    pg = _pages_per_step(n_pages)
    n_chains = 2 if pg % 2 == 0 else 1
    pc = pg // n_chains
    n_steps = n_pages // pg
    n_keys = (n_pages + 1) * page
    topk = min(TOPK_MAX, (n_pages * page + t_new) // 4)
    idx_spec = lambda r: pl.BlockSpec((1, 1, hd, page), lambda b, i, pt: (layer, pt[b, i * pg + r], 0, 0))
    kv_spec = lambda r: pl.BlockSpec((1, 1, kv_heads, hd, page), lambda b, i, pt: (layer, pt[b, i * pg + r], 0, 0, 0))
    row_spec = lambda w: pl.BlockSpec((1, t_new, w), lambda b, i, pt: (b, 0, 0))

    keys = pl.pallas_call(
        functools.partial(_dsa_sample_scores_kernel, pg=pg, t_new=t_new, idx_heads=idx_heads, page=page,
                          idx_w_scale=idx_w_scale, wi_lane=wi_lane, n_keys=n_keys),
        grid_spec=pltpu.PrefetchScalarGridSpec(
            num_scalar_prefetch=1,
            grid=(db, n_steps),
            in_specs=[row_spec(qiw.shape[-1]), row_spec(LANES), row_spec(LANES)] + [idx_spec(r) for r in range(pg)],
            out_specs=row_spec(n_keys),
            scratch_shapes=[pltpu.VMEM((idx_heads * t_new, hd), F32), pltpu.VMEM((idx_heads * t_new, LANES), F32),
                            pltpu.VMEM((page, hd), F32)],
        ),
        out_shape=jax.ShapeDtypeStruct((db, t_new, n_keys), I32),
        compiler_params=_params("parallel", "arbitrary"),
        name="dsa_sample_scores",
    )(page_table, qiw, sa, kk_new, *([ci] * pg))

    n_rows = db * t_new
    rb = math.gcd(n_rows, LANES)
    thr, cut = pl.pallas_call(
        functools.partial(_dsa_sample_threshold_kernel, rb=rb, topk=topk, n_keys=n_keys),
        out_shape=[jax.ShapeDtypeStruct((n_rows, LANES), I32)] * 2,
        compiler_params=pltpu.CompilerParams(vmem_limit_bytes=VMEM_LIMIT_BYTES),
        name="dsa_sample_threshold",
    )(keys.reshape(n_rows, n_keys))
    thr = thr.reshape(db, t_new, LANES)
    cut = cut.reshape(db, t_new, LANES)

    return pl.pallas_call(
        functools.partial(_dsa_sample_attend_kernel, pg=pg, pc=pc, t_new=t_new, n_heads=n_heads, kv_heads=kv_heads,
                          page=page, att_scale=att_scale, n_keys=n_keys),
        grid_spec=pltpu.PrefetchScalarGridSpec(
            num_scalar_prefetch=1,
            grid=(db, n_steps),
            in_specs=[row_spec(dq), row_spec(n_keys), row_spec(LANES), row_spec(LANES), row_spec(kvw), row_spec(kvw)]
                     + [kv_spec(r) for r in range(pg)] * 2,
            out_specs=row_spec(dq),
            scratch_shapes=[pltpu.VMEM((n_heads * t_new, kvw), F32), pltpu.VMEM((n_chains, n_heads * t_new, 1), F32),
                            pltpu.VMEM((n_chains, n_heads * t_new, 1), F32),
                            pltpu.VMEM((n_chains, n_heads * t_new, kvw), F32),
                            pltpu.VMEM((page, kvw), F32), pltpu.VMEM((page, kvw), F32)],
        ),
        out_shape=jax.ShapeDtypeStruct((db, t_new, dq), BF16),
        compiler_params=_params("parallel", "arbitrary"),
        name="dsa_sample_attend",
    )(page_table, q, keys, thr, cut, k_new, v_new, *([ck] * pg), *([cv] * pg))


def _merge_kernel(x_ref, ys_ref, ya_ref, gs_ref, ga_ref, ps_ref, pa_ref, wo_ref, o_ref):
    merged = (_sigmoid(gs_ref[...]) * _dot(ys_ref[...], ps_ref[...])
              + _sigmoid(ga_ref[...]) * _dot(ya_ref[...], pa_ref[...]))
    o_ref[...] = x_ref[...] + _dot(merged.astype(BF16), wo_ref[...])


def _merge(x, y_ssd, y_attn, g_s, g_a, p_ssd, p_attn, w_out):
    t, d = x.shape
    tm = min(ROW_TILE, t)
    assert t % tm == 0
    rows = lambda w: pl.BlockSpec((tm, w), lambda i: (i, 0))
    full = lambda a: pl.BlockSpec(a.shape, lambda i: (0, 0))
    return pl.pallas_call(
        _merge_kernel,
        grid=(t // tm,),
        in_specs=[rows(d), rows(y_ssd.shape[1]), rows(y_attn.shape[1]), rows(d), rows(d),
                  full(p_ssd), full(p_attn), full(w_out)],
        out_specs=rows(d),
        out_shape=jax.ShapeDtypeStruct((t, d), F32),
        compiler_params=_params("parallel"),
        name="merge",
    )(x, y_ssd, y_attn, g_s, g_a, p_ssd, p_attn, w_out)


def _pair_slot_perm(n_heads, kv_heads, hd):
    per_kv = n_heads // kv_heads
    cols = []
    for s in range(n_heads // 2):
        pb, r = divmod(s, per_kv)
        for h in ((2 * pb) * per_kv + r, (2 * pb + 1) * per_kv + r):
            cols.extend(range(h * hd, (h + 1) * hd))
    return np.asarray(cols, np.int32)


def kernel(x_prompt, x_sample, cache_k, cache_v, cache_idx_k, state_ssm, state_conv, page_table, ffn1_norm, ffn1_w1, ffn1_w2, mix_norm, w_in, conv_w, conv_b, dt_bias, a_log, d_skip, ssd_norm, w_branch_ssd, w_branch_attn, w_out, ffn2_norm, ffn2_w1, ffn2_w2, final_norm):
    bp, seq, d_model = x_prompt.shape
    db, dseq, _ = x_sample.shape
    depth, n_pool, page, kv_heads, head_dim = cache_k.shape
    idx_dim = cache_idx_k.shape[-1]
    ssd_heads, ssd_hd, d_state = state_ssm.shape[2:]
    conv_dim = state_conv.shape[-1]
    d_inner = ssd_norm.shape[-1]
    n_groups = (conv_dim - d_inner) // (2 * d_state)
    attn_dim = w_branch_attn.shape[1]
    n_heads = attn_dim // head_dim
    kvw = kv_heads * head_dim
    d_proj = w_in.shape[-1]
    idx_heads = (d_proj - (d_inner + conv_dim + ssd_heads + attn_dim + 2 * kvw + idx_dim + 2 * d_model)) // (idx_dim + 1)
    assert head_dim == LANES // 2 and idx_dim == LANES // 2 and ssd_heads + idx_heads <= LANES
    att_scale = head_dim ** -0.5
    idx_w_scale = (idx_heads ** -0.5) * (idx_dim ** -0.5)
    sizes = (d_inner, conv_dim, ssd_heads, attn_dim, kvw, kvw, idx_heads * idx_dim, idx_dim, idx_heads, d_model, d_model)
    assert sum(sizes) == d_proj
    offs = np.concatenate([[0], np.cumsum(sizes)])
    perm = _pair_slot_perm(n_heads, kv_heads, head_dim)
    wi_lane = ssd_heads

    tp, ts = bp * seq, db * dseq
    yp = x_prompt.reshape(tp, d_model)
    ys = x_sample.reshape(ts, d_model)
    ck = jnp.transpose(cache_k, (0, 1, 3, 4, 2))
    cv = jnp.transpose(cache_v, (0, 1, 3, 4, 2))
    ci = jnp.transpose(cache_idx_k, (0, 1, 3, 2))
    dsa_kw = dict(n_heads=n_heads, kv_heads=kv_heads, idx_heads=idx_heads, wi_lane=wi_lane, att_scale=att_scale,
                  idx_w_scale=idx_w_scale)
    outs = {n: [] for n in ("kp", "vp", "ip", "sp", "cp", "ks", "vs", "is", "ss", "cs")}
    hist = conv_w.shape[1] - 1

    for l in range(depth):
        wl = w_in[l]
        col = lambda i: wl[:, offs[i]:offs[i + 1]]
        w_z, w_xbc, w_dt, w_q, w_k, w_v, w_qi, w_ki, w_wi, w_gs, w_ga = [col(i) for i in range(11)]
        w_sa = jnp.concatenate([w_dt, w_wi, jnp.zeros((d_model, LANES - ssd_heads - idx_heads), F32)], axis=1)
        w_qi_wide = jnp.pad(w_qi.reshape(d_model, idx_heads, idx_dim), ((0, 0), (0, 0), (0, LANES - idx_dim)))
        bf = lambda w: w.astype(BF16)
        wa = [bf(w_z), bf(w_xbc)]
        wb_common = [bf(w_q[:, perm]), bf(w_k), bf(w_v), bf(jnp.concatenate([w_ki, w_ki], axis=1)), bf(w_sa),
                     bf(w_gs), bf(w_ga)]
        dt_common = [(BF16,), (F32, BF16), (F32, BF16), (F32, BF16), (F32,), (F32,), (F32,)]
        f1w1, f1w2, f2w1, f2w2 = bf(ffn1_w1[l]), bf(ffn1_w2[l]), bf(ffn2_w1[l]), bf(ffn2_w2[l])
        p_ssd, p_attn, wo = bf(w_branch_ssd[l]), bf(w_branch_attn[l][perm, :]), bf(w_out[l])
        last = l == depth - 1

        def mixer(y, b, s, qi_weight, attend, buf, h0, state_layer):
            z, xbc = _norm_linear(y, mix_norm[l], wa, [(F32,), (F32,)], PROJ_ROW_TILE)
            q, k, kb, v, vb, kk, kkb, sa, g_s, g_a, qi = _norm_linear(
                y, mix_norm[l], wb_common + [qi_weight], dt_common + [(BF16,)], PROJ_ROW_TILE)
            r3 = lambda a: a.reshape(b, s, a.shape[-1])
            y_ssd, h_fin = _ssd(r3(z), r3(xbc), r3(sa), buf, h0, state_layer, conv_w[l], conv_b[l], dt_bias[l],
                                a_log[l], d_skip[l], ssd_norm[l], n_groups=n_groups, d_state=d_state)
            y_attn = attend(r3(q), r3(qi), r3(sa), r3(kkb), r3(kb), r3(vb))
            y = _merge(y, y_ssd.reshape(b * s, d_inner), y_attn.reshape(b * s, attn_dim), g_s, g_a, p_ssd, p_attn, wo)
            new_buf = r3(xbc)[:, s - hist:, :]
            return (y, new_buf, h_fin, k.reshape(b, s, kv_heads, head_dim), v.reshape(b, s, kv_heads, head_dim),
                    r3(kk)[:, :, :idx_dim])

        yp = _ffn(yp, ffn1_norm[l], f1w1, f1w2)
        ys = _ffn(ys, ffn1_norm[l], f1w1, f1w2)

        yp, cbp, hfp, kp, vp, kip = mixer(
            yp, bp, seq, bf(w_qi), functools.partial(_dsa_prompt, **dsa_kw),
            jnp.zeros((1, bp, hist, conv_dim), F32), jnp.zeros((1, bp, ssd_heads, ssd_hd, d_state), F32), 0)
        att_s = functools.partial(_dsa_sample, ck=ck, cv=cv, ci=ci, page_table=page_table, layer=l, **dsa_kw)
        ys, cbs, hfs, kss, vss, kis = mixer(
            ys, db, dseq, bf(w_qi_wide.reshape(d_model, idx_heads * LANES)), att_s, state_conv, state_ssm, l)

        pg_ = final_norm if last else None
        yp = _ffn(yp, ffn2_norm[l], f2w1, f2w2, pg_)
        ys = _ffn(ys, ffn2_norm[l], f2w1, f2w2, pg_)
        for n, a in zip(("kp", "vp", "ip", "sp", "cp", "ks", "vs", "is", "ss", "cs"),
                        (kp, vp, kip, hfp, cbp, kss, vss, kis, hfs, cbs)):
            outs[n].append(a)

    st = lambda n: jnp.stack(outs[n])
    return (yp.reshape(bp, seq, d_model), ys.reshape(db, dseq, d_model),
            st("kp"), st("vp"), st("ip"), st("sp"), st("cp"),
            st("ks"), st("vs"), st("is"), st("ss"), st("cs"))
```

```python
import functools
import math

import jax
import jax.numpy as jnp
import numpy as np
from jax import lax
from jax.experimental import pallas as pl
from jax.experimental.pallas import tpu as pltpu

F32 = jnp.float32
BF16 = jnp.bfloat16
I32 = jnp.int32

EPS = 1e-6
SSD_CHUNK = 128
TOPK_MAX = 256
LANES = 128
SUBLANES = 8
VMEM_LIMIT_BYTES = 56 * 1024 * 1024
NEG = -1e30
INT_MIN = -(2 ** 31)

ROW_TILE = 512
PROJ_ROW_TILE = 256
FF_TILE_MAX = 1536
Q_TILE = 128
KEY_TILE = 512
PAGES_PER_STEP_MAX = 32


def _params(*sem):
    return pltpu.CompilerParams(dimension_semantics=sem, vmem_limit_bytes=VMEM_LIMIT_BYTES)


def _sigmoid(x):
    return 1.0 / (1.0 + jnp.exp(-x))


def _rms(x, g):
    return x * lax.rsqrt(jnp.mean(x * x, axis=-1, keepdims=True) + EPS) * g


def _dot(a, b):
    return jnp.dot(a, b, preferred_element_type=F32)


def _dot_nt(a, b):
    return lax.dot_general(a, b, (((1,), (1,)), ((), ())), preferred_element_type=F32)


def _split2(x):
    hi = x.astype(BF16)
    lo = (x - hi.astype(F32)).astype(BF16)
    return hi, lo


def _split3(x):
    hi = x.astype(BF16)
    r = x - hi.astype(F32)
    mid = r.astype(BF16)
    lo = (r - mid.astype(F32)).astype(BF16)
    return hi, mid, lo


def _ffn_kernel(*refs, post_norm):
    if post_norm:
        x_ref, g_ref, wa_ref, wb_ref, w2_ref, pg_ref, o_ref, h_s, acc_s = refs
    else:
        x_ref, g_ref, wa_ref, wb_ref, w2_ref, o_ref, h_s, acc_s = refs
    f = pl.program_id(1)

    @pl.when(f == 0)
    def _():
        h_s[...] = _rms(x_ref[...], g_ref[...]).astype(BF16)
        acc_s[...] = jnp.zeros_like(acc_s)

    h = h_s[...]
    a = _dot(h, wa_ref[...])
    b = _dot(h, wb_ref[...])
    u = (a * _sigmoid(a) * b).astype(BF16)
    acc_s[...] += _dot(u, w2_ref[...])

    @pl.when(f == pl.num_programs(1) - 1)
    def _():
        y = x_ref[...] + 0.5 * acc_s[...]
        if post_norm:
            y = _rms(y, pg_ref[...])
        o_ref[...] = y


def _ff_tile(d_ff):
    best = None
    for t in range(LANES, d_ff + 1, LANES):
        if d_ff % t == 0 and t <= FF_TILE_MAX:
            best = t
    assert best is not None, d_ff
    return best


def _ffn(x, g, w1, w2, post_gain=None):
    t, d = x.shape
    d_ff = w2.shape[0]
    tm = min(ROW_TILE, t)
    tf = _ff_tile(d_ff)
    nf = d_ff // tf
    assert t % tm == 0
    post_norm = post_gain is not None
    in_specs = [
        pl.BlockSpec((tm, d), lambda i, f: (i, 0)),
        pl.BlockSpec((1, d), lambda i, f: (0, 0)),
        pl.BlockSpec((d, tf), lambda i, f: (0, f)),
        pl.BlockSpec((d, tf), lambda i, f: (0, f + nf)),
        pl.BlockSpec((tf, d), lambda i, f: (f, 0)),
    ]
    args = [x, g.reshape(1, d), w1, w1, w2]
    if post_norm:
        in_specs.append(pl.BlockSpec((1, d), lambda i, f: (0, 0)))
        args.append(post_gain.reshape(1, d))
    return pl.pallas_call(
        functools.partial(_ffn_kernel, post_norm=post_norm),
        grid=(t // tm, nf),
        in_specs=in_specs,
        out_specs=pl.BlockSpec((tm, d), lambda i, f: (i, 0)),
        out_shape=jax.ShapeDtypeStruct((t, d), F32),
        scratch_shapes=[pltpu.VMEM((tm, d), BF16), pltpu.VMEM((tm, d), F32)],
        compiler_params=_params("parallel", "arbitrary"),
        name="ffn",
    )(*args)


def _norm_linear_kernel(*refs, plan):
    n_w = len(plan)
    n_b = sum(has_bias for _, _, has_bias in plan)
    x_ref, g_ref = refs[:2]
    w_refs = refs[2:2 + n_w]
    b_refs = list(refs[2 + n_w:2 + n_w + n_b])
    o_refs = list(refs[2 + n_w + n_b:])
    h = _rms(x_ref[...], g_ref[...]).astype(BF16)
    for w_ref, (dts, transposed, has_bias) in zip(w_refs, plan):
        r = _dot_nt(w_ref[...], h) if transposed else _dot(h, w_ref[...])
        if has_bias:
            r = r + b_refs.pop(0)[...]
        for dt in dts:
            o_ref = o_refs.pop(0)
            if transposed:
                o_ref[0] = r.astype(dt)
            else:
                o_ref[...] = r.astype(dt)


def _norm_linear(x, g, maps, tm, batch=1):
    t, d = x.shape
    tm = min(tm, t)
    s = t // batch
    assert t % tm == 0 and s % tm == 0
    steps = s // tm
    in_specs = [pl.BlockSpec((tm, d), lambda i: (i, 0)), pl.BlockSpec((1, d), lambda i: (0, 0))]
    biases, out_specs, out_shape = [], [], []
    for w, dts, transposed, bias in maps:
        in_specs.append(pl.BlockSpec(w.shape, lambda i: (0, 0)))
        n = w.shape[0] if transposed else w.shape[1]
        for dt in dts:
            if transposed:
                out_specs.append(pl.BlockSpec((1, n, tm), lambda i: (i // steps, 0, i % steps)))
                out_shape.append(jax.ShapeDtypeStruct((batch, n, s), dt))
            else:
                out_specs.append(pl.BlockSpec((tm, n), lambda i: (i, 0)))
                out_shape.append(jax.ShapeDtypeStruct((t, n), dt))
        if bias is not None:
            assert not transposed
            biases.append(bias)
    in_specs += [pl.BlockSpec(b.shape, lambda i: (0, 0)) for b in biases]
    plan = tuple((tuple(dts), transposed, bias is not None) for _, dts, transposed, bias in maps)
    return pl.pallas_call(
        functools.partial(_norm_linear_kernel, plan=plan),
        grid=(t // tm,),
        in_specs=in_specs,
        out_specs=out_specs,
        out_shape=out_shape,
        compiler_params=_params("parallel"),
        name="norm_linear",
    )(x, g.reshape(1, d), *[m[0] for m in maps], *biases)


def _ssd_kernel(z_ref, xbc_ref, dt_ref, buf_ref, h0_ref, cw_ref, cb_ref, dtb_ref, alog_ref, dsk_ref, ng_ref, e_ref,
                y_ref, hfin_ref, xp_s, ht_s, *, qin, d_inner, n_groups, d_state, conv_w):
    q = SSD_CHUNK
    c = pl.program_id(1)
    hp_blocks = d_inner // LANES
    gw = d_inner // n_groups
    assert d_state == LANES and gw % LANES == 0
    pad = SUBLANES
    hist = conv_w - 1

    @pl.when(c == 0)
    def _():
        xp_s[0:pad, :] = buf_ref[0, 0]
        if qin < q:
            xp_s[pad + qin:pad + q, :] = jnp.zeros((q - qin, xp_s.shape[1]), F32)
        for i in range(hp_blocks):
            ht_s[:, i * LANES:(i + 1) * LANES] = h0_ref[0, 0, i * LANES:(i + 1) * LANES, :].T

    xp_s[pad:pad + qin, :] = xbc_ref[0]
    x_cur = xp_s[pad:pad + q, :]
    x_prev = xp_s[0:pad, :]
    row8 = lax.broadcasted_iota(I32, (pad, 1), 0)
    acc = cb_ref[...] + x_cur * cw_ref[hist:hist + 1, :]
    for i in range(hist):
        s = hist - i
        rolled = pltpu.roll(x_cur, s, 0)
        head = jnp.where(row8 < s, pltpu.roll(x_prev, s, 0), rolled[0:pad])
        acc = acc + jnp.concatenate([head, rolled[pad:]], axis=0) * cw_ref[i:i + 1, :]
    xc = acc * _sigmoid(acc)
    tail = xp_s[pad + qin - hist:pad + qin, :]
    xp_s[pad - hist:pad, :] = tail

    xs = xc[:, :d_inner]
    bm = xc[:, d_inner:d_inner + n_groups * d_state]
    cm = xc[:, d_inner + n_groups * d_state:]

    dt_raw = dt_ref[0] + dtb_ref[...]
    dt = jnp.maximum(dt_raw, 0.0) + jnp.log1p(jnp.exp(-jnp.abs(dt_raw)))
    if qin < q:
        dt = jnp.concatenate([dt, jnp.zeros((q - qin, LANES), F32)], axis=0)
    la = dt * (-jnp.exp(alog_ref[...]))

    ri = lax.broadcasted_iota(I32, (q, q), 0)
    ci = lax.broadcasted_iota(I32, (q, q), 1)
    causal = ri >= ci
    tril = jnp.where(causal, 1.0, 0.0).astype(BF16)
    eye = jnp.where(ri == ci, 1.0, 0.0).astype(BF16)
    a_cs = sum(_dot(tril, p) for p in _split3(la))
    a_cs_t = sum(_dot_nt(eye, p) for p in _split3(a_cs))
    dec = jnp.exp(a_cs[q - 1:q, :] - a_cs)
    eac = jnp.exp(a_cs)
    stacked = jnp.concatenate([dt, dec, eac], axis=0)
    expd = sum(_dot(p, e_ref[...]) for p in _split2(stacked))
    dt_e, dec_e, eac_e = expd[0:q], expd[q:2 * q], expd[2 * q:3 * q]

    x = xs * dt_e
    xb = x.astype(BF16)
    xd = (x * dec_e).astype(BF16)
    lane = lax.broadcasted_iota(I32, (q, LANES), 1)
    hd = LANES // 2
    heads_per_group = gw // hd

    y_parts = []
    for g in range(n_groups):
        gs = slice(g * gw, (g + 1) * gw)
        cg = cm[:, g * d_state:(g + 1) * d_state].astype(BF16)
        bg = bm[:, g * d_state:(g + 1) * d_state]
        cb = _dot_nt(cg, bg.astype(BF16))
        y_off = _dot(cg, ht_s[:, gs].astype(BF16))
        pair_parts = []
        for p in range(heads_per_group // 2):
            xp = xb[:, g * gw + p * LANES:g * gw + (p + 1) * LANES]
            res = []
            for k in range(2):
                j = g * heads_per_group + 2 * p + k
                diff = a_cs[:, j:j + 1] - a_cs_t[j:j + 1, :]
                lm = jnp.exp(jnp.where(causal, diff, NEG))
                res.append(_dot((cb * lm).astype(BF16), xp))
            pair_parts.append(jnp.where(lane < hd, res[0], res[1]))
        y_diag = jnp.concatenate(pair_parts, axis=1)
        y_parts.append(y_diag + y_off * eac_e[:, gs])
        st = _dot(bg.T.astype(BF16), xd[:, gs])
        ht_s[:, gs] = ht_s[:, gs] * eac_e[q - 1:q, gs] + st

    zz = z_ref[0]
    outs = []
    for g in range(n_groups):
        gs = slice(g * gw, (g + 1) * gw)
        yv = (y_parts[g][:qin] + dsk_ref[:, gs] * xs[:qin, gs]) * (zz[:, gs] * _sigmoid(zz[:, gs]))
        ms = jnp.mean(yv * yv, axis=-1, keepdims=True)
        outs.append(yv * lax.rsqrt(ms + EPS) * ng_ref[:, gs])
    y_ref[0] = jnp.concatenate(outs, axis=1).astype(y_ref.dtype)

    @pl.when(c == pl.num_programs(1) - 1)
    def _():
        for i in range(hp_blocks):
            hfin_ref[0, i * LANES:(i + 1) * LANES, :] = ht_s[:, i * LANES:(i + 1) * LANES].T


def _ssd(z, xbc, dtp, buf, h0, layer, conv_w, conv_b, dt_bias, a_log, d_skip, norm_g, *, n_groups, d_state):
    b, l, d_inner = z.shape
    conv_dim = xbc.shape[-1]
    n_heads, p_dim, n_state = h0.shape[2:]
    width = conv_w.shape[0]
    hist = width - 1
    assert n_heads <= LANES and p_dim == LANES // 2 and n_state == d_state and hist <= SUBLANES
    qin = math.gcd(l, SSD_CHUNK)
    assert qin % SUBLANES == 0 and qin >= hist
    nc = l // qin
    assert nc == 1 or qin == SSD_CHUNK
    hp = n_heads * p_dim
    buf8 = jnp.pad(buf, ((0, 0), (0, 0), (SUBLANES - hist, 0), (0, 0)))
    cw8 = jnp.pad(conv_w, ((0, SUBLANES - width), (0, 0)))
    pad1 = lambda v: jnp.pad(v.reshape(1, -1), ((0, 0), (0, LANES - n_heads)))
    expand = (np.arange(LANES)[:, None] == (np.arange(d_inner)[None, :] // p_dim)).astype(np.float32)
    full = lambda shape: pl.BlockSpec(shape, lambda i, c: (0,) * len(shape))
    y, hfin = pl.pallas_call(
        functools.partial(_ssd_kernel, qin=qin, d_inner=d_inner, n_groups=n_groups, d_state=d_state, conv_w=width),
        grid=(b, nc),
        in_specs=[
            pl.BlockSpec((1, qin, d_inner), lambda i, c: (i, c, 0)),
            pl.BlockSpec((1, qin, conv_dim), lambda i, c: (i, c, 0)),
            pl.BlockSpec((1, qin, LANES), lambda i, c: (i, c, 0)),
            pl.BlockSpec((1, 1, SUBLANES, conv_dim), lambda i, c: (layer, i, 0, 0)),
            pl.BlockSpec((1, 1, hp, n_state), lambda i, c: (layer, i, 0, 0)),
            full((SUBLANES, conv_dim)), full((1, conv_dim)), full((1, LANES)), full((1, LANES)),
            full((1, d_inner)), full((1, d_inner)), full((LANES, d_inner)),
        ],
        out_specs=[
            pl.BlockSpec((1, qin, d_inner), lambda i, c: (i, c, 0)),
            pl.BlockSpec((1, hp, n_state), lambda i, c: (i, 0, 0)),
        ],
        out_shape=[jax.ShapeDtypeStruct((b, l, d_inner), BF16), jax.ShapeDtypeStruct((b, hp, n_state), F32)],
        scratch_shapes=[pltpu.VMEM((SUBLANES + SSD_CHUNK, conv_dim), F32), pltpu.VMEM((n_state, hp), F32)],
        compiler_params=_params("parallel", "arbitrary"),
        name="ssd",
    )(z, xbc, dtp, buf8, h0.reshape(-1, b, hp, n_state), cw8, conv_b.reshape(1, -1), pad1(dt_bias), pad1(a_log),
      jnp.repeat(d_skip, p_dim).reshape(1, -1), norm_g.reshape(1, -1), jnp.asarray(expand, BF16))
    return y, hfin.reshape(b, n_heads, p_dim, n_state)


def _score_key(sc, valid):
    bits = lax.bitcast_convert_type(sc + 0.0, I32)
    key = jnp.where(bits < 0, bits ^ jnp.int32(0x7FFFFFFF), bits)
    return jnp.where(valid, key, jnp.int32(INT_MIN))


def _kth_largest_key(count_ge, shape, k):
    def body(b, st):
        t, c_t = st
        cand = t + lax.shift_left(jnp.int32(1), jnp.int32(31) - b)
        c = count_ge(cand)
        return jnp.where(c >= k, cand, t), jnp.where(c >= k, c, c_t)
    return lax.fori_loop(0, 32, body, (jnp.full(shape, INT_MIN, I32), jnp.zeros(shape, F32)))


def _tie_cutoff(count_tie_before, need, shape, n_bits):
    def body(b, p):
        bit = lax.shift_left(jnp.int32(1), jnp.int32(n_bits - 1) - b)
        p_c = p | bit
        return jnp.where(count_tie_before(p_c) <= need - 1, p_c, p)
    return lax.fori_loop(0, n_bits, body, jnp.zeros(shape, I32))


def _dsa_prompt_kernel(q_ref, qi_ref, sa_ref, kk_ref, kt_ref, vx_ref, y_ref,
                       keys_s, qpad_s, qipad_s, w_s, m_s, acc_s,
                       *, tq, tk, n_heads, kv_heads, idx_heads, topk, att_scale, idx_w_scale, wi_lane, seq_len):
    i = pl.program_id(1)
    hd = LANES // 2
    per_kv = n_heads // kv_heads
    n_kt = (i * tq + tq - 1) // tk + 1
    lane = lax.broadcasted_iota(I32, (tq, LANES), 1)
    low = lane < hd

    qf = q_ref[0].astype(F32) * (att_scale * math.log2(math.e))
    zeros = jnp.zeros((tq, LANES), F32)
    for h in range(n_heads):
        g, r = divmod(h, per_kv)
        slot = (g // 2) * per_kv + r
        sl = qf[:, slot * LANES:(slot + 1) * LANES]
        half = jnp.where(low, sl, 0.0) if g % 2 == 0 else jnp.where(low, 0.0, sl)
        row = [zeros] * (kv_heads // 2)
        row[g // 2] = half
        qpad_s[h * tq:(h + 1) * tq, :] = jnp.concatenate(row, axis=1).astype(BF16)
    qif = qi_ref[0].astype(F32)
    for h in range(idx_heads):
        sl = qif[:, (h // 2) * LANES:(h // 2 + 1) * LANES]
        half = jnp.where(low, sl, 0.0) if h % 2 == 0 else jnp.where(low, 0.0, sl)
        qipad_s[h * tq:(h + 1) * tq, :] = half.astype(BF16)
    w_s[...] = sa_ref[0].T[wi_lane:wi_lane + idx_heads, :] * idx_w_scale

    qpos = i * tq + lax.broadcasted_iota(I32, (tk, tq), 1)
    kiota = lax.broadcasted_iota(I32, (tk, tq), 0)
    siota = lax.broadcasted_iota(I32, (SUBLANES, tq), 0)

    tiles_per_step = 2

    def score_tiles(jj, carry):
        for u in range(tiles_per_step):
            j = jj * tiles_per_step + u
            kt = kk_ref[0, pl.ds(pl.multiple_of(j * tk, tk), tk), :]
            r = _dot_nt(kt, qipad_s[...])
            sc = jnp.zeros((tk, tq), F32)
            for h in range(idx_heads):
                sc = sc + jnp.maximum(r[:, h * tq:(h + 1) * tq], 0.0) * w_s[h:h + 1, :]
            keys_s[j] = _score_key(sc, j * tk + kiota <= qpos)
        return carry
    lax.fori_loop(0, (n_kt + tiles_per_step - 1) // tiles_per_step, score_tiles, 0)

    n_acc = 4

    def count(pred):
        def body(j, cnts):
            cnts = list(cnts)
            for g in range(tk // SUBLANES):
                hit = pred(keys_s[j, g * SUBLANES:(g + 1) * SUBLANES, :], j * tk + g * SUBLANES)
                cnts[g % n_acc] = cnts[g % n_acc] + jnp.where(hit, 1.0, 0.0)
            return tuple(cnts)
        cnts = lax.fori_loop(0, n_kt, body, (jnp.zeros((SUBLANES, tq), F32),) * n_acc)
        return jnp.sum(sum(cnts), axis=0, keepdims=True)

    def count_ge(t):
        tb = jnp.broadcast_to(t, (SUBLANES, tq))
        return count(lambda key, base: key >= tb)

    thr, c_ge = _kth_largest_key(count_ge, (1, tq), topk)
    tie = c_ge > topk

    @pl.when(jnp.max(jnp.where(tie, 1, 0)) > 0)
    def _():
        need = topk - count(lambda key, base: key > thr)
        cut = _tie_cutoff(lambda p: count(lambda key, base: (key == thr) & (base + siota < p)),
                          need, (1, tq), int(seq_len).bit_length())
        def drop(j, carry):
            key = keys_s[j]
            keys_s[j] = jnp.where(tie & (key == thr) & (j * tk + kiota > cut), jnp.int32(INT_MIN), key)
            return carry
        lax.fori_loop(0, n_kt, drop, 0)

    thr_sel = jnp.maximum(thr, jnp.int32(INT_MIN + 1))

    n_chunks = tk // LANES
    m_s[...] = jnp.full(m_s.shape, NEG, F32)
    acc_s[...] = jnp.zeros(acc_s.shape, F32)

    def attend_tile(j, carry):
        ks = pl.ds(pl.multiple_of(j * tk, tk), tk)
        s_all = _dot(qpad_s[...], kt_ref[0, :, ks])
        bias = [jnp.where(keys_s[j, c * LANES:(c + 1) * LANES, :] >= thr_sel, 0.0, NEG).T for c in range(n_chunks)]
        for g in range(kv_heads):
            ps, alphas = [], []
            for r in range(per_kv):
                rows = slice((g * per_kv + r) * tq, (g * per_kv + r + 1) * tq)
                s = [s_all[rows, c * LANES:(c + 1) * LANES] + bias[c] for c in range(n_chunks)]
                smax = s[0]
                for c in range(1, n_chunks):
                    smax = jnp.maximum(smax, s[c])
                m_old = m_s[rows]
                m_new = jnp.maximum(m_old, jnp.max(smax, axis=1, keepdims=True))
                m_s[rows] = m_new
                alphas.append(jnp.exp2(m_old - m_new))
                ps.append(jnp.concatenate([jnp.exp2(s[c] - m_new) for c in range(n_chunks)], axis=1).astype(BF16))
            grows = slice(g * per_kv * tq, (g + 1) * per_kv * tq)
            pv = _dot(jnp.concatenate(ps, axis=0), vx_ref[0, ks, g * LANES:(g + 1) * LANES])
            acc_s[grows] = acc_s[grows] * jnp.concatenate(alphas, axis=0) + pv
        return carry
    lax.fori_loop(0, n_kt, attend_tile, 0)

    def normalised(h):
        a = acc_s[h * tq:(h + 1) * tq]
        return a / pltpu.roll(a, hd, 1)
    for s in range(n_heads // 2):
        pb, r = divmod(s, per_kv)
        o_lo = normalised((2 * pb) * per_kv + r)
        o_hi = normalised((2 * pb + 1) * per_kv + r)
        y_ref[0, :, s * LANES:(s + 1) * LANES] = jnp.where(low, o_lo, o_hi).astype(y_ref.dtype)


def _dsa_prompt(q, qi, sa, kk, kt, vx, *, n_heads, kv_heads, idx_heads, wi_lane, att_scale, idx_w_scale):
    b, l, dq = q.shape
    tq, tk = min(Q_TILE, l), min(KEY_TILE, l // 2)
    assert l % tq == 0 and l % (2 * tk) == 0 and kv_heads % 2 == 0 and idx_heads % 2 == 0
    assert wi_lane % SUBLANES == 0 and idx_heads == SUBLANES
    topk = min(TOPK_MAX, l // 4)
    kvw = kt.shape[1]
    return pl.pallas_call(
        functools.partial(_dsa_prompt_kernel, tq=tq, tk=tk, n_heads=n_heads, kv_heads=kv_heads, idx_heads=idx_heads,
                          topk=topk, att_scale=att_scale, idx_w_scale=idx_w_scale, wi_lane=wi_lane, seq_len=l),
        grid=(b, l // tq),
        in_specs=[
            pl.BlockSpec((1, tq, dq), lambda bi, i: (bi, i, 0)),
            pl.BlockSpec((1, tq, qi.shape[-1]), lambda bi, i: (bi, i, 0)),
            pl.BlockSpec((1, tq, LANES), lambda bi, i: (bi, i, 0)),
            pl.BlockSpec((1, l, LANES), lambda bi, i: (bi, 0, 0), pipeline_mode=pl.Buffered(1)),
            pl.BlockSpec((1, kvw, l), lambda bi, i: (bi, 0, 0), pipeline_mode=pl.Buffered(1)),
            pl.BlockSpec((1, l, vx.shape[-1]), lambda bi, i: (bi, 0, 0), pipeline_mode=pl.Buffered(1)),
        ],
        out_specs=pl.BlockSpec((1, tq, dq), lambda bi, i: (bi, i, 0)),
        out_shape=jax.ShapeDtypeStruct((b, l, dq), BF16),
        scratch_shapes=[
            pltpu.VMEM((l // tk, tk, tq), I32),
            pltpu.VMEM((n_heads * tq, kvw), BF16),
            pltpu.VMEM((idx_heads * tq, LANES), BF16),
            pltpu.VMEM((idx_heads, tq), F32),
            pltpu.VMEM((n_heads * tq, LANES), F32),
            pltpu.VMEM((n_heads * tq, LANES), F32),
        ],
        compiler_params=_params("parallel", "arbitrary"),
        name="dsa_prompt",
    )(q, qi, sa, kk, kt, vx)


def _dsa_sample_scores_kernel(pt_ref, qi_ref, sa_ref, kn_ref, *rest, pg, t_new, idx_heads, page,
                              idx_w_scale, wi_lane, n_keys):
    page_refs = rest[:pg]
    keys_ref, qi_s, w_s, kn_s = rest[pg:]
    i = pl.program_id(1)
    hd = LANES // 2
    n_steps = pl.num_programs(1)

    @pl.when(i == 0)
    def _():
        qif = qi_ref[0]
        sa = sa_ref[0]
        for h in range(idx_heads):
            qi_s[h * t_new:(h + 1) * t_new, :] = qif[:, h * LANES:h * LANES + hd].astype(F32)
            w_s[h * t_new:(h + 1) * t_new, :] = jnp.broadcast_to(
                sa[:, wi_lane + h:wi_lane + h + 1] * idx_w_scale, (t_new, LANES))
        kn_s[...] = jnp.zeros(kn_s.shape, F32)
        kn_s[0:t_new, :] = kn_ref[0][:, 0:hd].astype(F32)

    def scores(dots):
        ww = jnp.maximum(dots, 0.0) * w_s[...]
        sc = ww[0:t_new]
        for h in range(1, idx_heads):
            sc = sc + ww[h * t_new:(h + 1) * t_new]
        return sc

    always = jnp.full((t_new, page), True)
    kt = jnp.concatenate([page_refs[r][0, 0].astype(BF16) for r in range(pg)], axis=1)
    dots = _dot(qi_s[...].astype(BF16), kt)
    for r in range(pg):
        off = pl.multiple_of((i * pg + r) * page, page)
        keys_ref[0, :, pl.ds(off, page)] = _score_key(scores(dots[:, r * page:(r + 1) * page]), always)

    @pl.when(i == n_steps - 1)
    def _():
        ti = lax.broadcasted_iota(I32, (t_new, page), 0)
        ki = lax.broadcasted_iota(I32, (t_new, page), 1)
        past = n_keys - page
        dots_new = _dot_nt(qi_s[...].astype(BF16), kn_s[...].astype(BF16))
        keys_ref[0, :, past:n_keys] = _score_key(scores(dots_new), ki <= ti)


def _dsa_sample_threshold_kernel(keys_ref, thr_ref, cut_ref, *, rb, topk, n_keys):
    n_chunks = n_keys // LANES
    unroll = next(u for u in (4, 3, 2, 1) if n_chunks % u == 0)
    liota = lax.broadcasted_iota(I32, (rb, LANES), 1)
    for b in range(keys_ref.shape[0] // rb):
        rows = slice(b * rb, (b + 1) * rb)

        def count(pred):
            def body(cc, cnt):
                for u in range(unroll):
                    off = pl.multiple_of((cc * unroll + u) * LANES, LANES)
                    cnt = cnt + jnp.where(pred(keys_ref[rows, pl.ds(off, LANES)], off), 1.0, 0.0)
                return cnt
            cnt = lax.fori_loop(0, n_chunks // unroll, body, jnp.zeros((rb, LANES), F32))
            return jnp.sum(cnt, axis=1, keepdims=True)

        def count_ge(t):
            tb = jnp.broadcast_to(t, (rb, LANES))
            return count(lambda key, off: key >= tb)

        thr, c_ge = _kth_largest_key(count_ge, (rb, 1), topk)
        tie = c_ge > topk
        thr_ref[rows, :] = jnp.broadcast_to(jnp.maximum(thr, jnp.int32(INT_MIN + 1)), (rb, LANES))
        cut_ref[rows, :] = jnp.full((rb, LANES), n_keys, I32)

        @pl.when(jnp.max(jnp.where(tie, 1, 0)) > 0)
        def _():
            need = topk - count(lambda key, off: key > thr)
            cut = _tie_cutoff(lambda p: count(lambda key, off: (key == thr) & (off + liota < p)), need, (rb, 1),
                              int(n_keys).bit_length())
            cut_ref[rows, :] = jnp.broadcast_to(jnp.where(tie, cut, n_keys), (rb, LANES))


def _dsa_sample_attend_kernel(pt_ref, q_ref, keys_ref, thr_ref, cut_ref, kn_ref, vn_ref, *rest, pg, pc, t_new,
                              n_heads, kv_heads, page, att_scale, n_keys):
    k_refs = rest[:pg]
    v_refs = rest[pg:2 * pg]
    y_ref, qpad_s, m_s, l_s, acc_s, kn_s, vn_s = rest[2 * pg:]
    i = pl.program_id(1)
    hd = LANES // 2
    per_kv = n_heads // kv_heads
    n_steps = pl.num_programs(1)
    lane = lax.broadcasted_iota(I32, (t_new, LANES), 1)
    low = lane < hd

    @pl.when(i == 0)
    def _():
        qf = q_ref[0].astype(F32) * att_scale
        zeros = jnp.zeros((t_new, LANES), F32)
        for h in range(n_heads):
            g, r = divmod(h, per_kv)
            slot = (g // 2) * per_kv + r
            sl = qf[:, slot * LANES:(slot + 1) * LANES]
            half = jnp.where(low, sl, 0.0) if g % 2 == 0 else jnp.where(low, 0.0, sl)
            row = [zeros] * (kv_heads // 2)
            row[g // 2] = half
            qpad_s[h * t_new:(h + 1) * t_new, :] = jnp.concatenate(row, axis=1)
        m_s[...] = jnp.full(m_s.shape, NEG, F32)
        l_s[...] = jnp.zeros(l_s.shape, F32)
        acc_s[...] = jnp.zeros(acc_s.shape, F32)
        kn_s[...] = jnp.zeros(kn_s.shape, F32)
        vn_s[...] = jnp.zeros(vn_s.shape, F32)
        kn_s[0:t_new, :] = kn_ref[0].astype(F32)
        vn_s[0:t_new, :] = vn_ref[0].astype(F32)

    thr = thr_ref[0]
    cut = cut_ref[0]

    def attend(s, key, pos0, pv, chain):
        n = key.shape[1]
        wide = lambda a: jnp.concatenate([a] * (n // LANES), axis=1)
        pos = pos0 + lax.broadcasted_iota(I32, (t_new, n), 1)
        sel = (key > wide(thr)) | ((key == wide(thr)) & (pos <= wide(cut)))
        bias = jnp.where(sel, 0.0, NEG)
        s = s + jnp.concatenate([bias] * n_heads, axis=0)
        m_old = m_s[chain]
        m_new = jnp.maximum(m_old, jnp.max(s, axis=1, keepdims=True))
        alpha = jnp.exp(m_old - m_new)
        p = jnp.exp(s - m_new)
        l_s[chain] = alpha * l_s[chain] + jnp.sum(p, axis=1, keepdims=True)
        m_s[chain] = m_new
        acc_s[chain] = acc_s[chain] * alpha + pv(p.astype(BF16))

    n_chains = m_s.shape[0]
    qb = qpad_s[...].astype(BF16)
    kvw = qb.shape[1]
    for c in range(pg // pc):
        off = pl.multiple_of((i * pg + c * pc) * page, page)
        kt = jnp.concatenate([k_refs[c * pc + r][0, 0].reshape(kvw, page).astype(BF16) for r in range(pc)], axis=1)
        vt = jnp.concatenate([v_refs[c * pc + r][0, 0].reshape(kvw, page).astype(BF16) for r in range(pc)], axis=1)
        attend(_dot(qb, kt), keys_ref[0, :, pl.ds(off, pc * page)], off, lambda p: _dot_nt(p, vt), c % n_chains)

    @pl.when(i == n_steps - 1)
    def _():
        attend(_dot_nt(qb, kn_s[...].astype(BF16)), keys_ref[0, :, n_keys - page:n_keys], n_keys - page,
               lambda p: _dot(p, vn_s[...].astype(BF16)), 0)
        m = m_s[0]
        for c in range(1, n_chains):
            m = jnp.maximum(m, m_s[c])
        l = sum(l_s[c] * jnp.exp(m_s[c] - m) for c in range(n_chains))
        acc = sum(acc_s[c] * jnp.exp(m_s[c] - m) for c in range(n_chains))
        o = acc / l
        for s in range(n_heads // 2):
            pb, r = divmod(s, per_kv)
            h_lo = (2 * pb) * per_kv + r
            h_hi = (2 * pb + 1) * per_kv + r
            o_lo = o[h_lo * t_new:(h_lo + 1) * t_new, pb * LANES:(pb + 1) * LANES]
            o_hi = o[h_hi * t_new:(h_hi + 1) * t_new, pb * LANES:(pb + 1) * LANES]
            y_ref[0, :, s * LANES:(s + 1) * LANES] = jnp.where(low, o_lo, o_hi).astype(y_ref.dtype)


def _pages_per_step(n_pages):
    return math.gcd(n_pages, PAGES_PER_STEP_MAX)


def _dsa_sample(q, qiw, sa, kk_new, k_new, v_new, ck, cv, ci, page_table, layer, *, n_heads, kv_heads, idx_heads,
                wi_lane, att_scale, idx_w_scale):
    db, t_new, dq = q.shape
    _, n_pool, kv_heads_, hd, page = ck.shape
    kvw = kv_heads_ * hd
    n_pages = page_table.shape[1]
    assert page == LANES and t_new % SUBLANES == 0 and t_new <= page and kv_heads_ == kv_heads and hd == LANES // 2
    pg = _pages_per_step(n_pages)
    n_chains = 2 if pg % 2 == 0 else 1
    pc = pg // n_chains
    n_steps = n_pages // pg
    n_keys = (n_pages + 1) * page
    topk = min(TOPK_MAX, (n_pages * page + t_new) // 4)
    idx_spec = lambda r: pl.BlockSpec((1, 1, hd, page), lambda b, i, pt: (layer, pt[b, i * pg + r], 0, 0))
    kv_spec = lambda r: pl.BlockSpec((1, 1, kv_heads, hd, page), lambda b, i, pt: (layer, pt[b, i * pg + r], 0, 0, 0))
    row_spec = lambda w: pl.BlockSpec((1, t_new, w), lambda b, i, pt: (b, 0, 0))

    keys = pl.pallas_call(
        functools.partial(_dsa_sample_scores_kernel, pg=pg, t_new=t_new, idx_heads=idx_heads, page=page,
                          idx_w_scale=idx_w_scale, wi_lane=wi_lane, n_keys=n_keys),
        grid_spec=pltpu.PrefetchScalarGridSpec(
            num_scalar_prefetch=1,
            grid=(db, n_steps),
            in_specs=[row_spec(qiw.shape[-1]), row_spec(LANES), row_spec(LANES)] + [idx_spec(r) for r in range(pg)],
            out_specs=row_spec(n_keys),
            scratch_shapes=[pltpu.VMEM((idx_heads * t_new, hd), F32), pltpu.VMEM((idx_heads * t_new, LANES), F32),
                            pltpu.VMEM((page, hd), F32)],
        ),
        out_shape=jax.ShapeDtypeStruct((db, t_new, n_keys), I32),
        compiler_params=_params("parallel", "arbitrary"),
        name="dsa_sample_scores",
    )(page_table, qiw, sa, kk_new, *([ci] * pg))

    n_rows = db * t_new
    rb = math.gcd(n_rows, LANES)
    thr, cut = pl.pallas_call(
        functools.partial(_dsa_sample_threshold_kernel, rb=rb, topk=topk, n_keys=n_keys),
        out_shape=[jax.ShapeDtypeStruct((n_rows, LANES), I32)] * 2,
        compiler_params=pltpu.CompilerParams(vmem_limit_bytes=VMEM_LIMIT_BYTES),
        name="dsa_sample_threshold",
    )(keys.reshape(n_rows, n_keys))
    thr = thr.reshape(db, t_new, LANES)
    cut = cut.reshape(db, t_new, LANES)

    return pl.pallas_call(
        functools.partial(_dsa_sample_attend_kernel, pg=pg, pc=pc, t_new=t_new, n_heads=n_heads, kv_heads=kv_heads,
                          page=page, att_scale=att_scale, n_keys=n_keys),
        grid_spec=pltpu.PrefetchScalarGridSpec(
            num_scalar_prefetch=1,
            grid=(db, n_steps),
            in_specs=[row_spec(dq), row_spec(n_keys), row_spec(LANES), row_spec(LANES), row_spec(kvw), row_spec(kvw)]
                     + [kv_spec(r) for r in range(pg)] * 2,
            out_specs=row_spec(dq),
            scratch_shapes=[pltpu.VMEM((n_heads * t_new, kvw), F32), pltpu.VMEM((n_chains, n_heads * t_new, 1), F32),
                            pltpu.VMEM((n_chains, n_heads * t_new, 1), F32),
                            pltpu.VMEM((n_chains, n_heads * t_new, kvw), F32),
                            pltpu.VMEM((page, kvw), F32), pltpu.VMEM((page, kvw), F32)],
        ),
        out_shape=jax.ShapeDtypeStruct((db, t_new, dq), BF16),
        compiler_params=_params("parallel", "arbitrary"),
        name="dsa_sample_attend",
    )(page_table, q, keys, thr, cut, k_new, v_new, *([ck] * pg), *([cv] * pg))


def _merge_kernel(x_ref, ys_ref, ya_ref, gs_ref, ga_ref, ps_ref, pa_ref, wo_ref, o_ref):
    merged = (_sigmoid(gs_ref[...]) * _dot(ys_ref[...], ps_ref[...])
              + _sigmoid(ga_ref[...]) * _dot(ya_ref[...], pa_ref[...]))
    o_ref[...] = x_ref[...] + _dot(merged.astype(BF16), wo_ref[...])


def _merge(x, y_ssd, y_attn, g_s, g_a, p_ssd, p_attn, w_out):
    t, d = x.shape
    tm = min(ROW_TILE, t)
    assert t % tm == 0
    rows = lambda w: pl.BlockSpec((tm, w), lambda i: (i, 0))
    full = lambda a: pl.BlockSpec(a.shape, lambda i: (0, 0))
    return pl.pallas_call(
        _merge_kernel,
        grid=(t // tm,),
        in_specs=[rows(d), rows(y_ssd.shape[1]), rows(y_attn.shape[1]), rows(d), rows(d),
                  full(p_ssd), full(p_attn), full(w_out)],
        out_specs=rows(d),
        out_shape=jax.ShapeDtypeStruct((t, d), F32),
        compiler_params=_params("parallel"),
        name="merge",
    )(x, y_ssd, y_attn, g_s, g_a, p_ssd, p_attn, w_out)


def _pair_slot_perm(n_heads, kv_heads, hd):
    per_kv = n_heads // kv_heads
    cols = []
    for s in range(n_heads // 2):
        pb, r = divmod(s, per_kv)
        for h in ((2 * pb) * per_kv + r, (2 * pb + 1) * per_kv + r):
            cols.extend(range(h * hd, (h + 1) * hd))
    return np.asarray(cols, np.int32)


def kernel(x_prompt, x_sample, cache_k, cache_v, cache_idx_k, state_ssm, state_conv, page_table, ffn1_norm, ffn1_w1, ffn1_w2, mix_norm, w_in, conv_w, conv_b, dt_bias, a_log, d_skip, ssd_norm, w_branch_ssd, w_branch_attn, w_out, ffn2_norm, ffn2_w1, ffn2_w2, final_norm):
    bp, seq, d_model = x_prompt.shape
    db, dseq, _ = x_sample.shape
    depth, n_pool, page, kv_heads, head_dim = cache_k.shape
    idx_dim = cache_idx_k.shape[-1]
    ssd_heads, ssd_hd, d_state = state_ssm.shape[2:]
    conv_dim = state_conv.shape[-1]
    d_inner = ssd_norm.shape[-1]
    n_groups = (conv_dim - d_inner) // (2 * d_state)
    attn_dim = w_branch_attn.shape[1]
    n_heads = attn_dim // head_dim
    kvw = kv_heads * head_dim
    d_proj = w_in.shape[-1]
    idx_heads = (d_proj - (d_inner + conv_dim + ssd_heads + attn_dim + 2 * kvw + idx_dim + 2 * d_model)) // (idx_dim + 1)
    assert head_dim == LANES // 2 and idx_dim == LANES // 2 and ssd_heads + idx_heads <= LANES
    att_scale = head_dim ** -0.5
    idx_w_scale = (idx_heads ** -0.5) * (idx_dim ** -0.5)
    sizes = (d_inner, conv_dim, ssd_heads, attn_dim, kvw, kvw, idx_heads * idx_dim, idx_dim, idx_heads, d_model, d_model)
    assert sum(sizes) == d_proj
    offs = np.concatenate([[0], np.cumsum(sizes)])
    perm = _pair_slot_perm(n_heads, kv_heads, head_dim)
    wi_lane = ssd_heads

    tp, ts = bp * seq, db * dseq
    yp = x_prompt.reshape(tp, d_model)
    ys = x_sample.reshape(ts, d_model)
    ck = jnp.transpose(cache_k, (0, 1, 3, 4, 2))
    cv = jnp.transpose(cache_v, (0, 1, 3, 4, 2))
    ci = jnp.transpose(cache_idx_k, (0, 1, 3, 2))
    dsa_kw = dict(n_heads=n_heads, kv_heads=kv_heads, idx_heads=idx_heads, wi_lane=wi_lane, att_scale=att_scale,
                  idx_w_scale=idx_w_scale)
    outs = {n: [] for n in ("kp", "vp", "ip", "sp", "cp", "ks", "vs", "is", "ss", "cs")}
    hist = conv_w.shape[1] - 1

    for l in range(depth):
        wl = w_in[l]
        col = lambda i: wl[:, offs[i]:offs[i + 1]]
        w_z, w_xbc, w_dt, w_q, w_k, w_v, w_qi, w_ki, w_wi, w_gs, w_ga = [col(i) for i in range(11)]
        w_sa = jnp.concatenate([w_dt, w_wi, jnp.zeros((d_model, LANES - ssd_heads - idx_heads), F32)], axis=1)
        w_qi_wide = jnp.pad(w_qi.reshape(d_model, idx_heads, idx_dim), ((0, 0), (0, 0), (0, LANES - idx_dim)))
        bf = lambda w: w.astype(BF16)
        plain = lambda w, *dts: (bf(w), dts, False, None)
        cols = lambda w, *dts: (bf(w.T), dts, True, None)
        w_kk = jnp.concatenate([w_ki, w_ki], axis=1)
        zcol = jnp.zeros((d_model, head_dim), F32)
        w_vx = jnp.concatenate([a for g in range(kv_heads) for a in
                                ((w_v[:, g * head_dim:(g + 1) * head_dim], zcol) if g % 2 == 0 else
                                 (zcol, w_v[:, g * head_dim:(g + 1) * head_dim]))], axis=1)
        half = np.arange(kv_heads * LANES) // head_dim
        b_vx = jnp.asarray(((half % 2) != ((half // 2) % 2)).astype(np.float32)).reshape(1, -1)
        f1w1, f1w2, f2w1, f2w2 = bf(ffn1_w1[l]), bf(ffn1_w2[l]), bf(ffn2_w1[l]), bf(ffn2_w2[l])
        p_ssd, p_attn, wo = bf(w_branch_ssd[l]), bf(w_branch_attn[l][perm, :]), bf(w_out[l])
        last = l == depth - 1
        ssd_kw = dict(n_groups=n_groups, d_state=d_state)
        ssd_w = (conv_w[l], conv_b[l], dt_bias[l], a_log[l], d_skip[l], ssd_norm[l])

        yp = _ffn(yp, ffn1_norm[l], f1w1, f1w2)
        ys = _ffn(ys, ffn1_norm[l], f1w1, f1w2)

        z, xbc = _norm_linear(yp, mix_norm[l], [plain(w_z, F32), plain(w_xbc, F32)], PROJ_ROW_TILE)
        q, kt, ktb, vx, vt, kkb, kit, sa, g_s, g_a, qi = _norm_linear(
            yp, mix_norm[l],
            [plain(w_q[:, perm], BF16), cols(w_k, F32, BF16), (bf(w_vx), (BF16,), False, b_vx), cols(w_v, F32),
             plain(w_kk, BF16), cols(w_ki, F32), plain(w_sa, F32), plain(w_gs, F32), plain(w_ga, F32),
             plain(w_qi, BF16)], PROJ_ROW_TILE, batch=bp)
        r3 = lambda a: a.reshape(bp, seq, a.shape[-1])
        y_ssd, hfp = _ssd(r3(z), r3(xbc), r3(sa), jnp.zeros((1, bp, hist, conv_dim), F32),
                          jnp.zeros((1, bp, ssd_heads, ssd_hd, d_state), F32), 0, *ssd_w, **ssd_kw)
        y_attn = _dsa_prompt(r3(q), r3(qi), r3(sa), r3(kkb), ktb, r3(vx), **dsa_kw)
        yp = _merge(yp, y_ssd.reshape(tp, d_inner), y_attn.reshape(tp, attn_dim), g_s, g_a, p_ssd, p_attn, wo)
        cbp = r3(xbc)[:, seq - hist:, :]
        heads_last = lambda a: jnp.transpose(a.reshape(bp, kv_heads, head_dim, seq), (0, 3, 1, 2))
        kp, vp, kip = heads_last(kt), heads_last(vt), jnp.transpose(kit, (0, 2, 1))

        z, xbc = _norm_linear(ys, mix_norm[l], [plain(w_z, F32), plain(w_xbc, F32)], PROJ_ROW_TILE)
        q, k, kb, v, vb, kk, kkb, sa, g_s, g_a, qi = _norm_linear(
            ys, mix_norm[l],
            [plain(w_q[:, perm], BF16), plain(w_k, F32, BF16), plain(w_v, F32, BF16), plain(w_kk, F32, BF16),
             plain(w_sa, F32), plain(w_gs, F32), plain(w_ga, F32),
             plain(w_qi_wide.reshape(d_model, idx_heads * LANES), BF16)], PROJ_ROW_TILE)
        r3 = lambda a: a.reshape(db, dseq, a.shape[-1])
        y_ssd, hfs = _ssd(r3(z), r3(xbc), r3(sa), state_conv, state_ssm, l, *ssd_w, **ssd_kw)
        y_attn = _dsa_sample(r3(q), r3(qi), r3(sa), r3(kkb), r3(kb), r3(vb), ck, cv, ci, page_table, l, **dsa_kw)
        ys = _merge(ys, y_ssd.reshape(ts, d_inner), y_attn.reshape(ts, attn_dim), g_s, g_a, p_ssd, p_attn, wo)
        cbs = r3(xbc)[:, dseq - hist:, :]
        kss, vss = k.reshape(db, dseq, kv_heads, head_dim), v.reshape(db, dseq, kv_heads, head_dim)
        kis = r3(kk)[:, :, :idx_dim]

        pg_ = final_norm if last else None
        yp = _ffn(yp, ffn2_norm[l], f2w1, f2w2, pg_)
        ys = _ffn(ys, ffn2_norm[l], f2w1, f2w2, pg_)
        for n, a in zip(("kp", "vp", "ip", "sp", "cp", "ks", "vs", "is", "ss", "cs"),
                        (kp, vp, kip, hfp, cbp, kss, vss, kis, hfs, cbs)):
            outs[n].append(a)

    st = lambda n: jnp.stack(outs[n])
    return (yp.reshape(bp, seq, d_model), ys.reshape(db, dseq, d_model),
            st("kp"), st("vp"), st("ip"), st("sp"), st("cp"),
            st("ks"), st("vs"), st("is"), st("ss"), st("cs"))
```

```python
import functools
import math

import jax
import jax.numpy as jnp
import numpy as np
from jax import lax
from jax.experimental import pallas as pl
from jax.experimental.pallas import tpu as pltpu

F32 = jnp.float32
BF16 = jnp.bfloat16
I32 = jnp.int32

EPS = 1e-6
SSD_CHUNK = 128
TOPK_MAX = 256
LANES = 128
SUBLANES = 8
VMEM_LIMIT_BYTES = 56 * 1024 * 1024
NEG = -1e30
INT_MIN = -(2 ** 31)

ROW_TILE = 512
PROJ_ROW_TILE = 256
FF_TILE_MAX = 1536
Q_TILE = 128
KEY_TILE = 512
PAGES_PER_STEP_MAX = 32


def _params(*sem):
    return pltpu.CompilerParams(dimension_semantics=sem, vmem_limit_bytes=VMEM_LIMIT_BYTES)


def _sigmoid(x):
    return 1.0 / (1.0 + jnp.exp(-x))


def _rms(x, g):
    return x * lax.rsqrt(jnp.mean(x * x, axis=-1, keepdims=True) + EPS) * g


def _dot(a, b):
    return jnp.dot(a, b, preferred_element_type=F32)


def _dot_nt(a, b):
    return lax.dot_general(a, b, (((1,), (1,)), ((), ())), preferred_element_type=F32)


def _split2(x):
    hi = x.astype(BF16)
    lo = (x - hi.astype(F32)).astype(BF16)
    return hi, lo


def _split3(x):
    hi = x.astype(BF16)
    r = x - hi.astype(F32)
    mid = r.astype(BF16)
    lo = (r - mid.astype(F32)).astype(BF16)
    return hi, mid, lo


def _ffn_kernel(*refs, post_norm):
    if post_norm:
        x_ref, g_ref, wa_ref, wb_ref, w2_ref, pg_ref, o_ref, h_s, acc_s = refs
    else:
        x_ref, g_ref, wa_ref, wb_ref, w2_ref, o_ref, h_s, acc_s = refs
    f = pl.program_id(1)

    @pl.when(f == 0)
    def _():
        h_s[...] = _rms(x_ref[...], g_ref[...]).astype(BF16)
        acc_s[...] = jnp.zeros_like(acc_s)

    h = h_s[...]
    a = _dot(h, wa_ref[...])
    b = _dot(h, wb_ref[...])
    u = (a * _sigmoid(a) * b).astype(BF16)
    acc_s[...] += _dot(u, w2_ref[...])

    @pl.when(f == pl.num_programs(1) - 1)
    def _():
        y = x_ref[...] + 0.5 * acc_s[...]
        if post_norm:
            y = _rms(y, pg_ref[...])
        o_ref[...] = y


def _ff_tile(d_ff):
    best = None
    for t in range(LANES, d_ff + 1, LANES):
        if d_ff % t == 0 and t <= FF_TILE_MAX:
            best = t
    assert best is not None, d_ff
    return best


def _ffn(x, g, w1, w2, post_gain=None):
    t, d = x.shape
    d_ff = w2.shape[0]
    tm = min(ROW_TILE, t)
    tf = _ff_tile(d_ff)
    nf = d_ff // tf
    assert t % tm == 0
    post_norm = post_gain is not None
    in_specs = [
        pl.BlockSpec((tm, d), lambda i, f: (i, 0)),
        pl.BlockSpec((1, d), lambda i, f: (0, 0)),
        pl.BlockSpec((d, tf), lambda i, f: (0, f)),
        pl.BlockSpec((d, tf), lambda i, f: (0, f + nf)),
        pl.BlockSpec((tf, d), lambda i, f: (f, 0)),
    ]
    args = [x, g.reshape(1, d), w1, w1, w2]
    if post_norm:
        in_specs.append(pl.BlockSpec((1, d), lambda i, f: (0, 0)))
        args.append(post_gain.reshape(1, d))
    return pl.pallas_call(
        functools.partial(_ffn_kernel, post_norm=post_norm),
        grid=(t // tm, nf),
        in_specs=in_specs,
        out_specs=pl.BlockSpec((tm, d), lambda i, f: (i, 0)),
        out_shape=jax.ShapeDtypeStruct((t, d), F32),
        scratch_shapes=[pltpu.VMEM((tm, d), BF16), pltpu.VMEM((tm, d), F32)],
        compiler_params=_params("parallel", "arbitrary"),
        name="ffn",
    )(*args)


def _norm_linear_kernel(*refs, plan):
    n_w = len(plan)
    n_b = sum(has_bias for _, _, has_bias in plan)
    x_ref, g_ref = refs[:2]
    w_refs = refs[2:2 + n_w]
    b_refs = list(refs[2 + n_w:2 + n_w + n_b])
    o_refs = list(refs[2 + n_w + n_b:])
    h = _rms(x_ref[...], g_ref[...]).astype(BF16)
    for w_ref, (dts, transposed, has_bias) in zip(w_refs, plan):
        r = _dot_nt(w_ref[...], h) if transposed else _dot(h, w_ref[...])
        if has_bias:
            r = r + b_refs.pop(0)[...]
        for dt in dts:
            o_ref = o_refs.pop(0)
            if transposed:
                o_ref[0] = r.astype(dt)
            else:
                o_ref[...] = r.astype(dt)


def _norm_linear(x, g, maps, tm, batch=1):
    t, d = x.shape
    tm = min(tm, t)
    s = t // batch
    assert t % tm == 0 and s % tm == 0
    steps = s // tm
    in_specs = [pl.BlockSpec((tm, d), lambda i: (i, 0)), pl.BlockSpec((1, d), lambda i: (0, 0))]
    biases, out_specs, out_shape = [], [], []
    for w, dts, transposed, bias in maps:
        in_specs.append(pl.BlockSpec(w.shape, lambda i: (0, 0)))
        n = w.shape[0] if transposed else w.shape[1]
        for dt in dts:
            if transposed:
                out_specs.append(pl.BlockSpec((1, n, tm), lambda i: (i // steps, 0, i % steps)))
                out_shape.append(jax.ShapeDtypeStruct((batch, n, s), dt))
            else:
                out_specs.append(pl.BlockSpec((tm, n), lambda i: (i, 0)))
                out_shape.append(jax.ShapeDtypeStruct((t, n), dt))
        if bias is not None:
            assert not transposed
            biases.append(bias)
    in_specs += [pl.BlockSpec(b.shape, lambda i: (0, 0)) for b in biases]
    plan = tuple((tuple(dts), transposed, bias is not None) for _, dts, transposed, bias in maps)
    return pl.pallas_call(
        functools.partial(_norm_linear_kernel, plan=plan),
        grid=(t // tm,),
        in_specs=in_specs,
        out_specs=out_specs,
        out_shape=out_shape,
        compiler_params=_params("parallel"),
        name="norm_linear",
    )(x, g.reshape(1, d), *[m[0] for m in maps], *biases)


def _ssd_kernel(z_ref, xbc_ref, dt_ref, buf_ref, h0_ref, cw_ref, cb_ref, dtb_ref, alog_ref, dsk_ref, ng_ref, e_ref,
                y_ref, hfin_ref, xp_s, ht_s, *, qin, d_inner, n_groups, d_state, conv_w):
    q = SSD_CHUNK
    c = pl.program_id(1)
    hp_blocks = d_inner // LANES
    gw = d_inner // n_groups
    assert d_state == LANES and gw % LANES == 0
    pad = SUBLANES
    hist = conv_w - 1

    @pl.when(c == 0)
    def _():
        xp_s[0:pad, :] = buf_ref[0, 0]
        if qin < q:
            xp_s[pad + qin:pad + q, :] = jnp.zeros((q - qin, xp_s.shape[1]), F32)
        for i in range(hp_blocks):
            ht_s[:, i * LANES:(i + 1) * LANES] = h0_ref[0, 0, i * LANES:(i + 1) * LANES, :].T

    xp_s[pad:pad + qin, :] = xbc_ref[0]
    x_cur = xp_s[pad:pad + q, :]
    x_prev = xp_s[0:pad, :]
    row8 = lax.broadcasted_iota(I32, (pad, 1), 0)
    acc = cb_ref[...] + x_cur * cw_ref[hist:hist + 1, :]
    for i in range(hist):
        s = hist - i
        rolled = pltpu.roll(x_cur, s, 0)
        head = jnp.where(row8 < s, pltpu.roll(x_prev, s, 0), rolled[0:pad])
        acc = acc + jnp.concatenate([head, rolled[pad:]], axis=0) * cw_ref[i:i + 1, :]
    xc = acc * _sigmoid(acc)
    tail = xp_s[pad + qin - hist:pad + qin, :]
    xp_s[pad - hist:pad, :] = tail

    xs = xc[:, :d_inner]
    bm = xc[:, d_inner:d_inner + n_groups * d_state]
    cm = xc[:, d_inner + n_groups * d_state:]

    dt_raw = dt_ref[0] + dtb_ref[...]
    dt = jnp.maximum(dt_raw, 0.0) + jnp.log1p(jnp.exp(-jnp.abs(dt_raw)))
    if qin < q:
        dt = jnp.concatenate([dt, jnp.zeros((q - qin, LANES), F32)], axis=0)
    la = dt * (-jnp.exp(alog_ref[...]))

    ri = lax.broadcasted_iota(I32, (q, q), 0)
    ci = lax.broadcasted_iota(I32, (q, q), 1)
    causal = ri >= ci
    tril = jnp.where(causal, 1.0, 0.0).astype(BF16)
    eye = jnp.where(ri == ci, 1.0, 0.0).astype(BF16)
    a_cs = sum(_dot(tril, p) for p in _split3(la))
    a_cs_t = sum(_dot_nt(eye, p) for p in _split3(a_cs))
    dec = jnp.exp(a_cs[q - 1:q, :] - a_cs)
    eac = jnp.exp(a_cs)
    stacked = jnp.concatenate([dt, dec, eac], axis=0)
    expd = sum(_dot(p, e_ref[...]) for p in _split2(stacked))
    dt_e, dec_e, eac_e = expd[0:q], expd[q:2 * q], expd[2 * q:3 * q]

    x = xs * dt_e
    xb = x.astype(BF16)
    xd = (x * dec_e).astype(BF16)
    lane = lax.broadcasted_iota(I32, (q, LANES), 1)
    hd = LANES // 2
    heads_per_group = gw // hd

    y_parts = []
    for g in range(n_groups):
        gs = slice(g * gw, (g + 1) * gw)
        cg = cm[:, g * d_state:(g + 1) * d_state].astype(BF16)
        bg = bm[:, g * d_state:(g + 1) * d_state]
        cb = _dot_nt(cg, bg.astype(BF16))
        y_off = _dot(cg, ht_s[:, gs].astype(BF16))
        pair_parts = []
        for p in range(heads_per_group // 2):
            xp = xb[:, g * gw + p * LANES:g * gw + (p + 1) * LANES]
            res = []
            for k in range(2):
                j = g * heads_per_group + 2 * p + k
                diff = a_cs[:, j:j + 1] - a_cs_t[j:j + 1, :]
                lm = jnp.exp(jnp.where(causal, diff, NEG))
                res.append(_dot((cb * lm).astype(BF16), xp))
            pair_parts.append(jnp.where(lane < hd, res[0], res[1]))
        y_diag = jnp.concatenate(pair_parts, axis=1)
        y_parts.append(y_diag + y_off * eac_e[:, gs])
        st = _dot(bg.T.astype(BF16), xd[:, gs])
        ht_s[:, gs] = ht_s[:, gs] * eac_e[q - 1:q, gs] + st

    zz = z_ref[0]
    outs = []
    for g in range(n_groups):
        gs = slice(g * gw, (g + 1) * gw)
        yv = (y_parts[g][:qin] + dsk_ref[:, gs] * xs[:qin, gs]) * (zz[:, gs] * _sigmoid(zz[:, gs]))
        ms = jnp.mean(yv * yv, axis=-1, keepdims=True)
        outs.append(yv * lax.rsqrt(ms + EPS) * ng_ref[:, gs])
    y_ref[0] = jnp.concatenate(outs, axis=1).astype(y_ref.dtype)

    @pl.when(c == pl.num_programs(1) - 1)
    def _():
        for i in range(hp_blocks):
            hfin_ref[0, i * LANES:(i + 1) * LANES, :] = ht_s[:, i * LANES:(i + 1) * LANES].T


def _ssd(z, xbc, dtp, buf, h0, layer, conv_w, conv_b, dt_bias, a_log, d_skip, norm_g, *, n_groups, d_state):
    b, l, d_inner = z.shape
    conv_dim = xbc.shape[-1]
    n_heads, p_dim, n_state = h0.shape[2:]
    width = conv_w.shape[0]
    hist = width - 1
    assert n_heads <= LANES and p_dim == LANES // 2 and n_state == d_state and hist <= SUBLANES
    qin = math.gcd(l, SSD_CHUNK)
    assert qin % SUBLANES == 0 and qin >= hist
    nc = l // qin
    assert nc == 1 or qin == SSD_CHUNK
    hp = n_heads * p_dim
    buf8 = jnp.pad(buf, ((0, 0), (0, 0), (SUBLANES - hist, 0), (0, 0)))
    cw8 = jnp.pad(conv_w, ((0, SUBLANES - width), (0, 0)))
    pad1 = lambda v: jnp.pad(v.reshape(1, -1), ((0, 0), (0, LANES - n_heads)))
    expand = (np.arange(LANES)[:, None] == (np.arange(d_inner)[None, :] // p_dim)).astype(np.float32)
    full = lambda shape: pl.BlockSpec(shape, lambda i, c: (0,) * len(shape))
    y, hfin = pl.pallas_call(
        functools.partial(_ssd_kernel, qin=qin, d_inner=d_inner, n_groups=n_groups, d_state=d_state, conv_w=width),
        grid=(b, nc),
        in_specs=[
            pl.BlockSpec((1, qin, d_inner), lambda i, c: (i, c, 0)),
            pl.BlockSpec((1, qin, conv_dim), lambda i, c: (i, c, 0)),
            pl.BlockSpec((1, qin, LANES), lambda i, c: (i, c, 0)),
            pl.BlockSpec((1, 1, SUBLANES, conv_dim), lambda i, c: (layer, i, 0, 0)),
            pl.BlockSpec((1, 1, hp, n_state), lambda i, c: (layer, i, 0, 0)),
            full((SUBLANES, conv_dim)), full((1, conv_dim)), full((1, LANES)), full((1, LANES)),
            full((1, d_inner)), full((1, d_inner)), full((LANES, d_inner)),
        ],
        out_specs=[
            pl.BlockSpec((1, qin, d_inner), lambda i, c: (i, c, 0)),
            pl.BlockSpec((1, hp, n_state), lambda i, c: (i, 0, 0)),
        ],
        out_shape=[jax.ShapeDtypeStruct((b, l, d_inner), BF16), jax.ShapeDtypeStruct((b, hp, n_state), F32)],
        scratch_shapes=[pltpu.VMEM((SUBLANES + SSD_CHUNK, conv_dim), F32), pltpu.VMEM((n_state, hp), F32)],
        compiler_params=_params("parallel", "arbitrary"),
        name="ssd",
    )(z, xbc, dtp, buf8, h0.reshape(-1, b, hp, n_state), cw8, conv_b.reshape(1, -1), pad1(dt_bias), pad1(a_log),
      jnp.repeat(d_skip, p_dim).reshape(1, -1), norm_g.reshape(1, -1), jnp.asarray(expand, BF16))
    return y, hfin.reshape(b, n_heads, p_dim, n_state)


def _score_key(sc, valid):
    bits = lax.bitcast_convert_type(sc + 0.0, I32)
    key = jnp.where(bits < 0, bits ^ jnp.int32(0x7FFFFFFF), bits)
    return jnp.where(valid, key, jnp.int32(INT_MIN))


def _kth_largest_key(count_ge, shape, k):
    def body(b, st):
        t, c_t = st
        cand = t + lax.shift_left(jnp.int32(1), jnp.int32(31) - b)
        c = count_ge(cand)
        return jnp.where(c >= k, cand, t), jnp.where(c >= k, c, c_t)
    return lax.fori_loop(0, 32, body, (jnp.full(shape, INT_MIN, I32), jnp.zeros(shape, F32)))


def _tie_cutoff(count_tie_before, need, shape, n_bits):
    def body(b, p):
        bit = lax.shift_left(jnp.int32(1), jnp.int32(n_bits - 1) - b)
        p_c = p | bit
        return jnp.where(count_tie_before(p_c) <= need - 1, p_c, p)
    return lax.fori_loop(0, n_bits, body, jnp.zeros(shape, I32))


def _dsa_prompt_kernel(q_ref, qi_ref, sa_ref, kk_ref, kt_ref, vx_ref, y_ref,
                       keys_s, qpad_s, qipad_s, w_s, m_s, acc_s,
                       *, tq, tk, n_heads, kv_heads, idx_heads, topk, att_scale, idx_w_scale, wi_lane, seq_len):
    i = pl.program_id(1)
    hd = LANES // 2
    per_kv = n_heads // kv_heads
    n_kt = (i * tq + tq - 1) // tk + 1
    lane = lax.broadcasted_iota(I32, (tq, LANES), 1)
    low = lane < hd

    qf = q_ref[0].astype(F32) * (att_scale * math.log2(math.e))
    zeros = jnp.zeros((tq, LANES), F32)
    for h in range(n_heads):
        g, r = divmod(h, per_kv)
        slot = (g // 2) * per_kv + r
        sl = qf[:, slot * LANES:(slot + 1) * LANES]
        half = jnp.where(low, sl, 0.0) if g % 2 == 0 else jnp.where(low, 0.0, sl)
        row = [zeros] * (kv_heads // 2)
        row[g // 2] = half
        qpad_s[h * tq:(h + 1) * tq, :] = jnp.concatenate(row, axis=1).astype(BF16)
    qif = qi_ref[0].astype(F32)
    for h in range(idx_heads):
        sl = qif[:, (h // 2) * LANES:(h // 2 + 1) * LANES]
        half = jnp.where(low, sl, 0.0) if h % 2 == 0 else jnp.where(low, 0.0, sl)
        qipad_s[h * tq:(h + 1) * tq, :] = half.astype(BF16)
    w_s[...] = sa_ref[0].T[wi_lane:wi_lane + idx_heads, :] * idx_w_scale

    qpos = i * tq + lax.broadcasted_iota(I32, (tk, tq), 1)
    kiota = lax.broadcasted_iota(I32, (tk, tq), 0)
    siota = lax.broadcasted_iota(I32, (SUBLANES, tq), 0)

    tiles_per_step = 2

    def score_tiles(jj, carry):
        for u in range(tiles_per_step):
            j = jj * tiles_per_step + u
            kt = kk_ref[0, pl.ds(pl.multiple_of(j * tk, tk), tk), :]
            r = _dot_nt(kt, qipad_s[...])
            sc = jnp.zeros((tk, tq), F32)
            for h in range(idx_heads):
                sc = sc + jnp.maximum(r[:, h * tq:(h + 1) * tq], 0.0) * w_s[h:h + 1, :]
            keys_s[j] = _score_key(sc, j * tk + kiota <= qpos)
        return carry
    lax.fori_loop(0, (n_kt + tiles_per_step - 1) // tiles_per_step, score_tiles, 0)

    n_acc = 4

    def count(pred):
        def body(j, cnts):
            cnts = list(cnts)
            for g in range(tk // SUBLANES):
                hit = pred(keys_s[j, g * SUBLANES:(g + 1) * SUBLANES, :], j * tk + g * SUBLANES)
                cnts[g % n_acc] = cnts[g % n_acc] + jnp.where(hit, 1.0, 0.0)
            return tuple(cnts)
        cnts = lax.fori_loop(0, n_kt, body, (jnp.zeros((SUBLANES, tq), F32),) * n_acc)
        return jnp.sum(sum(cnts), axis=0, keepdims=True)

    def count_ge(t):
        tb = jnp.broadcast_to(t, (SUBLANES, tq))
        return count(lambda key, base: key >= tb)

    thr, c_ge = _kth_largest_key(count_ge, (1, tq), topk)
    tie = c_ge > topk

    @pl.when(jnp.max(jnp.where(tie, 1, 0)) > 0)
    def _():
        need = topk - count(lambda key, base: key > thr)
        cut = _tie_cutoff(lambda p: count(lambda key, base: (key == thr) & (base + siota < p)),
                          need, (1, tq), int(seq_len).bit_length())
        def drop(j, carry):
            key = keys_s[j]
            keys_s[j] = jnp.where(tie & (key == thr) & (j * tk + kiota > cut), jnp.int32(INT_MIN), key)
            return carry
        lax.fori_loop(0, n_kt, drop, 0)

    thr_sel = jnp.maximum(thr, jnp.int32(INT_MIN + 1))

    n_chunks = tk // LANES
    m_s[...] = jnp.full(m_s.shape, NEG, F32)
    acc_s[...] = jnp.zeros(acc_s.shape, F32)

    def attend_tile(j, carry):
        ks = pl.ds(pl.multiple_of(j * tk, tk), tk)
        kt = kt_ref[0, :, ks]
        half = n_heads * tq // 2
        s_halves = [_dot(qpad_s[0:half], kt), _dot(qpad_s[half:], kt)]
        bias = [jnp.where(keys_s[j, c * LANES:(c + 1) * LANES, :] >= thr_sel, 0.0, NEG).T for c in range(n_chunks)]
        for g in range(kv_heads):
            ps, alphas = [], []
            for r in range(per_kv):
                rows = slice((g * per_kv + r) * tq, (g * per_kv + r + 1) * tq)
                src, off = s_halves[rows.start // half], rows.start % half
                s = [src[off:off + tq, c * LANES:(c + 1) * LANES] + bias[c] for c in range(n_chunks)]
                smax = s[0]
                for c in range(1, n_chunks):
                    smax = jnp.maximum(smax, s[c])
                m_old = m_s[rows]
                m_new = jnp.maximum(m_old, jnp.max(smax, axis=1, keepdims=True))
                m_s[rows] = m_new
                alphas.append(jnp.exp2(m_old - m_new))
                ps.append(jnp.concatenate([jnp.exp2(s[c] - m_new) for c in range(n_chunks)], axis=1).astype(BF16))
            grows = slice(g * per_kv * tq, (g + 1) * per_kv * tq)
            pv = _dot(jnp.concatenate(ps, axis=0), vx_ref[0, ks, g * LANES:(g + 1) * LANES])
            acc_s[grows] = acc_s[grows] * jnp.concatenate(alphas, axis=0) + pv
        return carry
    lax.fori_loop(0, n_kt, attend_tile, 0)

    def normalised(h):
        a = acc_s[h * tq:(h + 1) * tq]
        return a / pltpu.roll(a, hd, 1)
    for s in range(n_heads // 2):
        pb, r = divmod(s, per_kv)
        o_lo = normalised((2 * pb) * per_kv + r)
        o_hi = normalised((2 * pb + 1) * per_kv + r)
        y_ref[0, :, s * LANES:(s + 1) * LANES] = jnp.where(low, o_lo, o_hi).astype(y_ref.dtype)


def _dsa_prompt(q, qi, sa, kk, kt, vx, *, n_heads, kv_heads, idx_heads, wi_lane, att_scale, idx_w_scale):
    b, l, dq = q.shape
    tq, tk = min(Q_TILE, l), min(KEY_TILE, l // 2)
    assert l % tq == 0 and l % (2 * tk) == 0 and kv_heads % 2 == 0 and idx_heads % 2 == 0
    assert wi_lane % SUBLANES == 0 and idx_heads == SUBLANES
    topk = min(TOPK_MAX, l // 4)
    kvw = kt.shape[1]
    return pl.pallas_call(
        functools.partial(_dsa_prompt_kernel, tq=tq, tk=tk, n_heads=n_heads, kv_heads=kv_heads, idx_heads=idx_heads,
                          topk=topk, att_scale=att_scale, idx_w_scale=idx_w_scale, wi_lane=wi_lane, seq_len=l),
        grid=(b, l // tq),
        in_specs=[
            pl.BlockSpec((1, tq, dq), lambda bi, i: (bi, i, 0)),
            pl.BlockSpec((1, tq, qi.shape[-1]), lambda bi, i: (bi, i, 0)),
            pl.BlockSpec((1, tq, LANES), lambda bi, i: (bi, i, 0)),
            pl.BlockSpec((1, l, LANES), lambda bi, i: (bi, 0, 0), pipeline_mode=pl.Buffered(1)),
            pl.BlockSpec((1, kvw, l), lambda bi, i: (bi, 0, 0), pipeline_mode=pl.Buffered(1)),
            pl.BlockSpec((1, l, vx.shape[-1]), lambda bi, i: (bi, 0, 0), pipeline_mode=pl.Buffered(1)),
        ],
        out_specs=pl.BlockSpec((1, tq, dq), lambda bi, i: (bi, i, 0)),
        out_shape=jax.ShapeDtypeStruct((b, l, dq), BF16),
        scratch_shapes=[
            pltpu.VMEM((l // tk, tk, tq), I32),
            pltpu.VMEM((n_heads * tq, kvw), BF16),
            pltpu.VMEM((idx_heads * tq, LANES), BF16),
            pltpu.VMEM((idx_heads, tq), F32),
            pltpu.VMEM((n_heads * tq, LANES), F32),
            pltpu.VMEM((n_heads * tq, LANES), F32),
        ],
        compiler_params=_params("parallel", "arbitrary"),
        name="dsa_prompt",
    )(q, qi, sa, kk, kt, vx)


def _dsa_sample_scores_kernel(pt_ref, qi_ref, sa_ref, kn_ref, *rest, pg, t_new, idx_heads, page,
                              idx_w_scale, wi_lane, n_keys):
    page_refs = rest[:pg]
    keys_ref, qi_s, w_s, kn_s = rest[pg:]
    i = pl.program_id(1)
    hd = LANES // 2
    n_steps = pl.num_programs(1)

    @pl.when(i == 0)
    def _():
        qif = qi_ref[0]
        sa = sa_ref[0]
        for h in range(idx_heads):
            qi_s[h * t_new:(h + 1) * t_new, :] = qif[:, h * LANES:h * LANES + hd].astype(F32)
            w_s[h * t_new:(h + 1) * t_new, :] = jnp.broadcast_to(
                sa[:, wi_lane + h:wi_lane + h + 1] * idx_w_scale, (t_new, LANES))
        kn_s[...] = jnp.zeros(kn_s.shape, F32)
        kn_s[0:t_new, :] = kn_ref[0][:, 0:hd].astype(F32)

    def scores(dots):
        ww = jnp.maximum(dots, 0.0) * w_s[...]
        sc = ww[0:t_new]
        for h in range(1, idx_heads):
            sc = sc + ww[h * t_new:(h + 1) * t_new]
        return sc

    always = jnp.full((t_new, page), True)
    kt = jnp.concatenate([page_refs[r][0, 0].astype(BF16) for r in range(pg)], axis=1)
    dots = _dot(qi_s[...].astype(BF16), kt)
    for r in range(pg):
        off = pl.multiple_of((i * pg + r) * page, page)
        keys_ref[0, :, pl.ds(off, page)] = _score_key(scores(dots[:, r * page:(r + 1) * page]), always)

    @pl.when(i == n_steps - 1)
    def _():
        ti = lax.broadcasted_iota(I32, (t_new, page), 0)
        ki = lax.broadcasted_iota(I32, (t_new, page), 1)
        past = n_keys - page
        dots_new = _dot_nt(qi_s[...].astype(BF16), kn_s[...].astype(BF16))
        keys_ref[0, :, past:n_keys] = _score_key(scores(dots_new), ki <= ti)


def _dsa_sample_threshold_kernel(keys_ref, thr_ref, cut_ref, *, rb, topk, n_keys):
    n_chunks = n_keys // LANES
    unroll = next(u for u in (4, 3, 2, 1) if n_chunks % u == 0)
    liota = lax.broadcasted_iota(I32, (rb, LANES), 1)
    for b in range(keys_ref.shape[0] // rb):
        rows = slice(b * rb, (b + 1) * rb)

        def count(pred):
            def body(cc, cnt):
                for u in range(unroll):
                    off = pl.multiple_of((cc * unroll + u) * LANES, LANES)
                    cnt = cnt + jnp.where(pred(keys_ref[rows, pl.ds(off, LANES)], off), 1.0, 0.0)
                return cnt
            cnt = lax.fori_loop(0, n_chunks // unroll, body, jnp.zeros((rb, LANES), F32))
            return jnp.sum(cnt, axis=1, keepdims=True)

        def count_ge(t):
            tb = jnp.broadcast_to(t, (rb, LANES))
            return count(lambda key, off: key >= tb)

        thr, c_ge = _kth_largest_key(count_ge, (rb, 1), topk)
        tie = c_ge > topk
        thr_ref[rows, :] = jnp.broadcast_to(jnp.maximum(thr, jnp.int32(INT_MIN + 1)), (rb, LANES))
        cut_ref[rows, :] = jnp.full((rb, LANES), n_keys, I32)

        @pl.when(jnp.max(jnp.where(tie, 1, 0)) > 0)
        def _():
            need = topk - count(lambda key, off: key > thr)
            cut = _tie_cutoff(lambda p: count(lambda key, off: (key == thr) & (off + liota < p)), need, (rb, 1),
                              int(n_keys).bit_length())
            cut_ref[rows, :] = jnp.broadcast_to(jnp.where(tie, cut, n_keys), (rb, LANES))


def _dsa_sample_attend_kernel(pt_ref, q_ref, keys_ref, thr_ref, cut_ref, kn_ref, vn_ref, *rest, pg, pc, t_new,
                              n_heads, kv_heads, page, att_scale, n_keys):
    k_refs = rest[:pg]
    v_refs = rest[pg:2 * pg]
    y_ref, qpad_s, m_s, l_s, acc_s, kn_s, vn_s = rest[2 * pg:]
    i = pl.program_id(1)
    hd = LANES // 2
    per_kv = n_heads // kv_heads
    n_steps = pl.num_programs(1)
    lane = lax.broadcasted_iota(I32, (t_new, LANES), 1)
    low = lane < hd

    @pl.when(i == 0)
    def _():
        qf = q_ref[0].astype(F32) * att_scale
        zeros = jnp.zeros((t_new, LANES), F32)
        for h in range(n_heads):
            g, r = divmod(h, per_kv)
            slot = (g // 2) * per_kv + r
            sl = qf[:, slot * LANES:(slot + 1) * LANES]
            half = jnp.where(low, sl, 0.0) if g % 2 == 0 else jnp.where(low, 0.0, sl)
            row = [zeros] * (kv_heads // 2)
            row[g // 2] = half
            qpad_s[h * t_new:(h + 1) * t_new, :] = jnp.concatenate(row, axis=1)
        m_s[...] = jnp.full(m_s.shape, NEG, F32)
        l_s[...] = jnp.zeros(l_s.shape, F32)
        acc_s[...] = jnp.zeros(acc_s.shape, F32)
        kn_s[...] = jnp.zeros(kn_s.shape, F32)
        vn_s[...] = jnp.zeros(vn_s.shape, F32)
        kn_s[0:t_new, :] = kn_ref[0].astype(F32)
        vn_s[0:t_new, :] = vn_ref[0].astype(F32)

    thr = thr_ref[0]
    cut = cut_ref[0]

    def attend(s, key, pos0, pv, chain):
        n = key.shape[1]
        wide = lambda a: jnp.concatenate([a] * (n // LANES), axis=1)
        pos = pos0 + lax.broadcasted_iota(I32, (t_new, n), 1)
        sel = (key > wide(thr)) | ((key == wide(thr)) & (pos <= wide(cut)))
        bias = jnp.where(sel, 0.0, NEG)
        s = s + jnp.concatenate([bias] * n_heads, axis=0)
        m_old = m_s[chain]
        m_new = jnp.maximum(m_old, jnp.max(s, axis=1, keepdims=True))
        alpha = jnp.exp(m_old - m_new)
        p = jnp.exp(s - m_new)
        l_s[chain] = alpha * l_s[chain] + jnp.sum(p, axis=1, keepdims=True)
        m_s[chain] = m_new
        acc_s[chain] = acc_s[chain] * alpha + pv(p.astype(BF16))

    n_chains = m_s.shape[0]
    qb = qpad_s[...].astype(BF16)
    kvw = qb.shape[1]
    for c in range(pg // pc):
        off = pl.multiple_of((i * pg + c * pc) * page, page)
        kt = jnp.concatenate([k_refs[c * pc + r][0, 0].reshape(kvw, page).astype(BF16) for r in range(pc)], axis=1)
        vt = jnp.concatenate([v_refs[c * pc + r][0, 0].reshape(kvw, page).astype(BF16) for r in range(pc)], axis=1)
        attend(_dot(qb, kt), keys_ref[0, :, pl.ds(off, pc * page)], off, lambda p: _dot_nt(p, vt), c % n_chains)

    @pl.when(i == n_steps - 1)
    def _():
        attend(_dot_nt(qb, kn_s[...].astype(BF16)), keys_ref[0, :, n_keys - page:n_keys], n_keys - page,
               lambda p: _dot(p, vn_s[...].astype(BF16)), 0)
        m = m_s[0]
        for c in range(1, n_chains):
            m = jnp.maximum(m, m_s[c])
        l = sum(l_s[c] * jnp.exp(m_s[c] - m) for c in range(n_chains))
        acc = sum(acc_s[c] * jnp.exp(m_s[c] - m) for c in range(n_chains))
        o = acc / l
        for s in range(n_heads // 2):
            pb, r = divmod(s, per_kv)
            h_lo = (2 * pb) * per_kv + r
            h_hi = (2 * pb + 1) * per_kv + r
            o_lo = o[h_lo * t_new:(h_lo + 1) * t_new, pb * LANES:(pb + 1) * LANES]
            o_hi = o[h_hi * t_new:(h_hi + 1) * t_new, pb * LANES:(pb + 1) * LANES]
            y_ref[0, :, s * LANES:(s + 1) * LANES] = jnp.where(low, o_lo, o_hi).astype(y_ref.dtype)


def _pages_per_step(n_pages):
    return math.gcd(n_pages, PAGES_PER_STEP_MAX)


def _dsa_sample(q, qiw, sa, kk_new, k_new, v_new, ck, cv, ci, page_table, layer, *, n_heads, kv_heads, idx_heads,
                wi_lane, att_scale, idx_w_scale):
    db, t_new, dq = q.shape
    _, _, kv_heads_, hd, page = ck.shape
    kvw = kv_heads_ * hd
    n_pages = page_table.shape[1]
    assert page == LANES and t_new % SUBLANES == 0 and t_new <= page and kv_heads_ == kv_heads and hd == LANES // 2
    pg = _pages_per_step(n_pages)
    n_chains = 2 if pg % 2 == 0 else 1
    pc = pg // n_chains
    n_steps = n_pages // pg
    n_keys = (n_pages + 1) * page
    topk = min(TOPK_MAX, (n_pages * page + t_new) // 4)
    idx_spec = lambda r: pl.BlockSpec((1, 1, hd, page), lambda b, i, pt: (layer, pt[b, i * pg + r], 0, 0))
    kv_spec = lambda r: pl.BlockSpec((1, 1, kv_heads, hd, page), lambda b, i, pt: (layer, pt[b, i * pg + r], 0, 0, 0))
    row_spec = lambda w: pl.BlockSpec((1, t_new, w), lambda b, i, pt: (b, 0, 0))

    keys = pl.pallas_call(
        functools.partial(_dsa_sample_scores_kernel, pg=pg, t_new=t_new, idx_heads=idx_heads, page=page,
                          idx_w_scale=idx_w_scale, wi_lane=wi_lane, n_keys=n_keys),
        grid_spec=pltpu.PrefetchScalarGridSpec(
            num_scalar_prefetch=1,
            grid=(db, n_steps),
            in_specs=[row_spec(qiw.shape[-1]), row_spec(LANES), row_spec(LANES)] + [idx_spec(r) for r in range(pg)],
            out_specs=row_spec(n_keys),
            scratch_shapes=[pltpu.VMEM((idx_heads * t_new, hd), F32), pltpu.VMEM((idx_heads * t_new, LANES), F32),
                            pltpu.VMEM((page, hd), F32)],
        ),
        out_shape=jax.ShapeDtypeStruct((db, t_new, n_keys), I32),
        compiler_params=_params("parallel", "arbitrary"),
        name="dsa_sample_scores",
    )(page_table, qiw, sa, kk_new, *([ci] * pg))

    n_rows = db * t_new
    rb = math.gcd(n_rows, LANES)
    thr, cut = pl.pallas_call(
        functools.partial(_dsa_sample_threshold_kernel, rb=rb, topk=topk, n_keys=n_keys),
        out_shape=[jax.ShapeDtypeStruct((n_rows, LANES), I32)] * 2,
        compiler_params=pltpu.CompilerParams(vmem_limit_bytes=VMEM_LIMIT_BYTES),
        name="dsa_sample_threshold",
    )(keys.reshape(n_rows, n_keys))
    thr = thr.reshape(db, t_new, LANES)
    cut = cut.reshape(db, t_new, LANES)

    return pl.pallas_call(
        functools.partial(_dsa_sample_attend_kernel, pg=pg, pc=pc, t_new=t_new, n_heads=n_heads, kv_heads=kv_heads,
                          page=page, att_scale=att_scale, n_keys=n_keys),
        grid_spec=pltpu.PrefetchScalarGridSpec(
            num_scalar_prefetch=1,
            grid=(db, n_steps),
            in_specs=[row_spec(dq), row_spec(n_keys), row_spec(LANES), row_spec(LANES), row_spec(kvw), row_spec(kvw)]
                     + [kv_spec(r) for r in range(pg)] * 2,
            out_specs=row_spec(dq),
            scratch_shapes=[pltpu.VMEM((n_heads * t_new, kvw), F32), pltpu.VMEM((n_chains, n_heads * t_new, 1), F32),
                            pltpu.VMEM((n_chains, n_heads * t_new, 1), F32),
                            pltpu.VMEM((n_chains, n_heads * t_new, kvw), F32),
                            pltpu.VMEM((page, kvw), F32), pltpu.VMEM((page, kvw), F32)],
        ),
        out_shape=jax.ShapeDtypeStruct((db, t_new, dq), BF16),
        compiler_params=_params("parallel", "arbitrary"),
        name="dsa_sample_attend",
    )(page_table, q, keys, thr, cut, k_new, v_new, *([ck] * pg), *([cv] * pg))


def _merge_kernel(x_ref, ys_ref, ya_ref, gs_ref, ga_ref, ps_ref, pa_ref, wo_ref, o_ref):
    merged = (_sigmoid(gs_ref[...]) * _dot(ys_ref[...], ps_ref[...])
              + _sigmoid(ga_ref[...]) * _dot(ya_ref[...], pa_ref[...]))
    o_ref[...] = x_ref[...] + _dot(merged.astype(BF16), wo_ref[...])


def _merge(x, y_ssd, y_attn, g_s, g_a, p_ssd, p_attn, w_out):
    t, d = x.shape
    tm = min(ROW_TILE, t)
    assert t % tm == 0
    rows = lambda w: pl.BlockSpec((tm, w), lambda i: (i, 0))
    full = lambda a: pl.BlockSpec(a.shape, lambda i: (0, 0))
    return pl.pallas_call(
        _merge_kernel,
        grid=(t // tm,),
        in_specs=[rows(d), rows(y_ssd.shape[1]), rows(y_attn.shape[1]), rows(d), rows(d),
                  full(p_ssd), full(p_attn), full(w_out)],
        out_specs=rows(d),
        out_shape=jax.ShapeDtypeStruct((t, d), F32),
        compiler_params=_params("parallel"),
        name="merge",
    )(x, y_ssd, y_attn, g_s, g_a, p_ssd, p_attn, w_out)


def _pair_slot_perm(n_heads, kv_heads, hd):
    per_kv = n_heads // kv_heads
    cols = []
    for s in range(n_heads // 2):
        pb, r = divmod(s, per_kv)
        for h in ((2 * pb) * per_kv + r, (2 * pb + 1) * per_kv + r):
            cols.extend(range(h * hd, (h + 1) * hd))
    return np.asarray(cols, np.int32)


def kernel(x_prompt, x_sample, cache_k, cache_v, cache_idx_k, state_ssm, state_conv, page_table, ffn1_norm, ffn1_w1, ffn1_w2, mix_norm, w_in, conv_w, conv_b, dt_bias, a_log, d_skip, ssd_norm, w_branch_ssd, w_branch_attn, w_out, ffn2_norm, ffn2_w1, ffn2_w2, final_norm):
    bp, seq, d_model = x_prompt.shape
    db, dseq, _ = x_sample.shape
    depth, _, _, kv_heads, head_dim = cache_k.shape
    idx_dim = cache_idx_k.shape[-1]
    ssd_heads, ssd_hd, d_state = state_ssm.shape[2:]
    conv_dim = state_conv.shape[-1]
    d_inner = ssd_norm.shape[-1]
    n_groups = (conv_dim - d_inner) // (2 * d_state)
    attn_dim = w_branch_attn.shape[1]
    n_heads = attn_dim // head_dim
    kvw = kv_heads * head_dim
    d_proj = w_in.shape[-1]
    idx_heads = (d_proj - (d_inner + conv_dim + ssd_heads + attn_dim + 2 * kvw + idx_dim + 2 * d_model)) // (idx_dim + 1)
    assert head_dim == LANES // 2 and idx_dim == LANES // 2 and ssd_heads + idx_heads <= LANES
    att_scale = head_dim ** -0.5
    idx_w_scale = (idx_heads ** -0.5) * (idx_dim ** -0.5)
    sizes = (d_inner, conv_dim, ssd_heads, attn_dim, kvw, kvw, idx_heads * idx_dim, idx_dim, idx_heads, d_model, d_model)
    assert sum(sizes) == d_proj
    offs = np.concatenate([[0], np.cumsum(sizes)])
    perm = _pair_slot_perm(n_heads, kv_heads, head_dim)
    wi_lane = ssd_heads

    tp, ts = bp * seq, db * dseq
    yp = x_prompt.reshape(tp, d_model)
    ys = x_sample.reshape(ts, d_model)
    ck = jnp.transpose(cache_k, (0, 1, 3, 4, 2))
    cv = jnp.transpose(cache_v, (0, 1, 3, 4, 2))
    ci = jnp.transpose(cache_idx_k, (0, 1, 3, 2))
    dsa_kw = dict(n_heads=n_heads, kv_heads=kv_heads, idx_heads=idx_heads, wi_lane=wi_lane, att_scale=att_scale,
                  idx_w_scale=idx_w_scale)
    outs = {n: [] for n in ("kp", "vp", "ip", "sp", "cp", "ks", "vs", "is", "ss", "cs")}
    hist = conv_w.shape[1] - 1

    for l in range(depth):
        wl = w_in[l]
        col = lambda i: wl[:, offs[i]:offs[i + 1]]
        w_z, w_xbc, w_dt, w_q, w_k, w_v, w_qi, w_ki, w_wi, w_gs, w_ga = [col(i) for i in range(11)]
        w_sa = jnp.concatenate([w_dt, w_wi, jnp.zeros((d_model, LANES - ssd_heads - idx_heads), F32)], axis=1)
        w_qi_wide = jnp.pad(w_qi.reshape(d_model, idx_heads, idx_dim), ((0, 0), (0, 0), (0, LANES - idx_dim)))
        bf = lambda w: w.astype(BF16)
        plain = lambda w, *dts: (bf(w), dts, False, None)
        cols = lambda w, *dts: (bf(w.T), dts, True, None)
        w_kk = jnp.concatenate([w_ki, w_ki], axis=1)
        zcol = jnp.zeros((d_model, head_dim), F32)
        w_vx = jnp.concatenate([a for g in range(kv_heads) for a in
                                ((w_v[:, g * head_dim:(g + 1) * head_dim], zcol) if g % 2 == 0 else
                                 (zcol, w_v[:, g * head_dim:(g + 1) * head_dim]))], axis=1)
        half = np.arange(kv_heads * LANES) // head_dim
        b_vx = jnp.asarray(((half % 2) != ((half // 2) % 2)).astype(np.float32)).reshape(1, -1)
        f1w1, f1w2, f2w1, f2w2 = bf(ffn1_w1[l]), bf(ffn1_w2[l]), bf(ffn2_w1[l]), bf(ffn2_w2[l])
        p_ssd, p_attn, wo = bf(w_branch_ssd[l]), bf(w_branch_attn[l][perm, :]), bf(w_out[l])
        last = l == depth - 1
        ssd_kw = dict(n_groups=n_groups, d_state=d_state)
        ssd_w = (conv_w[l], conv_b[l], dt_bias[l], a_log[l], d_skip[l], ssd_norm[l])

        yp = _ffn(yp, ffn1_norm[l], f1w1, f1w2)
        ys = _ffn(ys, ffn1_norm[l], f1w1, f1w2)

        z, xbc = _norm_linear(yp, mix_norm[l], [plain(w_z, F32), plain(w_xbc, F32)], PROJ_ROW_TILE)
        q, kt, ktb, vx, vt, kkb, kit, sa, g_s, g_a, qi = _norm_linear(
            yp, mix_norm[l],
            [plain(w_q[:, perm], BF16), cols(w_k, F32, BF16), (bf(w_vx), (BF16,), False, b_vx), cols(w_v, F32),
             plain(w_kk, BF16), cols(w_ki, F32), plain(w_sa, F32), plain(w_gs, F32), plain(w_ga, F32),
             plain(w_qi, BF16)], PROJ_ROW_TILE, batch=bp)
        r3 = lambda a: a.reshape(bp, seq, a.shape[-1])
        y_ssd, hfp = _ssd(r3(z), r3(xbc), r3(sa), jnp.zeros((1, bp, hist, conv_dim), F32),
                          jnp.zeros((1, bp, ssd_heads, ssd_hd, d_state), F32), 0, *ssd_w, **ssd_kw)
        y_attn = _dsa_prompt(r3(q), r3(qi), r3(sa), r3(kkb), ktb, r3(vx), **dsa_kw)
        yp = _merge(yp, y_ssd.reshape(tp, d_inner), y_attn.reshape(tp, attn_dim), g_s, g_a, p_ssd, p_attn, wo)
        cbp = r3(xbc)[:, seq - hist:, :]
        heads_last = lambda a: jnp.transpose(a.reshape(bp, kv_heads, head_dim, seq), (0, 3, 1, 2))
        kp, vp, kip = heads_last(kt), heads_last(vt), jnp.transpose(kit, (0, 2, 1))

        z, xbc = _norm_linear(ys, mix_norm[l], [plain(w_z, F32), plain(w_xbc, F32)], PROJ_ROW_TILE)
        q, k, kb, v, vb, kk, kkb, sa, g_s, g_a, qi = _norm_linear(
            ys, mix_norm[l],
            [plain(w_q[:, perm], BF16), plain(w_k, F32, BF16), plain(w_v, F32, BF16), plain(w_kk, F32, BF16),
             plain(w_sa, F32), plain(w_gs, F32), plain(w_ga, F32),
             plain(w_qi_wide.reshape(d_model, idx_heads * LANES), BF16)], PROJ_ROW_TILE)
        r3 = lambda a: a.reshape(db, dseq, a.shape[-1])
        y_ssd, hfs = _ssd(r3(z), r3(xbc), r3(sa), state_conv, state_ssm, l, *ssd_w, **ssd_kw)
        y_attn = _dsa_sample(r3(q), r3(qi), r3(sa), r3(kkb), r3(kb), r3(vb), ck, cv, ci, page_table, l, **dsa_kw)
        ys = _merge(ys, y_ssd.reshape(ts, d_inner), y_attn.reshape(ts, attn_dim), g_s, g_a, p_ssd, p_attn, wo)
        cbs = r3(xbc)[:, dseq - hist:, :]
        kss, vss = k.reshape(db, dseq, kv_heads, head_dim), v.reshape(db, dseq, kv_heads, head_dim)
        kis = r3(kk)[:, :, :idx_dim]

        pg_ = final_norm if last else None
        yp = _ffn(yp, ffn2_norm[l], f2w1, f2w2, pg_)
        ys = _ffn(ys, ffn2_norm[l], f2w1, f2w2, pg_)
        for n, a in zip(("kp", "vp", "ip", "sp", "cp", "ks", "vs", "is", "ss", "cs"),
                        (kp, vp, kip, hfp, cbp, kss, vss, kis, hfs, cbs)):
            outs[n].append(a)

    st = lambda n: jnp.stack(outs[n])
    return (yp.reshape(bp, seq, d_model), ys.reshape(db, dseq, d_model),
            st("kp"), st("vp"), st("ip"), st("sp"), st("cp"),
            st("ks"), st("vs"), st("is"), st("ss"), st("cs"))
```

```python
import functools
import math

import jax
import jax.numpy as jnp
import numpy as np
from jax import lax
from jax.experimental import pallas as pl
from jax.experimental.pallas import tpu as pltpu

F32 = jnp.float32
BF16 = jnp.bfloat16
I32 = jnp.int32

EPS = 1e-6
SSD_CHUNK = 128
TOPK_MAX = 256
LANES = 128
SUBLANES = 8
VMEM_LIMIT_BYTES = 56 * 1024 * 1024
NEG = -1e30
INT_MIN = -(2 ** 31)

ROW_TILE = 512
PROJ_ROW_TILE = 256
FF_TILE_MAX = 1536
Q_TILE = 128
KEY_TILE = 512
PAGES_PER_STEP_MAX = 64


def _params(*sem):
    return pltpu.CompilerParams(dimension_semantics=sem, vmem_limit_bytes=VMEM_LIMIT_BYTES)


def _sigmoid(x):
    return 1.0 / (1.0 + jnp.exp(-x))


def _rms(x, g):
    return x * lax.rsqrt(jnp.mean(x * x, axis=-1, keepdims=True) + EPS) * g


def _dot(a, b):
    return jnp.dot(a, b, preferred_element_type=F32)


def _dot_nt(a, b):
    return lax.dot_general(a, b, (((1,), (1,)), ((), ())), preferred_element_type=F32)


def _split2(x):
    hi = x.astype(BF16)
    lo = (x - hi.astype(F32)).astype(BF16)
    return hi, lo


def _split3(x):
    hi = x.astype(BF16)
    r = x - hi.astype(F32)
    mid = r.astype(BF16)
    lo = (r - mid.astype(F32)).astype(BF16)
    return hi, mid, lo


def _ffn_kernel(*refs, post_norm):
    if post_norm:
        x_ref, g_ref, wa_ref, wb_ref, w2_ref, pg_ref, o_ref, h_s, acc_s = refs
    else:
        x_ref, g_ref, wa_ref, wb_ref, w2_ref, o_ref, h_s, acc_s = refs
    f = pl.program_id(1)

    @pl.when(f == 0)
    def _():
        h_s[...] = _rms(x_ref[...], g_ref[...]).astype(BF16)
        acc_s[...] = jnp.zeros_like(acc_s)

    h = h_s[...]
    a = _dot(h, wa_ref[...])
    b = _dot(h, wb_ref[...])
    u = (a * _sigmoid(a) * b).astype(BF16)
    acc_s[...] += _dot(u, w2_ref[...])

    @pl.when(f == pl.num_programs(1) - 1)
    def _():
        y = x_ref[...] + 0.5 * acc_s[...]
        if post_norm:
            y = _rms(y, pg_ref[...])
        o_ref[...] = y


def _ff_tile(d_ff):
    best = None
    for t in range(LANES, d_ff + 1, LANES):
        if d_ff % t == 0 and t <= FF_TILE_MAX:
            best = t
    assert best is not None, d_ff
    return best


def _ffn(x, g, w1, w2, post_gain=None):
    t, d = x.shape
    d_ff = w2.shape[0]
    tm = min(ROW_TILE, t)
    tf = _ff_tile(d_ff)
    nf = d_ff // tf
    assert t % tm == 0
    post_norm = post_gain is not None
    in_specs = [
        pl.BlockSpec((tm, d), lambda i, f: (i, 0)),
        pl.BlockSpec((1, d), lambda i, f: (0, 0)),
        pl.BlockSpec((d, tf), lambda i, f: (0, f)),
        pl.BlockSpec((d, tf), lambda i, f: (0, f + nf)),
        pl.BlockSpec((tf, d), lambda i, f: (f, 0)),
    ]
    args = [x, g.reshape(1, d), w1, w1, w2]
    if post_norm:
        in_specs.append(pl.BlockSpec((1, d), lambda i, f: (0, 0)))
        args.append(post_gain.reshape(1, d))
    return pl.pallas_call(
        functools.partial(_ffn_kernel, post_norm=post_norm),
        grid=(t // tm, nf),
        in_specs=in_specs,
        out_specs=pl.BlockSpec((tm, d), lambda i, f: (i, 0)),
        out_shape=jax.ShapeDtypeStruct((t, d), F32),
        scratch_shapes=[pltpu.VMEM((tm, d), BF16), pltpu.VMEM((tm, d), F32)],
        compiler_params=_params("parallel", "arbitrary"),
        name="ffn",
    )(*args)


def _norm_linear_kernel(*refs, plan):
    n_w = len(plan)
    n_b = sum(has_bias for _, _, has_bias in plan)
    x_ref, g_ref = refs[:2]
    w_refs = refs[2:2 + n_w]
    b_refs = list(refs[2 + n_w:2 + n_w + n_b])
    o_refs = list(refs[2 + n_w + n_b:])
    h = _rms(x_ref[...], g_ref[...]).astype(BF16)
    for w_ref, (dts, transposed, has_bias) in zip(w_refs, plan):
        r = _dot_nt(w_ref[...], h) if transposed else _dot(h, w_ref[...])
        if has_bias:
            r = r + b_refs.pop(0)[...]
        for dt in dts:
            o_ref = o_refs.pop(0)
            if transposed:
                o_ref[0] = r.astype(dt)
            else:
                o_ref[...] = r.astype(dt)


def _norm_linear(x, g, maps, tm, batch=1):
    t, d = x.shape
    tm = min(tm, t)
    s = t // batch
    assert t % tm == 0 and s % tm == 0
    steps = s // tm
    in_specs = [pl.BlockSpec((tm, d), lambda i: (i, 0)), pl.BlockSpec((1, d), lambda i: (0, 0))]
    biases, out_specs, out_shape = [], [], []
    for w, dts, transposed, bias in maps:
        in_specs.append(pl.BlockSpec(w.shape, lambda i: (0, 0)))
        n = w.shape[0] if transposed else w.shape[1]
        for dt in dts:
            if transposed:
                out_specs.append(pl.BlockSpec((1, n, tm), lambda i: (i // steps, 0, i % steps)))
                out_shape.append(jax.ShapeDtypeStruct((batch, n, s), dt))
            else:
                out_specs.append(pl.BlockSpec((tm, n), lambda i: (i, 0)))
                out_shape.append(jax.ShapeDtypeStruct((t, n), dt))
        if bias is not None:
            assert not transposed
            biases.append(bias)
    in_specs += [pl.BlockSpec(b.shape, lambda i: (0, 0)) for b in biases]
    plan = tuple((tuple(dts), transposed, bias is not None) for _, dts, transposed, bias in maps)
    return pl.pallas_call(
        functools.partial(_norm_linear_kernel, plan=plan),
        grid=(t // tm,),
        in_specs=in_specs,
        out_specs=out_specs,
        out_shape=out_shape,
        compiler_params=_params("parallel"),
        name="norm_linear",
    )(x, g.reshape(1, d), *[m[0] for m in maps], *biases)


def _ssd_kernel(z_ref, xbc_ref, dt_ref, buf_ref, h0_ref, cw_ref, cb_ref, dtb_ref, alog_ref, dsk_ref, ng_ref, e_ref,
                y_ref, hfin_ref, xp_s, ht_s, *, qin, d_inner, n_groups, d_state, conv_w):
    q = SSD_CHUNK
    c = pl.program_id(1)
    hp_blocks = d_inner // LANES
    gw = d_inner // n_groups
    assert d_state == LANES and gw % LANES == 0
    pad = SUBLANES
    hist = conv_w - 1

    @pl.when(c == 0)
    def _():
        xp_s[0:pad, :] = buf_ref[0, 0]
        if qin < q:
            xp_s[pad + qin:pad + q, :] = jnp.zeros((q - qin, xp_s.shape[1]), F32)
        for i in range(hp_blocks):
            ht_s[:, i * LANES:(i + 1) * LANES] = h0_ref[0, 0, i * LANES:(i + 1) * LANES, :].T

    xp_s[pad:pad + qin, :] = xbc_ref[0]
    x_cur = xp_s[pad:pad + q, :]
    x_prev = xp_s[0:pad, :]
    row8 = lax.broadcasted_iota(I32, (pad, 1), 0)
    acc = cb_ref[...] + x_cur * cw_ref[hist:hist + 1, :]
    for i in range(hist):
        s = hist - i
        rolled = pltpu.roll(x_cur, s, 0)
        head = jnp.where(row8 < s, pltpu.roll(x_prev, s, 0), rolled[0:pad])
        acc = acc + jnp.concatenate([head, rolled[pad:]], axis=0) * cw_ref[i:i + 1, :]
    xc = acc * _sigmoid(acc)
    tail = xp_s[pad + qin - hist:pad + qin, :]
    xp_s[pad - hist:pad, :] = tail

    xs = xc[:, :d_inner]
    bm = xc[:, d_inner:d_inner + n_groups * d_state]
    cm = xc[:, d_inner + n_groups * d_state:]

    dt_raw = dt_ref[0] + dtb_ref[...]
    dt = jnp.maximum(dt_raw, 0.0) + jnp.log1p(jnp.exp(-jnp.abs(dt_raw)))
    if qin < q:
        dt = jnp.concatenate([dt, jnp.zeros((q - qin, LANES), F32)], axis=0)
    la = dt * (-jnp.exp(alog_ref[...]))

    ri = lax.broadcasted_iota(I32, (q, q), 0)
    ci = lax.broadcasted_iota(I32, (q, q), 1)
    causal = ri >= ci
    tril = jnp.where(causal, 1.0, 0.0).astype(BF16)
    eye = jnp.where(ri == ci, 1.0, 0.0).astype(BF16)
    a_cs = sum(_dot(tril, p) for p in _split3(la))
    a_cs_t = sum(_dot_nt(eye, p) for p in _split3(a_cs))
    dec = jnp.exp(a_cs[q - 1:q, :] - a_cs)
    eac = jnp.exp(a_cs)
    stacked = jnp.concatenate([dt, dec, eac], axis=0)
    expd = sum(_dot(p, e_ref[...]) for p in _split2(stacked))
    dt_e, dec_e, eac_e = expd[0:q], expd[q:2 * q], expd[2 * q:3 * q]

    x = xs * dt_e
    xb = x.astype(BF16)
    xd = (x * dec_e).astype(BF16)
    lane = lax.broadcasted_iota(I32, (q, LANES), 1)
    hd = LANES // 2
    heads_per_group = gw // hd

    y_parts = []
    for g in range(n_groups):
        gs = slice(g * gw, (g + 1) * gw)
        cg = cm[:, g * d_state:(g + 1) * d_state].astype(BF16)
        bg = bm[:, g * d_state:(g + 1) * d_state]
        cb = _dot_nt(cg, bg.astype(BF16))
        y_off = _dot(cg, ht_s[:, gs].astype(BF16))
        pair_parts = []
        for p in range(heads_per_group // 2):
            xp = xb[:, g * gw + p * LANES:g * gw + (p + 1) * LANES]
            res = []
            for k in range(2):
                j = g * heads_per_group + 2 * p + k
                diff = a_cs[:, j:j + 1] - a_cs_t[j:j + 1, :]
                lm = jnp.exp(jnp.where(causal, diff, NEG))
                res.append(_dot((cb * lm).astype(BF16), xp))
            pair_parts.append(jnp.where(lane < hd, res[0], res[1]))
        y_diag = jnp.concatenate(pair_parts, axis=1)
        y_parts.append(y_diag + y_off * eac_e[:, gs])
        st = _dot(bg.T.astype(BF16), xd[:, gs])
        ht_s[:, gs] = ht_s[:, gs] * eac_e[q - 1:q, gs] + st

    zz = z_ref[0]
    outs = []
    for g in range(n_groups):
        gs = slice(g * gw, (g + 1) * gw)
        yv = (y_parts[g][:qin] + dsk_ref[:, gs] * xs[:qin, gs]) * (zz[:, gs] * _sigmoid(zz[:, gs]))
        ms = jnp.mean(yv * yv, axis=-1, keepdims=True)
        outs.append(yv * lax.rsqrt(ms + EPS) * ng_ref[:, gs])
    y_ref[0] = jnp.concatenate(outs, axis=1).astype(y_ref.dtype)

    @pl.when(c == pl.num_programs(1) - 1)
    def _():
        for i in range(hp_blocks):
            hfin_ref[0, i * LANES:(i + 1) * LANES, :] = ht_s[:, i * LANES:(i + 1) * LANES].T


def _ssd(z, xbc, dtp, buf, h0, layer, conv_w, conv_b, dt_bias, a_log, d_skip, norm_g, *, n_groups, d_state):
    b, l, d_inner = z.shape
    conv_dim = xbc.shape[-1]
    n_heads, p_dim, n_state = h0.shape[2:]
    width = conv_w.shape[0]
    hist = width - 1
    assert n_heads <= LANES and p_dim == LANES // 2 and n_state == d_state and hist <= SUBLANES
    qin = math.gcd(l, SSD_CHUNK)
    assert qin % SUBLANES == 0 and qin >= hist
    nc = l // qin
    assert nc == 1 or qin == SSD_CHUNK
    hp = n_heads * p_dim
    buf8 = jnp.pad(buf, ((0, 0), (0, 0), (SUBLANES - hist, 0), (0, 0)))
    cw8 = jnp.pad(conv_w, ((0, SUBLANES - width), (0, 0)))
    pad1 = lambda v: jnp.pad(v.reshape(1, -1), ((0, 0), (0, LANES - n_heads)))
    expand = (np.arange(LANES)[:, None] == (np.arange(d_inner)[None, :] // p_dim)).astype(np.float32)
    full = lambda shape: pl.BlockSpec(shape, lambda i, c: (0,) * len(shape))
    y, hfin = pl.pallas_call(
        functools.partial(_ssd_kernel, qin=qin, d_inner=d_inner, n_groups=n_groups, d_state=d_state, conv_w=width),
        grid=(b, nc),
        in_specs=[
            pl.BlockSpec((1, qin, d_inner), lambda i, c: (i, c, 0)),
            pl.BlockSpec((1, qin, conv_dim), lambda i, c: (i, c, 0)),
            pl.BlockSpec((1, qin, LANES), lambda i, c: (i, c, 0)),
            pl.BlockSpec((1, 1, SUBLANES, conv_dim), lambda i, c: (layer, i, 0, 0)),
            pl.BlockSpec((1, 1, hp, n_state), lambda i, c: (layer, i, 0, 0)),
            full((SUBLANES, conv_dim)), full((1, conv_dim)), full((1, LANES)), full((1, LANES)),
            full((1, d_inner)), full((1, d_inner)), full((LANES, d_inner)),
        ],
        out_specs=[
            pl.BlockSpec((1, qin, d_inner), lambda i, c: (i, c, 0)),
            pl.BlockSpec((1, hp, n_state), lambda i, c: (i, 0, 0)),
        ],
        out_shape=[jax.ShapeDtypeStruct((b, l, d_inner), BF16), jax.ShapeDtypeStruct((b, hp, n_state), F32)],
        scratch_shapes=[pltpu.VMEM((SUBLANES + SSD_CHUNK, conv_dim), F32), pltpu.VMEM((n_state, hp), F32)],
        compiler_params=_params("parallel", "arbitrary"),
        name="ssd",
    )(z, xbc, dtp, buf8, h0.reshape(-1, b, hp, n_state), cw8, conv_b.reshape(1, -1), pad1(dt_bias), pad1(a_log),
      jnp.repeat(d_skip, p_dim).reshape(1, -1), norm_g.reshape(1, -1), jnp.asarray(expand, BF16))
    return y, hfin.reshape(b, n_heads, p_dim, n_state)


def _score_key(sc, valid):
    bits = lax.bitcast_convert_type(sc + 0.0, I32)
    key = jnp.where(bits < 0, bits ^ jnp.int32(0x7FFFFFFF), bits)
    return jnp.where(valid, key, jnp.int32(INT_MIN))


def _kth_largest_key(count_ge, shape, k):
    def body(b, st):
        t, c_t = st
        cand = t + lax.shift_left(jnp.int32(1), jnp.int32(31) - b)
        c = count_ge(cand)
        return jnp.where(c >= k, cand, t), jnp.where(c >= k, c, c_t)
    return lax.fori_loop(0, 32, body, (jnp.full(shape, INT_MIN, I32), jnp.zeros(shape, F32)))


def _tie_cutoff(count_tie_before, need, shape, n_bits):
    def body(b, p):
        bit = lax.shift_left(jnp.int32(1), jnp.int32(n_bits - 1) - b)
        p_c = p | bit
        return jnp.where(count_tie_before(p_c) <= need - 1, p_c, p)
    return lax.fori_loop(0, n_bits, body, jnp.zeros(shape, I32))


def _dsa_prompt_kernel(q_ref, qi_ref, sa_ref, kk_ref, kt_ref, vx_ref, y_ref,
                       keys_s, qpad_s, qipad_s, w_s, m_s, acc_s,
                       *, tq, tk, n_heads, kv_heads, idx_heads, topk, att_scale, idx_w_scale, wi_lane, seq_len):
    i = pl.program_id(1)
    hd = LANES // 2
    per_kv = n_heads // kv_heads
    n_kt = (i * tq + tq - 1) // tk + 1
    lane = lax.broadcasted_iota(I32, (tq, LANES), 1)
    low = lane < hd

    qf = q_ref[0].astype(F32) * (att_scale * math.log2(math.e))
    zeros = jnp.zeros((tq, LANES), F32)
    for h in range(n_heads):
        g, r = divmod(h, per_kv)
        slot = (g // 2) * per_kv + r
        sl = qf[:, slot * LANES:(slot + 1) * LANES]
        half = jnp.where(low, sl, 0.0) if g % 2 == 0 else jnp.where(low, 0.0, sl)
        row = [zeros] * (kv_heads // 2)
        row[g // 2] = half
        qpad_s[h * tq:(h + 1) * tq, :] = jnp.concatenate(row, axis=1).astype(BF16)
    qif = qi_ref[0].astype(F32)
    for h in range(idx_heads):
        sl = qif[:, (h // 2) * LANES:(h // 2 + 1) * LANES]
        half = jnp.where(low, sl, 0.0) if h % 2 == 0 else jnp.where(low, 0.0, sl)
        qipad_s[h * tq:(h + 1) * tq, :] = half.astype(BF16)
    w_s[...] = sa_ref[0].T[wi_lane:wi_lane + idx_heads, :] * idx_w_scale

    qpos = i * tq + lax.broadcasted_iota(I32, (tk, tq), 1)
    kiota = lax.broadcasted_iota(I32, (tk, tq), 0)
    siota = lax.broadcasted_iota(I32, (SUBLANES, tq), 0)

    tiles_per_step = 2

    def score_tiles(jj, carry):
        for u in range(tiles_per_step):
            j = jj * tiles_per_step + u
            kt = kk_ref[0, pl.ds(pl.multiple_of(j * tk, tk), tk), :]
            r = _dot_nt(kt, qipad_s[...])
            sc = jnp.zeros((tk, tq), F32)
            for h in range(idx_heads):
                sc = sc + jnp.maximum(r[:, h * tq:(h + 1) * tq], 0.0) * w_s[h:h + 1, :]
            keys_s[j] = _score_key(sc, j * tk + kiota <= qpos)
        return carry
    lax.fori_loop(0, (n_kt + tiles_per_step - 1) // tiles_per_step, score_tiles, 0)

    n_acc = 4

    def count(pred):
        def body(j, cnts):
            cnts = list(cnts)
            for g in range(tk // SUBLANES):
                hit = pred(keys_s[j, g * SUBLANES:(g + 1) * SUBLANES, :], j * tk + g * SUBLANES)
                cnts[g % n_acc] = cnts[g % n_acc] + jnp.where(hit, 1.0, 0.0)
            return tuple(cnts)
        cnts = lax.fori_loop(0, n_kt, body, (jnp.zeros((SUBLANES, tq), F32),) * n_acc)
        return jnp.sum(sum(cnts), axis=0, keepdims=True)

    def count_ge(t):
        tb = jnp.broadcast_to(t, (SUBLANES, tq))
        return count(lambda key, base: key >= tb)

    thr, c_ge = _kth_largest_key(count_ge, (1, tq), topk)
    tie = c_ge > topk

    @pl.when(jnp.max(jnp.where(tie, 1, 0)) > 0)
    def _():
        need = topk - count(lambda key, base: key > thr)
        cut = _tie_cutoff(lambda p: count(lambda key, base: (key == thr) & (base + siota < p)),
                          need, (1, tq), int(seq_len).bit_length())
        def drop(j, carry):
            key = keys_s[j]
            keys_s[j] = jnp.where(tie & (key == thr) & (j * tk + kiota > cut), jnp.int32(INT_MIN), key)
            return carry
        lax.fori_loop(0, n_kt, drop, 0)

    thr_sel = jnp.maximum(thr, jnp.int32(INT_MIN + 1))

    n_chunks = tk // LANES
    m_s[...] = jnp.full(m_s.shape, NEG, F32)
    acc_s[...] = jnp.zeros(acc_s.shape, F32)

    def attend_tile(j, carry):
        ks = pl.ds(pl.multiple_of(j * tk, tk), tk)
        kt = kt_ref[0, :, ks]
        half = n_heads * tq // 2
        s_halves = [_dot(qpad_s[0:half], kt), _dot(qpad_s[half:], kt)]
        bias = [jnp.where(keys_s[j, c * LANES:(c + 1) * LANES, :] >= thr_sel, 0.0, NEG).T for c in range(n_chunks)]
        for g in range(kv_heads):
            ps, alphas = [], []
            for r in range(per_kv):
                rows = slice((g * per_kv + r) * tq, (g * per_kv + r + 1) * tq)
                src, off = s_halves[rows.start // half], rows.start % half
                s = [src[off:off + tq, c * LANES:(c + 1) * LANES] + bias[c] for c in range(n_chunks)]
                smax = s[0]
                for c in range(1, n_chunks):
                    smax = jnp.maximum(smax, s[c])
                m_old = m_s[rows]
                m_new = jnp.maximum(m_old, jnp.max(smax, axis=1, keepdims=True))
                m_s[rows] = m_new
                alphas.append(jnp.exp2(m_old - m_new))
                ps.append(jnp.concatenate([jnp.exp2(s[c] - m_new) for c in range(n_chunks)], axis=1).astype(BF16))
            grows = slice(g * per_kv * tq, (g + 1) * per_kv * tq)
            pv = _dot(jnp.concatenate(ps, axis=0), vx_ref[0, ks, g * LANES:(g + 1) * LANES])
            acc_s[grows] = acc_s[grows] * jnp.concatenate(alphas, axis=0) + pv
        return carry
    lax.fori_loop(0, n_kt, attend_tile, 0)

    def normalised(h):
        a = acc_s[h * tq:(h + 1) * tq]
        return a / pltpu.roll(a, hd, 1)
    for s in range(n_heads // 2):
        pb, r = divmod(s, per_kv)
        o_lo = normalised((2 * pb) * per_kv + r)
        o_hi = normalised((2 * pb + 1) * per_kv + r)
        y_ref[0, :, s * LANES:(s + 1) * LANES] = jnp.where(low, o_lo, o_hi).astype(y_ref.dtype)


def _dsa_prompt(q, qi, sa, kk, kt, vx, *, n_heads, kv_heads, idx_heads, wi_lane, att_scale, idx_w_scale):
    b, l, dq = q.shape
    tq, tk = min(Q_TILE, l), min(KEY_TILE, l // 2)
    assert l % tq == 0 and l % (2 * tk) == 0 and kv_heads % 2 == 0 and idx_heads % 2 == 0
    assert wi_lane % SUBLANES == 0 and idx_heads == SUBLANES
    topk = min(TOPK_MAX, l // 4)
    kvw = kt.shape[1]
    return pl.pallas_call(
        functools.partial(_dsa_prompt_kernel, tq=tq, tk=tk, n_heads=n_heads, kv_heads=kv_heads, idx_heads=idx_heads,
                          topk=topk, att_scale=att_scale, idx_w_scale=idx_w_scale, wi_lane=wi_lane, seq_len=l),
        grid=(b, l // tq),
        in_specs=[
            pl.BlockSpec((1, tq, dq), lambda bi, i: (bi, i, 0)),
            pl.BlockSpec((1, tq, qi.shape[-1]), lambda bi, i: (bi, i, 0)),
            pl.BlockSpec((1, tq, LANES), lambda bi, i: (bi, i, 0)),
            pl.BlockSpec((1, l, LANES), lambda bi, i: (bi, 0, 0), pipeline_mode=pl.Buffered(1)),
            pl.BlockSpec((1, kvw, l), lambda bi, i: (bi, 0, 0), pipeline_mode=pl.Buffered(1)),
            pl.BlockSpec((1, l, vx.shape[-1]), lambda bi, i: (bi, 0, 0), pipeline_mode=pl.Buffered(1)),
        ],
        out_specs=pl.BlockSpec((1, tq, dq), lambda bi, i: (bi, i, 0)),
        out_shape=jax.ShapeDtypeStruct((b, l, dq), BF16),
        scratch_shapes=[
            pltpu.VMEM((l // tk, tk, tq), I32),
            pltpu.VMEM((n_heads * tq, kvw), BF16),
            pltpu.VMEM((idx_heads * tq, LANES), BF16),
            pltpu.VMEM((idx_heads, tq), F32),
            pltpu.VMEM((n_heads * tq, LANES), F32),
            pltpu.VMEM((n_heads * tq, LANES), F32),
        ],
        compiler_params=_params("parallel", "arbitrary"),
        name="dsa_prompt",
    )(q, qi, sa, kk, kt, vx)


def _dsa_sample_scores_kernel(pt_ref, qi_ref, sa_ref, kn_ref, *rest, pg, t_new, idx_heads, page,
                              idx_w_scale, wi_lane, n_keys):
    page_refs = rest[:pg]
    keys_ref, qi_s, w_s, kn_s = rest[pg:]
    i = pl.program_id(1)
    hd = LANES // 2
    n_steps = pl.num_programs(1)

    @pl.when(i == 0)
    def _():
        qif = qi_ref[0]
        sa = sa_ref[0]
        for h in range(idx_heads):
            qi_s[h * t_new:(h + 1) * t_new, :] = qif[:, h * LANES:h * LANES + hd].astype(F32)
            w_s[h * t_new:(h + 1) * t_new, :] = jnp.broadcast_to(
                sa[:, wi_lane + h:wi_lane + h + 1] * idx_w_scale, (t_new, LANES))
        kn_s[...] = jnp.zeros(kn_s.shape, F32)
        kn_s[0:t_new, :] = kn_ref[0][:, 0:hd].astype(F32)

    def scores(dots):
        ww = jnp.maximum(dots, 0.0) * w_s[...]
        sc = ww[0:t_new]
        for h in range(1, idx_heads):
            sc = sc + ww[h * t_new:(h + 1) * t_new]
        return sc

    always = jnp.full((t_new, page), True)
    kt = jnp.concatenate([page_refs[r][0, 0].astype(BF16) for r in range(pg)], axis=1)
    dots = _dot(qi_s[...].astype(BF16), kt)
    for r in range(pg):
        off = pl.multiple_of((i * pg + r) * page, page)
        keys_ref[0, :, pl.ds(off, page)] = _score_key(scores(dots[:, r * page:(r + 1) * page]), always)

    @pl.when(i == n_steps - 1)
    def _():
        ti = lax.broadcasted_iota(I32, (t_new, page), 0)
        ki = lax.broadcasted_iota(I32, (t_new, page), 1)
        past = n_keys - page
        dots_new = _dot_nt(qi_s[...].astype(BF16), kn_s[...].astype(BF16))
        keys_ref[0, :, past:n_keys] = _score_key(scores(dots_new), ki <= ti)


def _dsa_sample_threshold_kernel(keys_ref, thr_ref, cut_ref, *, rb, topk, n_keys):
    n_chunks = n_keys // LANES
    unroll = next(u for u in (4, 3, 2, 1) if n_chunks % u == 0)
    liota = lax.broadcasted_iota(I32, (rb, LANES), 1)
    for b in range(keys_ref.shape[0] // rb):
        rows = slice(b * rb, (b + 1) * rb)

        def count(pred):
            def body(cc, cnt):
                for u in range(unroll):
                    off = pl.multiple_of((cc * unroll + u) * LANES, LANES)
                    cnt = cnt + jnp.where(pred(keys_ref[rows, pl.ds(off, LANES)], off), 1.0, 0.0)
                return cnt
            cnt = lax.fori_loop(0, n_chunks // unroll, body, jnp.zeros((rb, LANES), F32))
            return jnp.sum(cnt, axis=1, keepdims=True)

        def count_ge(t):
            tb = jnp.broadcast_to(t, (rb, LANES))
            return count(lambda key, off: key >= tb)

        thr, c_ge = _kth_largest_key(count_ge, (rb, 1), topk)
        tie = c_ge > topk
        thr_ref[rows, :] = jnp.broadcast_to(jnp.maximum(thr, jnp.int32(INT_MIN + 1)), (rb, LANES))
        cut_ref[rows, :] = jnp.full((rb, LANES), n_keys, I32)

        @pl.when(jnp.max(jnp.where(tie, 1, 0)) > 0)
        def _():
            need = topk - count(lambda key, off: key > thr)
            cut = _tie_cutoff(lambda p: count(lambda key, off: (key == thr) & (off + liota < p)), need, (rb, 1),
                              int(n_keys).bit_length())
            cut_ref[rows, :] = jnp.broadcast_to(jnp.where(tie, cut, n_keys), (rb, LANES))


def _dsa_sample_attend_kernel(pt_ref, q_ref, keys_ref, thr_ref, cut_ref, kn_ref, vn_ref, *rest, pg, pc, t_new,
                              n_heads, kv_heads, page, att_scale, n_keys):
    k_refs = rest[:pg]
    v_refs = rest[pg:2 * pg]
    y_ref, qpad_s, m_s, l_s, acc_s, kn_s, vn_s = rest[2 * pg:]
    i = pl.program_id(1)
    hd = LANES // 2
    per_kv = n_heads // kv_heads
    n_steps = pl.num_programs(1)
    lane = lax.broadcasted_iota(I32, (t_new, LANES), 1)
    low = lane < hd

    @pl.when(i == 0)
    def _():
        qf = q_ref[0].astype(F32) * att_scale
        zeros = jnp.zeros((t_new, LANES), F32)
        for h in range(n_heads):
            g, r = divmod(h, per_kv)
            slot = (g // 2) * per_kv + r
            sl = qf[:, slot * LANES:(slot + 1) * LANES]
            half = jnp.where(low, sl, 0.0) if g % 2 == 0 else jnp.where(low, 0.0, sl)
            row = [zeros] * (kv_heads // 2)
            row[g // 2] = half
            qpad_s[h * t_new:(h + 1) * t_new, :] = jnp.concatenate(row, axis=1)
        m_s[...] = jnp.full(m_s.shape, NEG, F32)
        l_s[...] = jnp.zeros(l_s.shape, F32)
        acc_s[...] = jnp.zeros(acc_s.shape, F32)
        kn_s[...] = jnp.zeros(kn_s.shape, F32)
        vn_s[...] = jnp.zeros(vn_s.shape, F32)
        kn_s[0:t_new, :] = kn_ref[0].astype(F32)
        vn_s[0:t_new, :] = vn_ref[0].astype(F32)

    thr = thr_ref[0]
    cut = cut_ref[0]

    def attend(s, key, pos0, pv, chain):
        n = key.shape[1]
        wide = lambda a: jnp.concatenate([a] * (n // LANES), axis=1)
        pos = pos0 + lax.broadcasted_iota(I32, (t_new, n), 1)
        sel = (key > wide(thr)) | ((key == wide(thr)) & (pos <= wide(cut)))
        bias = jnp.where(sel, 0.0, NEG)
        s = s + jnp.concatenate([bias] * n_heads, axis=0)
        m_old = m_s[chain]
        m_new = jnp.maximum(m_old, jnp.max(s, axis=1, keepdims=True))
        alpha = jnp.exp(m_old - m_new)
        p = jnp.exp(s - m_new)
        l_s[chain] = alpha * l_s[chain] + jnp.sum(p, axis=1, keepdims=True)
        m_s[chain] = m_new
        acc_s[chain] = acc_s[chain] * alpha + pv(p.astype(BF16))

    n_chains = m_s.shape[0]
    qb = qpad_s[...].astype(BF16)
    kvw = qb.shape[1]
    for c in range(pg // pc):
        off = pl.multiple_of((i * pg + c * pc) * page, page)
        kt = jnp.concatenate([k_refs[c * pc + r][0, 0].reshape(kvw, page).astype(BF16) for r in range(pc)], axis=1)
        vt = jnp.concatenate([v_refs[c * pc + r][0, 0].reshape(kvw, page).astype(BF16) for r in range(pc)], axis=1)
        attend(_dot(qb, kt), keys_ref[0, :, pl.ds(off, pc * page)], off, lambda p: _dot_nt(p, vt), c % n_chains)

    @pl.when(i == n_steps - 1)
    def _():
        attend(_dot_nt(qb, kn_s[...].astype(BF16)), keys_ref[0, :, n_keys - page:n_keys], n_keys - page,
               lambda p: _dot(p, vn_s[...].astype(BF16)), 0)
        m = m_s[0]
        for c in range(1, n_chains):
            m = jnp.maximum(m, m_s[c])
        l = sum(l_s[c] * jnp.exp(m_s[c] - m) for c in range(n_chains))
        acc = sum(acc_s[c] * jnp.exp(m_s[c] - m) for c in range(n_chains))
        o = acc / l
        for s in range(n_heads // 2):
            pb, r = divmod(s, per_kv)
            h_lo = (2 * pb) * per_kv + r
            h_hi = (2 * pb + 1) * per_kv + r
            o_lo = o[h_lo * t_new:(h_lo + 1) * t_new, pb * LANES:(pb + 1) * LANES]
            o_hi = o[h_hi * t_new:(h_hi + 1) * t_new, pb * LANES:(pb + 1) * LANES]
            y_ref[0, :, s * LANES:(s + 1) * LANES] = jnp.where(low, o_lo, o_hi).astype(y_ref.dtype)


def _pages_per_step(n_pages):
    return math.gcd(n_pages, PAGES_PER_STEP_MAX)


def _dsa_sample(q, qiw, sa, kk_new, k_new, v_new, ck, cv, ci, page_table, layer, *, n_heads, kv_heads, idx_heads,
                wi_lane, att_scale, idx_w_scale):
    db, t_new, dq = q.shape
    _, _, kv_heads_, hd, page = ck.shape
    kvw = kv_heads_ * hd
    n_pages = page_table.shape[1]
    assert page == LANES and t_new % SUBLANES == 0 and t_new <= page and kv_heads_ == kv_heads and hd == LANES // 2
    pg = _pages_per_step(n_pages)
    n_chains = 2 if pg % 2 == 0 else 1
    pc = pg // n_chains
    n_steps = n_pages // pg
    n_keys = (n_pages + 1) * page
    topk = min(TOPK_MAX, (n_pages * page + t_new) // 4)
    idx_spec = lambda r: pl.BlockSpec((1, 1, hd, page), lambda b, i, pt: (layer, pt[b, i * pg + r], 0, 0))
    kv_spec = lambda r: pl.BlockSpec((1, 1, kv_heads, hd, page), lambda b, i, pt: (layer, pt[b, i * pg + r], 0, 0, 0))
    row_spec = lambda w: pl.BlockSpec((1, t_new, w), lambda b, i, pt: (b, 0, 0))

    keys = pl.pallas_call(
        functools.partial(_dsa_sample_scores_kernel, pg=pg, t_new=t_new, idx_heads=idx_heads, page=page,
                          idx_w_scale=idx_w_scale, wi_lane=wi_lane, n_keys=n_keys),
        grid_spec=pltpu.PrefetchScalarGridSpec(
            num_scalar_prefetch=1,
            grid=(db, n_steps),
            in_specs=[row_spec(qiw.shape[-1]), row_spec(LANES), row_spec(LANES)] + [idx_spec(r) for r in range(pg)],
            out_specs=row_spec(n_keys),
            scratch_shapes=[pltpu.VMEM((idx_heads * t_new, hd), F32), pltpu.VMEM((idx_heads * t_new, LANES), F32),
                            pltpu.VMEM((page, hd), F32)],
        ),
        out_shape=jax.ShapeDtypeStruct((db, t_new, n_keys), I32),
        compiler_params=_params("parallel", "arbitrary"),
        name="dsa_sample_scores",
    )(page_table, qiw, sa, kk_new, *([ci] * pg))

    n_rows = db * t_new
    rb = math.gcd(n_rows, LANES)
    thr, cut = pl.pallas_call(
        functools.partial(_dsa_sample_threshold_kernel, rb=rb, topk=topk, n_keys=n_keys),
        out_shape=[jax.ShapeDtypeStruct((n_rows, LANES), I32)] * 2,
        compiler_params=pltpu.CompilerParams(vmem_limit_bytes=VMEM_LIMIT_BYTES),
        name="dsa_sample_threshold",
    )(keys.reshape(n_rows, n_keys))
    thr = thr.reshape(db, t_new, LANES)
    cut = cut.reshape(db, t_new, LANES)

    return pl.pallas_call(
        functools.partial(_dsa_sample_attend_kernel, pg=pg, pc=pc, t_new=t_new, n_heads=n_heads, kv_heads=kv_heads,
                          page=page, att_scale=att_scale, n_keys=n_keys),
        grid_spec=pltpu.PrefetchScalarGridSpec(
            num_scalar_prefetch=1,
            grid=(db, n_steps),
            in_specs=[row_spec(dq), row_spec(n_keys), row_spec(LANES), row_spec(LANES), row_spec(kvw), row_spec(kvw)]
                     + [kv_spec(r) for r in range(pg)] * 2,
            out_specs=row_spec(dq),
            scratch_shapes=[pltpu.VMEM((n_heads * t_new, kvw), F32), pltpu.VMEM((n_chains, n_heads * t_new, 1), F32),
                            pltpu.VMEM((n_chains, n_heads * t_new, 1), F32),
                            pltpu.VMEM((n_chains, n_heads * t_new, kvw), F32),
                            pltpu.VMEM((page, kvw), F32), pltpu.VMEM((page, kvw), F32)],
        ),
        out_shape=jax.ShapeDtypeStruct((db, t_new, dq), BF16),
        compiler_params=_params("parallel", "arbitrary"),
        name="dsa_sample_attend",
    )(page_table, q, keys, thr, cut, k_new, v_new, *([ck] * pg), *([cv] * pg))


def _merge_kernel(x_ref, ys_ref, ya_ref, gs_ref, ga_ref, ps_ref, pa_ref, wo_ref, o_ref):
    merged = (_sigmoid(gs_ref[...]) * _dot(ys_ref[...], ps_ref[...])
              + _sigmoid(ga_ref[...]) * _dot(ya_ref[...], pa_ref[...]))
    o_ref[...] = x_ref[...] + _dot(merged.astype(BF16), wo_ref[...])


def _merge(x, y_ssd, y_attn, g_s, g_a, p_ssd, p_attn, w_out):
    t, d = x.shape
    tm = min(ROW_TILE, t)
    assert t % tm == 0
    rows = lambda w: pl.BlockSpec((tm, w), lambda i: (i, 0))
    full = lambda a: pl.BlockSpec(a.shape, lambda i: (0, 0))
    return pl.pallas_call(
        _merge_kernel,
        grid=(t // tm,),
        in_specs=[rows(d), rows(y_ssd.shape[1]), rows(y_attn.shape[1]), rows(d), rows(d),
                  full(p_ssd), full(p_attn), full(w_out)],
        out_specs=rows(d),
        out_shape=jax.ShapeDtypeStruct((t, d), F32),
        compiler_params=_params("parallel"),
        name="merge",
    )(x, y_ssd, y_attn, g_s, g_a, p_ssd, p_attn, w_out)


def _pair_slot_perm(n_heads, kv_heads, hd):
    per_kv = n_heads // kv_heads
    cols = []
    for s in range(n_heads // 2):
        pb, r = divmod(s, per_kv)
        for h in ((2 * pb) * per_kv + r, (2 * pb + 1) * per_kv + r):
            cols.extend(range(h * hd, (h + 1) * hd))
    return np.asarray(cols, np.int32)


def kernel(x_prompt, x_sample, cache_k, cache_v, cache_idx_k, state_ssm, state_conv, page_table, ffn1_norm, ffn1_w1, ffn1_w2, mix_norm, w_in, conv_w, conv_b, dt_bias, a_log, d_skip, ssd_norm, w_branch_ssd, w_branch_attn, w_out, ffn2_norm, ffn2_w1, ffn2_w2, final_norm):
    bp, seq, d_model = x_prompt.shape
    db, dseq, _ = x_sample.shape
    depth, _, _, kv_heads, head_dim = cache_k.shape
    idx_dim = cache_idx_k.shape[-1]
    ssd_heads, ssd_hd, d_state = state_ssm.shape[2:]
    conv_dim = state_conv.shape[-1]
    d_inner = ssd_norm.shape[-1]
    n_groups = (conv_dim - d_inner) // (2 * d_state)
    attn_dim = w_branch_attn.shape[1]
    n_heads = attn_dim // head_dim
    kvw = kv_heads * head_dim
    d_proj = w_in.shape[-1]
    idx_heads = (d_proj - (d_inner + conv_dim + ssd_heads + attn_dim + 2 * kvw + idx_dim + 2 * d_model)) // (idx_dim + 1)
    assert head_dim == LANES // 2 and idx_dim == LANES // 2 and ssd_heads + idx_heads <= LANES
    att_scale = head_dim ** -0.5
    idx_w_scale = (idx_heads ** -0.5) * (idx_dim ** -0.5)
    sizes = (d_inner, conv_dim, ssd_heads, attn_dim, kvw, kvw, idx_heads * idx_dim, idx_dim, idx_heads, d_model, d_model)
    assert sum(sizes) == d_proj
    offs = np.concatenate([[0], np.cumsum(sizes)])
    perm = _pair_slot_perm(n_heads, kv_heads, head_dim)
    wi_lane = ssd_heads

    tp, ts = bp * seq, db * dseq
    yp = x_prompt.reshape(tp, d_model)
    ys = x_sample.reshape(ts, d_model)
    ck = jnp.transpose(cache_k, (0, 1, 3, 4, 2))
    cv = jnp.transpose(cache_v, (0, 1, 3, 4, 2))
    ci = jnp.transpose(cache_idx_k, (0, 1, 3, 2))
    dsa_kw = dict(n_heads=n_heads, kv_heads=kv_heads, idx_heads=idx_heads, wi_lane=wi_lane, att_scale=att_scale,
                  idx_w_scale=idx_w_scale)
    outs = {n: [] for n in ("kp", "vp", "ip", "sp", "cp", "ks", "vs", "is", "ss", "cs")}
    hist = conv_w.shape[1] - 1

    for l in range(depth):
        wl = w_in[l]
        col = lambda i: wl[:, offs[i]:offs[i + 1]]
        w_z, w_xbc, w_dt, w_q, w_k, w_v, w_qi, w_ki, w_wi, w_gs, w_ga = [col(i) for i in range(11)]
        w_sa = jnp.concatenate([w_dt, w_wi, jnp.zeros((d_model, LANES - ssd_heads - idx_heads), F32)], axis=1)
        w_qi_wide = jnp.pad(w_qi.reshape(d_model, idx_heads, idx_dim), ((0, 0), (0, 0), (0, LANES - idx_dim)))
        bf = lambda w: w.astype(BF16)
        plain = lambda w, *dts: (bf(w), dts, False, None)
        cols = lambda w, *dts: (bf(w.T), dts, True, None)
        w_kk = jnp.concatenate([w_ki, w_ki], axis=1)
        zcol = jnp.zeros((d_model, head_dim), F32)
        w_vx = jnp.concatenate([a for g in range(kv_heads) for a in
                                ((w_v[:, g * head_dim:(g + 1) * head_dim], zcol) if g % 2 == 0 else
                                 (zcol, w_v[:, g * head_dim:(g + 1) * head_dim]))], axis=1)
        half = np.arange(kv_heads * LANES) // head_dim
        b_vx = jnp.asarray(((half % 2) != ((half // 2) % 2)).astype(np.float32)).reshape(1, -1)
        f1w1, f1w2, f2w1, f2w2 = bf(ffn1_w1[l]), bf(ffn1_w2[l]), bf(ffn2_w1[l]), bf(ffn2_w2[l])
        p_ssd, p_attn, wo = bf(w_branch_ssd[l]), bf(w_branch_attn[l][perm, :]), bf(w_out[l])
        last = l == depth - 1
        ssd_kw = dict(n_groups=n_groups, d_state=d_state)
        ssd_w = (conv_w[l], conv_b[l], dt_bias[l], a_log[l], d_skip[l], ssd_norm[l])

        yp = _ffn(yp, ffn1_norm[l], f1w1, f1w2)
        ys = _ffn(ys, ffn1_norm[l], f1w1, f1w2)

        z, xbc = _norm_linear(yp, mix_norm[l], [plain(w_z, F32), plain(w_xbc, F32)], PROJ_ROW_TILE)
        q, kt, ktb, vx, vt, kkb, kit, sa, g_s, g_a, qi = _norm_linear(
            yp, mix_norm[l],
            [plain(w_q[:, perm], BF16), cols(w_k, F32, BF16), (bf(w_vx), (BF16,), False, b_vx), cols(w_v, F32),
             plain(w_kk, BF16), cols(w_ki, F32), plain(w_sa, F32), plain(w_gs, F32), plain(w_ga, F32),
             plain(w_qi, BF16)], PROJ_ROW_TILE, batch=bp)
        r3 = lambda a: a.reshape(bp, seq, a.shape[-1])
        y_ssd, hfp = _ssd(r3(z), r3(xbc), r3(sa), jnp.zeros((1, bp, hist, conv_dim), F32),
                          jnp.zeros((1, bp, ssd_heads, ssd_hd, d_state), F32), 0, *ssd_w, **ssd_kw)
        y_attn = _dsa_prompt(r3(q), r3(qi), r3(sa), r3(kkb), ktb, r3(vx), **dsa_kw)
        yp = _merge(yp, y_ssd.reshape(tp, d_inner), y_attn.reshape(tp, attn_dim), g_s, g_a, p_ssd, p_attn, wo)
        cbp = r3(xbc)[:, seq - hist:, :]
        heads_last = lambda a: jnp.transpose(a.reshape(bp, kv_heads, head_dim, seq), (0, 3, 1, 2))
        kp, vp, kip = heads_last(kt), heads_last(vt), jnp.transpose(kit, (0, 2, 1))

        z, xbc = _norm_linear(ys, mix_norm[l], [plain(w_z, F32), plain(w_xbc, F32)], PROJ_ROW_TILE)
        q, k, kb, v, vb, kk, kkb, sa, g_s, g_a, qi = _norm_linear(
            ys, mix_norm[l],
            [plain(w_q[:, perm], BF16), plain(w_k, F32, BF16), plain(w_v, F32, BF16), plain(w_kk, F32, BF16),
             plain(w_sa, F32), plain(w_gs, F32), plain(w_ga, F32),
             plain(w_qi_wide.reshape(d_model, idx_heads * LANES), BF16)], PROJ_ROW_TILE)
        r3 = lambda a: a.reshape(db, dseq, a.shape[-1])
        y_ssd, hfs = _ssd(r3(z), r3(xbc), r3(sa), state_conv, state_ssm, l, *ssd_w, **ssd_kw)
        y_attn = _dsa_sample(r3(q), r3(qi), r3(sa), r3(kkb), r3(kb), r3(vb), ck, cv, ci, page_table, l, **dsa_kw)
        ys = _merge(ys, y_ssd.reshape(ts, d_inner), y_attn.reshape(ts, attn_dim), g_s, g_a, p_ssd, p_attn, wo)
        cbs = r3(xbc)[:, dseq - hist:, :]
        kss, vss = k.reshape(db, dseq, kv_heads, head_dim), v.reshape(db, dseq, kv_heads, head_dim)
        kis = r3(kk)[:, :, :idx_dim]

        pg_ = final_norm if last else None
        yp = _ffn(yp, ffn2_norm[l], f2w1, f2w2, pg_)
        ys = _ffn(ys, ffn2_norm[l], f2w1, f2w2, pg_)
        for n, a in zip(("kp", "vp", "ip", "sp", "cp", "ks", "vs", "is", "ss", "cs"),
                        (kp, vp, kip, hfp, cbp, kss, vss, kis, hfs, cbs)):
            outs[n].append(a)

    st = lambda n: jnp.stack(outs[n])
    return (yp.reshape(bp, seq, d_model), ys.reshape(db, dseq, d_model),
            st("kp"), st("vp"), st("ip"), st("sp"), st("cp"),
            st("ks"), st("vs"), st("is"), st("ss"), st("cs"))
```

```python
import functools
import math

import jax
import jax.numpy as jnp
import numpy as np
from jax import lax
from jax.experimental import pallas as pl
from jax.experimental.pallas import tpu as pltpu

F32 = jnp.float32
BF16 = jnp.bfloat16
I32 = jnp.int32

EPS = 1e-6
SSD_CHUNK = 128
TOPK_MAX = 256
LANES = 128
SUBLANES = 8
VMEM_LIMIT_BYTES = 56 * 1024 * 1024
NEG = -1e30
INT_MIN = -(2 ** 31)

ROW_TILE = 512
PROJ_ROW_TILE = 256
FF_TILE_MAX = 1536
Q_TILE = 128
KEY_TILE = 512
PAGES_PER_STEP_MAX = 64


def _params(*sem):
    return pltpu.CompilerParams(dimension_semantics=sem, vmem_limit_bytes=VMEM_LIMIT_BYTES)


def _sigmoid(x):
    return 1.0 / (1.0 + jnp.exp(-x))


def _rms(x, g):
    return x * lax.rsqrt(jnp.mean(x * x, axis=-1, keepdims=True) + EPS) * g


def _dot(a, b):
    return jnp.dot(a, b, preferred_element_type=F32)


def _dot_nt(a, b):
    return lax.dot_general(a, b, (((1,), (1,)), ((), ())), preferred_element_type=F32)


def _split2(x):
    hi = x.astype(BF16)
    lo = (x - hi.astype(F32)).astype(BF16)
    return hi, lo


def _split3(x):
    hi = x.astype(BF16)
    r = x - hi.astype(F32)
    mid = r.astype(BF16)
    lo = (r - mid.astype(F32)).astype(BF16)
    return hi, mid, lo


def _ffn_kernel(*refs, post_norm):
    if post_norm:
        x_ref, g_ref, wa_ref, wb_ref, w2_ref, pg_ref, o_ref, h_s, acc_s = refs
    else:
        x_ref, g_ref, wa_ref, wb_ref, w2_ref, o_ref, h_s, acc_s = refs
    f = pl.program_id(1)

    @pl.when(f == 0)
    def _():
        h_s[...] = _rms(x_ref[...], g_ref[...]).astype(BF16)
        acc_s[...] = jnp.zeros_like(acc_s)

    h = h_s[...]
    a = _dot(h, wa_ref[...])
    b = _dot(h, wb_ref[...])
    u = (a * _sigmoid(a) * b).astype(BF16)
    acc_s[...] += _dot(u, w2_ref[...])

    @pl.when(f == pl.num_programs(1) - 1)
    def _():
        y = x_ref[...] + 0.5 * acc_s[...]
        if post_norm:
            y = _rms(y, pg_ref[...])
        o_ref[...] = y


def _ff_tile(d_ff):
    best = None
    for t in range(LANES, d_ff + 1, LANES):
        if d_ff % t == 0 and t <= FF_TILE_MAX:
            best = t
    assert best is not None, d_ff
    return best


def _ffn(x, g, w1, w2, post_gain=None):
    t, d = x.shape
    d_ff = w2.shape[0]
    tm = min(ROW_TILE, t)
    tf = _ff_tile(d_ff)
    nf = d_ff // tf
    assert t % tm == 0
    post_norm = post_gain is not None
    in_specs = [
        pl.BlockSpec((tm, d), lambda i, f: (i, 0)),
        pl.BlockSpec((1, d), lambda i, f: (0, 0)),
        pl.BlockSpec((d, tf), lambda i, f: (0, f)),
        pl.BlockSpec((d, tf), lambda i, f: (0, f + nf)),
        pl.BlockSpec((tf, d), lambda i, f: (f, 0)),
    ]
    args = [x, g.reshape(1, d), w1, w1, w2]
    if post_norm:
        in_specs.append(pl.BlockSpec((1, d), lambda i, f: (0, 0)))
        args.append(post_gain.reshape(1, d))
    return pl.pallas_call(
        functools.partial(_ffn_kernel, post_norm=post_norm),
        grid=(t // tm, nf),
        in_specs=in_specs,
        out_specs=pl.BlockSpec((tm, d), lambda i, f: (i, 0)),
        out_shape=jax.ShapeDtypeStruct((t, d), F32),
        scratch_shapes=[pltpu.VMEM((tm, d), BF16), pltpu.VMEM((tm, d), F32)],
        compiler_params=_params("parallel", "arbitrary"),
        name="ffn",
    )(*args)


def _norm_linear_kernel(*refs, plan):
    n_w = len(plan)
    n_b = sum(has_bias for _, _, has_bias in plan)
    x_ref, g_ref = refs[:2]
    w_refs = refs[2:2 + n_w]
    b_refs = list(refs[2 + n_w:2 + n_w + n_b])
    o_refs = list(refs[2 + n_w + n_b:])
    h = _rms(x_ref[...], g_ref[...]).astype(BF16)
    for w_ref, (dts, transposed, has_bias) in zip(w_refs, plan):
        r = _dot_nt(w_ref[...], h) if transposed else _dot(h, w_ref[...])
        if has_bias:
            r = r + b_refs.pop(0)[...]
        for dt in dts:
            o_ref = o_refs.pop(0)
            if transposed:
                o_ref[0] = r.astype(dt)
            else:
                o_ref[...] = r.astype(dt)


def _norm_linear(x, g, maps, tm, batch=1):
    t, d = x.shape
    tm = min(tm, t)
    s = t // batch
    assert t % tm == 0 and s % tm == 0
    steps = s // tm
    in_specs = [pl.BlockSpec((tm, d), lambda i: (i, 0)), pl.BlockSpec((1, d), lambda i: (0, 0))]
    biases, out_specs, out_shape = [], [], []
    for w, dts, transposed, bias in maps:
        in_specs.append(pl.BlockSpec(w.shape, lambda i: (0, 0), pipeline_mode=pl.Buffered(1)))
        n = w.shape[0] if transposed else w.shape[1]
        for dt in dts:
            if transposed:
                out_specs.append(pl.BlockSpec((1, n, tm), lambda i: (i // steps, 0, i % steps)))
                out_shape.append(jax.ShapeDtypeStruct((batch, n, s), dt))
            else:
                out_specs.append(pl.BlockSpec((tm, n), lambda i: (i, 0)))
                out_shape.append(jax.ShapeDtypeStruct((t, n), dt))
        if bias is not None:
            assert not transposed
            biases.append(bias)
    in_specs += [pl.BlockSpec(b.shape, lambda i: (0, 0)) for b in biases]
    plan = tuple((tuple(dts), transposed, bias is not None) for _, dts, transposed, bias in maps)
    return pl.pallas_call(
        functools.partial(_norm_linear_kernel, plan=plan),
        grid=(t // tm,),
        in_specs=in_specs,
        out_specs=out_specs,
        out_shape=out_shape,
        compiler_params=_params("parallel"),
        name="norm_linear",
    )(x, g.reshape(1, d), *[m[0] for m in maps], *biases)


def _ssd_kernel(z_ref, xbc_ref, dt_ref, buf_ref, h0_ref, cw_ref, cb_ref, dtb_ref, alog_ref, dsk_ref, ng_ref, e_ref,
                y_ref, hfin_ref, xp_s, ht_s, *, qin, d_inner, n_groups, d_state, conv_w):
    q = SSD_CHUNK
    c = pl.program_id(1)
    hp_blocks = d_inner // LANES
    gw = d_inner // n_groups
    assert d_state == LANES and gw % LANES == 0
    pad = SUBLANES
    hist = conv_w - 1

    @pl.when(c == 0)
    def _():
        xp_s[0:pad, :] = buf_ref[0, 0]
        if qin < q:
            xp_s[pad + qin:pad + q, :] = jnp.zeros((q - qin, xp_s.shape[1]), F32)
        for i in range(hp_blocks):
            ht_s[:, i * LANES:(i + 1) * LANES] = h0_ref[0, 0, i * LANES:(i + 1) * LANES, :].T

    xp_s[pad:pad + qin, :] = xbc_ref[0]
    x_cur = xp_s[pad:pad + q, :]
    x_prev = xp_s[0:pad, :]
    row8 = lax.broadcasted_iota(I32, (pad, 1), 0)
    acc = cb_ref[...] + x_cur * cw_ref[hist:hist + 1, :]
    for i in range(hist):
        s = hist - i
        rolled = pltpu.roll(x_cur, s, 0)
        head = jnp.where(row8 < s, pltpu.roll(x_prev, s, 0), rolled[0:pad])
        acc = acc + jnp.concatenate([head, rolled[pad:]], axis=0) * cw_ref[i:i + 1, :]
    xc = acc * _sigmoid(acc)
    tail = xp_s[pad + qin - hist:pad + qin, :]
    xp_s[pad - hist:pad, :] = tail

    xs = xc[:, :d_inner]
    bm = xc[:, d_inner:d_inner + n_groups * d_state]
    cm = xc[:, d_inner + n_groups * d_state:]

    dt_raw = dt_ref[0] + dtb_ref[...]
    dt = jnp.maximum(dt_raw, 0.0) + jnp.log1p(jnp.exp(-jnp.abs(dt_raw)))
    if qin < q:
        dt = jnp.concatenate([dt, jnp.zeros((q - qin, LANES), F32)], axis=0)
    la = dt * (-jnp.exp(alog_ref[...]))

    ri = lax.broadcasted_iota(I32, (q, q), 0)
    ci = lax.broadcasted_iota(I32, (q, q), 1)
    causal = ri >= ci
    tril = jnp.where(causal, 1.0, 0.0).astype(BF16)
    eye = jnp.where(ri == ci, 1.0, 0.0).astype(BF16)
    a_cs = sum(_dot(tril, p) for p in _split3(la))
    a_cs_t = sum(_dot_nt(eye, p) for p in _split3(a_cs))
    dec = jnp.exp(a_cs[q - 1:q, :] - a_cs)
    eac = jnp.exp(a_cs)
    stacked = jnp.concatenate([dt, dec, eac], axis=0)
    expd = sum(_dot(p, e_ref[...]) for p in _split2(stacked))
    dt_e, dec_e, eac_e = expd[0:q], expd[q:2 * q], expd[2 * q:3 * q]

    x = xs * dt_e
    xb = x.astype(BF16)
    xd = (x * dec_e).astype(BF16)
    lane = lax.broadcasted_iota(I32, (q, LANES), 1)
    hd = LANES // 2
    heads_per_group = gw // hd

    y_parts = []
    for g in range(n_groups):
        gs = slice(g * gw, (g + 1) * gw)
        cg = cm[:, g * d_state:(g + 1) * d_state].astype(BF16)
        bg = bm[:, g * d_state:(g + 1) * d_state]
        cb = _dot_nt(cg, bg.astype(BF16))
        y_off = _dot(cg, ht_s[:, gs].astype(BF16))
        pair_parts = []
        for p in range(heads_per_group // 2):
            xp = xb[:, g * gw + p * LANES:g * gw + (p + 1) * LANES]
            res = []
            for k in range(2):
                j = g * heads_per_group + 2 * p + k
                diff = a_cs[:, j:j + 1] - a_cs_t[j:j + 1, :]
                lm = jnp.exp(jnp.where(causal, diff, NEG))
                res.append(_dot((cb * lm).astype(BF16), xp))
            pair_parts.append(jnp.where(lane < hd, res[0], res[1]))
        y_diag = jnp.concatenate(pair_parts, axis=1)
        y_parts.append(y_diag + y_off * eac_e[:, gs])
        st = _dot(bg.T.astype(BF16), xd[:, gs])
        ht_s[:, gs] = ht_s[:, gs] * eac_e[q - 1:q, gs] + st

    zz = z_ref[0]
    outs = []
    for g in range(n_groups):
        gs = slice(g * gw, (g + 1) * gw)
        yv = (y_parts[g][:qin] + dsk_ref[:, gs] * xs[:qin, gs]) * (zz[:, gs] * _sigmoid(zz[:, gs]))
        ms = jnp.mean(yv * yv, axis=-1, keepdims=True)
        outs.append(yv * lax.rsqrt(ms + EPS) * ng_ref[:, gs])
    y_ref[0] = jnp.concatenate(outs, axis=1).astype(y_ref.dtype)

    @pl.when(c == pl.num_programs(1) - 1)
    def _():
        for i in range(hp_blocks):
            hfin_ref[0, i * LANES:(i + 1) * LANES, :] = ht_s[:, i * LANES:(i + 1) * LANES].T


def _ssd(z, xbc, dtp, buf, h0, layer, conv_w, conv_b, dt_bias, a_log, d_skip, norm_g, *, n_groups, d_state):
    b, l, d_inner = z.shape
    conv_dim = xbc.shape[-1]
    n_heads, p_dim, n_state = h0.shape[2:]
    width = conv_w.shape[0]
    hist = width - 1
    assert n_heads <= LANES and p_dim == LANES // 2 and n_state == d_state and hist <= SUBLANES
    qin = math.gcd(l, SSD_CHUNK)
    assert qin % SUBLANES == 0 and qin >= hist
    nc = l // qin
    assert nc == 1 or qin == SSD_CHUNK
    hp = n_heads * p_dim
    buf8 = jnp.pad(buf, ((0, 0), (0, 0), (SUBLANES - hist, 0), (0, 0)))
    cw8 = jnp.pad(conv_w, ((0, SUBLANES - width), (0, 0)))
    pad1 = lambda v: jnp.pad(v.reshape(1, -1), ((0, 0), (0, LANES - n_heads)))
    expand = (np.arange(LANES)[:, None] == (np.arange(d_inner)[None, :] // p_dim)).astype(np.float32)
    full = lambda shape: pl.BlockSpec(shape, lambda i, c: (0,) * len(shape))
    y, hfin = pl.pallas_call(
        functools.partial(_ssd_kernel, qin=qin, d_inner=d_inner, n_groups=n_groups, d_state=d_state, conv_w=width),
        grid=(b, nc),
        in_specs=[
            pl.BlockSpec((1, qin, d_inner), lambda i, c: (i, c, 0)),
            pl.BlockSpec((1, qin, conv_dim), lambda i, c: (i, c, 0)),
            pl.BlockSpec((1, qin, LANES), lambda i, c: (i, c, 0)),
            pl.BlockSpec((1, 1, SUBLANES, conv_dim), lambda i, c: (layer, i, 0, 0)),
            pl.BlockSpec((1, 1, hp, n_state), lambda i, c: (layer, i, 0, 0)),
            full((SUBLANES, conv_dim)), full((1, conv_dim)), full((1, LANES)), full((1, LANES)),
            full((1, d_inner)), full((1, d_inner)), full((LANES, d_inner)),
        ],
        out_specs=[
            pl.BlockSpec((1, qin, d_inner), lambda i, c: (i, c, 0)),
            pl.BlockSpec((1, hp, n_state), lambda i, c: (i, 0, 0)),
        ],
        out_shape=[jax.ShapeDtypeStruct((b, l, d_inner), BF16), jax.ShapeDtypeStruct((b, hp, n_state), F32)],
        scratch_shapes=[pltpu.VMEM((SUBLANES + SSD_CHUNK, conv_dim), F32), pltpu.VMEM((n_state, hp), F32)],
        compiler_params=_params("parallel", "arbitrary"),
        name="ssd",
    )(z, xbc, dtp, buf8, h0.reshape(-1, b, hp, n_state), cw8, conv_b.reshape(1, -1), pad1(dt_bias), pad1(a_log),
      jnp.repeat(d_skip, p_dim).reshape(1, -1), norm_g.reshape(1, -1), jnp.asarray(expand, BF16))
    return y, hfin.reshape(b, n_heads, p_dim, n_state)


def _score_key(sc, valid):
    bits = lax.bitcast_convert_type(sc + 0.0, I32)
    key = jnp.where(bits < 0, bits ^ jnp.int32(0x7FFFFFFF), bits)
    return jnp.where(valid, key, jnp.int32(INT_MIN))


def _kth_largest_key(count_ge, shape, k):
    def body(b, st):
        t, c_t = st
        cand = t + lax.shift_left(jnp.int32(1), jnp.int32(31) - b)
        c = count_ge(cand)
        return jnp.where(c >= k, cand, t), jnp.where(c >= k, c, c_t)
    return lax.fori_loop(0, 32, body, (jnp.full(shape, INT_MIN, I32), jnp.zeros(shape, F32)))


def _tie_cutoff(count_tie_before, need, shape, n_bits):
    def body(b, p):
        bit = lax.shift_left(jnp.int32(1), jnp.int32(n_bits - 1) - b)
        p_c = p | bit
        return jnp.where(count_tie_before(p_c) <= need - 1, p_c, p)
    return lax.fori_loop(0, n_bits, body, jnp.zeros(shape, I32))


def _dsa_prompt_kernel(q_ref, qi_ref, sa_ref, kk_ref, kt_ref, vx_ref, y_ref,
                       keys_s, qpad_s, qipad_s, w_s, m_s, acc_s,
                       *, tq, tk, n_heads, kv_heads, idx_heads, topk, att_scale, idx_w_scale, wi_lane, seq_len):
    i = pl.program_id(1)
    hd = LANES // 2
    per_kv = n_heads // kv_heads
    n_kt = (i * tq + tq - 1) // tk + 1
    lane = lax.broadcasted_iota(I32, (tq, LANES), 1)
    low = lane < hd

    qf = q_ref[0].astype(F32) * (att_scale * math.log2(math.e))
    zeros = jnp.zeros((tq, LANES), F32)
    for h in range(n_heads):
        g, r = divmod(h, per_kv)
        slot = (g // 2) * per_kv + r
        sl = qf[:, slot * LANES:(slot + 1) * LANES]
        half = jnp.where(low, sl, 0.0) if g % 2 == 0 else jnp.where(low, 0.0, sl)
        row = [zeros] * (kv_heads // 2)
        row[g // 2] = half
        qpad_s[h * tq:(h + 1) * tq, :] = jnp.concatenate(row, axis=1).astype(BF16)
    qif = qi_ref[0].astype(F32)
    for h in range(idx_heads):
        sl = qif[:, (h // 2) * LANES:(h // 2 + 1) * LANES]
        half = jnp.where(low, sl, 0.0) if h % 2 == 0 else jnp.where(low, 0.0, sl)
        qipad_s[h * tq:(h + 1) * tq, :] = half.astype(BF16)
    w_s[...] = sa_ref[0].T[wi_lane:wi_lane + idx_heads, :] * idx_w_scale

    qpos = i * tq + lax.broadcasted_iota(I32, (tk, tq), 1)
    kiota = lax.broadcasted_iota(I32, (tk, tq), 0)
    siota = lax.broadcasted_iota(I32, (SUBLANES, tq), 0)

    tiles_per_step = 2

    def score_tiles(jj, carry):
        for u in range(tiles_per_step):
            j = jj * tiles_per_step + u
            kt = kk_ref[0, pl.ds(pl.multiple_of(j * tk, tk), tk), :]
            r = _dot_nt(kt, qipad_s[...])
            sc = jnp.zeros((tk, tq), F32)
            for h in range(idx_heads):
                sc = sc + jnp.maximum(r[:, h * tq:(h + 1) * tq], 0.0) * w_s[h:h + 1, :]
            keys_s[j] = _score_key(sc, j * tk + kiota <= qpos)
        return carry
    lax.fori_loop(0, (n_kt + tiles_per_step - 1) // tiles_per_step, score_tiles, 0)

    n_acc = 4

    def count(pred):
        def body(j, cnts):
            cnts = list(cnts)
            for g in range(tk // SUBLANES):
                hit = pred(keys_s[j, g * SUBLANES:(g + 1) * SUBLANES, :], j * tk + g * SUBLANES)
                cnts[g % n_acc] = cnts[g % n_acc] + jnp.where(hit, 1.0, 0.0)
            return tuple(cnts)
        cnts = lax.fori_loop(0, n_kt, body, (jnp.zeros((SUBLANES, tq), F32),) * n_acc)
        return jnp.sum(sum(cnts), axis=0, keepdims=True)

    def count_ge(t):
        tb = jnp.broadcast_to(t, (SUBLANES, tq))
        return count(lambda key, base: key >= tb)

    thr, c_ge = _kth_largest_key(count_ge, (1, tq), topk)
    tie = c_ge > topk

    @pl.when(jnp.max(jnp.where(tie, 1, 0)) > 0)
    def _():
        need = topk - count(lambda key, base: key > thr)
        cut = _tie_cutoff(lambda p: count(lambda key, base: (key == thr) & (base + siota < p)),
                          need, (1, tq), int(seq_len).bit_length())
        def drop(j, carry):
            key = keys_s[j]
            keys_s[j] = jnp.where(tie & (key == thr) & (j * tk + kiota > cut), jnp.int32(INT_MIN), key)
            return carry
        lax.fori_loop(0, n_kt, drop, 0)

    thr_sel = jnp.maximum(thr, jnp.int32(INT_MIN + 1))

    n_chunks = tk // LANES
    m_s[...] = jnp.full(m_s.shape, NEG, F32)
    acc_s[...] = jnp.zeros(acc_s.shape, F32)

    def attend_tile(j, carry):
        ks = pl.ds(pl.multiple_of(j * tk, tk), tk)
        kt = kt_ref[0, :, ks]
        half = n_heads * tq // 2
        s_halves = [_dot(qpad_s[0:half], kt), _dot(qpad_s[half:], kt)]
        bias = [jnp.where(keys_s[j, c * LANES:(c + 1) * LANES, :] >= thr_sel, 0.0, NEG).T for c in range(n_chunks)]
        for g in range(kv_heads):
            ps, alphas = [], []
            for r in range(per_kv):
                rows = slice((g * per_kv + r) * tq, (g * per_kv + r + 1) * tq)
                src, off = s_halves[rows.start // half], rows.start % half
                s = [src[off:off + tq, c * LANES:(c + 1) * LANES] + bias[c] for c in range(n_chunks)]
                smax = s[0]
                for c in range(1, n_chunks):
                    smax = jnp.maximum(smax, s[c])
                m_old = m_s[rows]
                m_new = jnp.maximum(m_old, jnp.max(smax, axis=1, keepdims=True))
                m_s[rows] = m_new
                alphas.append(jnp.exp2(m_old - m_new))
                ps.append(jnp.concatenate([jnp.exp2(s[c] - m_new) for c in range(n_chunks)], axis=1).astype(BF16))
            grows = slice(g * per_kv * tq, (g + 1) * per_kv * tq)
            pv = _dot(jnp.concatenate(ps, axis=0), vx_ref[0, ks, g * LANES:(g + 1) * LANES])
            acc_s[grows] = acc_s[grows] * jnp.concatenate(alphas, axis=0) + pv
        return carry
    lax.fori_loop(0, n_kt, attend_tile, 0)

    def normalised(h):
        a = acc_s[h * tq:(h + 1) * tq]
        return a / pltpu.roll(a, hd, 1)
    for s in range(n_heads // 2):
        pb, r = divmod(s, per_kv)
        o_lo = normalised((2 * pb) * per_kv + r)
        o_hi = normalised((2 * pb + 1) * per_kv + r)
        y_ref[0, :, s * LANES:(s + 1) * LANES] = jnp.where(low, o_lo, o_hi).astype(y_ref.dtype)


def _dsa_prompt(q, qi, sa, kk, kt, vx, *, n_heads, kv_heads, idx_heads, wi_lane, att_scale, idx_w_scale):
    b, l, dq = q.shape
    tq, tk = min(Q_TILE, l), min(KEY_TILE, l // 2)
    assert l % tq == 0 and l % (2 * tk) == 0 and kv_heads % 2 == 0 and idx_heads % 2 == 0
    assert wi_lane % SUBLANES == 0 and idx_heads == SUBLANES
    topk = min(TOPK_MAX, l // 4)
    kvw = kt.shape[1]
    return pl.pallas_call(
        functools.partial(_dsa_prompt_kernel, tq=tq, tk=tk, n_heads=n_heads, kv_heads=kv_heads, idx_heads=idx_heads,
                          topk=topk, att_scale=att_scale, idx_w_scale=idx_w_scale, wi_lane=wi_lane, seq_len=l),
        grid=(b, l // tq),
        in_specs=[
            pl.BlockSpec((1, tq, dq), lambda bi, i: (bi, i, 0)),
            pl.BlockSpec((1, tq, qi.shape[-1]), lambda bi, i: (bi, i, 0)),
            pl.BlockSpec((1, tq, LANES), lambda bi, i: (bi, i, 0)),
            pl.BlockSpec((1, l, LANES), lambda bi, i: (bi, 0, 0), pipeline_mode=pl.Buffered(1)),
            pl.BlockSpec((1, kvw, l), lambda bi, i: (bi, 0, 0), pipeline_mode=pl.Buffered(1)),
            pl.BlockSpec((1, l, vx.shape[-1]), lambda bi, i: (bi, 0, 0), pipeline_mode=pl.Buffered(1)),
        ],
        out_specs=pl.BlockSpec((1, tq, dq), lambda bi, i: (bi, i, 0)),
        out_shape=jax.ShapeDtypeStruct((b, l, dq), BF16),
        scratch_shapes=[
            pltpu.VMEM((l // tk, tk, tq), I32),
            pltpu.VMEM((n_heads * tq, kvw), BF16),
            pltpu.VMEM((idx_heads * tq, LANES), BF16),
            pltpu.VMEM((idx_heads, tq), F32),
            pltpu.VMEM((n_heads * tq, LANES), F32),
            pltpu.VMEM((n_heads * tq, LANES), F32),
        ],
        compiler_params=_params("parallel", "arbitrary"),
        name="dsa_prompt",
    )(q, qi, sa, kk, kt, vx)


def _dsa_sample_scores_kernel(pt_ref, qi_ref, sa_ref, kn_ref, *rest, pg, t_new, idx_heads, page,
                              idx_w_scale, wi_lane, n_keys):
    page_refs = rest[:pg]
    keys_ref, qi_s, w_s, kn_s = rest[pg:]
    i = pl.program_id(1)
    hd = LANES // 2
    n_steps = pl.num_programs(1)

    @pl.when(i == 0)
    def _():
        qif = qi_ref[0]
        sa = sa_ref[0]
        for h in range(idx_heads):
            qi_s[h * t_new:(h + 1) * t_new, :] = qif[:, h * LANES:h * LANES + hd].astype(F32)
            w_s[h * t_new:(h + 1) * t_new, :] = jnp.broadcast_to(
                sa[:, wi_lane + h:wi_lane + h + 1] * idx_w_scale, (t_new, LANES))
        kn_s[...] = jnp.zeros(kn_s.shape, F32)
        kn_s[0:t_new, :] = kn_ref[0][:, 0:hd].astype(F32)

    def scores(dots):
        ww = jnp.maximum(dots, 0.0) * w_s[...]
        sc = ww[0:t_new]
        for h in range(1, idx_heads):
            sc = sc + ww[h * t_new:(h + 1) * t_new]
        return sc

    always = jnp.full((t_new, page), True)
    kt = jnp.concatenate([page_refs[r][0, 0].astype(BF16) for r in range(pg)], axis=1)
    dots = _dot(qi_s[...].astype(BF16), kt)
    for r in range(pg):
        off = pl.multiple_of((i * pg + r) * page, page)
        keys_ref[0, :, pl.ds(off, page)] = _score_key(scores(dots[:, r * page:(r + 1) * page]), always)

    @pl.when(i == n_steps - 1)
    def _():
        ti = lax.broadcasted_iota(I32, (t_new, page), 0)
        ki = lax.broadcasted_iota(I32, (t_new, page), 1)
        past = n_keys - page
        dots_new = _dot_nt(qi_s[...].astype(BF16), kn_s[...].astype(BF16))
        keys_ref[0, :, past:n_keys] = _score_key(scores(dots_new), ki <= ti)


def _dsa_sample_threshold_kernel(keys_ref, thr_ref, cut_ref, *, rb, topk, n_keys):
    n_chunks = n_keys // LANES
    unroll = next(u for u in (4, 3, 2, 1) if n_chunks % u == 0)
    liota = lax.broadcasted_iota(I32, (rb, LANES), 1)
    for b in range(keys_ref.shape[0] // rb):
        rows = slice(b * rb, (b + 1) * rb)

        def count(pred):
            def body(cc, cnt):
                for u in range(unroll):
                    off = pl.multiple_of((cc * unroll + u) * LANES, LANES)
                    cnt = cnt + jnp.where(pred(keys_ref[rows, pl.ds(off, LANES)], off), 1.0, 0.0)
                return cnt
            cnt = lax.fori_loop(0, n_chunks // unroll, body, jnp.zeros((rb, LANES), F32))
            return jnp.sum(cnt, axis=1, keepdims=True)

        def count_ge(t):
            tb = jnp.broadcast_to(t, (rb, LANES))
            return count(lambda key, off: key >= tb)

        thr, c_ge = _kth_largest_key(count_ge, (rb, 1), topk)
        tie = c_ge > topk
        thr_ref[rows, :] = jnp.broadcast_to(jnp.maximum(thr, jnp.int32(INT_MIN + 1)), (rb, LANES))
        cut_ref[rows, :] = jnp.full((rb, LANES), n_keys, I32)

        @pl.when(jnp.max(jnp.where(tie, 1, 0)) > 0)
        def _():
            need = topk - count(lambda key, off: key > thr)
            cut = _tie_cutoff(lambda p: count(lambda key, off: (key == thr) & (off + liota < p)), need, (rb, 1),
                              int(n_keys).bit_length())
            cut_ref[rows, :] = jnp.broadcast_to(jnp.where(tie, cut, n_keys), (rb, LANES))


def _dsa_sample_attend_kernel(pt_ref, q_ref, keys_ref, thr_ref, cut_ref, kn_ref, vn_ref, *rest, pg, pc, t_new,
                              n_heads, kv_heads, page, att_scale, n_keys):
    k_refs = rest[:pg]
    v_refs = rest[pg:2 * pg]
    y_ref, qpad_s, m_s, l_s, acc_s, kn_s, vn_s = rest[2 * pg:]
    i = pl.program_id(1)
    hd = LANES // 2
    per_kv = n_heads // kv_heads
    n_steps = pl.num_programs(1)
    lane = lax.broadcasted_iota(I32, (t_new, LANES), 1)
    low = lane < hd

    @pl.when(i == 0)
    def _():
        qf = q_ref[0].astype(F32) * att_scale
        zeros = jnp.zeros((t_new, LANES), F32)
        for h in range(n_heads):
            g, r = divmod(h, per_kv)
            slot = (g // 2) * per_kv + r
            sl = qf[:, slot * LANES:(slot + 1) * LANES]
            half = jnp.where(low, sl, 0.0) if g % 2 == 0 else jnp.where(low, 0.0, sl)
            row = [zeros] * (kv_heads // 2)
            row[g // 2] = half
            qpad_s[h * t_new:(h + 1) * t_new, :] = jnp.concatenate(row, axis=1)
        m_s[...] = jnp.full(m_s.shape, NEG, F32)
        l_s[...] = jnp.zeros(l_s.shape, F32)
        acc_s[...] = jnp.zeros(acc_s.shape, F32)
        kn_s[...] = jnp.zeros(kn_s.shape, F32)
        vn_s[...] = jnp.zeros(vn_s.shape, F32)
        kn_s[0:t_new, :] = kn_ref[0].astype(F32)
        vn_s[0:t_new, :] = vn_ref[0].astype(F32)

    thr = thr_ref[0]
    cut = cut_ref[0]

    def attend(s, key, pos0, pv, chain):
        n = key.shape[1]
        wide = lambda a: jnp.concatenate([a] * (n // LANES), axis=1)
        pos = pos0 + lax.broadcasted_iota(I32, (t_new, n), 1)
        sel = (key > wide(thr)) | ((key == wide(thr)) & (pos <= wide(cut)))
        bias = jnp.where(sel, 0.0, NEG)
        s = s + jnp.concatenate([bias] * n_heads, axis=0)
        m_old = m_s[chain]
        m_new = jnp.maximum(m_old, jnp.max(s, axis=1, keepdims=True))
        alpha = jnp.exp(m_old - m_new)
        p = jnp.exp(s - m_new)
        l_s[chain] = alpha * l_s[chain] + jnp.sum(p, axis=1, keepdims=True)
        m_s[chain] = m_new
        acc_s[chain] = acc_s[chain] * alpha + pv(p.astype(BF16))

    n_chains = m_s.shape[0]
    qb = qpad_s[...].astype(BF16)
    kvw = qb.shape[1]
    for c in range(pg // pc):
        off = pl.multiple_of((i * pg + c * pc) * page, page)
        kt = jnp.concatenate([k_refs[c * pc + r][0, 0].reshape(kvw, page).astype(BF16) for r in range(pc)], axis=1)
        vt = jnp.concatenate([v_refs[c * pc + r][0, 0].reshape(kvw, page).astype(BF16) for r in range(pc)], axis=1)
        attend(_dot(qb, kt), keys_ref[0, :, pl.ds(off, pc * page)], off, lambda p: _dot_nt(p, vt), c % n_chains)

    @pl.when(i == n_steps - 1)
    def _():
        attend(_dot_nt(qb, kn_s[...].astype(BF16)), keys_ref[0, :, n_keys - page:n_keys], n_keys - page,
               lambda p: _dot(p, vn_s[...].astype(BF16)), 0)
        m = m_s[0]
        for c in range(1, n_chains):
            m = jnp.maximum(m, m_s[c])
        l = sum(l_s[c] * jnp.exp(m_s[c] - m) for c in range(n_chains))
        acc = sum(acc_s[c] * jnp.exp(m_s[c] - m) for c in range(n_chains))
        o = acc / l
        for s in range(n_heads // 2):
            pb, r = divmod(s, per_kv)
            h_lo = (2 * pb) * per_kv + r
            h_hi = (2 * pb + 1) * per_kv + r
            o_lo = o[h_lo * t_new:(h_lo + 1) * t_new, pb * LANES:(pb + 1) * LANES]
            o_hi = o[h_hi * t_new:(h_hi + 1) * t_new, pb * LANES:(pb + 1) * LANES]
            y_ref[0, :, s * LANES:(s + 1) * LANES] = jnp.where(low, o_lo, o_hi).astype(y_ref.dtype)


def _pages_per_step(n_pages):
    return math.gcd(n_pages, PAGES_PER_STEP_MAX)


def _dsa_sample(q, qiw, sa, kk_new, k_new, v_new, ck, cv, ci, page_table, layer, *, n_heads, kv_heads, idx_heads,
                wi_lane, att_scale, idx_w_scale):
    db, t_new, dq = q.shape
    _, _, kv_heads_, hd, page = ck.shape
    kvw = kv_heads_ * hd
    n_pages = page_table.shape[1]
    assert page == LANES and t_new % SUBLANES == 0 and t_new <= page and kv_heads_ == kv_heads and hd == LANES // 2
    pg = _pages_per_step(n_pages)
    n_chains = 2 if pg % 2 == 0 else 1
    pc = pg // n_chains
    n_steps = n_pages // pg
    n_keys = (n_pages + 1) * page
    topk = min(TOPK_MAX, (n_pages * page + t_new) // 4)
    idx_spec = lambda r: pl.BlockSpec((1, 1, hd, page), lambda b, i, pt: (layer, pt[b, i * pg + r], 0, 0))
    kv_spec = lambda r: pl.BlockSpec((1, 1, kv_heads, hd, page), lambda b, i, pt: (layer, pt[b, i * pg + r], 0, 0, 0))
    row_spec = lambda w: pl.BlockSpec((1, t_new, w), lambda b, i, pt: (b, 0, 0))

    keys = pl.pallas_call(
        functools.partial(_dsa_sample_scores_kernel, pg=pg, t_new=t_new, idx_heads=idx_heads, page=page,
                          idx_w_scale=idx_w_scale, wi_lane=wi_lane, n_keys=n_keys),
        grid_spec=pltpu.PrefetchScalarGridSpec(
            num_scalar_prefetch=1,
            grid=(db, n_steps),
            in_specs=[row_spec(qiw.shape[-1]), row_spec(LANES), row_spec(LANES)] + [idx_spec(r) for r in range(pg)],
            out_specs=row_spec(n_keys),
            scratch_shapes=[pltpu.VMEM((idx_heads * t_new, hd), F32), pltpu.VMEM((idx_heads * t_new, LANES), F32),
                            pltpu.VMEM((page, hd), F32)],
        ),
        out_shape=jax.ShapeDtypeStruct((db, t_new, n_keys), I32),
        compiler_params=_params("parallel", "arbitrary"),
        name="dsa_sample_scores",
    )(page_table, qiw, sa, kk_new, *([ci] * pg))

    n_rows = db * t_new
    rb = math.gcd(n_rows, LANES)
    thr, cut = pl.pallas_call(
        functools.partial(_dsa_sample_threshold_kernel, rb=rb, topk=topk, n_keys=n_keys),
        out_shape=[jax.ShapeDtypeStruct((n_rows, LANES), I32)] * 2,
        compiler_params=pltpu.CompilerParams(vmem_limit_bytes=VMEM_LIMIT_BYTES),
        name="dsa_sample_threshold",
    )(keys.reshape(n_rows, n_keys))
    thr = thr.reshape(db, t_new, LANES)
    cut = cut.reshape(db, t_new, LANES)

    return pl.pallas_call(
        functools.partial(_dsa_sample_attend_kernel, pg=pg, pc=pc, t_new=t_new, n_heads=n_heads, kv_heads=kv_heads,
                          page=page, att_scale=att_scale, n_keys=n_keys),
        grid_spec=pltpu.PrefetchScalarGridSpec(
            num_scalar_prefetch=1,
            grid=(db, n_steps),
            in_specs=[row_spec(dq), row_spec(n_keys), row_spec(LANES), row_spec(LANES), row_spec(kvw), row_spec(kvw)]
                     + [kv_spec(r) for r in range(pg)] * 2,
            out_specs=row_spec(dq),
            scratch_shapes=[pltpu.VMEM((n_heads * t_new, kvw), F32), pltpu.VMEM((n_chains, n_heads * t_new, 1), F32),
                            pltpu.VMEM((n_chains, n_heads * t_new, 1), F32),
                            pltpu.VMEM((n_chains, n_heads * t_new, kvw), F32),
                            pltpu.VMEM((page, kvw), F32), pltpu.VMEM((page, kvw), F32)],
        ),
        out_shape=jax.ShapeDtypeStruct((db, t_new, dq), BF16),
        compiler_params=_params("parallel", "arbitrary"),
        name="dsa_sample_attend",
    )(page_table, q, keys, thr, cut, k_new, v_new, *([ck] * pg), *([cv] * pg))


def _merge_kernel(x_ref, ys_ref, ya_ref, gs_ref, ga_ref, ps_ref, pa_ref, wo_ref, o_ref):
    merged = (_sigmoid(gs_ref[...]) * _dot(ys_ref[...], ps_ref[...])
              + _sigmoid(ga_ref[...]) * _dot(ya_ref[...], pa_ref[...]))
    o_ref[...] = x_ref[...] + _dot(merged.astype(BF16), wo_ref[...])


def _merge(x, y_ssd, y_attn, g_s, g_a, p_ssd, p_attn, w_out):
    t, d = x.shape
    tm = min(ROW_TILE, t)
    assert t % tm == 0
    rows = lambda w: pl.BlockSpec((tm, w), lambda i: (i, 0))
    full = lambda a: pl.BlockSpec(a.shape, lambda i: (0, 0))
    return pl.pallas_call(
        _merge_kernel,
        grid=(t // tm,),
        in_specs=[rows(d), rows(y_ssd.shape[1]), rows(y_attn.shape[1]), rows(d), rows(d),
                  full(p_ssd), full(p_attn), full(w_out)],
        out_specs=rows(d),
        out_shape=jax.ShapeDtypeStruct((t, d), F32),
        compiler_params=_params("parallel"),
        name="merge",
    )(x, y_ssd, y_attn, g_s, g_a, p_ssd, p_attn, w_out)


def _pair_slot_perm(n_heads, kv_heads, hd):
    per_kv = n_heads // kv_heads
    cols = []
    for s in range(n_heads // 2):
        pb, r = divmod(s, per_kv)
        for h in ((2 * pb) * per_kv + r, (2 * pb + 1) * per_kv + r):
            cols.extend(range(h * hd, (h + 1) * hd))
    return np.asarray(cols, np.int32)


def kernel(x_prompt, x_sample, cache_k, cache_v, cache_idx_k, state_ssm, state_conv, page_table, ffn1_norm, ffn1_w1, ffn1_w2, mix_norm, w_in, conv_w, conv_b, dt_bias, a_log, d_skip, ssd_norm, w_branch_ssd, w_branch_attn, w_out, ffn2_norm, ffn2_w1, ffn2_w2, final_norm):
    bp, seq, d_model = x_prompt.shape
    db, dseq, _ = x_sample.shape
    depth, _, _, kv_heads, head_dim = cache_k.shape
    idx_dim = cache_idx_k.shape[-1]
    ssd_heads, ssd_hd, d_state = state_ssm.shape[2:]
    conv_dim = state_conv.shape[-1]
    d_inner = ssd_norm.shape[-1]
    n_groups = (conv_dim - d_inner) // (2 * d_state)
    attn_dim = w_branch_attn.shape[1]
    n_heads = attn_dim // head_dim
    kvw = kv_heads * head_dim
    d_proj = w_in.shape[-1]
    idx_heads = (d_proj - (d_inner + conv_dim + ssd_heads + attn_dim + 2 * kvw + idx_dim + 2 * d_model)) // (idx_dim + 1)
    assert head_dim == LANES // 2 and idx_dim == LANES // 2 and ssd_heads + idx_heads <= LANES
    att_scale = head_dim ** -0.5
    idx_w_scale = (idx_heads ** -0.5) * (idx_dim ** -0.5)
    sizes = (d_inner, conv_dim, ssd_heads, attn_dim, kvw, kvw, idx_heads * idx_dim, idx_dim, idx_heads, d_model, d_model)
    assert sum(sizes) == d_proj
    offs = np.concatenate([[0], np.cumsum(sizes)])
    perm = _pair_slot_perm(n_heads, kv_heads, head_dim)
    wi_lane = ssd_heads

    tp, ts = bp * seq, db * dseq
    yp = x_prompt.reshape(tp, d_model)
    ys = x_sample.reshape(ts, d_model)
    ck = jnp.transpose(cache_k, (0, 1, 3, 4, 2))
    cv = jnp.transpose(cache_v, (0, 1, 3, 4, 2))
    ci = jnp.transpose(cache_idx_k, (0, 1, 3, 2))
    dsa_kw = dict(n_heads=n_heads, kv_heads=kv_heads, idx_heads=idx_heads, wi_lane=wi_lane, att_scale=att_scale,
                  idx_w_scale=idx_w_scale)
    outs = {n: [] for n in ("kp", "vp", "ip", "sp", "cp", "ks", "vs", "is", "ss", "cs")}
    hist = conv_w.shape[1] - 1

    for l in range(depth):
        wl = w_in[l]
        col = lambda i: wl[:, offs[i]:offs[i + 1]]
        w_z, w_xbc, w_dt, w_q, w_k, w_v, w_qi, w_ki, w_wi, w_gs, w_ga = [col(i) for i in range(11)]
        w_sa = jnp.concatenate([w_dt, w_wi, jnp.zeros((d_model, LANES - ssd_heads - idx_heads), F32)], axis=1)
        w_qi_wide = jnp.pad(w_qi.reshape(d_model, idx_heads, idx_dim), ((0, 0), (0, 0), (0, LANES - idx_dim)))
        bf = lambda w: w.astype(BF16)
        plain = lambda w, *dts: (bf(w), dts, False, None)
        cols = lambda w, *dts: (bf(w.T), dts, True, None)
        w_kk = jnp.concatenate([w_ki, w_ki], axis=1)
        zcol = jnp.zeros((d_model, head_dim), F32)
        w_vx = jnp.concatenate([a for g in range(kv_heads) for a in
                                ((w_v[:, g * head_dim:(g + 1) * head_dim], zcol) if g % 2 == 0 else
                                 (zcol, w_v[:, g * head_dim:(g + 1) * head_dim]))], axis=1)
        half = np.arange(kv_heads * LANES) // head_dim
        b_vx = jnp.asarray(((half % 2) != ((half // 2) % 2)).astype(np.float32)).reshape(1, -1)
        f1w1, f1w2, f2w1, f2w2 = bf(ffn1_w1[l]), bf(ffn1_w2[l]), bf(ffn2_w1[l]), bf(ffn2_w2[l])
        p_ssd, p_attn, wo = bf(w_branch_ssd[l]), bf(w_branch_attn[l][perm, :]), bf(w_out[l])
        last = l == depth - 1
        ssd_kw = dict(n_groups=n_groups, d_state=d_state)
        ssd_w = (conv_w[l], conv_b[l], dt_bias[l], a_log[l], d_skip[l], ssd_norm[l])

        yp = _ffn(yp, ffn1_norm[l], f1w1, f1w2)
        ys = _ffn(ys, ffn1_norm[l], f1w1, f1w2)

        z, xbc, q, kt, ktb, vx, vt, kkb, kit, sa, g_s, g_a, qi = _norm_linear(
            yp, mix_norm[l],
            [plain(w_z, F32), plain(w_xbc, F32),
             plain(w_q[:, perm], BF16), cols(w_k, F32, BF16), (bf(w_vx), (BF16,), False, b_vx), cols(w_v, F32),
             plain(w_kk, BF16), cols(w_ki, F32), plain(w_sa, F32), plain(w_gs, F32), plain(w_ga, F32),
             plain(w_qi, BF16)], PROJ_ROW_TILE, batch=bp)
        r3 = lambda a: a.reshape(bp, seq, a.shape[-1])
        y_ssd, hfp = _ssd(r3(z), r3(xbc), r3(sa), jnp.zeros((1, bp, hist, conv_dim), F32),
                          jnp.zeros((1, bp, ssd_heads, ssd_hd, d_state), F32), 0, *ssd_w, **ssd_kw)
        y_attn = _dsa_prompt(r3(q), r3(qi), r3(sa), r3(kkb), ktb, r3(vx), **dsa_kw)
        yp = _merge(yp, y_ssd.reshape(tp, d_inner), y_attn.reshape(tp, attn_dim), g_s, g_a, p_ssd, p_attn, wo)
        cbp = r3(xbc)[:, seq - hist:, :]
        heads_last = lambda a: jnp.transpose(a.reshape(bp, kv_heads, head_dim, seq), (0, 3, 1, 2))
        kp, vp, kip = heads_last(kt), heads_last(vt), jnp.transpose(kit, (0, 2, 1))

        z, xbc, q, k, kb, v, vb, kk, kkb, sa, g_s, g_a, qi = _norm_linear(
            ys, mix_norm[l],
            [plain(w_z, F32), plain(w_xbc, F32),
             plain(w_q[:, perm], BF16), plain(w_k, F32, BF16), plain(w_v, F32, BF16), plain(w_kk, F32, BF16),
             plain(w_sa, F32), plain(w_gs, F32), plain(w_ga, F32),
             plain(w_qi_wide.reshape(d_model, idx_heads * LANES), BF16)], PROJ_ROW_TILE)
        r3 = lambda a: a.reshape(db, dseq, a.shape[-1])
        y_ssd, hfs = _ssd(r3(z), r3(xbc), r3(sa), state_conv, state_ssm, l, *ssd_w, **ssd_kw)
        y_attn = _dsa_sample(r3(q), r3(qi), r3(sa), r3(kkb), r3(kb), r3(vb), ck, cv, ci, page_table, l, **dsa_kw)
        ys = _merge(ys, y_ssd.reshape(ts, d_inner), y_attn.reshape(ts, attn_dim), g_s, g_a, p_ssd, p_attn, wo)
        cbs = r3(xbc)[:, dseq - hist:, :]
        kss, vss = k.reshape(db, dseq, kv_heads, head_dim), v.reshape(db, dseq, kv_heads, head_dim)
        kis = r3(kk)[:, :, :idx_dim]

        pg_ = final_norm if last else None
        yp = _ffn(yp, ffn2_norm[l], f2w1, f2w2, pg_)
        ys = _ffn(ys, ffn2_norm[l], f2w1, f2w2, pg_)
        for n, a in zip(("kp", "vp", "ip", "sp", "cp", "ks", "vs", "is", "ss", "cs"),
                        (kp, vp, kip, hfp, cbp, kss, vss, kis, hfs, cbs)):
            outs[n].append(a)

    st = lambda n: jnp.stack(outs[n])
    return (yp.reshape(bp, seq, d_model), ys.reshape(db, dseq, d_model),
            st("kp"), st("vp"), st("ip"), st("sp"), st("cp"),
            st("ks"), st("vs"), st("is"), st("ss"), st("cs"))
```
